```python
import math
import jax, jax.numpy as jnp
from jax import lax
import numpy as np


D_MODEL = 1024
BATCH = 8
SEQ = 2048
DEPTH = 1

N_META = 16
NORM_EPS = 1e-6

RWKV_HEADS = 16
RWKV_HEAD_DIM = 64
RWKV_WIDTH = RWKV_HEADS * RWKV_HEAD_DIM
DECAY_LORA = 64
AAA_LORA = 64
GATE_LORA = 128
RWKV_GN_EPS = 64e-5
RWKV_COLS = 3 * RWKV_WIDTH + DECAY_LORA + AAA_LORA + GATE_LORA

DIFF_HEADS = 8
DIFF_HEAD_DIM = 64
DIFF_V_DIM = 2 * DIFF_HEAD_DIM
DIFF_QK_WIDTH = DIFF_HEADS * 2 * DIFF_HEAD_DIM
DIFF_V_WIDTH = DIFF_HEADS * DIFF_V_DIM
DIFF_COLS = 2 * DIFF_QK_WIDTH + DIFF_V_WIDTH
ROPE_THETA = 500000.0
ROPE_DIM = DIFF_HEAD_DIM // 4
Q_BLOCK = 128

GATE_COLS = 2 * D_MODEL
IN_COLS = RWKV_COLS + DIFF_COLS + GATE_COLS

N_GROUPS = 4
EXPERTS_PER_GROUP = 8
N_EXPERTS = N_GROUPS * EXPERTS_PER_GROUP
TOP_K = 2
EXPERT_FF = 512
MOE_BLOCK = 128

kernel_name = 'hybrid_rwkv7_diffattn_hmoe'


def rms_norm(x, w, eps=NORM_EPS):
    xf = x.astype(jnp.float32)
    y = xf * lax.rsqrt(jnp.mean(xf * xf, axis=-1, keepdims=True) + eps)
    return (y * w.astype(jnp.float32)).astype(x.dtype)


def token_shift(p):
    return jnp.pad(p, ((0, 0), (1, 0), (0, 0)))[:, :-1]


def partial_rope(t, pos):
    half = ROPE_DIM // 2
    inv_freq = jnp.exp(-math.log(ROPE_THETA) * jnp.arange(half, dtype=jnp.float32) * 2.0 / ROPE_DIM)
    ang = pos.astype(jnp.float32)[:, None] * inv_freq[None, :]
    cos = jnp.cos(ang)[None, :, None, None, :]
    sin = jnp.sin(ang)[None, :, None, None, :]
    tf = t.astype(jnp.float32)
    x1 = tf[..., :half]
    x2 = tf[..., half:ROPE_DIM]
    out = jnp.concatenate([x1 * cos - x2 * sin, x2 * cos + x1 * sin, tf[..., ROPE_DIM:]], axis=-1)
    return out.astype(t.dtype)


def rwkv7_time_mix(p, mu, w0, w2, a0, a2, g2, k_k, k_a, r_k, ln_w, ln_b):
    bsz, length, _ = p.shape
    f32 = jnp.float32
    p = p + (token_shift(p) - p) * mu
    c0 = RWKV_WIDTH
    c1 = 2 * RWKV_WIDTH
    c2 = 3 * RWKV_WIDTH
    c3 = c2 + DECAY_LORA
    c4 = c3 + AAA_LORA
    r, k, v = p[..., :c0], p[..., c0:c1], p[..., c1:c2]
    xw, xa, xg = p[..., c2:c3], p[..., c3:c4], p[..., c4:]
    w_log = -jax.nn.softplus(-(w0 + jnp.tanh(xw) @ w2).astype(f32)) - 0.5
    decay = jnp.exp(-jnp.exp(w_log))
    a = jax.nn.sigmoid((a0 + xa @ a2).astype(f32))
    g = jax.nn.sigmoid(xg) @ g2
    heads = lambda t: t.astype(f32).reshape(bsz, length, RWKV_HEADS, RWKV_HEAD_DIM)
    kk = heads(k * k_k)
    kk = kk / jnp.maximum(jnp.sqrt(jnp.sum(kk * kk, axis=-1, keepdims=True)), 1e-12)
    k = k.astype(f32) * (1.0 + (a - 1.0) * k_a.astype(f32))
    r, k, v, a, decay = heads(r), heads(k), heads(v), heads(a), heads(decay)

    def step(state, inp):
        r_t, w_t, k_t, v_t, kk_t, a_t = inp
        sa = jnp.einsum('bhvk,bhk->bhv', state, -kk_t)
        state = (state * w_t[:, :, None, :]
                 + sa[..., None] * (kk_t * a_t)[:, :, None, :]
                 + v_t[..., None] * k_t[:, :, None, :])
        return state, jnp.einsum('bhvk,bhk->bhv', state, r_t)

    seq_first = lambda t: jnp.moveaxis(t, 1, 0)
    s0 = jnp.zeros((bsz, RWKV_HEADS, RWKV_HEAD_DIM, RWKV_HEAD_DIM), f32)
    _, y = lax.scan(step, s0, tuple(map(seq_first, (r, decay, k, v, kk, a))))
    y = jnp.moveaxis(y, 0, 1)
    mean = jnp.mean(y, axis=-1, keepdims=True)
    var = jnp.mean(jnp.square(y - mean), axis=-1, keepdims=True)
    yn = ((y - mean) * lax.rsqrt(var + RWKV_GN_EPS)).reshape(bsz, length, RWKV_WIDTH)
    yn = yn * ln_w.astype(f32) + ln_b.astype(f32)
    bonus = (jnp.sum(r * k * r_k.astype(f32), axis=-1, keepdims=True) * v).reshape(bsz, length, RWKV_WIDTH)
    return ((yn + bonus) * g.astype(f32)).astype(p.dtype)


def diff_attention(p, pos, q_norm_w, k_norm_w, lq1, lk1, lq2, lk2, subln_w, lambda_init):
    bsz, length, _ = p.shape
    f32 = jnp.float32
    q = p[..., :DIFF_QK_WIDTH].reshape(bsz, length, DIFF_HEADS, 2, DIFF_HEAD_DIM)
    k = p[..., DIFF_QK_WIDTH:2 * DIFF_QK_WIDTH].reshape(bsz, length, DIFF_HEADS, 2, DIFF_HEAD_DIM)
    v = p[..., 2 * DIFF_QK_WIDTH:].reshape(bsz, length, DIFF_HEADS, DIFF_V_DIM)
    q = partial_rope(rms_norm(q, q_norm_w), pos)
    k = partial_rope(rms_norm(k, k_norm_w), pos)
    lam = (jnp.exp(jnp.sum(lq1.astype(f32) * lk1.astype(f32)))
           - jnp.exp(jnp.sum(lq2.astype(f32) * lk2.astype(f32))) + lambda_init)
    scale = DIFF_HEAD_DIM ** -0.5
    bounds = [(0, N_META)] + [(s, min(s + Q_BLOCK, length)) for s in range(N_META, length, Q_BLOCK)]
    outs = []
    for s, e in bounds:
        qb = q[:, s:e]
        kb = k[:, :e]
        vb = v[:, :e]
        sc = jnp.einsum('bqhcd,bkhcd->bhcqk', qb, kb, preferred_element_type=f32) * scale
        mask = jnp.arange(e)[None, :] <= jnp.arange(s, e)[:, None]
        pr = jax.nn.softmax(jnp.where(mask, sc, -jnp.inf), axis=-1)
        att = pr[:, :, 0] - lam * pr[:, :, 1]
        outs.append(jnp.einsum('bhqk,bkhe->bqhe', att.astype(vb.dtype), vb))
    o = jnp.concatenate(outs, axis=1)
    o = rms_norm(o, subln_w) * (1.0 - lambda_init)
    return o.reshape(bsz, length, DIFF_V_WIDTH)


def hierarchical_moe(x, wg, bg, we, be, w_gate, w_up, w_down):
    n_tok, d = x.shape
    f32 = jnp.float32
    g_prob = jax.nn.softmax((x @ wg).astype(f32) + bg.astype(f32), axis=-1)
    g_val, g_idx = lax.top_k(g_prob, 1)
    e_logits = ((x @ we).astype(f32) + be.astype(f32)).reshape(n_tok, N_GROUPS, EXPERTS_PER_GROUP)
    e_logits = jnp.take_along_axis(e_logits, g_idx[:, :, None], axis=1)[:, 0]
    e_prob = jax.nn.softmax(e_logits, axis=-1)
    e_val, e_idx = lax.top_k(e_prob, TOP_K)
    gate = g_val * e_val / jnp.sum(e_val, axis=-1, keepdims=True)
    expert_id = g_idx * EXPERTS_PER_GROUP + e_idx
    n_assign = n_tok * TOP_K
    flat_e = expert_id.reshape(n_assign)
    flat_t = jnp.repeat(jnp.arange(n_tok, dtype=jnp.int32), TOP_K)
    flat_g = gate.reshape(n_assign)
    order = jnp.argsort(flat_e)
    se = flat_e[order]
    counts = jnp.bincount(flat_e, length=N_EXPERTS)
    padded = (counts + MOE_BLOCK - 1) // MOE_BLOCK * MOE_BLOCK
    pad_end = jnp.cumsum(padded)
    pad_start = pad_end - padded
    start = jnp.cumsum(counts) - counts
    dest = pad_start[se] + jnp.arange(n_assign) - start[se]
    n_blocks = -(-n_assign // MOE_BLOCK) + N_EXPERTS
    cap = n_blocks * MOE_BLOCK
    buf_t = jnp.full((cap,), n_tok, jnp.int32).at[dest].set(flat_t[order])
    buf_g = jnp.zeros((cap,), f32).at[dest].set(flat_g[order])
    block_e = jnp.minimum(jnp.searchsorted(pad_end, jnp.arange(n_blocks) * MOE_BLOCK, side='right'),
                          N_EXPERTS - 1)
    xp = jnp.concatenate([x, jnp.zeros((1, d), x.dtype)], axis=0)
    xb = xp[buf_t].reshape(n_blocks, MOE_BLOCK, d)

    def expert_block(args):
        xblk, e = args
        hid = jax.nn.silu(xblk @ w_gate[e]) * (xblk @ w_up[e])
        return hid @ w_down[e]

    yb = lax.map(expert_block, (xb, block_e))
    y = jnp.zeros((n_tok + 1, d), f32).at[buf_t].add(yb.reshape(cap, d).astype(f32) * buf_g[:, None])
    return y[:n_tok].astype(x.dtype)


def setup_inputs(seed: int = 0) -> dict:
    key = jax.random.key(seed)
    ks = iter(jax.random.split(key, 40))
    f32 = jnp.float32

    def nrm(shape, scale):
        return jax.random.normal(next(ks), shape, f32) * scale

    def gain(shape):
        return 1.0 + nrm(shape, 0.02)

    return {
        'x': nrm((BATCH, SEQ, D_MODEL), 1.0),
        'meta_tokens': nrm((N_META, D_MODEL), 1.0),
        'norm1_w': gain((DEPTH, D_MODEL)),
        'w_in': nrm((DEPTH, D_MODEL, IN_COLS), D_MODEL ** -0.5),
        'rwkv_mu': jax.random.uniform(next(ks), (DEPTH, RWKV_COLS), f32),
        'rwkv_w0': -1.5 + nrm((DEPTH, RWKV_WIDTH), 1.0),
        'rwkv_w2': nrm((DEPTH, DECAY_LORA, RWKV_WIDTH), 0.1 * DECAY_LORA ** -0.5),
        'rwkv_a0': nrm((DEPTH, RWKV_WIDTH), 0.5),
        'rwkv_a2': nrm((DEPTH, AAA_LORA, RWKV_WIDTH), 0.5 * AAA_LORA ** -0.5),
        'rwkv_g2': nrm((DEPTH, GATE_LORA, RWKV_WIDTH), GATE_LORA ** -0.5),
        'rwkv_k_k': 0.85 + nrm((DEPTH, RWKV_WIDTH), 0.05),
        'rwkv_k_a': 1.0 + nrm((DEPTH, RWKV_WIDTH), 0.05),
        'rwkv_r_k': nrm((DEPTH, RWKV_HEADS, RWKV_HEAD_DIM), 0.1),
        'rwkv_ln_w': gain((DEPTH, RWKV_WIDTH)),
        'rwkv_ln_b': nrm((DEPTH, RWKV_WIDTH), 0.02),
        'q_norm_w': gain((DEPTH, DIFF_HEAD_DIM)),
        'k_norm_w': gain((DEPTH, DIFF_HEAD_DIM)),
        'lambda_q1': nrm((DEPTH, DIFF_HEAD_DIM), 0.1),
        'lambda_k1': nrm((DEPTH, DIFF_HEAD_DIM), 0.1),
        'lambda_q2': nrm((DEPTH, DIFF_HEAD_DIM), 0.1),
        'lambda_k2': nrm((DEPTH, DIFF_HEAD_DIM), 0.1),
        'diff_subln_w': gain((DEPTH, DIFF_V_DIM)),
        'w_branch_rwkv': nrm((DEPTH, RWKV_WIDTH, D_MODEL), RWKV_WIDTH ** -0.5),
        'w_branch_diff': nrm((DEPTH, DIFF_V_WIDTH, D_MODEL), DIFF_V_WIDTH ** -0.5),
        'w_out': nrm((DEPTH, D_MODEL, D_MODEL), D_MODEL ** -0.5),
        'norm2_w': gain((DEPTH, D_MODEL)),
        'router_group_w': nrm((DEPTH, D_MODEL, N_GROUPS), D_MODEL ** -0.5),
        'router_group_b': nrm((DEPTH, N_GROUPS), 0.01),
        'router_expert_w': nrm((DEPTH, D_MODEL, N_EXPERTS), D_MODEL ** -0.5),
        'router_expert_b': nrm((DEPTH, N_EXPERTS), 0.01),
        'expert_w_gate': nrm((DEPTH, N_EXPERTS, D_MODEL, EXPERT_FF), D_MODEL ** -0.5),
        'expert_w_up': nrm((DEPTH, N_EXPERTS, D_MODEL, EXPERT_FF), D_MODEL ** -0.5),
        'expert_w_down': nrm((DEPTH, N_EXPERTS, EXPERT_FF, D_MODEL), EXPERT_FF ** -0.5),
    }


def reference(x, meta_tokens, norm1_w, w_in, rwkv_mu, rwkv_w0, rwkv_w2, rwkv_a0, rwkv_a2, rwkv_g2,
              rwkv_k_k, rwkv_k_a, rwkv_r_k, rwkv_ln_w, rwkv_ln_b, q_norm_w, k_norm_w,
              lambda_q1, lambda_k1, lambda_q2, lambda_k2, diff_subln_w, w_branch_rwkv, w_branch_diff,
              w_out, norm2_w, router_group_w, router_group_b, router_expert_w, router_expert_b,
              expert_w_gate, expert_w_up, expert_w_down):
    bsz = x.shape[0]
    meta = jnp.broadcast_to(meta_tokens[None].astype(x.dtype), (bsz, N_META, D_MODEL))
    h = jnp.concatenate([meta, x], axis=1)
    length = h.shape[1]
    pos = jnp.arange(length)
    for l in range(DEPTH):
        lambda_init = 0.8 - 0.6 * math.exp(-0.3 * l)
        u = rms_norm(h, norm1_w[l])
        proj = u @ w_in[l]
        p_rwkv = proj[..., :RWKV_COLS]
        p_diff = proj[..., RWKV_COLS:RWKV_COLS + DIFF_COLS]
        gates = jax.nn.sigmoid(proj[..., RWKV_COLS + DIFF_COLS:])
        y_rwkv = rwkv7_time_mix(p_rwkv, rwkv_mu[l], rwkv_w0[l], rwkv_w2[l], rwkv_a0[l], rwkv_a2[l],
                                rwkv_g2[l], rwkv_k_k[l], rwkv_k_a[l], rwkv_r_k[l],
                                rwkv_ln_w[l], rwkv_ln_b[l]) @ w_branch_rwkv[l]
        y_diff = diff_attention(p_diff, pos, q_norm_w[l], k_norm_w[l], lambda_q1[l], lambda_k1[l],
                                lambda_q2[l], lambda_k2[l], diff_subln_w[l], lambda_init) @ w_branch_diff[l]
        merged = gates[..., :D_MODEL] * y_rwkv + gates[..., D_MODEL:] * y_diff
        h = h + merged @ w_out[l]
        u2 = rms_norm(h, norm2_w[l]).reshape(-1, D_MODEL)
        h = h + hierarchical_moe(u2, router_group_w[l], router_group_b[l], router_expert_w[l],
                                 router_expert_b[l], expert_w_gate[l], expert_w_up[l],
                                 expert_w_down[l]).reshape(h.shape)
    return h[:, N_META:]
```

```python
import functools
import math

import jax
import jax.numpy as jnp
from jax import lax
from jax.experimental import pallas as pl
from jax.experimental.pallas import tpu as pltpu

F32 = jnp.float32
BF16 = jnp.bfloat16

D_MODEL = 1024
N_META = 16
NORM_EPS = 1e-6
RWKV_HEADS = 16
RWKV_HEAD_DIM = 64
RWKV_WIDTH = RWKV_HEADS * RWKV_HEAD_DIM
DECAY_LORA = 64
AAA_LORA = 64
GATE_LORA = 128
LORA_COLS = DECAY_LORA + AAA_LORA + GATE_LORA
RWKV_GN_EPS = 64e-5
DIFF_HEADS = 8
DIFF_HEAD_DIM = 64
DIFF_V_DIM = 2 * DIFF_HEAD_DIM
DIFF_QK_WIDTH = DIFF_HEADS * 2 * DIFF_HEAD_DIM
DIFF_V_WIDTH = DIFF_HEADS * DIFF_V_DIM
ROPE_THETA = 500000.0
ROPE_DIM = DIFF_HEAD_DIM // 4
N_GROUPS = 4
EXPERTS_PER_GROUP = 8
N_EXPERTS = N_GROUPS * EXPERTS_PER_GROUP
EXPERT_FF = 512

LANES = 128
FRONT_PAD = LANES - N_META
RWKV_CHUNK = 64
ATTN_BLOCK = 128
MOE_BLOCK = 128
NEG_BIG = -1e30
VMEM_LIMIT = 48 * 1024 * 1024

COL_GATE = 0
COL_DIFF = 2 * D_MODEL
COL_RWKV = COL_DIFF + 2 * DIFF_QK_WIDTH + DIFF_V_WIDTH
COL_LORA = COL_RWKV + 3 * RWKV_WIDTH
IN_COLS = COL_LORA + LORA_COLS


def _dot(a, b):
    return jnp.dot(a, b, preferred_element_type=F32)


def _dot_nt(a, b):
    return lax.dot_general(a, b, (((1,), (1,)), ((), ())), preferred_element_type=F32)


def _split2(x):
    hi = x.astype(BF16)
    lo = (x - hi.astype(F32)).astype(BF16)
    return hi, lo


def _mm3(a, b):
    ah, al = _split2(a)
    bh, bl = _split2(b)
    return _dot(ah, bh) + _dot(ah, bl) + _dot(al, bh)


def _sigmoid(x):
    return 1.0 / (1.0 + jnp.exp(-x))


def _head_ones(width=LANES, head=RWKV_HEAD_DIM):
    r = lax.broadcasted_iota(jnp.int32, (width, width), 0) // head
    c = lax.broadcasted_iota(jnp.int32, (width, width), 1) // head
    return jnp.where(r == c, 1.0, 0.0).astype(BF16)


def _seg_sum(x, ones_bd):
    hi, lo = _split2(x)
    return _dot(hi, ones_bd) + _dot(lo, ones_bd)


def _seg_sum_wide(x, ones_bd):
    rows, width = x.shape
    n = width // LANES
    xs = jnp.concatenate([x[:, i * LANES:(i + 1) * LANES] for i in range(n)], axis=0)
    ys = _seg_sum(xs, ones_bd)
    return jnp.concatenate([ys[i * rows:(i + 1) * rows] for i in range(n)], axis=1)


def _proj_kernel(h_ref, nw_ref, w_ref, o_ref, u_ref):
    @pl.when(pl.program_id(1) == 0)
    def _():
        x = h_ref[...]
        ms = jnp.mean(x * x, axis=-1, keepdims=True)
        u_ref[...] = (x * lax.rsqrt(ms + NORM_EPS) * nw_ref[...]).astype(BF16)

    o_ref[...] = _dot(u_ref[...], w_ref[...])


def _proj(h, norm_w, w_bf16, tm, tn):
    t, d = h.shape
    n = w_bf16.shape[1]
    return pl.pallas_call(
        _proj_kernel,
        out_shape=jax.ShapeDtypeStruct((t, n), F32),
        grid=(t // tm, n // tn),
        in_specs=[
            pl.BlockSpec((tm, d), lambda i, j: (i, 0)),
            pl.BlockSpec((1, d), lambda i, j: (0, 0)),
            pl.BlockSpec((d, tn), lambda i, j: (0, j)),
        ],
        out_specs=pl.BlockSpec((tm, tn), lambda i, j: (i, j)),
        scratch_shapes=[pltpu.VMEM((tm, d), BF16)],
        compiler_params=pltpu.CompilerParams(
            dimension_semantics=("parallel", "arbitrary"), vmem_limit_bytes=VMEM_LIMIT),
        name="norm_proj",
    )(h, norm_w.reshape(1, d), w_bf16)


def _rwkv_kernel(r_ref, k_ref, v_ref, lo_ref, mu_ref, mul_ref, w0_ref, w2_ref, a0_ref, a2_ref, g2_ref,
                 kkw_ref, kaw_ref, rkw_ref, lnw_ref, lnb_ref, o_ref,
                 xr, xk, xv, xl, s_ref, kt_s, bt_s, kn_s, rt_s, v_s, y_s, gc_s, bon_s, g_s):
    C = r_ref.shape[1]
    W = RWKV_WIDTH
    n_pairs = W // LANES

    @pl.when(pl.program_id(1) == 0)
    def _init():
        for xs in (xr, xk, xv, xl):
            xs[0:8, :] = jnp.zeros((8, xs.shape[1]), F32)
        s_ref[...] = jnp.zeros(s_ref.shape, F32)

    def shift_mix(in_ref, xs, mu):
        x = in_ref[0]
        xs[8:8 + C, :] = x
        prev = xs[7:7 + C, :]
        xs[7:8, :] = x[C - 1:C, :]
        return x + (prev - x) * mu

    r = shift_mix(r_ref, xr, mu_ref[:, 0:W])
    k = shift_mix(k_ref, xk, mu_ref[:, W:2 * W])
    v = shift_mix(v_ref, xv, mu_ref[:, 2 * W:3 * W])
    lo = shift_mix(lo_ref, xl, mul_ref[...])
    xw = jnp.tanh(lo[:, 0:DECAY_LORA])
    xa = lo[:, DECAY_LORA:DECAY_LORA + AAA_LORA]
    xg = _sigmoid(lo[:, DECAY_LORA + AAA_LORA:LORA_COLS])

    z = -(w0_ref[...] + _mm3(xw, w2_ref[...]))
    softplus = jnp.maximum(z, 0.0) + jnp.log(1.0 + jnp.exp(-jnp.abs(z)))
    lw = -jnp.exp(-softplus - 0.5)
    a = _sigmoid(a0_ref[...] + _mm3(xa, a2_ref[...]))
    g_s[...] = _mm3(xg, g2_ref[...])

    ones_bd = _head_ones()
    kk = k * kkw_ref[...]
    kkn = kk / jnp.maximum(jnp.sqrt(_seg_sum_wide(kk * kk, ones_bd)), 1e-12)
    k2 = k * (1.0 + (a - 1.0) * kaw_ref[...])
    bon_s[...] = _seg_sum_wide(r * k2 * rkw_ref[...], ones_bd) * v

    ti = lax.broadcasted_iota(jnp.int32, (C, C), 0)
    tj = lax.broadcasted_iota(jnp.int32, (C, C), 1)
    ltri = jnp.where(ti >= tj, 1.0, 0.0).astype(BF16)
    l1 = lw.astype(BF16)
    rem = lw - l1.astype(F32)
    l2 = rem.astype(BF16)
    l3 = (rem - l2.astype(F32)).astype(BF16)
    cum = _dot(ltri, l1) + _dot(ltri, l2) + _dot(ltri, l3)
    e_pos = jnp.exp(cum)
    e_neg = jnp.exp(-cum)
    kt_s[...] = kkn * jnp.exp(cum - lw)
    bt_s[...] = kkn * a * e_neg
    kn_s[...] = k2 * e_neg
    rt_s[...] = r * e_pos
    v_s[...] = v
    gc_s[...] = jnp.exp(cum[C - 1:C, :])

    lane = lax.broadcasted_iota(jnp.int32, (1, LANES), 1)
    m0 = jnp.where(lane < RWKV_HEAD_DIM, 1.0, 0.0)
    m1 = 1.0 - m0
    trow = lax.broadcasted_iota(jnp.int32, (C, 2 * C), 0)
    tcol = lax.broadcasted_iota(jnp.int32, (C, 2 * C), 1) % C
    strict = trow > tcol
    incl = trow >= tcol
    eye2 = jnp.where(trow == tcol, 1.0, 0.0)
    br = lax.broadcasted_iota(jnp.int32, (LANES, LANES), 0) // RWKV_HEAD_DIM
    bc = lax.broadcasted_iota(jnp.int32, (LANES, LANES), 1) // RWKV_HEAD_DIM
    bdmask = jnp.where(br == bc, 1.0, 0.0)

    def stack2(y):
        return jnp.concatenate([y * m0, y * m1], axis=0).astype(BF16)

    for p in range(n_pairs):
        sl = slice(p * LANES, (p + 1) * LANES)
        kt = kt_s[:, sl]
        bt = bt_s[:, sl]
        kn = kn_s[:, sl]
        rt = rt_s[:, sl]
        vv = v_s[:, sl]
        gc = gc_s[:, sl]
        s_old = s_ref[p]
        s_bf = s_old.astype(BF16)

        wcat = jnp.concatenate([bt * m0, bt * m1, kn * m0, kn * m1], axis=0).astype(BF16)
        lcat = jnp.concatenate([kt, rt], axis=0).astype(BF16)
        m_all = _dot_nt(lcat, wcat)
        a_mat = jnp.where(strict, m_all[0:C, 0:2 * C], 0.0)
        b_mat = jnp.where(strict, m_all[0:C, 2 * C:4 * C], 0.0)
        p_mat = jnp.where(incl, m_all[C:2 * C, 0:2 * C], 0.0)
        q_mat = jnp.where(incl, m_all[C:2 * C, 2 * C:4 * C], 0.0)

        pw = -a_mat
        t_inv = eye2 + pw
        for _ in range(int(math.log2(C)) - 1):
            pw = _dot(pw.astype(BF16), stack2(pw))
            t_inv = t_inv + _dot(t_inv.astype(BF16), stack2(pw))

        v_st = stack2(vv)
        rhs = _dot_nt(kt.astype(BF16), s_bf) + _dot(b_mat.astype(BF16), v_st)
        u = -_dot(t_inv.astype(BF16), stack2(rhs))
        y = _dot_nt(rt.astype(BF16), s_bf) + _dot(
            jnp.concatenate([p_mat, q_mat], axis=1).astype(BF16),
            jnp.concatenate([stack2(u), v_st], axis=0))
        y_s[:, sl] = y
        uc = jnp.concatenate([u, vv], axis=0)
        xc = jnp.concatenate([bt * gc, kn * gc], axis=0)
        s_ref[p] = s_old * gc + bdmask * _dot(jnp.transpose(uc).astype(BF16), xc.astype(BF16))

    y = y_s[...]
    inv_n = 1.0 / RWKV_HEAD_DIM
    mean = _seg_sum_wide(y, ones_bd) * inv_n
    dlt = y - mean
    var = _seg_sum_wide(dlt * dlt, ones_bd) * inv_n
    yn = dlt * lax.rsqrt(var + RWKV_GN_EPS) * lnw_ref[...] + lnb_ref[...]
    o_ref[0] = (yn + bon_s[...]) * g_s[...]


def _rwkv(proj3, mu_rkv, mu_lo, w0, w2, a0, a2, g2, k_k, k_a, r_k, ln_w, ln_b):
    b, lp, _ = proj3.shape
    C = RWKV_CHUNK
    W = RWKV_WIDTH
    cb = COL_RWKV // W
    lb = COL_LORA // LORA_COLS
    row = lambda x: x.reshape(1, -1)
    full = lambda shape: pl.BlockSpec(shape, lambda i, c: (0,) * len(shape))
    wide = pltpu.VMEM((C, W), F32)
    return pl.pallas_call(
        _rwkv_kernel,
        out_shape=jax.ShapeDtypeStruct((b, lp, W), F32),
        grid=(b, lp // C),
        in_specs=[
            pl.BlockSpec((1, C, W), lambda i, c: (i, c, cb)),
            pl.BlockSpec((1, C, W), lambda i, c: (i, c, cb + 1)),
            pl.BlockSpec((1, C, W), lambda i, c: (i, c, cb + 2)),
            pl.BlockSpec((1, C, LORA_COLS), lambda i, c: (i, c, lb)),
            full((1, 3 * W)), full((1, LORA_COLS)),
            full((1, W)), full((DECAY_LORA, W)), full((1, W)), full((AAA_LORA, W)), full((GATE_LORA, W)),
            full((1, W)), full((1, W)), full((1, W)), full((1, W)), full((1, W)),
        ],
        out_specs=pl.BlockSpec((1, C, W), lambda i, c: (i, c, 0)),
        scratch_shapes=[
            pltpu.VMEM((C + 8, W), F32), pltpu.VMEM((C + 8, W), F32), pltpu.VMEM((C + 8, W), F32),
            pltpu.VMEM((C + 8, LORA_COLS), F32),
            pltpu.VMEM((W // LANES, LANES, LANES), F32),
            wide, wide, wide, wide, wide, wide, pltpu.VMEM((1, W), F32), wide, wide,
        ],
        compiler_params=pltpu.CompilerParams(
            dimension_semantics=("parallel", "arbitrary"), vmem_limit_bytes=VMEM_LIMIT),
        name="rwkv7_time_mix",
    )(proj3, proj3, proj3, proj3, row(mu_rkv), row(mu_lo), row(w0), w2, row(a0), a2, g2,
      row(k_k), row(k_a), row(r_k), row(ln_w), row(ln_b))


def _attn_kernel(q_ref, k_ref, v_ref, cos_ref, s1_ref, s2_ref, qw_ref, kw_ref, lam_ref, sw_ref, o_ref,
                 kp_s, vp_s, *, blk, lambda_init):
    qi = pl.program_id(2)
    lp = k_ref.shape[1]
    ones_bd = _head_ones(LANES, DIFF_HEAD_DIM)
    shift = ROPE_DIM // 2

    def norm_rope(x, w, rows):
        ms = _seg_sum(x * x, ones_bd) * (1.0 / DIFF_HEAD_DIM)
        xn = x * lax.rsqrt(ms + NORM_EPS) * w
        return (xn * cos_ref[rows, :] + pltpu.roll(xn, shift, 1) * s1_ref[rows, :]
                + pltpu.roll(xn, LANES - shift, 1) * s2_ref[rows, :])

    @pl.when(qi == 0)
    def _prep():
        def body(i, carry):
            rows = pl.ds(pl.multiple_of(i * blk, blk), blk)
            kp_s[rows, :] = norm_rope(k_ref[0, rows, :], kw_ref[...], rows).astype(BF16)
            vp_s[rows, :] = v_ref[0, rows, :].astype(BF16)
            return carry
        lax.fori_loop(0, lp // blk, body, 0)

    lane = lax.broadcasted_iota(jnp.int32, (1, LANES), 1)
    m0 = jnp.where(lane < DIFF_HEAD_DIM, 1.0, 0.0)
    m1 = 1.0 - m0
    rows_q = pl.ds(pl.multiple_of(qi * blk, blk), blk)
    qn = norm_rope(q_ref[0], qw_ref[...], rows_q) * (DIFF_HEAD_DIM ** -0.5)
    qs = jnp.concatenate([qn * m0, qn * m1], axis=0).astype(BF16)
    row = qi * blk + lax.broadcasted_iota(jnp.int32, (2 * blk, blk), 0) % blk
    col0 = lax.broadcasted_iota(jnp.int32, (2 * blk, blk), 1)

    def body(j, carry):
        m, l, acc = carry
        ks = pl.ds(pl.multiple_of(j * blk, blk), blk)
        s = _dot_nt(qs, kp_s[ks, :])
        col = j * blk + col0
        mask = (col <= row) & ((col >= FRONT_PAD) | (row < FRONT_PAD))
        s = jnp.where(mask, s, NEG_BIG)
        m_new = jnp.maximum(m, jnp.max(s, axis=1, keepdims=True))
        alpha = jnp.exp(m - m_new)
        p = jnp.exp(s - m_new)
        l = alpha * l + jnp.sum(p, axis=1, keepdims=True)
        acc = alpha * acc + _dot(p.astype(BF16), vp_s[ks, :])
        return m_new, l, acc

    init = (jnp.full((2 * blk, 1), NEG_BIG, F32), jnp.zeros((2 * blk, 1), F32),
            jnp.zeros((2 * blk, LANES), F32))
    _, l, acc = lax.fori_loop(0, qi + 1, body, init)
    o = acc / l
    lam = (jnp.exp(jnp.sum(lam_ref[0:1, :] * lam_ref[1:2, :], axis=1, keepdims=True))
           - jnp.exp(jnp.sum(lam_ref[2:3, :] * lam_ref[3:4, :], axis=1, keepdims=True)) + lambda_init)
    od = o[0:blk] - lam * o[blk:2 * blk]
    ms = jnp.mean(od * od, axis=1, keepdims=True)
    o_ref[0] = od * lax.rsqrt(ms + NORM_EPS) * sw_ref[...] * (1.0 - lambda_init)


def _rope_tables(lp):
    half = ROPE_DIM // 2
    inv_freq = jnp.exp(-math.log(ROPE_THETA) * jnp.arange(half, dtype=F32) * 2.0 / ROPE_DIM)
    pos = (jnp.arange(lp) - FRONT_PAD).astype(F32)
    ang = pos[:, None] * inv_freq[None, :]
    cos, sin = jnp.cos(ang), jnp.sin(ang)
    one = jnp.ones((lp, DIFF_HEAD_DIM - ROPE_DIM), F32)
    zero = jnp.zeros((lp, DIFF_HEAD_DIM - ROPE_DIM), F32)
    zh = jnp.zeros((lp, half), F32)
    c = jnp.concatenate([cos, cos, one], axis=1)
    s1 = jnp.concatenate([zh, sin, zero], axis=1)
    s2 = jnp.concatenate([-sin, zh, zero], axis=1)
    dup = lambda t: jnp.concatenate([t, t], axis=1)
    return dup(c), dup(s1), dup(s2)


def _attention(proj3, q_norm_w, k_norm_w, lam4, subln_w, lambda_init):
    b, lp, _ = proj3.shape
    blk = ATTN_BLOCK
    qb = COL_DIFF // LANES
    kb = qb + DIFF_QK_WIDTH // LANES
    vb = kb + DIFF_QK_WIDTH // LANES
    cos, s1, s2 = _rope_tables(lp)
    dup = lambda w: jnp.concatenate([w, w]).reshape(1, LANES)
    full = lambda shape: pl.BlockSpec(shape, lambda i, h, q: (0,) * len(shape))
    return pl.pallas_call(
        functools.partial(_attn_kernel, blk=blk, lambda_init=lambda_init),
        out_shape=jax.ShapeDtypeStruct((b, lp, DIFF_V_WIDTH), F32),
        grid=(b, DIFF_HEADS, lp // blk),
        in_specs=[
            pl.BlockSpec((1, blk, LANES), lambda i, h, q: (i, q, qb + h)),
            pl.BlockSpec((1, lp, LANES), lambda i, h, q: (i, 0, kb + h)),
            pl.BlockSpec((1, lp, LANES), lambda i, h, q: (i, 0, vb + h)),
            full((lp, LANES)), full((lp, LANES)), full((lp, LANES)),
            full((1, LANES)), full((1, LANES)), full((4, DIFF_HEAD_DIM)), full((1, LANES)),
        ],
        out_specs=pl.BlockSpec((1, blk, LANES), lambda i, h, q: (i, q, h)),
        scratch_shapes=[pltpu.VMEM((lp, LANES), BF16), pltpu.VMEM((lp, LANES), BF16)],
        compiler_params=pltpu.CompilerParams(
            dimension_semantics=("parallel", "parallel", "arbitrary"), vmem_limit_bytes=VMEM_LIMIT),
        name="diff_attention",
    )(proj3, proj3, proj3, cos, s1, s2, dup(q_norm_w), dup(k_norm_w), lam4, subln_w.reshape(1, LANES))


def _merge_kernel(rw_ref, da_ref, g1_ref, g2_ref, h_ref, wbr_ref, wbd_ref, wo_ref, n2_ref, wr_ref, br_ref,
                  h1_ref, u2_ref, il_ref, ic_ref, cnt_ref, base_s, *, lp):
    i = pl.program_id(0)
    tm = rw_ref.shape[0]

    @pl.when(i == 0)
    def _():
        base_s[...] = jnp.zeros(base_s.shape, F32)

    y1 = _dot(rw_ref[...].astype(BF16), wbr_ref[...])
    y2 = _dot(da_ref[...].astype(BF16), wbd_ref[...])
    merged = _sigmoid(g1_ref[...]) * y1 + _sigmoid(g2_ref[...]) * y2
    h1 = h_ref[...] + _dot(merged.astype(BF16), wo_ref[...])
    h1_ref[...] = h1
    u2 = h1 * lax.rsqrt(jnp.mean(h1 * h1, axis=-1, keepdims=True) + NORM_EPS) * n2_ref[...]
    u2_ref[...] = u2

    uh, ul = _split2(u2)
    wh, wl = _split2(wr_ref[...])
    lt = _dot_nt(wh, uh) + _dot_nt(wh, ul) + _dot_nt(wl, uh) + br_ref[...]

    gi8 = lax.broadcasted_iota(jnp.int32, (8, tm), 0)
    lg = lt[0:8]
    ge = jnp.exp(lg - jnp.max(lg, axis=0, keepdims=True))
    gp = ge / jnp.sum(ge, axis=0, keepdims=True)
    gv = jnp.max(gp, axis=0, keepdims=True)
    gidx = jnp.min(jnp.where(gp == gv, gi8, N_EXPERTS), axis=0, keepdims=True)

    ei = lax.broadcasted_iota(jnp.int32, (N_EXPERTS, tm), 0)
    sel = (ei // EXPERTS_PER_GROUP) == gidx
    le = jnp.where(sel, lt[8:8 + N_EXPERTS], NEG_BIG)
    ee = jnp.where(sel, jnp.exp(le - jnp.max(le, axis=0, keepdims=True)), 0.0)
    ep = jnp.where(sel, ee / jnp.sum(ee, axis=0, keepdims=True), -1.0)
    v1 = jnp.max(ep, axis=0, keepdims=True)
    i1 = jnp.min(jnp.where(ep == v1, ei, N_EXPERTS), axis=0, keepdims=True)
    ep2 = jnp.where(ei == i1, -1.0, ep)
    v2 = jnp.max(ep2, axis=0, keepdims=True)
    i2 = jnp.min(jnp.where(ep2 == v2, ei, N_EXPERTS), axis=0, keepdims=True)
    den = v1 + v2
    gate1 = gv * v1 / den
    gate2 = gv * v2 / den

    tok = (i * tm + lax.broadcasted_iota(jnp.int32, (1, tm), 1)).astype(F32)
    pos = tok - jnp.floor((tok + 0.5) / lp) * lp
    valid = pos > (FRONT_PAD - 0.5)

    oh1 = jnp.where((ei == i1) & valid, 1.0, 0.0)
    oh2 = jnp.where((ei == i2) & valid, 1.0, 0.0)
    oh = oh1 + oh2
    ur = lax.broadcasted_iota(jnp.int32, (tm, tm), 0)
    uc = lax.broadcasted_iota(jnp.int32, (tm, tm), 1)
    before = jnp.where(ur < uc, 1.0, 0.0).astype(BF16)
    tot = base_s[:, 0:1] + _dot(oh.astype(BF16), before)
    rank1 = jnp.sum(oh1 * tot, axis=0, keepdims=True)
    rank2 = jnp.sum(oh2 * tot, axis=0, keepdims=True)
    base_s[...] = base_s[...] + jnp.sum(oh, axis=1, keepdims=True)
    cnt_ref[...] = base_s[...]

    il = jnp.where(gi8 == 0, i1, jnp.where(gi8 == 1, i2, jnp.where(
        gi8 == 2, rank1.astype(jnp.int32), jnp.where(gi8 == 3, rank2.astype(jnp.int32), jnp.where(
            gi8 == 4, valid.astype(jnp.int32), 0)))))
    il_ref[...] = il
    ri = lax.broadcasted_iota(jnp.int32, (LANES, tm), 0)
    ic = jnp.where(ri == 0, gate1, jnp.where(ri == 1, gate2, 0.0))
    ic_ref[...] = jnp.transpose(ic)


def _merge(rw, da, proj, h0, wbr, wbd, wo, norm2_w, wr, br, lp, tm):
    t, d = h0.shape
    gb = COL_GATE // d
    full = lambda shape: pl.BlockSpec(shape, lambda i: (0,) * len(shape))
    tile = lambda c: pl.BlockSpec((tm, d), lambda i: (i, c))
    return pl.pallas_call(
        functools.partial(_merge_kernel, lp=lp),
        out_shape=(
            jax.ShapeDtypeStruct((t, d), F32),
            jax.ShapeDtypeStruct((t, d), F32),
            jax.ShapeDtypeStruct((8, t), jnp.int32),
            jax.ShapeDtypeStruct((t, LANES), F32),
            jax.ShapeDtypeStruct((N_EXPERTS, LANES), F32),
        ),
        grid=(t // tm,),
        in_specs=[tile(0), tile(0), tile(gb), tile(gb + 1), tile(0),
                  full((d, d)), full((d, d)), full((d, d)), full((1, d)), full((LANES, d)), full((LANES, 1))],
        out_specs=(
            tile(0), tile(0),
            pl.BlockSpec((8, tm), lambda i: (0, i)),
            pl.BlockSpec((tm, LANES), lambda i: (i, 0)),
            full((N_EXPERTS, LANES)),
        ),
        scratch_shapes=[pltpu.VMEM((N_EXPERTS, LANES), F32)],
        compiler_params=pltpu.CompilerParams(
            dimension_semantics=("arbitrary",), vmem_limit_bytes=VMEM_LIMIT),
        name="merge_router",
    )(rw, da, proj, proj, h0, wbr, wbd, wo, norm2_w.reshape(1, d), wr, br)


def _dispatch_kernel(d1_ref, d2_ref, u_hbm, xin_hbm, xb_hbm, sem):
    del xin_hbm
    i = pl.program_id(0)
    tm = d1_ref.shape[2]

    def copies(r):
        t = i * tm + r
        return [(d_ref[0, 0, r], pltpu.make_async_copy(
            u_hbm.at[pl.ds(t, 1)], xb_hbm.at[pl.ds(jnp.maximum(d_ref[0, 0, r], 0), 1)], sem))
            for d_ref in (d1_ref, d2_ref)]

    def start(r, carry):
        for d, cp in copies(r):
            @pl.when(d >= 0)
            def _():
                cp.start()
        return carry

    def wait(r, carry):
        for d, cp in copies(r):
            @pl.when(d >= 0)
            def _():
                cp.wait()
        return carry

    lax.fori_loop(0, tm, start, 0)
    lax.fori_loop(0, tm, wait, 0)


def _dispatch(dest1, dest2, u2, cap, tm):
    t, d = u2.shape
    nt = t // tm
    smem = lambda: pl.BlockSpec((1, 1, tm), lambda i: (i, 0, 0), memory_space=pltpu.SMEM)
    return pl.pallas_call(
        _dispatch_kernel,
        out_shape=jax.ShapeDtypeStruct((cap, d), F32),
        grid=(nt,),
        in_specs=[smem(), smem(), pl.BlockSpec(memory_space=pl.ANY), pl.BlockSpec(memory_space=pl.ANY)],
        out_specs=pl.BlockSpec(memory_space=pl.ANY),
        scratch_shapes=[pltpu.SemaphoreType.DMA(())],
        input_output_aliases={3: 0},
        compiler_params=pltpu.CompilerParams(dimension_semantics=("arbitrary",)),
        name="moe_dispatch",
    )(dest1.reshape(nt, 1, tm), dest2.reshape(nt, 1, tm), u2, jnp.zeros((cap, d), F32))


def _moe_kernel(be_ref, nb_ref, x_ref, wg_ref, wu_ref, wd_ref, o_ref):
    del be_ref
    i = pl.program_id(0)

    @pl.when(i < nb_ref[0])
    def _():
        x = x_ref[...].astype(BF16)
        hg = _dot(x, wg_ref[0])
        hu = _dot(x, wu_ref[0])
        hid = hg * _sigmoid(hg) * hu
        o_ref[...] = _dot(hid.astype(BF16), wd_ref[0])

    @pl.when(i >= nb_ref[0])
    def _():
        o_ref[...] = jnp.zeros(o_ref.shape, F32)


def _moe(block_e, n_used, xb, wg, wu, wd):
    cap, d = xb.shape
    ff = wg.shape[2]
    bm = MOE_BLOCK
    return pl.pallas_call(
        _moe_kernel,
        out_shape=jax.ShapeDtypeStruct((cap, d), F32),
        grid_spec=pltpu.PrefetchScalarGridSpec(
            num_scalar_prefetch=2,
            grid=(cap // bm,),
            in_specs=[
                pl.BlockSpec((bm, d), lambda i, be, nb: (i, 0)),
                pl.BlockSpec((1, d, ff), lambda i, be, nb: (be[i], 0, 0)),
                pl.BlockSpec((1, d, ff), lambda i, be, nb: (be[i], 0, 0)),
                pl.BlockSpec((1, ff, d), lambda i, be, nb: (be[i], 0, 0)),
            ],
            out_specs=pl.BlockSpec((bm, d), lambda i, be, nb: (i, 0)),
        ),
        compiler_params=pltpu.CompilerParams(
            dimension_semantics=("arbitrary",), vmem_limit_bytes=VMEM_LIMIT),
        name="moe_experts",
    )(block_e, n_used, xb, wg, wu, wd)


def _combine_kernel(d1_ref, d2_ref, h_ref, ic_ref, yb_hbm, o_ref, ga, gb, sem):
    tm = h_ref.shape[0]

    def copies(r):
        return [pltpu.make_async_copy(yb_hbm.at[pl.ds(d_ref[0, 0, r], 1)], buf.at[pl.ds(r, 1)], sem)
                for d_ref, buf in ((d1_ref, ga), (d2_ref, gb))]

    def start(r, carry):
        for cp in copies(r):
            cp.start()
        return carry

    def wait(r, carry):
        for cp in copies(r):
            cp.wait()
        return carry

    lax.fori_loop(0, tm, start, 0)
    lax.fori_loop(0, tm, wait, 0)
    ic = ic_ref[...]
    o_ref[0] = h_ref[...] + ic[:, 0:1] * ga[...] + ic[:, 1:2] * gb[...]


def _combine(dest1, dest2, h1, ic, yb, b, lp, tm):
    t, d = h1.shape
    per = lp // tm
    first = (FRONT_PAD + N_META) // tm
    nt = t // tm
    smem = lambda: pl.BlockSpec((1, 1, tm), lambda i, j: (i * per + j + first, 0, 0), memory_space=pltpu.SMEM)
    return pl.pallas_call(
        _combine_kernel,
        out_shape=jax.ShapeDtypeStruct((b, lp - FRONT_PAD - N_META, d), F32),
        grid=(b, per - first),
        in_specs=[
            smem(), smem(),
            pl.BlockSpec((tm, d), lambda i, j: (i * per + j + first, 0)),
            pl.BlockSpec((tm, LANES), lambda i, j: (i * per + j + first, 0)),
            pl.BlockSpec(memory_space=pl.ANY),
        ],
        out_specs=pl.BlockSpec((1, tm, d), lambda i, j: (i, j, 0)),
        scratch_shapes=[pltpu.VMEM((tm, d), F32), pltpu.VMEM((tm, d), F32), pltpu.SemaphoreType.DMA(())],
        compiler_params=pltpu.CompilerParams(
            dimension_semantics=("arbitrary", "arbitrary"), vmem_limit_bytes=VMEM_LIMIT),
        name="moe_combine",
    )(dest1.reshape(nt, 1, tm), dest2.reshape(nt, 1, tm), h1, ic, yb)


def _routing_tables(il, cnt, n_blocks):
    bm = MOE_BLOCK
    counts = cnt[:, 0].astype(jnp.int32)
    padded = (counts + bm - 1) // bm * bm
    pad_end = jnp.cumsum(padded)
    pad_start = pad_end - padded
    valid = il[4] > 0
    dest1 = jnp.where(valid, pad_start[il[0]] + il[2], -1).astype(jnp.int32)
    dest2 = jnp.where(valid, pad_start[il[1]] + il[3], -1).astype(jnp.int32)
    block_e = jnp.minimum(
        jnp.searchsorted(pad_end, jnp.arange(n_blocks, dtype=jnp.int32) * bm, side='right'),
        N_EXPERTS - 1).astype(jnp.int32)
    n_used = (pad_end[-1:] // bm).astype(jnp.int32)
    return dest1, dest2, block_e, n_used


def _layer(h0, lp, l, norm1_w, w_in, rwkv_mu, rwkv_w0, rwkv_w2, rwkv_a0, rwkv_a2, rwkv_g2,
           rwkv_k_k, rwkv_k_a, rwkv_r_k, rwkv_ln_w, rwkv_ln_b, q_norm_w, k_norm_w,
           lambda_q1, lambda_k1, lambda_q2, lambda_k2, diff_subln_w, w_branch_rwkv, w_branch_diff,
           w_out, norm2_w, router_group_w, router_group_b, router_expert_w, router_expert_b,
           expert_w_gate, expert_w_up, expert_w_down, proj_tm, tok_tm):
    t, d = h0.shape
    b = t // lp
    lambda_init = 0.8 - 0.6 * math.exp(-0.3 * l)
    rw_cols = 3 * RWKV_WIDTH
    diff_cols = 2 * DIFF_QK_WIDTH + DIFF_V_WIDTH
    w_perm = jnp.concatenate([
        w_in[:, rw_cols + LORA_COLS + diff_cols:],
        w_in[:, rw_cols + LORA_COLS:rw_cols + LORA_COLS + diff_cols],
        w_in[:, :rw_cols + LORA_COLS],
    ], axis=1).astype(BF16)
    proj = _proj(h0, norm1_w, w_perm, proj_tm, 768)
    proj3 = proj.reshape(b, lp, IN_COLS)

    rw = _rwkv(proj3, rwkv_mu[:rw_cols], rwkv_mu[rw_cols:], rwkv_w0, rwkv_w2, rwkv_a0, rwkv_a2, rwkv_g2,
               rwkv_k_k, rwkv_k_a, rwkv_r_k.reshape(-1), rwkv_ln_w, rwkv_ln_b)
    lam4 = jnp.stack([lambda_q1, lambda_k1, lambda_q2, lambda_k2])
    da = _attention(proj3, q_norm_w, k_norm_w, lam4, diff_subln_w, lambda_init)

    wr = jnp.zeros((LANES, d), F32).at[0:N_GROUPS].set(router_group_w.T).at[8:8 + N_EXPERTS].set(router_expert_w.T)
    br = jnp.zeros((LANES,), F32).at[0:N_GROUPS].set(router_group_b).at[N_GROUPS:8].set(NEG_BIG)
    br = br.at[8:8 + N_EXPERTS].set(router_expert_b).reshape(LANES, 1)
    h1, u2, il, ic, cnt = _merge(
        rw.reshape(t, RWKV_WIDTH), da.reshape(t, DIFF_V_WIDTH), proj, h0,
        w_branch_rwkv.astype(BF16), w_branch_diff.astype(BF16), w_out.astype(BF16), norm2_w, wr, br, lp, tok_tm)

    n_real = b * (lp - FRONT_PAD)
    n_blocks = -(-(2 * n_real) // MOE_BLOCK) + N_EXPERTS
    dest1, dest2, block_e, n_used = _routing_tables(il, cnt, n_blocks)
    xb = _dispatch(dest1, dest2, u2, n_blocks * MOE_BLOCK, tok_tm)
    yb = _moe(block_e, n_used, xb, expert_w_gate.astype(BF16), expert_w_up.astype(BF16),
              expert_w_down.astype(BF16))
    return h1, ic, dest1, dest2, yb


def kernel(x, meta_tokens, norm1_w, w_in, rwkv_mu, rwkv_w0, rwkv_w2, rwkv_a0, rwkv_a2, rwkv_g2, rwkv_k_k, rwkv_k_a, rwkv_r_k, rwkv_ln_w, rwkv_ln_b, q_norm_w, k_norm_w, lambda_q1, lambda_k1, lambda_q2, lambda_k2, diff_subln_w, w_branch_rwkv, w_branch_diff, w_out, norm2_w, router_group_w, router_group_b, router_expert_w, router_expert_b, expert_w_gate, expert_w_up, expert_w_down):
    b, seq, d = x.shape
    depth = norm1_w.shape[0]
    assert depth == 1, "the combine step emits the final output; deeper stacks need an intermediate form"
    lp = FRONT_PAD + N_META + seq
    meta = jnp.broadcast_to(meta_tokens[None].astype(x.dtype), (b, N_META, d))
    h0 = jnp.concatenate([jnp.zeros((b, FRONT_PAD, d), x.dtype), meta, x], axis=1).reshape(b * lp, d)
    proj_tm = 1024 if (b * lp) % 1024 == 0 else 128
    tok_tm = 256 if (b * lp) % 256 == 0 else 128
    l = 0
    h1, ic, dest1, dest2, yb = _layer(
        h0, lp, l, norm1_w[l], w_in[l], rwkv_mu[l], rwkv_w0[l], rwkv_w2[l], rwkv_a0[l], rwkv_a2[l],
        rwkv_g2[l], rwkv_k_k[l], rwkv_k_a[l], rwkv_r_k[l], rwkv_ln_w[l], rwkv_ln_b[l], q_norm_w[l],
        k_norm_w[l], lambda_q1[l], lambda_k1[l], lambda_q2[l], lambda_k2[l], diff_subln_w[l],
        w_branch_rwkv[l], w_branch_diff[l], w_out[l], norm2_w[l], router_group_w[l], router_group_b[l],
        router_expert_w[l], router_expert_b[l], expert_w_gate[l], expert_w_up[l], expert_w_down[l],
        proj_tm, tok_tm)
    return _combine(dest1, dest2, h1, ic, yb, b, lp, ATTN_BLOCK)
```

```python
import functools
import math

import jax
import jax.numpy as jnp
from jax import lax
from jax.experimental import pallas as pl
from jax.experimental.pallas import tpu as pltpu

F32 = jnp.float32
BF16 = jnp.bfloat16

D_MODEL = 1024
N_META = 16
NORM_EPS = 1e-6
RWKV_HEADS = 16
RWKV_HEAD_DIM = 64
RWKV_WIDTH = RWKV_HEADS * RWKV_HEAD_DIM
DECAY_LORA = 64
AAA_LORA = 64
GATE_LORA = 128
LORA_COLS = DECAY_LORA + AAA_LORA + GATE_LORA
RWKV_GN_EPS = 64e-5
DIFF_HEADS = 8
DIFF_HEAD_DIM = 64
DIFF_V_DIM = 2 * DIFF_HEAD_DIM
DIFF_QK_WIDTH = DIFF_HEADS * 2 * DIFF_HEAD_DIM
DIFF_V_WIDTH = DIFF_HEADS * DIFF_V_DIM
ROPE_THETA = 500000.0
ROPE_DIM = DIFF_HEAD_DIM // 4
N_GROUPS = 4
EXPERTS_PER_GROUP = 8
N_EXPERTS = N_GROUPS * EXPERTS_PER_GROUP
EXPERT_FF = 512

LANES = 128
FRONT_PAD = LANES - N_META
RWKV_CHUNK = 64
ATTN_BLOCK = 128
ATTN_KEY_BLOCK = 512
ATTN_HEADS_PER_STEP = 2
MOE_BLOCK = 128
NEG_BIG = -1e30
VMEM_LIMIT = 48 * 1024 * 1024

COL_GATE = 0
COL_DIFF = 2 * D_MODEL
COL_RWKV = COL_DIFF + 2 * DIFF_QK_WIDTH + DIFF_V_WIDTH
COL_LORA = COL_RWKV + 3 * RWKV_WIDTH
IN_COLS = COL_LORA + LORA_COLS


def _dot(a, b):
    return jnp.dot(a, b, preferred_element_type=F32)


def _dot_nt(a, b):
    return lax.dot_general(a, b, (((1,), (1,)), ((), ())), preferred_element_type=F32)


def _split2(x):
    hi = x.astype(BF16)
    lo = (x - hi.astype(F32)).astype(BF16)
    return hi, lo


def _mm3(a, b):
    ah, al = _split2(a)
    bh, bl = _split2(b)
    return _dot(ah, bh) + _dot(ah, bl) + _dot(al, bh)


def _sigmoid(x):
    return 1.0 / (1.0 + jnp.exp(-x))


def _head_ones(width=LANES, head=RWKV_HEAD_DIM):
    r = lax.broadcasted_iota(jnp.int32, (width, width), 0) // head
    c = lax.broadcasted_iota(jnp.int32, (width, width), 1) // head
    return jnp.where(r == c, 1.0, 0.0).astype(BF16)


def _seg_sum(x, ones_bd):
    hi, lo = _split2(x)
    return _dot(hi, ones_bd) + _dot(lo, ones_bd)


def _seg_sum_wide(x, ones_bd):
    rows, width = x.shape
    n = width // LANES
    xs = jnp.concatenate([x[:, i * LANES:(i + 1) * LANES] for i in range(n)], axis=0)
    ys = _seg_sum(xs, ones_bd)
    return jnp.concatenate([ys[i * rows:(i + 1) * rows] for i in range(n)], axis=1)


def _proj_kernel(h_ref, nw_ref, w_ref, o_ref, u_ref):
    @pl.when(pl.program_id(1) == 0)
    def _():
        x = h_ref[...]
        ms = jnp.mean(x * x, axis=-1, keepdims=True)
        u_ref[...] = (x * lax.rsqrt(ms + NORM_EPS) * nw_ref[...]).astype(BF16)

    o_ref[...] = _dot(u_ref[...], w_ref[...])


def _proj(h, norm_w, w_bf16, tm, tn):
    t, d = h.shape
    n = w_bf16.shape[1]
    return pl.pallas_call(
        _proj_kernel,
        out_shape=jax.ShapeDtypeStruct((t, n), F32),
        grid=(t // tm, n // tn),
        in_specs=[
            pl.BlockSpec((tm, d), lambda i, j: (i, 0)),
            pl.BlockSpec((1, d), lambda i, j: (0, 0)),
            pl.BlockSpec((d, tn), lambda i, j: (0, j)),
        ],
        out_specs=pl.BlockSpec((tm, tn), lambda i, j: (i, j)),
        scratch_shapes=[pltpu.VMEM((tm, d), BF16)],
        compiler_params=pltpu.CompilerParams(
            dimension_semantics=("parallel", "arbitrary"), vmem_limit_bytes=VMEM_LIMIT),
        name="norm_proj",
    )(h, norm_w.reshape(1, d), w_bf16)


def _rwkv_kernel(r_ref, k_ref, v_ref, lo_ref, mu_ref, mul_ref, w0_ref, w2_ref, a0_ref, a2_ref, g2_ref,
                 kkw_ref, kaw_ref, rkw_ref, lnw_ref, lnb_ref, o_ref,
                 xr, xk, xv, xl, s_ref, kt_s, bt_s, kn_s, rt_s, v_s, y_s, gc_s, bon_s, g_s):
    C = r_ref.shape[1]
    W = RWKV_WIDTH
    n_pairs = W // LANES

    @pl.when(pl.program_id(1) == 0)
    def _init():
        for xs in (xr, xk, xv, xl):
            xs[0:8, :] = jnp.zeros((8, xs.shape[1]), F32)
        s_ref[...] = jnp.zeros(s_ref.shape, F32)

    def shift_mix(in_ref, xs, mu):
        x = in_ref[0]
        xs[8:8 + C, :] = x
        prev = xs[7:7 + C, :]
        xs[7:8, :] = x[C - 1:C, :]
        return x + (prev - x) * mu

    r = shift_mix(r_ref, xr, mu_ref[:, 0:W])
    k = shift_mix(k_ref, xk, mu_ref[:, W:2 * W])
    v = shift_mix(v_ref, xv, mu_ref[:, 2 * W:3 * W])
    lo = shift_mix(lo_ref, xl, mul_ref[...])
    xw = jnp.tanh(lo[:, 0:DECAY_LORA])
    xa = lo[:, DECAY_LORA:DECAY_LORA + AAA_LORA]
    xg = _sigmoid(lo[:, DECAY_LORA + AAA_LORA:LORA_COLS])

    z = -(w0_ref[...] + _mm3(xw, w2_ref[...]))
    softplus = jnp.maximum(z, 0.0) + jnp.log(1.0 + jnp.exp(-jnp.abs(z)))
    lw = -jnp.exp(-softplus - 0.5)
    a = _sigmoid(a0_ref[...] + _mm3(xa, a2_ref[...]))
    g_s[...] = _mm3(xg, g2_ref[...])

    ones_bd = _head_ones()
    kk = k * kkw_ref[...]
    kkn = kk / jnp.maximum(jnp.sqrt(_seg_sum_wide(kk * kk, ones_bd)), 1e-12)
    k2 = k * (1.0 + (a - 1.0) * kaw_ref[...])
    bon_s[...] = _seg_sum_wide(r * k2 * rkw_ref[...], ones_bd) * v

    ti = lax.broadcasted_iota(jnp.int32, (C, C), 0)
    tj = lax.broadcasted_iota(jnp.int32, (C, C), 1)
    ltri = jnp.where(ti >= tj, 1.0, 0.0).astype(BF16)
    l1 = lw.astype(BF16)
    rem = lw - l1.astype(F32)
    l2 = rem.astype(BF16)
    l3 = (rem - l2.astype(F32)).astype(BF16)
    cum = _dot(ltri, l1) + _dot(ltri, l2) + _dot(ltri, l3)
    e_pos = jnp.exp(cum)
    e_neg = jnp.exp(-cum)
    kt_s[...] = kkn * jnp.exp(cum - lw)
    bt_s[...] = kkn * a * e_neg
    kn_s[...] = k2 * e_neg
    rt_s[...] = r * e_pos
    v_s[...] = v
    gc_s[...] = jnp.exp(cum[C - 1:C, :])

    lane = lax.broadcasted_iota(jnp.int32, (1, LANES), 1)
    m0 = jnp.where(lane < RWKV_HEAD_DIM, 1.0, 0.0)
    m1 = 1.0 - m0
    trow = lax.broadcasted_iota(jnp.int32, (C, 2 * C), 0)
    tcol = lax.broadcasted_iota(jnp.int32, (C, 2 * C), 1) % C
    strict = trow > tcol
    incl = trow >= tcol
    eye2 = jnp.where(trow == tcol, 1.0, 0.0)
    br = lax.broadcasted_iota(jnp.int32, (LANES, LANES), 0) // RWKV_HEAD_DIM
    bc = lax.broadcasted_iota(jnp.int32, (LANES, LANES), 1) // RWKV_HEAD_DIM
    bdmask = jnp.where(br == bc, 1.0, 0.0)

    def stack2(y):
        return jnp.concatenate([y * m0, y * m1], axis=0).astype(BF16)

    pairs = range(n_pairs)
    sl = [slice(p * LANES, (p + 1) * LANES) for p in pairs]
    kt = [kt_s[:, sl[p]] for p in pairs]
    bt = [bt_s[:, sl[p]] for p in pairs]
    kn = [kn_s[:, sl[p]] for p in pairs]
    rt = [rt_s[:, sl[p]] for p in pairs]
    vv = [v_s[:, sl[p]] for p in pairs]
    gc = [gc_s[:, sl[p]] for p in pairs]
    m_all = [_dot_nt(jnp.concatenate([kt[p], rt[p]], axis=0).astype(BF16),
                     jnp.concatenate([bt[p] * m0, bt[p] * m1, kn[p] * m0, kn[p] * m1], axis=0).astype(BF16))
             for p in pairs]
    b_mat = [jnp.where(strict, m_all[p][0:C, 2 * C:4 * C], 0.0).astype(BF16) for p in pairs]
    pq_mat = [jnp.concatenate([jnp.where(incl, m_all[p][C:2 * C, 0:2 * C], 0.0),
                               jnp.where(incl, m_all[p][C:2 * C, 2 * C:4 * C], 0.0)], axis=1).astype(BF16)
              for p in pairs]

    pw = [-jnp.where(strict, m_all[p][0:C, 0:2 * C], 0.0) for p in pairs]
    t_inv = [eye2 + pw[p] for p in pairs]
    for _ in range(int(math.log2(C)) - 1):
        pw = [_dot(pw[p].astype(BF16), stack2(pw[p])) for p in pairs]
        t_inv = [t_inv[p] + _dot(t_inv[p].astype(BF16), stack2(pw[p])) for p in pairs]

    s_old = [s_ref[p] for p in pairs]
    s_bf = [s_old[p].astype(BF16) for p in pairs]
    v_st = [stack2(vv[p]) for p in pairs]
    rhs = [_dot_nt(kt[p].astype(BF16), s_bf[p]) + _dot(b_mat[p], v_st[p]) for p in pairs]
    u = [-_dot(t_inv[p].astype(BF16), stack2(rhs[p])) for p in pairs]
    for p in pairs:
        y_s[:, sl[p]] = _dot_nt(rt[p].astype(BF16), s_bf[p]) + _dot(
            pq_mat[p], jnp.concatenate([stack2(u[p]), v_st[p]], axis=0))
    for p in pairs:
        uc = jnp.concatenate([u[p], vv[p]], axis=0)
        xc = jnp.concatenate([bt[p] * gc[p], kn[p] * gc[p]], axis=0)
        s_ref[p] = s_old[p] * gc[p] + bdmask * _dot(jnp.transpose(uc).astype(BF16), xc.astype(BF16))

    y = y_s[...]
    inv_n = 1.0 / RWKV_HEAD_DIM
    mean = _seg_sum_wide(y, ones_bd) * inv_n
    dlt = y - mean
    var = _seg_sum_wide(dlt * dlt, ones_bd) * inv_n
    yn = dlt * lax.rsqrt(var + RWKV_GN_EPS) * lnw_ref[...] + lnb_ref[...]
    o_ref[0] = (yn + bon_s[...]) * g_s[...]


def _rwkv(proj3, mu_rkv, mu_lo, w0, w2, a0, a2, g2, k_k, k_a, r_k, ln_w, ln_b):
    b, lp, _ = proj3.shape
    C = RWKV_CHUNK
    W = RWKV_WIDTH
    cb = COL_RWKV // W
    lb = COL_LORA // LORA_COLS
    row = lambda x: x.reshape(1, -1)
    full = lambda shape: pl.BlockSpec(shape, lambda i, c: (0,) * len(shape))
    wide = pltpu.VMEM((C, W), F32)
    return pl.pallas_call(
        _rwkv_kernel,
        out_shape=jax.ShapeDtypeStruct((b, lp, W), F32),
        grid=(b, lp // C),
        in_specs=[
            pl.BlockSpec((1, C, W), lambda i, c: (i, c, cb)),
            pl.BlockSpec((1, C, W), lambda i, c: (i, c, cb + 1)),
            pl.BlockSpec((1, C, W), lambda i, c: (i, c, cb + 2)),
            pl.BlockSpec((1, C, LORA_COLS), lambda i, c: (i, c, lb)),
            full((1, 3 * W)), full((1, LORA_COLS)),
            full((1, W)), full((DECAY_LORA, W)), full((1, W)), full((AAA_LORA, W)), full((GATE_LORA, W)),
            full((1, W)), full((1, W)), full((1, W)), full((1, W)), full((1, W)),
        ],
        out_specs=pl.BlockSpec((1, C, W), lambda i, c: (i, c, 0)),
        scratch_shapes=[
            pltpu.VMEM((C + 8, W), F32), pltpu.VMEM((C + 8, W), F32), pltpu.VMEM((C + 8, W), F32),
            pltpu.VMEM((C + 8, LORA_COLS), F32),
            pltpu.VMEM((W // LANES, LANES, LANES), F32),
            wide, wide, wide, wide, wide, wide, pltpu.VMEM((1, W), F32), wide, wide,
        ],
        compiler_params=pltpu.CompilerParams(
            dimension_semantics=("parallel", "arbitrary"), vmem_limit_bytes=VMEM_LIMIT),
        name="rwkv7_time_mix",
    )(proj3, proj3, proj3, proj3, row(mu_rkv), row(mu_lo), row(w0), w2, row(a0), a2, g2,
      row(k_k), row(k_a), row(r_k), row(ln_w), row(ln_b))


def _attn_kernel(q_ref, k_ref, v_ref, cos_ref, s1_ref, s2_ref, qw_ref, kw_ref, lam_ref, sw_ref, o_ref,
                 kp_s, vp_s, m_s, l_s, acc_s, *, tq, tk, lambda_init):
    qi = pl.program_id(2)
    lp = k_ref.shape[1]
    lk = kp_s.shape[0]
    nh = k_ref.shape[2] // LANES
    heads = [slice(h * LANES, (h + 1) * LANES) for h in range(nh)]
    ones_bd = _head_ones(LANES, DIFF_HEAD_DIM)
    shift = ROPE_DIM // 2

    def norm_rope(x, w, rows):
        ms = _seg_sum(x * x, ones_bd) * (1.0 / DIFF_HEAD_DIM)
        xn = x * lax.rsqrt(ms + NORM_EPS) * w
        return (xn * cos_ref[rows, :] + pltpu.roll(xn, shift, 1) * s1_ref[rows, :]
                + pltpu.roll(xn, LANES - shift, 1) * s2_ref[rows, :])

    @pl.when(qi == 0)
    def _prep():
        def body(i, carry):
            rows = pl.ds(pl.multiple_of(i * tq, tq), tq)
            for hs in heads:
                kp_s[rows, hs] = norm_rope(k_ref[0, rows, hs], kw_ref[...], rows).astype(BF16)
            vp_s[rows, :] = v_ref[0, rows, :].astype(BF16)
            return carry
        lax.fori_loop(0, lp // tq, body, 0)
        if lk > lp:
            kp_s[lp:lk, :] = jnp.zeros((lk - lp, nh * LANES), BF16)
            vp_s[lp:lk, :] = jnp.zeros((lk - lp, nh * LANES), BF16)

    lane = lax.broadcasted_iota(jnp.int32, (1, LANES), 1)
    m0 = jnp.where(lane < DIFF_HEAD_DIM, 1.0, 0.0)
    m1 = 1.0 - m0
    rows_q = pl.ds(pl.multiple_of(qi * tq, tq), tq)
    qs = []
    for hs in heads:
        qn = norm_rope(q_ref[0, :, hs], qw_ref[...], rows_q) * (DIFF_HEAD_DIM ** -0.5)
        qs.append(jnp.concatenate([qn * m0, qn * m1], axis=0).astype(BF16))
    m_s[...] = jnp.full(m_s.shape, NEG_BIG, F32)
    l_s[...] = jnp.zeros(l_s.shape, F32)
    acc_s[...] = jnp.zeros(acc_s.shape, F32)
    row = qi * tq + lax.broadcasted_iota(jnp.int32, (2 * tq, tk), 0) % tq
    col0 = lax.broadcasted_iota(jnp.int32, (2 * tq, tk), 1)

    def step(j, causal, pad):
        start = j * tk
        if not isinstance(j, int):
            start = pl.multiple_of(start, tk)
        ks = pl.ds(start, tk)
        mask = None
        if causal:
            col = start + col0
            mask = col <= row
            if pad:
                mask = mask & ((col >= FRONT_PAD) | (row < FRONT_PAD))
        for h, hs in enumerate(heads):
            s = _dot_nt(qs[h], kp_s[ks, hs])
            if mask is not None:
                s = jnp.where(mask, s, NEG_BIG)
            m_old = m_s[h]
            m_new = jnp.maximum(m_old, jnp.max(s, axis=1, keepdims=True))
            alpha = jnp.exp(m_old - m_new)
            p = jnp.exp(s - m_new)
            l_s[h] = alpha * l_s[h] + jnp.sum(p, axis=1, keepdims=True)
            acc_s[h] = alpha * acc_s[h] + _dot(p.astype(BF16), vp_s[ks, hs])
            m_s[h] = m_new

    last = (qi * tq + (tq - 1)) // tk
    step(0, True, True)

    def mid(j, carry):
        step(j, False, False)
        return carry
    lax.fori_loop(1, last, mid, 0)

    @pl.when(last > 0)
    def _diag():
        step(last, True, False)

    lam = (jnp.exp(jnp.sum(lam_ref[0:1, :] * lam_ref[1:2, :], axis=1, keepdims=True))
           - jnp.exp(jnp.sum(lam_ref[2:3, :] * lam_ref[3:4, :], axis=1, keepdims=True)) + lambda_init)
    for h, hs in enumerate(heads):
        o = acc_s[h] / l_s[h]
        od = o[0:tq] - lam * o[tq:2 * tq]
        ms = jnp.mean(od * od, axis=1, keepdims=True)
        o_ref[0, :, hs] = od * lax.rsqrt(ms + NORM_EPS) * sw_ref[...] * (1.0 - lambda_init)


def _rope_tables(lp):
    half = ROPE_DIM // 2
    inv_freq = jnp.exp(-math.log(ROPE_THETA) * jnp.arange(half, dtype=F32) * 2.0 / ROPE_DIM)
    pos = (jnp.arange(lp) - FRONT_PAD).astype(F32)
    ang = pos[:, None] * inv_freq[None, :]
    cos, sin = jnp.cos(ang), jnp.sin(ang)
    one = jnp.ones((lp, DIFF_HEAD_DIM - ROPE_DIM), F32)
    zero = jnp.zeros((lp, DIFF_HEAD_DIM - ROPE_DIM), F32)
    zh = jnp.zeros((lp, half), F32)
    c = jnp.concatenate([cos, cos, one], axis=1)
    s1 = jnp.concatenate([zh, sin, zero], axis=1)
    s2 = jnp.concatenate([-sin, zh, zero], axis=1)
    dup = lambda t: jnp.concatenate([t, t], axis=1)
    return dup(c), dup(s1), dup(s2)


def _attention(proj3, q_norm_w, k_norm_w, lam4, subln_w, lambda_init):
    b, lp, _ = proj3.shape
    tq, tk, nh = ATTN_BLOCK, ATTN_KEY_BLOCK, ATTN_HEADS_PER_STEP
    lk = -(-lp // tk) * tk
    hw = nh * LANES
    qb = COL_DIFF // hw
    kb = qb + DIFF_QK_WIDTH // hw
    vb = kb + DIFF_QK_WIDTH // hw
    cos, s1, s2 = _rope_tables(lp)
    dup = lambda w: jnp.concatenate([w, w]).reshape(1, LANES)
    full = lambda shape: pl.BlockSpec(shape, lambda i, h, q: (0,) * len(shape))
    return pl.pallas_call(
        functools.partial(_attn_kernel, tq=tq, tk=tk, lambda_init=lambda_init),
        out_shape=jax.ShapeDtypeStruct((b, lp, DIFF_V_WIDTH), F32),
        grid=(b, DIFF_HEADS // nh, lp // tq),
        in_specs=[
            pl.BlockSpec((1, tq, hw), lambda i, h, q: (i, q, qb + h)),
            pl.BlockSpec((1, lp, hw), lambda i, h, q: (i, 0, kb + h)),
            pl.BlockSpec((1, lp, hw), lambda i, h, q: (i, 0, vb + h)),
            full((lp, LANES)), full((lp, LANES)), full((lp, LANES)),
            full((1, LANES)), full((1, LANES)), full((4, DIFF_HEAD_DIM)), full((1, LANES)),
        ],
        out_specs=pl.BlockSpec((1, tq, hw), lambda i, h, q: (i, q, h)),
        scratch_shapes=[pltpu.VMEM((lk, hw), BF16), pltpu.VMEM((lk, hw), BF16),
                        pltpu.VMEM((nh, 2 * tq, 1), F32), pltpu.VMEM((nh, 2 * tq, 1), F32),
                        pltpu.VMEM((nh, 2 * tq, LANES), F32)],
        compiler_params=pltpu.CompilerParams(
            dimension_semantics=("parallel", "parallel", "arbitrary"), vmem_limit_bytes=VMEM_LIMIT),
        name="diff_attention",
    )(proj3, proj3, proj3, cos, s1, s2, dup(q_norm_w), dup(k_norm_w), lam4, subln_w.reshape(1, LANES))


def _merge_kernel(rw_ref, da_ref, g1_ref, g2_ref, h_ref, wbr_ref, wbd_ref, wo_ref, n2_ref, wr_ref, br_ref,
                  h1_ref, u2_ref, il_ref, ic_ref, cnt_ref, base_s, *, lp):
    i = pl.program_id(0)
    tm = rw_ref.shape[0]

    @pl.when(i == 0)
    def _():
        base_s[...] = jnp.zeros(base_s.shape, F32)

    y1 = _dot(rw_ref[...].astype(BF16), wbr_ref[...])
    y2 = _dot(da_ref[...].astype(BF16), wbd_ref[...])
    merged = _sigmoid(g1_ref[...]) * y1 + _sigmoid(g2_ref[...]) * y2
    h1 = h_ref[...] + _dot(merged.astype(BF16), wo_ref[...])
    h1_ref[...] = h1
    u2 = h1 * lax.rsqrt(jnp.mean(h1 * h1, axis=-1, keepdims=True) + NORM_EPS) * n2_ref[...]
    u2_ref[...] = u2

    uh, ul = _split2(u2)
    wh, wl = _split2(wr_ref[...])
    lt = _dot_nt(wh, uh) + _dot_nt(wh, ul) + _dot_nt(wl, uh) + br_ref[...]

    gi8 = lax.broadcasted_iota(jnp.int32, (8, tm), 0)
    lg = lt[0:8]
    ge = jnp.exp(lg - jnp.max(lg, axis=0, keepdims=True))
    gp = ge / jnp.sum(ge, axis=0, keepdims=True)
    gv = jnp.max(gp, axis=0, keepdims=True)
    gidx = jnp.min(jnp.where(gp == gv, gi8, N_EXPERTS), axis=0, keepdims=True)

    ei = lax.broadcasted_iota(jnp.int32, (N_EXPERTS, tm), 0)
    sel = (ei // EXPERTS_PER_GROUP) == gidx
    le = jnp.where(sel, lt[8:8 + N_EXPERTS], NEG_BIG)
    ee = jnp.where(sel, jnp.exp(le - jnp.max(le, axis=0, keepdims=True)), 0.0)
    ep = jnp.where(sel, ee / jnp.sum(ee, axis=0, keepdims=True), -1.0)
    v1 = jnp.max(ep, axis=0, keepdims=True)
    i1 = jnp.min(jnp.where(ep == v1, ei, N_EXPERTS), axis=0, keepdims=True)
    ep2 = jnp.where(ei == i1, -1.0, ep)
    v2 = jnp.max(ep2, axis=0, keepdims=True)
    i2 = jnp.min(jnp.where(ep2 == v2, ei, N_EXPERTS), axis=0, keepdims=True)
    den = v1 + v2
    gate1 = gv * v1 / den
    gate2 = gv * v2 / den

    tok = (i * tm + lax.broadcasted_iota(jnp.int32, (1, tm), 1)).astype(F32)
    pos = tok - jnp.floor((tok + 0.5) / lp) * lp
    valid = pos > (FRONT_PAD - 0.5)

    oh1 = jnp.where((ei == i1) & valid, 1.0, 0.0)
    oh2 = jnp.where((ei == i2) & valid, 1.0, 0.0)
    oh = oh1 + oh2
    ur = lax.broadcasted_iota(jnp.int32, (tm, tm), 0)
    uc = lax.broadcasted_iota(jnp.int32, (tm, tm), 1)
    before = jnp.where(ur < uc, 1.0, 0.0).astype(BF16)
    tot = base_s[:, 0:1] + _dot(oh.astype(BF16), before)
    rank1 = jnp.sum(oh1 * tot, axis=0, keepdims=True)
    rank2 = jnp.sum(oh2 * tot, axis=0, keepdims=True)
    base_s[...] = base_s[...] + jnp.sum(oh, axis=1, keepdims=True)
    cnt_ref[...] = base_s[...]

    il = jnp.where(gi8 == 0, i1, jnp.where(gi8 == 1, i2, jnp.where(
        gi8 == 2, rank1.astype(jnp.int32), jnp.where(gi8 == 3, rank2.astype(jnp.int32), jnp.where(
            gi8 == 4, valid.astype(jnp.int32), 0)))))
    il_ref[...] = il
    ri = lax.broadcasted_iota(jnp.int32, (LANES, tm), 0)
    ic = jnp.where(ri == 0, gate1, jnp.where(ri == 1, gate2, 0.0))
    ic_ref[...] = jnp.transpose(ic)


def _merge(rw, da, proj, h0, wbr, wbd, wo, norm2_w, wr, br, lp, tm):
    t, d = h0.shape
    gb = COL_GATE // d
    full = lambda shape: pl.BlockSpec(shape, lambda i: (0,) * len(shape))
    tile = lambda c: pl.BlockSpec((tm, d), lambda i: (i, c))
    return pl.pallas_call(
        functools.partial(_merge_kernel, lp=lp),
        out_shape=(
            jax.ShapeDtypeStruct((t, d), F32),
            jax.ShapeDtypeStruct((t, d), F32),
            jax.ShapeDtypeStruct((8, t), jnp.int32),
            jax.ShapeDtypeStruct((t, LANES), F32),
            jax.ShapeDtypeStruct((N_EXPERTS, LANES), F32),
        ),
        grid=(t // tm,),
        in_specs=[tile(0), tile(0), tile(gb), tile(gb + 1), tile(0),
                  full((d, d)), full((d, d)), full((d, d)), full((1, d)), full((LANES, d)), full((LANES, 1))],
        out_specs=(
            tile(0), tile(0),
            pl.BlockSpec((8, tm), lambda i: (0, i)),
            pl.BlockSpec((tm, LANES), lambda i: (i, 0)),
            full((N_EXPERTS, LANES)),
        ),
        scratch_shapes=[pltpu.VMEM((N_EXPERTS, LANES), F32)],
        compiler_params=pltpu.CompilerParams(
            dimension_semantics=("arbitrary",), vmem_limit_bytes=VMEM_LIMIT),
        name="merge_router",
    )(rw, da, proj, proj, h0, wbr, wbd, wo, norm2_w.reshape(1, d), wr, br)


def _dispatch_kernel(d1_ref, d2_ref, u_ref, xin_hbm, xb_hbm, sem):
    del xin_hbm
    tm = d1_ref.shape[2]

    def copies(r):
        return [(d_ref[0, 0, r], pltpu.make_async_copy(
            u_ref.at[pl.ds(r, 1)], xb_hbm.at[pl.ds(jnp.maximum(d_ref[0, 0, r], 0), 1)], sem))
            for d_ref in (d1_ref, d2_ref)]

    def start(r, carry):
        for d, cp in copies(r):
            @pl.when(d >= 0)
            def _():
                cp.start()
        return carry

    def wait(r, carry):
        for d, cp in copies(r):
            @pl.when(d >= 0)
            def _():
                cp.wait()
        return carry

    lax.fori_loop(0, tm, start, 0)
    lax.fori_loop(0, tm, wait, 0)


def _dispatch(dest1, dest2, u2, cap, tm):
    t, d = u2.shape
    nt = t // tm
    smem = lambda: pl.BlockSpec((1, 1, tm), lambda i: (i, 0, 0), memory_space=pltpu.SMEM)
    return pl.pallas_call(
        _dispatch_kernel,
        out_shape=jax.ShapeDtypeStruct((cap, d), F32),
        grid=(nt,),
        in_specs=[smem(), smem(), pl.BlockSpec((tm, d), lambda i: (i, 0)), pl.BlockSpec(memory_space=pl.ANY)],
        out_specs=pl.BlockSpec(memory_space=pl.ANY),
        scratch_shapes=[pltpu.SemaphoreType.DMA(())],
        input_output_aliases={3: 0},
        compiler_params=pltpu.CompilerParams(
            dimension_semantics=("arbitrary",), vmem_limit_bytes=VMEM_LIMIT),
        name="moe_dispatch",
    )(dest1.reshape(nt, 1, tm), dest2.reshape(nt, 1, tm), u2, jnp.zeros((cap, d), F32))


def _moe_kernel(be_ref, nb_ref, x_ref, wg_ref, wu_ref, wd_ref, o_ref):
    del be_ref
    i = pl.program_id(0)

    @pl.when(i < nb_ref[0])
    def _():
        x = x_ref[...].astype(BF16)
        hg = _dot(x, wg_ref[0])
        hu = _dot(x, wu_ref[0])
        hid = hg * _sigmoid(hg) * hu
        o_ref[...] = _dot(hid.astype(BF16), wd_ref[0])

    @pl.when(i >= nb_ref[0])
    def _():
        o_ref[...] = jnp.zeros(o_ref.shape, F32)


def _moe(block_e, n_used, xb, wg, wu, wd):
    cap, d = xb.shape
    ff = wg.shape[2]
    bm = MOE_BLOCK
    return pl.pallas_call(
        _moe_kernel,
        out_shape=jax.ShapeDtypeStruct((cap, d), F32),
        grid_spec=pltpu.PrefetchScalarGridSpec(
            num_scalar_prefetch=2,
            grid=(cap // bm,),
            in_specs=[
                pl.BlockSpec((bm, d), lambda i, be, nb: (i, 0)),
                pl.BlockSpec((1, d, ff), lambda i, be, nb: (be[i], 0, 0)),
                pl.BlockSpec((1, d, ff), lambda i, be, nb: (be[i], 0, 0)),
                pl.BlockSpec((1, ff, d), lambda i, be, nb: (be[i], 0, 0)),
            ],
            out_specs=pl.BlockSpec((bm, d), lambda i, be, nb: (i, 0)),
        ),
        compiler_params=pltpu.CompilerParams(
            dimension_semantics=("arbitrary",), vmem_limit_bytes=VMEM_LIMIT),
        name="moe_experts",
    )(block_e, n_used, xb, wg, wu, wd)


def _combine_kernel(d1_ref, d2_ref, h_ref, ic_ref, yb_hbm, o_ref, ga, gb, sem):
    tm = h_ref.shape[0]

    def copies(r):
        return [pltpu.make_async_copy(yb_hbm.at[pl.ds(d_ref[0, 0, r], 1)], buf.at[pl.ds(r, 1)], sem)
                for d_ref, buf in ((d1_ref, ga), (d2_ref, gb))]

    def start(r, carry):
        for cp in copies(r):
            cp.start()
        return carry

    def wait(r, carry):
        for cp in copies(r):
            cp.wait()
        return carry

    lax.fori_loop(0, tm, start, 0)
    lax.fori_loop(0, tm, wait, 0)
    ic = ic_ref[...]
    o_ref[0] = h_ref[...] + ic[:, 0:1] * ga[...] + ic[:, 1:2] * gb[...]


def _combine(dest1, dest2, h1, ic, yb, b, lp, tm):
    t, d = h1.shape
    per = lp // tm
    first = (FRONT_PAD + N_META) // tm
    nt = t // tm
    smem = lambda: pl.BlockSpec((1, 1, tm), lambda i, j: (i * per + j + first, 0, 0), memory_space=pltpu.SMEM)
    return pl.pallas_call(
        _combine_kernel,
        out_shape=jax.ShapeDtypeStruct((b, lp - FRONT_PAD - N_META, d), F32),
        grid=(b, per - first),
        in_specs=[
            smem(), smem(),
            pl.BlockSpec((tm, d), lambda i, j: (i * per + j + first, 0)),
            pl.BlockSpec((tm, LANES), lambda i, j: (i * per + j + first, 0)),
            pl.BlockSpec(memory_space=pl.ANY),
        ],
        out_specs=pl.BlockSpec((1, tm, d), lambda i, j: (i, j, 0)),
        scratch_shapes=[pltpu.VMEM((tm, d), F32), pltpu.VMEM((tm, d), F32), pltpu.SemaphoreType.DMA(())],
        compiler_params=pltpu.CompilerParams(
            dimension_semantics=("arbitrary", "arbitrary"), vmem_limit_bytes=VMEM_LIMIT),
        name="moe_combine",
    )(dest1.reshape(nt, 1, tm), dest2.reshape(nt, 1, tm), h1, ic, yb)


def _routing_tables(il, cnt, n_blocks):
    bm = MOE_BLOCK
    counts = cnt[:, 0].astype(jnp.int32)
    padded = (counts + bm - 1) // bm * bm
    pad_end = jnp.cumsum(padded)
    pad_start = pad_end - padded
    valid = il[4] > 0
    dest1 = jnp.where(valid, pad_start[il[0]] + il[2], -1).astype(jnp.int32)
    dest2 = jnp.where(valid, pad_start[il[1]] + il[3], -1).astype(jnp.int32)
    starts = jnp.arange(n_blocks, dtype=jnp.int32) * bm
    block_e = jnp.minimum(jnp.sum((pad_end[None, :] <= starts[:, None]).astype(jnp.int32), axis=1),
                          N_EXPERTS - 1)
    n_used = (pad_end[-1:] // bm).astype(jnp.int32)
    return dest1, dest2, block_e, n_used


def _layer(h0, lp, l, norm1_w, w_in, rwkv_mu, rwkv_w0, rwkv_w2, rwkv_a0, rwkv_a2, rwkv_g2,
           rwkv_k_k, rwkv_k_a, rwkv_r_k, rwkv_ln_w, rwkv_ln_b, q_norm_w, k_norm_w,
           lambda_q1, lambda_k1, lambda_q2, lambda_k2, diff_subln_w, w_branch_rwkv, w_branch_diff,
           w_out, norm2_w, router_group_w, router_group_b, router_expert_w, router_expert_b,
           expert_w_gate, expert_w_up, expert_w_down, proj_tm, tok_tm):
    t, d = h0.shape
    b = t // lp
    lambda_init = 0.8 - 0.6 * math.exp(-0.3 * l)
    rw_cols = 3 * RWKV_WIDTH
    diff_cols = 2 * DIFF_QK_WIDTH + DIFF_V_WIDTH
    w_perm = jnp.concatenate([
        w_in[:, rw_cols + LORA_COLS + diff_cols:],
        w_in[:, rw_cols + LORA_COLS:rw_cols + LORA_COLS + diff_cols],
        w_in[:, :rw_cols + LORA_COLS],
    ], axis=1).astype(BF16)
    proj = _proj(h0, norm1_w, w_perm, proj_tm, 768)
    proj3 = proj.reshape(b, lp, IN_COLS)

    rw = _rwkv(proj3, rwkv_mu[:rw_cols], rwkv_mu[rw_cols:], rwkv_w0, rwkv_w2, rwkv_a0, rwkv_a2, rwkv_g2,
               rwkv_k_k, rwkv_k_a, rwkv_r_k.reshape(-1), rwkv_ln_w, rwkv_ln_b)
    lam4 = jnp.stack([lambda_q1, lambda_k1, lambda_q2, lambda_k2])
    da = _attention(proj3, q_norm_w, k_norm_w, lam4, diff_subln_w, lambda_init)

    wr = jnp.zeros((LANES, d), F32).at[0:N_GROUPS].set(router_group_w.T).at[8:8 + N_EXPERTS].set(router_expert_w.T)
    br = jnp.zeros((LANES,), F32).at[0:N_GROUPS].set(router_group_b).at[N_GROUPS:8].set(NEG_BIG)
    br = br.at[8:8 + N_EXPERTS].set(router_expert_b).reshape(LANES, 1)
    h1, u2, il, ic, cnt = _merge(
        rw.reshape(t, RWKV_WIDTH), da.reshape(t, DIFF_V_WIDTH), proj, h0,
        w_branch_rwkv.astype(BF16), w_branch_diff.astype(BF16), w_out.astype(BF16), norm2_w, wr, br, lp, tok_tm)

    n_real = b * (lp - FRONT_PAD)
    n_blocks = -(-(2 * n_real) // MOE_BLOCK) + N_EXPERTS
    dest1, dest2, block_e, n_used = _routing_tables(il, cnt, n_blocks)
    xb = _dispatch(dest1, dest2, u2, n_blocks * MOE_BLOCK, tok_tm)
    yb = _moe(block_e, n_used, xb, expert_w_gate.astype(BF16), expert_w_up.astype(BF16),
              expert_w_down.astype(BF16))
    return h1, ic, dest1, dest2, yb


def kernel(x, meta_tokens, norm1_w, w_in, rwkv_mu, rwkv_w0, rwkv_w2, rwkv_a0, rwkv_a2, rwkv_g2, rwkv_k_k, rwkv_k_a, rwkv_r_k, rwkv_ln_w, rwkv_ln_b, q_norm_w, k_norm_w, lambda_q1, lambda_k1, lambda_q2, lambda_k2, diff_subln_w, w_branch_rwkv, w_branch_diff, w_out, norm2_w, router_group_w, router_group_b, router_expert_w, router_expert_b, expert_w_gate, expert_w_up, expert_w_down):
    b, seq, d = x.shape
    depth = norm1_w.shape[0]
    assert depth == 1, "the combine step emits the final output; deeper stacks need an intermediate form"
    lp = FRONT_PAD + N_META + seq
    meta = jnp.broadcast_to(meta_tokens[None].astype(x.dtype), (b, N_META, d))
    h0 = jnp.concatenate([jnp.zeros((b, FRONT_PAD, d), x.dtype), meta, x], axis=1).reshape(b * lp, d)
    proj_tm = 1024 if (b * lp) % 1024 == 0 else 128
    tok_tm = 256 if (b * lp) % 256 == 0 else 128
    l = 0
    h1, ic, dest1, dest2, yb = _layer(
        h0, lp, l, norm1_w[l], w_in[l], rwkv_mu[l], rwkv_w0[l], rwkv_w2[l], rwkv_a0[l], rwkv_a2[l],
        rwkv_g2[l], rwkv_k_k[l], rwkv_k_a[l], rwkv_r_k[l], rwkv_ln_w[l], rwkv_ln_b[l], q_norm_w[l],
        k_norm_w[l], lambda_q1[l], lambda_k1[l], lambda_q2[l], lambda_k2[l], diff_subln_w[l],
        w_branch_rwkv[l], w_branch_diff[l], w_out[l], norm2_w[l], router_group_w[l], router_group_b[l],
        router_expert_w[l], router_expert_b[l], expert_w_gate[l], expert_w_up[l], expert_w_down[l],
        proj_tm, tok_tm)
    return _combine(dest1, dest2, h1, ic, yb, b, lp, ATTN_BLOCK)
```

```python
import functools
import math

import jax
import jax.numpy as jnp
from jax import lax
from jax.experimental import pallas as pl
from jax.experimental.pallas import tpu as pltpu

F32 = jnp.float32
BF16 = jnp.bfloat16

D_MODEL = 1024
N_META = 16
NORM_EPS = 1e-6
RWKV_HEADS = 16
RWKV_HEAD_DIM = 64
RWKV_WIDTH = RWKV_HEADS * RWKV_HEAD_DIM
DECAY_LORA = 64
AAA_LORA = 64
GATE_LORA = 128
LORA_COLS = DECAY_LORA + AAA_LORA + GATE_LORA
RWKV_GN_EPS = 64e-5
DIFF_HEADS = 8
DIFF_HEAD_DIM = 64
DIFF_V_DIM = 2 * DIFF_HEAD_DIM
DIFF_QK_WIDTH = DIFF_HEADS * 2 * DIFF_HEAD_DIM
DIFF_V_WIDTH = DIFF_HEADS * DIFF_V_DIM
ROPE_THETA = 500000.0
ROPE_DIM = DIFF_HEAD_DIM // 4
N_GROUPS = 4
EXPERTS_PER_GROUP = 8
N_EXPERTS = N_GROUPS * EXPERTS_PER_GROUP
EXPERT_FF = 512

LANES = 128
ATTN_BLOCK = 256
ATTN_KEY_BLOCK = 512
ATTN_HEADS_PER_STEP = 2
FRONT_PAD = ATTN_BLOCK - N_META
RWKV_CHUNK = 64
OUT_TILE = 128
MOE_BLOCK = 256
NEG_BIG = -1e30
VMEM_LIMIT = 48 * 1024 * 1024

COL_GATE = 0
COL_DIFF = 2 * D_MODEL
COL_RWKV = COL_DIFF + 2 * DIFF_QK_WIDTH + DIFF_V_WIDTH
COL_LORA = COL_RWKV + 3 * RWKV_WIDTH
IN_COLS = COL_LORA + LORA_COLS


def _dot(a, b):
    return jnp.dot(a, b, preferred_element_type=F32)


def _dot_nt(a, b):
    return lax.dot_general(a, b, (((1,), (1,)), ((), ())), preferred_element_type=F32)


def _split2(x):
    hi = x.astype(BF16)
    lo = (x - hi.astype(F32)).astype(BF16)
    return hi, lo


def _mm3(a, b):
    ah, al = _split2(a)
    bh, bl = _split2(b)
    return _dot(ah, bh) + _dot(ah, bl) + _dot(al, bh)


def _sigmoid(x):
    return 1.0 / (1.0 + jnp.exp(-x))


def _head_ones(width=LANES, head=RWKV_HEAD_DIM):
    r = lax.broadcasted_iota(jnp.int32, (width, width), 0) // head
    c = lax.broadcasted_iota(jnp.int32, (width, width), 1) // head
    return jnp.where(r == c, 1.0, 0.0).astype(BF16)


def _seg_sum(x, ones_bd):
    hi, lo = _split2(x)
    return _dot(hi, ones_bd) + _dot(lo, ones_bd)


def _seg_sum_wide(x, ones_bd):
    rows, width = x.shape
    n = width // LANES
    xs = jnp.concatenate([x[:, i * LANES:(i + 1) * LANES] for i in range(n)], axis=0)
    ys = _seg_sum(xs, ones_bd)
    return jnp.concatenate([ys[i * rows:(i + 1) * rows] for i in range(n)], axis=1)


def _proj_kernel(h_ref, nw_ref, w_ref, o_ref, u_ref):
    @pl.when(pl.program_id(1) == 0)
    def _():
        x = h_ref[...]
        ms = jnp.mean(x * x, axis=-1, keepdims=True)
        u_ref[...] = (x * lax.rsqrt(ms + NORM_EPS) * nw_ref[...]).astype(BF16)

    o_ref[...] = _dot(u_ref[...], w_ref[...])


def _proj(h, norm_w, w_bf16, tm, tn):
    t, d = h.shape
    n = w_bf16.shape[1]
    return pl.pallas_call(
        _proj_kernel,
        out_shape=jax.ShapeDtypeStruct((t, n), F32),
        grid=(t // tm, n // tn),
        in_specs=[
            pl.BlockSpec((tm, d), lambda i, j: (i, 0)),
            pl.BlockSpec((1, d), lambda i, j: (0, 0)),
            pl.BlockSpec((d, tn), lambda i, j: (0, j)),
        ],
        out_specs=pl.BlockSpec((tm, tn), lambda i, j: (i, j)),
        scratch_shapes=[pltpu.VMEM((tm, d), BF16)],
        compiler_params=pltpu.CompilerParams(
            dimension_semantics=("parallel", "arbitrary"), vmem_limit_bytes=VMEM_LIMIT),
        name="norm_proj",
    )(h, norm_w.reshape(1, d), w_bf16)


def _rwkv_kernel(*refs):
    o_ref = refs[16]
    xr, xk, xv, xl, s_ref = refs[17:22]
    c = pl.program_id(1)
    n_skip = FRONT_PAD // o_ref.shape[1]

    @pl.when(c == 0)
    def _init():
        for xs in (xr, xk, xv, xl):
            xs[0:8, :] = jnp.zeros((8, xs.shape[1]), F32)
        s_ref[...] = jnp.zeros(s_ref.shape, F32)

    @pl.when(c < n_skip)
    def _pad():
        o_ref[...] = jnp.zeros(o_ref.shape, F32)

    @pl.when(c >= n_skip)
    def _chunk():
        _rwkv_chunk(*refs)


def _rwkv_chunk(r_ref, k_ref, v_ref, lo_ref, mu_ref, mul_ref, w0_ref, w2_ref, a0_ref, a2_ref, g2_ref,
                kkw_ref, kaw_ref, rkw_ref, lnw_ref, lnb_ref, o_ref,
                xr, xk, xv, xl, s_ref, kt_s, bt_s, kn_s, rt_s, v_s, y_s, gc_s, bon_s, g_s):
    C = r_ref.shape[1]
    W = RWKV_WIDTH
    n_pairs = W // LANES

    def shift_mix(in_ref, xs, mu):
        x = in_ref[0]
        xs[8:8 + C, :] = x
        prev = xs[7:7 + C, :]
        xs[7:8, :] = x[C - 1:C, :]
        return x + (prev - x) * mu

    r = shift_mix(r_ref, xr, mu_ref[:, 0:W])
    k = shift_mix(k_ref, xk, mu_ref[:, W:2 * W])
    v = shift_mix(v_ref, xv, mu_ref[:, 2 * W:3 * W])
    lo = shift_mix(lo_ref, xl, mul_ref[...])
    xw = jnp.tanh(lo[:, 0:DECAY_LORA])
    xa = lo[:, DECAY_LORA:DECAY_LORA + AAA_LORA]
    xg = _sigmoid(lo[:, DECAY_LORA + AAA_LORA:LORA_COLS])

    z = -(w0_ref[...] + _mm3(xw, w2_ref[...]))
    softplus = jnp.maximum(z, 0.0) + jnp.log(1.0 + jnp.exp(-jnp.abs(z)))
    lw = -jnp.exp(-softplus - 0.5)
    a = _sigmoid(a0_ref[...] + _mm3(xa, a2_ref[...]))
    g_s[...] = _mm3(xg, g2_ref[...])

    ones_bd = _head_ones()
    kk = k * kkw_ref[...]
    kkn = kk / jnp.maximum(jnp.sqrt(_seg_sum_wide(kk * kk, ones_bd)), 1e-12)
    k2 = k * (1.0 + (a - 1.0) * kaw_ref[...])
    bon_s[...] = _seg_sum_wide(r * k2 * rkw_ref[...], ones_bd) * v

    ti = lax.broadcasted_iota(jnp.int32, (C, C), 0)
    tj = lax.broadcasted_iota(jnp.int32, (C, C), 1)
    ltri = jnp.where(ti >= tj, 1.0, 0.0).astype(BF16)
    l1 = lw.astype(BF16)
    rem = lw - l1.astype(F32)
    l2 = rem.astype(BF16)
    l3 = (rem - l2.astype(F32)).astype(BF16)
    cum = _dot(ltri, l1) + _dot(ltri, l2) + _dot(ltri, l3)
    e_pos = jnp.exp(cum)
    e_neg = jnp.exp(-cum)
    kt_s[...] = kkn * jnp.exp(cum - lw)
    bt_s[...] = kkn * a * e_neg
    kn_s[...] = k2 * e_neg
    rt_s[...] = r * e_pos
    v_s[...] = v
    gc_s[...] = jnp.exp(cum[C - 1:C, :])

    lane = lax.broadcasted_iota(jnp.int32, (1, LANES), 1)
    m0 = jnp.where(lane < RWKV_HEAD_DIM, 1.0, 0.0)
    m1 = 1.0 - m0
    trow = lax.broadcasted_iota(jnp.int32, (C, 2 * C), 0)
    tcol = lax.broadcasted_iota(jnp.int32, (C, 2 * C), 1) % C
    strict = trow > tcol
    incl = trow >= tcol
    eye2 = jnp.where(trow == tcol, 1.0, 0.0)
    br = lax.broadcasted_iota(jnp.int32, (LANES, LANES), 0) // RWKV_HEAD_DIM
    bc = lax.broadcasted_iota(jnp.int32, (LANES, LANES), 1) // RWKV_HEAD_DIM
    bdmask = jnp.where(br == bc, 1.0, 0.0)

    def stack2(y):
        return jnp.concatenate([y * m0, y * m1], axis=0).astype(BF16)

    pairs = range(n_pairs)
    sl = [slice(p * LANES, (p + 1) * LANES) for p in pairs]
    kt = [kt_s[:, sl[p]] for p in pairs]
    bt = [bt_s[:, sl[p]] for p in pairs]
    kn = [kn_s[:, sl[p]] for p in pairs]
    rt = [rt_s[:, sl[p]] for p in pairs]
    vv = [v_s[:, sl[p]] for p in pairs]
    gc = [gc_s[:, sl[p]] for p in pairs]
    m_all = [_dot_nt(jnp.concatenate([kt[p], rt[p]], axis=0).astype(BF16),
                     jnp.concatenate([bt[p] * m0, bt[p] * m1, kn[p] * m0, kn[p] * m1], axis=0).astype(BF16))
             for p in pairs]
    b_mat = [jnp.where(strict, m_all[p][0:C, 2 * C:4 * C], 0.0).astype(BF16) for p in pairs]
    pq_mat = [jnp.concatenate([jnp.where(incl, m_all[p][C:2 * C, 0:2 * C], 0.0),
                               jnp.where(incl, m_all[p][C:2 * C, 2 * C:4 * C], 0.0)], axis=1).astype(BF16)
              for p in pairs]

    pw = [-jnp.where(strict, m_all[p][0:C, 0:2 * C], 0.0) for p in pairs]
    t_inv = [eye2 + pw[p] for p in pairs]
    for _ in range(int(math.log2(C)) - 1):
        pw = [_dot(pw[p].astype(BF16), stack2(pw[p])) for p in pairs]
        t_inv = [t_inv[p] + _dot(t_inv[p].astype(BF16), stack2(pw[p])) for p in pairs]

    s_old = [s_ref[p] for p in pairs]
    s_bf = [s_old[p].astype(BF16) for p in pairs]
    v_st = [stack2(vv[p]) for p in pairs]
    rhs = [_dot_nt(kt[p].astype(BF16), s_bf[p]) + _dot(b_mat[p], v_st[p]) for p in pairs]
    u = [-_dot(t_inv[p].astype(BF16), stack2(rhs[p])) for p in pairs]
    for p in pairs:
        y_s[:, sl[p]] = _dot_nt(rt[p].astype(BF16), s_bf[p]) + _dot(
            pq_mat[p], jnp.concatenate([stack2(u[p]), v_st[p]], axis=0))
    for p in pairs:
        uc = jnp.concatenate([u[p], vv[p]], axis=0)
        xc = jnp.concatenate([bt[p] * gc[p], kn[p] * gc[p]], axis=0)
        s_ref[p] = s_old[p] * gc[p] + bdmask * _dot(jnp.transpose(uc).astype(BF16), xc.astype(BF16))

    y = y_s[...]
    inv_n = 1.0 / RWKV_HEAD_DIM
    mean = _seg_sum_wide(y, ones_bd) * inv_n
    dlt = y - mean
    var = _seg_sum_wide(dlt * dlt, ones_bd) * inv_n
    yn = dlt * lax.rsqrt(var + RWKV_GN_EPS) * lnw_ref[...] + lnb_ref[...]
    o_ref[0] = (yn + bon_s[...]) * g_s[...]


def _rwkv(proj3, mu_rkv, mu_lo, w0, w2, a0, a2, g2, k_k, k_a, r_k, ln_w, ln_b):
    b, lp, _ = proj3.shape
    C = RWKV_CHUNK
    W = RWKV_WIDTH
    cb = COL_RWKV // W
    lb = COL_LORA // LORA_COLS
    row = lambda x: x.reshape(1, -1)
    full = lambda shape: pl.BlockSpec(shape, lambda i, c: (0,) * len(shape))
    wide = pltpu.VMEM((C, W), F32)
    return pl.pallas_call(
        _rwkv_kernel,
        out_shape=jax.ShapeDtypeStruct((b, lp, W), F32),
        grid=(b, lp // C),
        in_specs=[
            pl.BlockSpec((1, C, W), lambda i, c: (i, c, cb)),
            pl.BlockSpec((1, C, W), lambda i, c: (i, c, cb + 1)),
            pl.BlockSpec((1, C, W), lambda i, c: (i, c, cb + 2)),
            pl.BlockSpec((1, C, LORA_COLS), lambda i, c: (i, c, lb)),
            full((1, 3 * W)), full((1, LORA_COLS)),
            full((1, W)), full((DECAY_LORA, W)), full((1, W)), full((AAA_LORA, W)), full((GATE_LORA, W)),
            full((1, W)), full((1, W)), full((1, W)), full((1, W)), full((1, W)),
        ],
        out_specs=pl.BlockSpec((1, C, W), lambda i, c: (i, c, 0)),
        scratch_shapes=[
            pltpu.VMEM((C + 8, W), F32), pltpu.VMEM((C + 8, W), F32), pltpu.VMEM((C + 8, W), F32),
            pltpu.VMEM((C + 8, LORA_COLS), F32),
            pltpu.VMEM((W // LANES, LANES, LANES), F32),
            wide, wide, wide, wide, wide, wide, pltpu.VMEM((1, W), F32), wide, wide,
        ],
        compiler_params=pltpu.CompilerParams(
            dimension_semantics=("parallel", "arbitrary"), vmem_limit_bytes=VMEM_LIMIT),
        name="rwkv7_time_mix",
    )(proj3, proj3, proj3, proj3, row(mu_rkv), row(mu_lo), row(w0), w2, row(a0), a2, g2,
      row(k_k), row(k_a), row(r_k), row(ln_w), row(ln_b))


def _attn_kernel(q_ref, k_ref, v_ref, cos_ref, s1_ref, s2_ref, qw_ref, kw_ref, lam_ref, sw_ref, o_ref,
                 kp_s, vp_s, m_s, acc_s, *, tq, tk, lambda_init):
    qi = pl.program_id(2)
    lp = k_ref.shape[1]
    n_real = lp - FRONT_PAD
    lk = kp_s.shape[0]
    nh = k_ref.shape[2] // LANES
    vw = 2 * LANES
    heads = range(nh)
    hs = [slice(h * LANES, (h + 1) * LANES) for h in heads]
    ones_bd = _head_ones(LANES, DIFF_HEAD_DIM)
    shift = ROPE_DIM // 2

    def norm_rope(x, w, rows):
        ms = _seg_sum(x * x, ones_bd) * (1.0 / DIFF_HEAD_DIM)
        xn = x * lax.rsqrt(ms + NORM_EPS) * w
        return (xn * cos_ref[rows, :] + pltpu.roll(xn, shift, 1) * s1_ref[rows, :]
                + pltpu.roll(xn, LANES - shift, 1) * s2_ref[rows, :])

    @pl.when(qi == 0)
    def _prep():
        def put(dst, n):
            src = pl.ds(FRONT_PAD + dst, n)
            rows = pl.ds(dst, n)
            for h in heads:
                kp_s[rows, hs[h]] = norm_rope(k_ref[0, src, hs[h]], kw_ref[...], src).astype(BF16)
                vp_s[rows, h * vw:h * vw + LANES] = v_ref[0, src, hs[h]].astype(BF16)
                vp_s[rows, h * vw + LANES:(h + 1) * vw] = jnp.ones((n, LANES), BF16)

        def body(i, carry):
            put(pl.multiple_of(i * LANES, LANES), LANES)
            return carry
        lax.fori_loop(0, n_real // LANES, body, 0)
        tail = n_real % LANES
        if tail:
            put(n_real - tail, tail)
        if lk > n_real:
            kp_s[n_real:lk, :] = jnp.zeros((lk - n_real, nh * LANES), BF16)
            vp_s[n_real:lk, :] = jnp.zeros((lk - n_real, nh * vw), BF16)

    lane = lax.broadcasted_iota(jnp.int32, (1, LANES), 1)
    m0 = jnp.where(lane < DIFF_HEAD_DIM, 1.0, 0.0)
    m1 = 1.0 - m0
    rows_q = pl.ds(pl.multiple_of(qi * tq, tq), tq)
    qs = []
    for h in heads:
        qn = norm_rope(q_ref[0, :, hs[h]], qw_ref[...], rows_q) * (DIFF_HEAD_DIM ** -0.5)
        qs.append(jnp.concatenate([qn * m0, qn * m1], axis=0).astype(BF16))
    m_s[...] = jnp.full(m_s.shape, NEG_BIG, F32)
    acc_s[...] = jnp.zeros(acc_s.shape, F32)
    first_row = qi * tq - FRONT_PAD
    row = first_row + lax.broadcasted_iota(jnp.int32, (2 * tq, LANES), 0) % tq
    col0 = lax.broadcasted_iota(jnp.int32, (2 * tq, LANES), 1)
    n_sub = tk // LANES

    def step(j, causal):
        start = j * tk
        if not isinstance(j, int):
            start = pl.multiple_of(start, tk)
        ks = pl.ds(start, tk)
        s = [_dot_nt(qs[h], kp_s[ks, hs[h]]) for h in heads]
        sub = [[s[h][:, c * LANES:(c + 1) * LANES] for c in range(n_sub)] for h in heads]
        if causal:
            sub = [[jnp.where(start + c * LANES + col0 <= row, sub[h][c], NEG_BIG) for c in range(n_sub)]
                   for h in heads]
        m_old = [m_s[h] for h in heads]
        m_new = []
        for h in heads:
            mx = functools.reduce(jnp.maximum, sub[h])
            m_new.append(jnp.maximum(m_old[h], jnp.broadcast_to(
                jnp.max(mx, axis=1, keepdims=True), mx.shape)))
        p = [jnp.concatenate([jnp.exp(sub[h][c] - m_new[h]) for c in range(n_sub)], axis=1).astype(BF16)
             for h in heads]
        pv = [_dot(p[h], vp_s[ks, h * vw:(h + 1) * vw]) for h in heads]
        for h in heads:
            alpha = jnp.exp(m_old[h] - m_new[h])
            acc_s[h] = jnp.concatenate([alpha, alpha], axis=1) * acc_s[h] + pv[h]
            m_s[h] = m_new[h]

    n_full = jnp.maximum(first_row + 1, 0) // tk
    last = jnp.maximum(first_row + tq - 1, 0) // tk

    def mid(j, carry):
        step(j, False)
        return carry
    lax.fori_loop(0, n_full, mid, 0)
    step(n_full, True)

    @pl.when(last > n_full)
    def _diag():
        step(last, True)

    lam = (jnp.exp(jnp.sum(lam_ref[0:1, :] * lam_ref[1:2, :], axis=1, keepdims=True))
           - jnp.exp(jnp.sum(lam_ref[2:3, :] * lam_ref[3:4, :], axis=1, keepdims=True)) + lambda_init)
    for h in heads:
        acc = acc_s[h]
        o = acc[:, 0:LANES] / acc[:, LANES:vw]
        od = o[0:tq] - lam * o[tq:2 * tq]
        ms = jnp.mean(od * od, axis=1, keepdims=True)
        o_ref[0, :, hs[h]] = od * lax.rsqrt(ms + NORM_EPS) * sw_ref[...] * (1.0 - lambda_init)


def _rope_tables(lp):
    half = ROPE_DIM // 2
    inv_freq = jnp.exp(-math.log(ROPE_THETA) * jnp.arange(half, dtype=F32) * 2.0 / ROPE_DIM)
    pos = (jnp.arange(lp) - FRONT_PAD).astype(F32)
    ang = pos[:, None] * inv_freq[None, :]
    cos, sin = jnp.cos(ang), jnp.sin(ang)
    one = jnp.ones((lp, DIFF_HEAD_DIM - ROPE_DIM), F32)
    zero = jnp.zeros((lp, DIFF_HEAD_DIM - ROPE_DIM), F32)
    zh = jnp.zeros((lp, half), F32)
    c = jnp.concatenate([cos, cos, one], axis=1)
    s1 = jnp.concatenate([zh, sin, zero], axis=1)
    s2 = jnp.concatenate([-sin, zh, zero], axis=1)
    dup = lambda t: jnp.concatenate([t, t], axis=1)
    return dup(c), dup(s1), dup(s2)


def _attention(proj3, q_norm_w, k_norm_w, lam4, subln_w, lambda_init):
    b, lp, _ = proj3.shape
    tq, tk, nh = ATTN_BLOCK, ATTN_KEY_BLOCK, ATTN_HEADS_PER_STEP
    lk = -(-(lp - FRONT_PAD) // tk) * tk
    hw = nh * LANES
    qb = COL_DIFF // hw
    kb = qb + DIFF_QK_WIDTH // hw
    vb = kb + DIFF_QK_WIDTH // hw
    cos, s1, s2 = _rope_tables(lp)
    dup = lambda w: jnp.concatenate([w, w]).reshape(1, LANES)
    full = lambda shape: pl.BlockSpec(shape, lambda i, h, q: (0,) * len(shape))
    return pl.pallas_call(
        functools.partial(_attn_kernel, tq=tq, tk=tk, lambda_init=lambda_init),
        out_shape=jax.ShapeDtypeStruct((b, lp, DIFF_V_WIDTH), F32),
        grid=(b, DIFF_HEADS // nh, lp // tq),
        in_specs=[
            pl.BlockSpec((1, tq, hw), lambda i, h, q: (i, q, qb + h)),
            pl.BlockSpec((1, lp, hw), lambda i, h, q: (i, 0, kb + h)),
            pl.BlockSpec((1, lp, hw), lambda i, h, q: (i, 0, vb + h)),
            full((lp, LANES)), full((lp, LANES)), full((lp, LANES)),
            full((1, LANES)), full((1, LANES)), full((4, DIFF_HEAD_DIM)), full((1, LANES)),
        ],
        out_specs=pl.BlockSpec((1, tq, hw), lambda i, h, q: (i, q, h)),
        scratch_shapes=[pltpu.VMEM((lk, hw), BF16), pltpu.VMEM((lk, 2 * hw), BF16),
                        pltpu.VMEM((nh, 2 * tq, LANES), F32), pltpu.VMEM((nh, 2 * tq, 2 * LANES), F32)],
        compiler_params=pltpu.CompilerParams(
            dimension_semantics=("parallel", "parallel", "arbitrary"), vmem_limit_bytes=VMEM_LIMIT),
        name="diff_attention",
    )(proj3, proj3, proj3, cos, s1, s2, dup(q_norm_w), dup(k_norm_w), lam4, subln_w.reshape(1, LANES))


def _merge_kernel(rw_ref, da_ref, g1_ref, g2_ref, h_ref, wbr_ref, wbd_ref, wo_ref, n2_ref, wr_ref, br_ref,
                  h1_ref, u2_ref, il_ref, ic_ref, cnt_ref, base_s, *, lp):
    i = pl.program_id(0)
    tm = rw_ref.shape[0]

    @pl.when(i == 0)
    def _():
        base_s[...] = jnp.zeros(base_s.shape, F32)

    y1 = _dot(rw_ref[...].astype(BF16), wbr_ref[...])
    y2 = _dot(da_ref[...].astype(BF16), wbd_ref[...])
    merged = _sigmoid(g1_ref[...]) * y1 + _sigmoid(g2_ref[...]) * y2
    h1 = h_ref[...] + _dot(merged.astype(BF16), wo_ref[...])
    h1_ref[...] = h1
    u2 = h1 * lax.rsqrt(jnp.mean(h1 * h1, axis=-1, keepdims=True) + NORM_EPS) * n2_ref[...]
    u2_ref[...] = u2

    uh, ul = _split2(u2)
    wh, wl = _split2(wr_ref[...])
    lt = _dot_nt(wh, uh) + _dot_nt(wh, ul) + _dot_nt(wl, uh) + br_ref[...]

    gi8 = lax.broadcasted_iota(jnp.int32, (8, tm), 0)
    lg = lt[0:8]
    ge = jnp.exp(lg - jnp.max(lg, axis=0, keepdims=True))
    gp = ge / jnp.sum(ge, axis=0, keepdims=True)
    gv = jnp.max(gp, axis=0, keepdims=True)
    gidx = jnp.min(jnp.where(gp == gv, gi8, N_EXPERTS), axis=0, keepdims=True)

    ei = lax.broadcasted_iota(jnp.int32, (N_EXPERTS, tm), 0)
    sel = (ei // EXPERTS_PER_GROUP) == gidx
    le = jnp.where(sel, lt[8:8 + N_EXPERTS], NEG_BIG)
    ee = jnp.where(sel, jnp.exp(le - jnp.max(le, axis=0, keepdims=True)), 0.0)
    ep = jnp.where(sel, ee / jnp.sum(ee, axis=0, keepdims=True), -1.0)
    v1 = jnp.max(ep, axis=0, keepdims=True)
    i1 = jnp.min(jnp.where(ep == v1, ei, N_EXPERTS), axis=0, keepdims=True)
    ep2 = jnp.where(ei == i1, -1.0, ep)
    v2 = jnp.max(ep2, axis=0, keepdims=True)
    i2 = jnp.min(jnp.where(ep2 == v2, ei, N_EXPERTS), axis=0, keepdims=True)
    den = v1 + v2
    gate1 = gv * v1 / den
    gate2 = gv * v2 / den

    tok = (i * tm + lax.broadcasted_iota(jnp.int32, (1, tm), 1)).astype(F32)
    pos = tok - jnp.floor((tok + 0.5) / lp) * lp
    valid = pos > (FRONT_PAD - 0.5)

    oh1 = jnp.where((ei == i1) & valid, 1.0, 0.0)
    oh2 = jnp.where((ei == i2) & valid, 1.0, 0.0)
    oh = oh1 + oh2
    ur = lax.broadcasted_iota(jnp.int32, (tm, tm), 0)
    uc = lax.broadcasted_iota(jnp.int32, (tm, tm), 1)
    before = jnp.where(ur < uc, 1.0, 0.0).astype(BF16)
    tot = base_s[:, 0:1] + _dot(oh.astype(BF16), before)
    rank1 = jnp.sum(oh1 * tot, axis=0, keepdims=True)
    rank2 = jnp.sum(oh2 * tot, axis=0, keepdims=True)
    base_s[...] = base_s[...] + jnp.sum(oh, axis=1, keepdims=True)
    cnt_ref[...] = base_s[...]

    il = jnp.where(gi8 == 0, i1, jnp.where(gi8 == 1, i2, jnp.where(
        gi8 == 2, rank1.astype(jnp.int32), jnp.where(gi8 == 3, rank2.astype(jnp.int32), jnp.where(
            gi8 == 4, valid.astype(jnp.int32), 0)))))
    il_ref[...] = il
    ri = lax.broadcasted_iota(jnp.int32, (LANES, tm), 0)
    ic = jnp.where(ri == 0, gate1, jnp.where(ri == 1, gate2, 0.0))
    ic_ref[...] = jnp.transpose(ic)


def _merge(rw, da, proj, h0, wbr, wbd, wo, norm2_w, wr, br, lp, tm):
    t, d = h0.shape
    gb = COL_GATE // d
    full = lambda shape: pl.BlockSpec(shape, lambda i: (0,) * len(shape))
    tile = lambda c: pl.BlockSpec((tm, d), lambda i: (i, c))
    return pl.pallas_call(
        functools.partial(_merge_kernel, lp=lp),
        out_shape=(
            jax.ShapeDtypeStruct((t, d), F32),
            jax.ShapeDtypeStruct((t, d), F32),
            jax.ShapeDtypeStruct((8, t), jnp.int32),
            jax.ShapeDtypeStruct((t, LANES), F32),
            jax.ShapeDtypeStruct((N_EXPERTS, LANES), F32),
        ),
        grid=(t // tm,),
        in_specs=[tile(0), tile(0), tile(gb), tile(gb + 1), tile(0),
                  full((d, d)), full((d, d)), full((d, d)), full((1, d)), full((LANES, d)), full((LANES, 1))],
        out_specs=(
            tile(0), tile(0),
            pl.BlockSpec((8, tm), lambda i: (0, i)),
            pl.BlockSpec((tm, LANES), lambda i: (i, 0)),
            full((N_EXPERTS, LANES)),
        ),
        scratch_shapes=[pltpu.VMEM((N_EXPERTS, LANES), F32)],
        compiler_params=pltpu.CompilerParams(
            dimension_semantics=("arbitrary",), vmem_limit_bytes=VMEM_LIMIT),
        name="merge_router",
    )(rw, da, proj, proj, h0, wbr, wbd, wo, norm2_w.reshape(1, d), wr, br)


def _dispatch_kernel(d1_ref, d2_ref, u_ref, xin_hbm, xb_hbm, sem):
    del xin_hbm
    tm = d1_ref.shape[2]

    def copies(r):
        return [(d_ref[0, 0, r], pltpu.make_async_copy(
            u_ref.at[pl.ds(r, 1)], xb_hbm.at[pl.ds(jnp.maximum(d_ref[0, 0, r], 0), 1)], sem))
            for d_ref in (d1_ref, d2_ref)]

    def start(r, carry):
        for prio, (d, cp) in enumerate(copies(r)):
            @pl.when(d >= 0)
            def _():
                cp.start(priority=prio)
        return carry

    def wait(r, carry):
        for d, cp in copies(r):
            @pl.when(d >= 0)
            def _():
                cp.wait()
        return carry

    lax.fori_loop(0, tm, start, 0)
    lax.fori_loop(0, tm, wait, 0)


def _dispatch(dest1, dest2, u2, cap, tm):
    t, d = u2.shape
    nt = t // tm
    smem = lambda: pl.BlockSpec((1, 1, tm), lambda i: (i, 0, 0), memory_space=pltpu.SMEM)
    return pl.pallas_call(
        _dispatch_kernel,
        out_shape=jax.ShapeDtypeStruct((cap, d), F32),
        grid=(nt,),
        in_specs=[smem(), smem(), pl.BlockSpec((tm, d), lambda i: (i, 0)), pl.BlockSpec(memory_space=pl.ANY)],
        out_specs=pl.BlockSpec(memory_space=pl.ANY),
        scratch_shapes=[pltpu.SemaphoreType.DMA(())],
        input_output_aliases={3: 0},
        compiler_params=pltpu.CompilerParams(
            dimension_semantics=("arbitrary",), vmem_limit_bytes=VMEM_LIMIT),
        name="moe_dispatch",
    )(dest1.reshape(nt, 1, tm), dest2.reshape(nt, 1, tm), u2, jnp.zeros((cap, d), F32))


def _moe_kernel(be_ref, nb_ref, x_ref, wg_ref, wu_ref, wd_ref, o_ref):
    del be_ref
    i = pl.program_id(0)

    @pl.when(i < nb_ref[0])
    def _():
        x = x_ref[...].astype(BF16)
        hg = _dot(x, wg_ref[0])
        hu = _dot(x, wu_ref[0])
        hid = hg * _sigmoid(hg) * hu
        o_ref[...] = _dot(hid.astype(BF16), wd_ref[0])

    @pl.when(i >= nb_ref[0])
    def _():
        o_ref[...] = jnp.zeros(o_ref.shape, F32)


def _moe(block_e, n_used, xb, wg, wu, wd):
    cap, d = xb.shape
    ff = wg.shape[2]
    bm = MOE_BLOCK
    return pl.pallas_call(
        _moe_kernel,
        out_shape=jax.ShapeDtypeStruct((cap, d), F32),
        grid_spec=pltpu.PrefetchScalarGridSpec(
            num_scalar_prefetch=2,
            grid=(cap // bm,),
            in_specs=[
                pl.BlockSpec((bm, d), lambda i, be, nb: (i, 0)),
                pl.BlockSpec((1, d, ff), lambda i, be, nb: (be[i], 0, 0)),
                pl.BlockSpec((1, d, ff), lambda i, be, nb: (be[i], 0, 0)),
                pl.BlockSpec((1, ff, d), lambda i, be, nb: (be[i], 0, 0)),
            ],
            out_specs=pl.BlockSpec((bm, d), lambda i, be, nb: (i, 0)),
        ),
        compiler_params=pltpu.CompilerParams(
            dimension_semantics=("arbitrary",), vmem_limit_bytes=VMEM_LIMIT),
        name="moe_experts",
    )(block_e, n_used, xb, wg, wu, wd)


def _combine_kernel(d1_ref, d2_ref, h_ref, ic_ref, yb_hbm, o_ref, ga, gb, sem):
    tm = h_ref.shape[0]

    def copies(r):
        return [pltpu.make_async_copy(yb_hbm.at[pl.ds(d_ref[0, 0, r], 1)], buf.at[pl.ds(r, 1)], sem)
                for d_ref, buf in ((d1_ref, ga), (d2_ref, gb))]

    def start(r, carry):
        for prio, cp in enumerate(copies(r)):
            cp.start(priority=prio)
        return carry

    def wait(r, carry):
        for cp in copies(r):
            cp.wait()
        return carry

    lax.fori_loop(0, tm, start, 0)
    lax.fori_loop(0, tm, wait, 0)
    ic = ic_ref[...]
    o_ref[0] = h_ref[...] + ic[:, 0:1] * ga[...] + ic[:, 1:2] * gb[...]


def _combine(dest1, dest2, h1, ic, yb, b, lp, tm):
    t, d = h1.shape
    per = lp // tm
    first = (FRONT_PAD + N_META) // tm
    nt = t // tm
    smem = lambda: pl.BlockSpec((1, 1, tm), lambda i, j: (i * per + j + first, 0, 0), memory_space=pltpu.SMEM)
    return pl.pallas_call(
        _combine_kernel,
        out_shape=jax.ShapeDtypeStruct((b, lp - FRONT_PAD - N_META, d), F32),
        grid=(b, per - first),
        in_specs=[
            smem(), smem(),
            pl.BlockSpec((tm, d), lambda i, j: (i * per + j + first, 0)),
            pl.BlockSpec((tm, LANES), lambda i, j: (i * per + j + first, 0)),
            pl.BlockSpec(memory_space=pl.ANY),
        ],
        out_specs=pl.BlockSpec((1, tm, d), lambda i, j: (i, j, 0)),
        scratch_shapes=[pltpu.VMEM((tm, d), F32), pltpu.VMEM((tm, d), F32), pltpu.SemaphoreType.DMA(())],
        compiler_params=pltpu.CompilerParams(
            dimension_semantics=("arbitrary", "arbitrary"), vmem_limit_bytes=VMEM_LIMIT),
        name="moe_combine",
    )(dest1.reshape(nt, 1, tm), dest2.reshape(nt, 1, tm), h1, ic, yb)


def _routing_tables(il, cnt, n_blocks):
    bm = MOE_BLOCK
    counts = cnt[:, 0].astype(jnp.int32)
    padded = (counts + bm - 1) // bm * bm
    pad_end = jnp.cumsum(padded)
    pad_start = pad_end - padded
    valid = il[4] > 0
    dest1 = jnp.where(valid, pad_start[il[0]] + il[2], -1).astype(jnp.int32)
    dest2 = jnp.where(valid, pad_start[il[1]] + il[3], -1).astype(jnp.int32)
    starts = jnp.arange(n_blocks, dtype=jnp.int32) * bm
    block_e = jnp.minimum(jnp.sum((pad_end[None, :] <= starts[:, None]).astype(jnp.int32), axis=1),
                          N_EXPERTS - 1)
    n_used = (pad_end[-1:] // bm).astype(jnp.int32)
    return dest1, dest2, block_e, n_used


def _layer(h0, lp, l, norm1_w, w_in, rwkv_mu, rwkv_w0, rwkv_w2, rwkv_a0, rwkv_a2, rwkv_g2,
           rwkv_k_k, rwkv_k_a, rwkv_r_k, rwkv_ln_w, rwkv_ln_b, q_norm_w, k_norm_w,
           lambda_q1, lambda_k1, lambda_q2, lambda_k2, diff_subln_w, w_branch_rwkv, w_branch_diff,
           w_out, norm2_w, router_group_w, router_group_b, router_expert_w, router_expert_b,
           expert_w_gate, expert_w_up, expert_w_down, proj_tm, tok_tm):
    t, d = h0.shape
    b = t // lp
    lambda_init = 0.8 - 0.6 * math.exp(-0.3 * l)
    rw_cols = 3 * RWKV_WIDTH
    diff_cols = 2 * DIFF_QK_WIDTH + DIFF_V_WIDTH
    w_perm = jnp.concatenate([
        w_in[:, rw_cols + LORA_COLS + diff_cols:],
        w_in[:, rw_cols + LORA_COLS:rw_cols + LORA_COLS + diff_cols],
        w_in[:, :rw_cols + LORA_COLS],
    ], axis=1).astype(BF16)
    proj = _proj(h0, norm1_w, w_perm, proj_tm, 768)
    proj3 = proj.reshape(b, lp, IN_COLS)

    rw = _rwkv(proj3, rwkv_mu[:rw_cols], rwkv_mu[rw_cols:], rwkv_w0, rwkv_w2, rwkv_a0, rwkv_a2, rwkv_g2,
               rwkv_k_k, rwkv_k_a, rwkv_r_k.reshape(-1), rwkv_ln_w, rwkv_ln_b)
    lam4 = jnp.stack([lambda_q1, lambda_k1, lambda_q2, lambda_k2])
    da = _attention(proj3, q_norm_w, k_norm_w, lam4, diff_subln_w, lambda_init)

    wr = jnp.zeros((LANES, d), F32).at[0:N_GROUPS].set(router_group_w.T).at[8:8 + N_EXPERTS].set(router_expert_w.T)
    br = jnp.zeros((LANES,), F32).at[0:N_GROUPS].set(router_group_b).at[N_GROUPS:8].set(NEG_BIG)
    br = br.at[8:8 + N_EXPERTS].set(router_expert_b).reshape(LANES, 1)
    h1, u2, il, ic, cnt = _merge(
        rw.reshape(t, RWKV_WIDTH), da.reshape(t, DIFF_V_WIDTH), proj, h0,
        w_branch_rwkv.astype(BF16), w_branch_diff.astype(BF16), w_out.astype(BF16), norm2_w, wr, br, lp, tok_tm)

    n_real = b * (lp - FRONT_PAD)
    n_blocks = -(-(2 * n_real) // MOE_BLOCK) + N_EXPERTS
    dest1, dest2, block_e, n_used = _routing_tables(il, cnt, n_blocks)
    xb = _dispatch(dest1, dest2, u2, n_blocks * MOE_BLOCK, tok_tm)
    yb = _moe(block_e, n_used, xb, expert_w_gate.astype(BF16), expert_w_up.astype(BF16),
              expert_w_down.astype(BF16))
    return h1, ic, dest1, dest2, yb


def kernel(x, meta_tokens, norm1_w, w_in, rwkv_mu, rwkv_w0, rwkv_w2, rwkv_a0, rwkv_a2, rwkv_g2, rwkv_k_k, rwkv_k_a, rwkv_r_k, rwkv_ln_w, rwkv_ln_b, q_norm_w, k_norm_w, lambda_q1, lambda_k1, lambda_q2, lambda_k2, diff_subln_w, w_branch_rwkv, w_branch_diff, w_out, norm2_w, router_group_w, router_group_b, router_expert_w, router_expert_b, expert_w_gate, expert_w_up, expert_w_down):
    b, seq, d = x.shape
    depth = norm1_w.shape[0]
    assert depth == 1, "the combine step emits the final output; deeper stacks need an intermediate form"
    lp = FRONT_PAD + N_META + seq
    meta = jnp.broadcast_to(meta_tokens[None].astype(x.dtype), (b, N_META, d))
    h0 = jnp.concatenate([jnp.zeros((b, FRONT_PAD, d), x.dtype), meta, x], axis=1).reshape(b * lp, d)
    proj_tm = 1024 if (b * lp) % 1024 == 0 else 128
    tok_tm = 256 if (b * lp) % 256 == 0 else 128
    l = 0
    h1, ic, dest1, dest2, yb = _layer(
        h0, lp, l, norm1_w[l], w_in[l], rwkv_mu[l], rwkv_w0[l], rwkv_w2[l], rwkv_a0[l], rwkv_a2[l],
        rwkv_g2[l], rwkv_k_k[l], rwkv_k_a[l], rwkv_r_k[l], rwkv_ln_w[l], rwkv_ln_b[l], q_norm_w[l],
        k_norm_w[l], lambda_q1[l], lambda_k1[l], lambda_q2[l], lambda_k2[l], diff_subln_w[l],
        w_branch_rwkv[l], w_branch_diff[l], w_out[l], norm2_w[l], router_group_w[l], router_group_b[l],
        router_expert_w[l], router_expert_b[l], expert_w_gate[l], expert_w_up[l], expert_w_down[l],
        proj_tm, tok_tm)
    return _combine(dest1, dest2, h1, ic, yb, b, lp, OUT_TILE)
```

```python
import functools
import math

import jax
import jax.numpy as jnp
from jax import lax
from jax.experimental import pallas as pl
from jax.experimental.pallas import tpu as pltpu

F32 = jnp.float32
BF16 = jnp.bfloat16

D_MODEL = 1024
N_META = 16
NORM_EPS = 1e-6
RWKV_HEADS = 16
RWKV_HEAD_DIM = 64
RWKV_WIDTH = RWKV_HEADS * RWKV_HEAD_DIM
DECAY_LORA = 64
AAA_LORA = 64
GATE_LORA = 128
LORA_COLS = DECAY_LORA + AAA_LORA + GATE_LORA
RWKV_GN_EPS = 64e-5
DIFF_HEADS = 8
DIFF_HEAD_DIM = 64
DIFF_V_DIM = 2 * DIFF_HEAD_DIM
DIFF_QK_WIDTH = DIFF_HEADS * 2 * DIFF_HEAD_DIM
DIFF_V_WIDTH = DIFF_HEADS * DIFF_V_DIM
ROPE_THETA = 500000.0
ROPE_DIM = DIFF_HEAD_DIM // 4
N_GROUPS = 4
EXPERTS_PER_GROUP = 8
N_EXPERTS = N_GROUPS * EXPERTS_PER_GROUP
EXPERT_FF = 512

LANES = 128
ATTN_BLOCK = 256
ATTN_KEY_BLOCK = 512
ATTN_HEADS_PER_STEP = 2
FRONT_PAD = ATTN_BLOCK - N_META
RWKV_CHUNK = 64
OUT_TILE = 128
MOE_BLOCK = 256
NEG_BIG = -1e30
VMEM_LIMIT = 48 * 1024 * 1024

COL_GATE = 0
COL_DIFF = 2 * D_MODEL
COL_RWKV = COL_DIFF + 2 * DIFF_QK_WIDTH + DIFF_V_WIDTH
COL_LORA = COL_RWKV + 3 * RWKV_WIDTH
IN_COLS = COL_LORA + LORA_COLS


def _dot(a, b):
    return jnp.dot(a, b, preferred_element_type=F32)


def _dot_nt(a, b):
    return lax.dot_general(a, b, (((1,), (1,)), ((), ())), preferred_element_type=F32)


def _split2(x):
    hi = x.astype(BF16)
    lo = (x - hi.astype(F32)).astype(BF16)
    return hi, lo


def _mm3(a, b_ref):
    ah, al = _split2(a)
    return _dot(ah, b_ref[0]) + _dot(ah, b_ref[1]) + _dot(al, b_ref[0])


def _hi_lo(w):
    hi = w.astype(BF16)
    return jnp.stack([hi, (w - hi.astype(F32)).astype(BF16)])


def _sigmoid(x):
    return 1.0 / (1.0 + jnp.exp(-x))


def _head_ones(width=LANES, head=RWKV_HEAD_DIM):
    r = lax.broadcasted_iota(jnp.int32, (width, width), 0) // head
    c = lax.broadcasted_iota(jnp.int32, (width, width), 1) // head
    return jnp.where(r == c, 1.0, 0.0).astype(BF16)


def _seg_sum(x, ones_bd):
    hi, lo = _split2(x)
    return _dot(hi, ones_bd) + _dot(lo, ones_bd)


def _seg_sum_wide(x, ones_bd):
    rows, width = x.shape
    n = width // LANES
    xs = jnp.concatenate([x[:, i * LANES:(i + 1) * LANES] for i in range(n)], axis=0)
    ys = _seg_sum(xs, ones_bd)
    return jnp.concatenate([ys[i * rows:(i + 1) * rows] for i in range(n)], axis=1)


def _proj_kernel(h_ref, nw_ref, w_ref, o_ref, u_ref):
    @pl.when(pl.program_id(1) == 0)
    def _():
        x = h_ref[...]
        ms = jnp.mean(x * x, axis=-1, keepdims=True)
        u_ref[...] = (x * lax.rsqrt(ms + NORM_EPS) * nw_ref[...]).astype(BF16)

    o_ref[...] = _dot(u_ref[...], w_ref[...]).astype(o_ref.dtype)


def _proj(h, norm_w, w_bf16, tm, tn):
    t, d = h.shape
    n = w_bf16.shape[1]
    return pl.pallas_call(
        _proj_kernel,
        out_shape=jax.ShapeDtypeStruct((t, n), BF16),
        grid=(t // tm, n // tn),
        in_specs=[
            pl.BlockSpec((tm, d), lambda i, j: (i, 0)),
            pl.BlockSpec((1, d), lambda i, j: (0, 0)),
            pl.BlockSpec((d, tn), lambda i, j: (0, j)),
        ],
        out_specs=pl.BlockSpec((tm, tn), lambda i, j: (i, j)),
        scratch_shapes=[pltpu.VMEM((tm, d), BF16)],
        compiler_params=pltpu.CompilerParams(
            dimension_semantics=("parallel", "arbitrary"), vmem_limit_bytes=VMEM_LIMIT),
        name="norm_proj",
    )(h, norm_w.reshape(1, d), w_bf16)


def _rwkv_kernel(*refs):
    o_ref = refs[16]
    xr, xk, xv, xl, s_ref = refs[17:22]
    c = pl.program_id(1)
    n_skip = FRONT_PAD // o_ref.shape[1]

    @pl.when(c == 0)
    def _init():
        for xs in (xr, xk, xv, xl):
            xs[0:8, :] = jnp.zeros((8, xs.shape[1]), F32)
        s_ref[...] = jnp.zeros(s_ref.shape, F32)

    @pl.when(c < n_skip)
    def _pad():
        o_ref[...] = jnp.zeros(o_ref.shape, o_ref.dtype)

    @pl.when(c >= n_skip)
    def _chunk():
        _rwkv_chunk(*refs)


def _rwkv_chunk(r_ref, k_ref, v_ref, lo_ref, mu_ref, mul_ref, w0_ref, w2_ref, a0_ref, a2_ref, g2_ref,
                kkw_ref, kaw_ref, rkw_ref, lnw_ref, lnb_ref, o_ref,
                xr, xk, xv, xl, s_ref, kt_s, bt_s, kn_s, rt_s, v_s, y_s, gc_s, bon_s, g_s):
    C = r_ref.shape[1]
    W = RWKV_WIDTH
    n_pairs = W // LANES

    def shift_mix(in_ref, xs, mu):
        x = in_ref[0].astype(F32)
        xs[8:8 + C, :] = x
        prev = xs[7:7 + C, :]
        xs[7:8, :] = x[C - 1:C, :]
        return x + (prev - x) * mu

    r = shift_mix(r_ref, xr, mu_ref[:, 0:W])
    k = shift_mix(k_ref, xk, mu_ref[:, W:2 * W])
    v = shift_mix(v_ref, xv, mu_ref[:, 2 * W:3 * W])
    lo = shift_mix(lo_ref, xl, mul_ref[...])
    xw = jnp.tanh(lo[:, 0:DECAY_LORA])
    xa = lo[:, DECAY_LORA:DECAY_LORA + AAA_LORA]
    xg = _sigmoid(lo[:, DECAY_LORA + AAA_LORA:LORA_COLS])

    z = -(w0_ref[...] + _mm3(xw, w2_ref))
    softplus = jnp.maximum(z, 0.0) + jnp.log(1.0 + jnp.exp(-jnp.abs(z)))
    lw = -jnp.exp(-softplus - 0.5)
    a = _sigmoid(a0_ref[...] + _mm3(xa, a2_ref))
    g_s[...] = _mm3(xg, g2_ref)

    ones_bd = _head_ones()
    kk = k * kkw_ref[...]
    kkn = kk / jnp.maximum(jnp.sqrt(_seg_sum_wide(kk * kk, ones_bd)), 1e-12)
    k2 = k * (1.0 + (a - 1.0) * kaw_ref[...])
    bon_s[...] = _seg_sum_wide(r * k2 * rkw_ref[...], ones_bd) * v

    ti = lax.broadcasted_iota(jnp.int32, (C, C), 0)
    tj = lax.broadcasted_iota(jnp.int32, (C, C), 1)
    ltri = jnp.where(ti >= tj, 1.0, 0.0).astype(BF16)
    l1 = lw.astype(BF16)
    rem = lw - l1.astype(F32)
    l2 = rem.astype(BF16)
    l3 = (rem - l2.astype(F32)).astype(BF16)
    cum = _dot(ltri, l1) + _dot(ltri, l2) + _dot(ltri, l3)
    e_pos = jnp.exp(cum)
    e_neg = jnp.exp(-cum)
    kt_s[...] = kkn * jnp.exp(cum - lw)
    bt_s[...] = kkn * a * e_neg
    kn_s[...] = k2 * e_neg
    rt_s[...] = r * e_pos
    v_s[...] = v
    gc_s[...] = jnp.exp(cum[C - 1:C, :])

    lane = lax.broadcasted_iota(jnp.int32, (1, LANES), 1)
    m0 = jnp.where(lane < RWKV_HEAD_DIM, 1.0, 0.0)
    m1 = 1.0 - m0
    trow = lax.broadcasted_iota(jnp.int32, (C, 2 * C), 0)
    tcol = lax.broadcasted_iota(jnp.int32, (C, 2 * C), 1) % C
    strict = trow > tcol
    incl = trow >= tcol
    eye2 = jnp.where(trow == tcol, 1.0, 0.0)
    br = lax.broadcasted_iota(jnp.int32, (LANES, LANES), 0) // RWKV_HEAD_DIM
    bc = lax.broadcasted_iota(jnp.int32, (LANES, LANES), 1) // RWKV_HEAD_DIM
    bdmask = jnp.where(br == bc, 1.0, 0.0)

    m0b = m0.astype(BF16)
    m1b = m1.astype(BF16)

    def stack2(y):
        yb = y.astype(BF16)
        return jnp.concatenate([yb * m0b, yb * m1b], axis=0)

    pairs = range(n_pairs)
    sl = [slice(p * LANES, (p + 1) * LANES) for p in pairs]
    kt = [kt_s[:, sl[p]] for p in pairs]
    bt = [bt_s[:, sl[p]] for p in pairs]
    kn = [kn_s[:, sl[p]] for p in pairs]
    rt = [rt_s[:, sl[p]] for p in pairs]
    vv = [v_s[:, sl[p]] for p in pairs]
    gc = [gc_s[:, sl[p]] for p in pairs]
    m_all = [_dot_nt(jnp.concatenate([kt[p], rt[p]], axis=0).astype(BF16),
                     jnp.concatenate([stack2(bt[p]), stack2(kn[p])], axis=0))
             for p in pairs]
    b_mat = [jnp.where(strict, m_all[p][0:C, 2 * C:4 * C], 0.0).astype(BF16) for p in pairs]
    pq_mat = [jnp.concatenate([jnp.where(incl, m_all[p][C:2 * C, 0:2 * C], 0.0),
                               jnp.where(incl, m_all[p][C:2 * C, 2 * C:4 * C], 0.0)], axis=1).astype(BF16)
              for p in pairs]

    pw = [-jnp.where(strict, m_all[p][0:C, 0:2 * C], 0.0) for p in pairs]
    t_inv = [eye2 + pw[p] for p in pairs]
    for _ in range(int(math.log2(C)) - 1):
        pw = [_dot(pw[p].astype(BF16), stack2(pw[p])) for p in pairs]
        t_inv = [t_inv[p] + _dot(t_inv[p].astype(BF16), stack2(pw[p])) for p in pairs]

    s_old = [s_ref[p] for p in pairs]
    s_bf = [s_old[p].astype(BF16) for p in pairs]
    v_st = [stack2(vv[p]) for p in pairs]
    rhs = [_dot_nt(kt[p].astype(BF16), s_bf[p]) + _dot(b_mat[p], v_st[p]) for p in pairs]
    u = [-_dot(t_inv[p].astype(BF16), stack2(rhs[p])) for p in pairs]
    for p in pairs:
        y_s[:, sl[p]] = _dot_nt(rt[p].astype(BF16), s_bf[p]) + _dot(
            pq_mat[p], jnp.concatenate([stack2(u[p]), v_st[p]], axis=0))
    for p in pairs:
        uc = jnp.concatenate([u[p], vv[p]], axis=0)
        xc = jnp.concatenate([bt[p] * gc[p], kn[p] * gc[p]], axis=0)
        s_ref[p] = s_old[p] * gc[p] + bdmask * _dot(jnp.transpose(uc).astype(BF16), xc.astype(BF16))

    y = y_s[...]
    inv_n = 1.0 / RWKV_HEAD_DIM
    mean = _seg_sum_wide(y, ones_bd) * inv_n
    dlt = y - mean
    var = _seg_sum_wide(dlt * dlt, ones_bd) * inv_n
    yn = dlt * lax.rsqrt(var + RWKV_GN_EPS) * lnw_ref[...] + lnb_ref[...]
    o_ref[0] = ((yn + bon_s[...]) * g_s[...]).astype(o_ref.dtype)


def _rwkv(proj3, mu_rkv, mu_lo, w0, w2, a0, a2, g2, k_k, k_a, r_k, ln_w, ln_b):
    b, lp, _ = proj3.shape
    C = RWKV_CHUNK
    W = RWKV_WIDTH
    cb = COL_RWKV // W
    lb = COL_LORA // LORA_COLS
    row = lambda x: x.reshape(1, -1)
    full = lambda shape: pl.BlockSpec(shape, lambda i, c: (0,) * len(shape))
    wide = pltpu.VMEM((C, W), F32)
    return pl.pallas_call(
        _rwkv_kernel,
        out_shape=jax.ShapeDtypeStruct((b, lp, W), BF16),
        grid=(b, lp // C),
        in_specs=[
            pl.BlockSpec((1, C, W), lambda i, c: (i, c, cb)),
            pl.BlockSpec((1, C, W), lambda i, c: (i, c, cb + 1)),
            pl.BlockSpec((1, C, W), lambda i, c: (i, c, cb + 2)),
            pl.BlockSpec((1, C, LORA_COLS), lambda i, c: (i, c, lb)),
            full((1, 3 * W)), full((1, LORA_COLS)),
            full((1, W)), full((2, DECAY_LORA, W)), full((1, W)), full((2, AAA_LORA, W)), full((2, GATE_LORA, W)),
            full((1, W)), full((1, W)), full((1, W)), full((1, W)), full((1, W)),
        ],
        out_specs=pl.BlockSpec((1, C, W), lambda i, c: (i, c, 0)),
        scratch_shapes=[
            pltpu.VMEM((C + 8, W), F32), pltpu.VMEM((C + 8, W), F32), pltpu.VMEM((C + 8, W), F32),
            pltpu.VMEM((C + 8, LORA_COLS), F32),
            pltpu.VMEM((W // LANES, LANES, LANES), F32),
            wide, wide, wide, wide, wide, wide, pltpu.VMEM((1, W), F32), wide, wide,
        ],
        compiler_params=pltpu.CompilerParams(
            dimension_semantics=("parallel", "arbitrary"), vmem_limit_bytes=VMEM_LIMIT),
        name="rwkv7_time_mix",
    )(proj3, proj3, proj3, proj3, row(mu_rkv), row(mu_lo), row(w0), _hi_lo(w2), row(a0), _hi_lo(a2), _hi_lo(g2),
      row(k_k), row(k_a), row(r_k), row(ln_w), row(ln_b))


def _attn_kernel(q_ref, k_ref, v_ref, cos_ref, s1_ref, s2_ref, qw_ref, kw_ref, lam_ref, sw_ref, o_ref,
                 kp_s, vp_s, m_s, acc_s, *, tq, tk, lambda_init):
    qi = pl.program_id(2)
    lp = k_ref.shape[1]
    n_real = lp - FRONT_PAD
    lk = kp_s.shape[0]
    nh = k_ref.shape[2] // LANES
    vw = 2 * LANES
    heads = range(nh)
    hs = [slice(h * LANES, (h + 1) * LANES) for h in heads]
    ones_bd = _head_ones(LANES, DIFF_HEAD_DIM)
    shift = ROPE_DIM // 2

    def norm_rope(x, w, rows):
        ms = _seg_sum(x * x, ones_bd) * (1.0 / DIFF_HEAD_DIM)
        xn = x * lax.rsqrt(ms + NORM_EPS) * w
        return (xn * cos_ref[rows, :] + pltpu.roll(xn, shift, 1) * s1_ref[rows, :]
                + pltpu.roll(xn, LANES - shift, 1) * s2_ref[rows, :])

    @pl.when(qi == 0)
    def _prep():
        def put(dst, n):
            src = pl.ds(FRONT_PAD + dst, n)
            rows = pl.ds(dst, n)
            for h in heads:
                kp_s[rows, hs[h]] = norm_rope(k_ref[0, src, hs[h]].astype(F32), kw_ref[...], src).astype(BF16)
                vp_s[rows, h * vw:h * vw + LANES] = v_ref[0, src, hs[h]].astype(BF16)
                vp_s[rows, h * vw + LANES:(h + 1) * vw] = jnp.ones((n, LANES), BF16)

        def body(i, carry):
            put(pl.multiple_of(i * LANES, LANES), LANES)
            return carry
        lax.fori_loop(0, n_real // LANES, body, 0)
        tail = n_real % LANES
        if tail:
            put(n_real - tail, tail)
        if lk > n_real:
            kp_s[n_real:lk, :] = jnp.zeros((lk - n_real, nh * LANES), BF16)
            vp_s[n_real:lk, :] = jnp.zeros((lk - n_real, nh * vw), BF16)

    lane = lax.broadcasted_iota(jnp.int32, (1, LANES), 1)
    m0 = jnp.where(lane < DIFF_HEAD_DIM, 1.0, 0.0)
    m1 = 1.0 - m0
    rows_q = pl.ds(pl.multiple_of(qi * tq, tq), tq)
    qs = []
    for h in heads:
        qn = norm_rope(q_ref[0, :, hs[h]].astype(F32), qw_ref[...], rows_q) * (DIFF_HEAD_DIM ** -0.5)
        qs.append(jnp.concatenate([qn * m0, qn * m1], axis=0).astype(BF16))
    m_s[...] = jnp.full(m_s.shape, NEG_BIG, F32)
    acc_s[...] = jnp.zeros(acc_s.shape, F32)
    first_row = qi * tq - FRONT_PAD
    row = first_row + lax.broadcasted_iota(jnp.int32, (2 * tq, LANES), 0) % tq
    col0 = lax.broadcasted_iota(jnp.int32, (2 * tq, LANES), 1)
    n_sub = tk // LANES

    def step(j, causal):
        start = j * tk
        if not isinstance(j, int):
            start = pl.multiple_of(start, tk)
        ks = pl.ds(start, tk)
        s = [_dot_nt(qs[h], kp_s[ks, hs[h]]) for h in heads]
        sub = [[s[h][:, c * LANES:(c + 1) * LANES] for c in range(n_sub)] for h in heads]
        if causal:
            sub = [[jnp.where(start + c * LANES + col0 <= row, sub[h][c], NEG_BIG) for c in range(n_sub)]
                   for h in heads]
        m_old = [m_s[h] for h in heads]
        m_new = []
        for h in heads:
            mx = functools.reduce(jnp.maximum, sub[h])
            m_new.append(jnp.maximum(m_old[h], jnp.broadcast_to(
                jnp.max(mx, axis=1, keepdims=True), mx.shape)))
        p = [jnp.concatenate([jnp.exp(sub[h][c] - m_new[h]) for c in range(n_sub)], axis=1).astype(BF16)
             for h in heads]
        pv = [_dot(p[h], vp_s[ks, h * vw:(h + 1) * vw]) for h in heads]
        for h in heads:
            alpha = jnp.exp(m_old[h] - m_new[h])
            acc_s[h] = jnp.concatenate([alpha, alpha], axis=1) * acc_s[h] + pv[h]
            m_s[h] = m_new[h]

    n_full = jnp.maximum(first_row + 1, 0) // tk
    last = jnp.maximum(first_row + tq - 1, 0) // tk

    def mid(j, carry):
        step(j, False)
        return carry
    lax.fori_loop(0, n_full, mid, 0)
    step(n_full, True)

    @pl.when(last > n_full)
    def _diag():
        step(last, True)

    lam = (jnp.exp(jnp.sum(lam_ref[0:1, :] * lam_ref[1:2, :], axis=1, keepdims=True))
           - jnp.exp(jnp.sum(lam_ref[2:3, :] * lam_ref[3:4, :], axis=1, keepdims=True)) + lambda_init)
    for h in heads:
        acc = acc_s[h]
        o = acc[:, 0:LANES] / acc[:, LANES:vw]
        od = o[0:tq] - lam * o[tq:2 * tq]
        ms = jnp.mean(od * od, axis=1, keepdims=True)
        o_ref[0, :, hs[h]] = (od * lax.rsqrt(ms + NORM_EPS) * sw_ref[...]
                              * (1.0 - lambda_init)).astype(o_ref.dtype)


def _rope_tables(lp):
    half = ROPE_DIM // 2
    inv_freq = jnp.exp(-math.log(ROPE_THETA) * jnp.arange(half, dtype=F32) * 2.0 / ROPE_DIM)
    pos = (jnp.arange(lp) - FRONT_PAD).astype(F32)
    ang = pos[:, None] * inv_freq[None, :]
    cos, sin = jnp.cos(ang), jnp.sin(ang)
    one = jnp.ones((lp, DIFF_HEAD_DIM - ROPE_DIM), F32)
    zero = jnp.zeros((lp, DIFF_HEAD_DIM - ROPE_DIM), F32)
    zh = jnp.zeros((lp, half), F32)
    c = jnp.concatenate([cos, cos, one], axis=1)
    s1 = jnp.concatenate([zh, sin, zero], axis=1)
    s2 = jnp.concatenate([-sin, zh, zero], axis=1)
    dup = lambda t: jnp.concatenate([t, t], axis=1)
    return dup(c), dup(s1), dup(s2)


def _attention(proj3, q_norm_w, k_norm_w, lam4, subln_w, lambda_init):
    b, lp, _ = proj3.shape
    tq, tk, nh = ATTN_BLOCK, ATTN_KEY_BLOCK, ATTN_HEADS_PER_STEP
    lk = -(-(lp - FRONT_PAD) // tk) * tk
    hw = nh * LANES
    qb = COL_DIFF // hw
    kb = qb + DIFF_QK_WIDTH // hw
    vb = kb + DIFF_QK_WIDTH // hw
    cos, s1, s2 = _rope_tables(lp)
    dup = lambda w: jnp.concatenate([w, w]).reshape(1, LANES)
    full = lambda shape: pl.BlockSpec(shape, lambda i, h, q: (0,) * len(shape))
    return pl.pallas_call(
        functools.partial(_attn_kernel, tq=tq, tk=tk, lambda_init=lambda_init),
        out_shape=jax.ShapeDtypeStruct((b, lp, DIFF_V_WIDTH), BF16),
        grid=(b, DIFF_HEADS // nh, lp // tq),
        in_specs=[
            pl.BlockSpec((1, tq, hw), lambda i, h, q: (i, q, qb + h)),
            pl.BlockSpec((1, lp, hw), lambda i, h, q: (i, 0, kb + h)),
            pl.BlockSpec((1, lp, hw), lambda i, h, q: (i, 0, vb + h)),
            full((lp, LANES)), full((lp, LANES)), full((lp, LANES)),
            full((1, LANES)), full((1, LANES)), full((4, DIFF_HEAD_DIM)), full((1, LANES)),
        ],
        out_specs=pl.BlockSpec((1, tq, hw), lambda i, h, q: (i, q, h)),
        scratch_shapes=[pltpu.VMEM((lk, hw), BF16), pltpu.VMEM((lk, 2 * hw), BF16),
                        pltpu.VMEM((nh, 2 * tq, LANES), F32), pltpu.VMEM((nh, 2 * tq, 2 * LANES), F32)],
        compiler_params=pltpu.CompilerParams(
            dimension_semantics=("parallel", "parallel", "arbitrary"), vmem_limit_bytes=VMEM_LIMIT),
        name="diff_attention",
    )(proj3, proj3, proj3, cos, s1, s2, dup(q_norm_w), dup(k_norm_w), lam4, subln_w.reshape(1, LANES))


def _merge_kernel(rw_ref, da_ref, g1_ref, g2_ref, h_ref, wbr_ref, wbd_ref, wo_ref, n2_ref, wr_ref, br_ref,
                  h1_ref, u2_ref, il_ref, ic_ref, cnt_ref, base_s, *, lp):
    i = pl.program_id(0)
    tm = rw_ref.shape[0]

    @pl.when(i == 0)
    def _():
        base_s[...] = jnp.zeros(base_s.shape, F32)

    y1 = _dot(rw_ref[...], wbr_ref[...])
    y2 = _dot(da_ref[...], wbd_ref[...])
    merged = _sigmoid(g1_ref[...].astype(F32)) * y1 + _sigmoid(g2_ref[...].astype(F32)) * y2
    h1 = h_ref[...] + _dot(merged.astype(BF16), wo_ref[...])
    h1_ref[...] = h1
    u2 = h1 * lax.rsqrt(jnp.mean(h1 * h1, axis=-1, keepdims=True) + NORM_EPS) * n2_ref[...]
    u2_ref[...] = u2

    uh, ul = _split2(u2)
    wh, wl = _split2(wr_ref[...])
    lt = _dot_nt(wh, uh) + _dot_nt(wh, ul) + _dot_nt(wl, uh) + br_ref[...]

    gi8 = lax.broadcasted_iota(jnp.int32, (8, tm), 0)
    lg = lt[0:8]
    ge = jnp.exp(lg - jnp.max(lg, axis=0, keepdims=True))
    gp = ge / jnp.sum(ge, axis=0, keepdims=True)
    gv = jnp.max(gp, axis=0, keepdims=True)
    gidx = jnp.min(jnp.where(gp == gv, gi8, N_EXPERTS), axis=0, keepdims=True)

    ei = lax.broadcasted_iota(jnp.int32, (N_EXPERTS, tm), 0)
    sel = (ei // EXPERTS_PER_GROUP) == gidx
    le = jnp.where(sel, lt[8:8 + N_EXPERTS], NEG_BIG)
    ee = jnp.where(sel, jnp.exp(le - jnp.max(le, axis=0, keepdims=True)), 0.0)
    ep = jnp.where(sel, ee / jnp.sum(ee, axis=0, keepdims=True), -1.0)
    v1 = jnp.max(ep, axis=0, keepdims=True)
    i1 = jnp.min(jnp.where(ep == v1, ei, N_EXPERTS), axis=0, keepdims=True)
    ep2 = jnp.where(ei == i1, -1.0, ep)
    v2 = jnp.max(ep2, axis=0, keepdims=True)
    i2 = jnp.min(jnp.where(ep2 == v2, ei, N_EXPERTS), axis=0, keepdims=True)
    den = v1 + v2
    gate1 = gv * v1 / den
    gate2 = gv * v2 / den

    tok = (i * tm + lax.broadcasted_iota(jnp.int32, (1, tm), 1)).astype(F32)
    pos = tok - jnp.floor((tok + 0.5) / lp) * lp
    valid = pos > (FRONT_PAD - 0.5)

    oh1 = jnp.where((ei == i1) & valid, 1.0, 0.0)
    oh2 = jnp.where((ei == i2) & valid, 1.0, 0.0)
    oh = oh1 + oh2
    ur = lax.broadcasted_iota(jnp.int32, (tm, tm), 0)
    uc = lax.broadcasted_iota(jnp.int32, (tm, tm), 1)
    before = jnp.where(ur < uc, 1.0, 0.0).astype(BF16)
    tot = base_s[:, 0:1] + _dot(oh.astype(BF16), before)
    rank1 = jnp.sum(oh1 * tot, axis=0, keepdims=True)
    rank2 = jnp.sum(oh2 * tot, axis=0, keepdims=True)
    base_s[...] = base_s[...] + jnp.sum(oh, axis=1, keepdims=True)
    cnt_ref[...] = base_s[...]

    il = jnp.where(gi8 == 0, i1, jnp.where(gi8 == 1, i2, jnp.where(
        gi8 == 2, rank1.astype(jnp.int32), jnp.where(gi8 == 3, rank2.astype(jnp.int32), jnp.where(
            gi8 == 4, valid.astype(jnp.int32), 0)))))
    il_ref[...] = il
    ri = lax.broadcasted_iota(jnp.int32, (LANES, tm), 0)
    ic = jnp.where(ri == 0, gate1, jnp.where(ri == 1, gate2, 0.0))
    ic_ref[...] = jnp.transpose(ic)


def _merge(rw, da, proj, h0, wbr, wbd, wo, norm2_w, wr, br, lp, tm):
    t, d = h0.shape
    gb = COL_GATE // d
    full = lambda shape: pl.BlockSpec(shape, lambda i: (0,) * len(shape))
    tile = lambda c: pl.BlockSpec((tm, d), lambda i: (i, c))
    return pl.pallas_call(
        functools.partial(_merge_kernel, lp=lp),
        out_shape=(
            jax.ShapeDtypeStruct((t, d), F32),
            jax.ShapeDtypeStruct((t, d), F32),
            jax.ShapeDtypeStruct((8, t), jnp.int32),
            jax.ShapeDtypeStruct((t, LANES), F32),
            jax.ShapeDtypeStruct((N_EXPERTS, LANES), F32),
        ),
        grid=(t // tm,),
        in_specs=[tile(0), tile(0), tile(gb), tile(gb + 1), tile(0),
                  full((d, d)), full((d, d)), full((d, d)), full((1, d)), full((LANES, d)), full((LANES, 1))],
        out_specs=(
            tile(0), tile(0),
            pl.BlockSpec((8, tm), lambda i: (0, i)),
            pl.BlockSpec((tm, LANES), lambda i: (i, 0)),
            full((N_EXPERTS, LANES)),
        ),
        scratch_shapes=[pltpu.VMEM((N_EXPERTS, LANES), F32)],
        compiler_params=pltpu.CompilerParams(
            dimension_semantics=("arbitrary",), vmem_limit_bytes=VMEM_LIMIT),
        name="merge_router",
    )(rw, da, proj, proj, h0, wbr, wbd, wo, norm2_w.reshape(1, d), wr, br)


def _dispatch_kernel(d1_ref, d2_ref, u_ref, xin_hbm, xb_hbm, sem):
    del xin_hbm
    tm = d1_ref.shape[2]

    def start(r, carry):
        for prio, d_ref in enumerate((d1_ref, d2_ref)):
            pltpu.make_async_copy(u_ref.at[pl.ds(r, 1)], xb_hbm.at[pl.ds(d_ref[0, 0, r], 1)],
                                  sem).start(priority=prio)
        return carry

    lax.fori_loop(0, tm, start, 0, unroll=8)
    for _ in range(2):
        pltpu.make_async_copy(u_ref, xb_hbm.at[pl.ds(0, tm)], sem).wait()


def _dispatch(dest1, dest2, u2, cap, tm):
    t, d = u2.shape
    nt = t // tm
    smem = lambda: pl.BlockSpec((1, 1, tm), lambda i: (i, 0, 0), memory_space=pltpu.SMEM)
    return pl.pallas_call(
        _dispatch_kernel,
        out_shape=jax.ShapeDtypeStruct((cap, d), F32),
        grid=(nt,),
        in_specs=[smem(), smem(), pl.BlockSpec((tm, d), lambda i: (i, 0)), pl.BlockSpec(memory_space=pl.ANY)],
        out_specs=pl.BlockSpec(memory_space=pl.ANY),
        scratch_shapes=[pltpu.SemaphoreType.DMA(())],
        input_output_aliases={3: 0},
        compiler_params=pltpu.CompilerParams(
            dimension_semantics=("arbitrary",), vmem_limit_bytes=VMEM_LIMIT),
        name="moe_dispatch",
    )(dest1.reshape(nt, 1, tm), dest2.reshape(nt, 1, tm), u2, jnp.zeros((cap, d), F32))


def _moe_kernel(be_ref, nb_ref, x_ref, wg_ref, wu_ref, wd_ref, o_ref):
    del be_ref
    i = pl.program_id(0)

    @pl.when(i < nb_ref[0])
    def _():
        x = x_ref[...].astype(BF16)
        hg = _dot(x, wg_ref[0])
        hu = _dot(x, wu_ref[0])
        hid = hg * _sigmoid(hg) * hu
        o_ref[...] = _dot(hid.astype(BF16), wd_ref[0])

    @pl.when(i >= nb_ref[0])
    def _():
        o_ref[...] = jnp.zeros(o_ref.shape, F32)


def _moe(block_e, n_used, xb, wg, wu, wd):
    d = xb.shape[1]
    ff = wg.shape[2]
    bm = MOE_BLOCK
    cap = block_e.shape[0] * bm
    return pl.pallas_call(
        _moe_kernel,
        out_shape=jax.ShapeDtypeStruct((cap, d), F32),
        grid_spec=pltpu.PrefetchScalarGridSpec(
            num_scalar_prefetch=2,
            grid=(cap // bm,),
            in_specs=[
                pl.BlockSpec((bm, d), lambda i, be, nb: (i, 0)),
                pl.BlockSpec((1, d, ff), lambda i, be, nb: (be[i], 0, 0)),
                pl.BlockSpec((1, d, ff), lambda i, be, nb: (be[i], 0, 0)),
                pl.BlockSpec((1, ff, d), lambda i, be, nb: (be[i], 0, 0)),
            ],
            out_specs=pl.BlockSpec((bm, d), lambda i, be, nb: (i, 0)),
        ),
        compiler_params=pltpu.CompilerParams(
            dimension_semantics=("arbitrary",), vmem_limit_bytes=VMEM_LIMIT),
        name="moe_experts",
    )(block_e, n_used, xb, wg, wu, wd)


def _combine_kernel(d1_ref, d2_ref, h_ref, ic_ref, yb_hbm, o_ref, ga, gb, sem):
    tm = h_ref.shape[0]

    def start(r, carry):
        for prio, (d_ref, buf) in enumerate(((d1_ref, ga), (d2_ref, gb))):
            pltpu.make_async_copy(yb_hbm.at[pl.ds(d_ref[0, 0, r], 1)], buf.at[pl.ds(r, 1)],
                                  sem).start(priority=prio)
        return carry

    lax.fori_loop(0, tm, start, 0, unroll=8)
    for buf in (ga, gb):
        pltpu.make_async_copy(yb_hbm.at[pl.ds(0, tm)], buf, sem).wait()
    ic = ic_ref[...]
    o_ref[0] = h_ref[...] + ic[:, 0:1] * ga[...] + ic[:, 1:2] * gb[...]


def _combine(dest1, dest2, h1, ic, yb, b, lp, tm):
    t, d = h1.shape
    per = lp // tm
    first = (FRONT_PAD + N_META) // tm
    nt = t // tm
    smem = lambda: pl.BlockSpec((1, 1, tm), lambda i, j: (i * per + j + first, 0, 0), memory_space=pltpu.SMEM)
    return pl.pallas_call(
        _combine_kernel,
        out_shape=jax.ShapeDtypeStruct((b, lp - FRONT_PAD - N_META, d), F32),
        grid=(b, per - first),
        in_specs=[
            smem(), smem(),
            pl.BlockSpec((tm, d), lambda i, j: (i * per + j + first, 0)),
            pl.BlockSpec((tm, LANES), lambda i, j: (i * per + j + first, 0)),
            pl.BlockSpec(memory_space=pl.ANY),
        ],
        out_specs=pl.BlockSpec((1, tm, d), lambda i, j: (i, j, 0)),
        scratch_shapes=[pltpu.VMEM((tm, d), F32), pltpu.VMEM((tm, d), F32), pltpu.SemaphoreType.DMA(())],
        compiler_params=pltpu.CompilerParams(
            dimension_semantics=("arbitrary", "arbitrary"), vmem_limit_bytes=VMEM_LIMIT),
        name="moe_combine",
    )(dest1.reshape(nt, 1, tm), dest2.reshape(nt, 1, tm), h1, ic, yb)


def _routing_tables(il, cnt, n_blocks, tm):
    bm = MOE_BLOCK
    counts = cnt[:, 0].astype(jnp.int32)
    padded = (counts + bm - 1) // bm * bm
    pad_end = jnp.cumsum(padded)
    pad_start = pad_end - padded
    valid = il[4] > 0
    spare = n_blocks * bm + jnp.arange(il.shape[1], dtype=jnp.int32) % tm
    dest1 = jnp.where(valid, pad_start[il[0]] + il[2], spare).astype(jnp.int32)
    dest2 = jnp.where(valid, pad_start[il[1]] + il[3], spare + tm).astype(jnp.int32)
    starts = jnp.arange(n_blocks, dtype=jnp.int32) * bm
    block_e = jnp.minimum(jnp.sum((pad_end[None, :] <= starts[:, None]).astype(jnp.int32), axis=1),
                          N_EXPERTS - 1)
    n_used = (pad_end[-1:] // bm).astype(jnp.int32)
    return dest1, dest2, block_e, n_used


def _layer(h0, lp, l, norm1_w, w_in, rwkv_mu, rwkv_w0, rwkv_w2, rwkv_a0, rwkv_a2, rwkv_g2,
           rwkv_k_k, rwkv_k_a, rwkv_r_k, rwkv_ln_w, rwkv_ln_b, q_norm_w, k_norm_w,
           lambda_q1, lambda_k1, lambda_q2, lambda_k2, diff_subln_w, w_branch_rwkv, w_branch_diff,
           w_out, norm2_w, router_group_w, router_group_b, router_expert_w, router_expert_b,
           expert_w_gate, expert_w_up, expert_w_down, proj_tm, tok_tm):
    t, d = h0.shape
    b = t // lp
    lambda_init = 0.8 - 0.6 * math.exp(-0.3 * l)
    rw_cols = 3 * RWKV_WIDTH
    diff_cols = 2 * DIFF_QK_WIDTH + DIFF_V_WIDTH
    w_perm = jnp.concatenate([
        w_in[:, rw_cols + LORA_COLS + diff_cols:],
        w_in[:, rw_cols + LORA_COLS:rw_cols + LORA_COLS + diff_cols],
        w_in[:, :rw_cols + LORA_COLS],
    ], axis=1).astype(BF16)
    proj = _proj(h0, norm1_w, w_perm, proj_tm, 768)
    proj3 = proj.reshape(b, lp, IN_COLS)

    rw = _rwkv(proj3, rwkv_mu[:rw_cols], rwkv_mu[rw_cols:], rwkv_w0, rwkv_w2, rwkv_a0, rwkv_a2, rwkv_g2,
               rwkv_k_k, rwkv_k_a, rwkv_r_k.reshape(-1), rwkv_ln_w, rwkv_ln_b)
    lam4 = jnp.stack([lambda_q1, lambda_k1, lambda_q2, lambda_k2])
    da = _attention(proj3, q_norm_w, k_norm_w, lam4, diff_subln_w, lambda_init)

    wr = jnp.zeros((LANES, d), F32).at[0:N_GROUPS].set(router_group_w.T).at[8:8 + N_EXPERTS].set(router_expert_w.T)
    br = jnp.zeros((LANES,), F32).at[0:N_GROUPS].set(router_group_b).at[N_GROUPS:8].set(NEG_BIG)
    br = br.at[8:8 + N_EXPERTS].set(router_expert_b).reshape(LANES, 1)
    h1, u2, il, ic, cnt = _merge(
        rw.reshape(t, RWKV_WIDTH), da.reshape(t, DIFF_V_WIDTH), proj, h0,
        w_branch_rwkv.astype(BF16), w_branch_diff.astype(BF16), w_out.astype(BF16), norm2_w, wr, br, lp, tok_tm)

    n_real = b * (lp - FRONT_PAD)
    n_blocks = -(-(2 * n_real) // MOE_BLOCK) + N_EXPERTS
    dest1, dest2, block_e, n_used = _routing_tables(il, cnt, n_blocks, tok_tm)
    xb = _dispatch(dest1, dest2, u2, n_blocks * MOE_BLOCK + 2 * tok_tm, tok_tm)
    yb = _moe(block_e, n_used, xb, expert_w_gate.astype(BF16), expert_w_up.astype(BF16),
              expert_w_down.astype(BF16))
    return h1, ic, dest1, dest2, yb


def kernel(x, meta_tokens, norm1_w, w_in, rwkv_mu, rwkv_w0, rwkv_w2, rwkv_a0, rwkv_a2, rwkv_g2, rwkv_k_k, rwkv_k_a, rwkv_r_k, rwkv_ln_w, rwkv_ln_b, q_norm_w, k_norm_w, lambda_q1, lambda_k1, lambda_q2, lambda_k2, diff_subln_w, w_branch_rwkv, w_branch_diff, w_out, norm2_w, router_group_w, router_group_b, router_expert_w, router_expert_b, expert_w_gate, expert_w_up, expert_w_down):
    b, seq, d = x.shape
    depth = norm1_w.shape[0]
    assert depth == 1, "the combine step emits the final output; deeper stacks need an intermediate form"
    lp = FRONT_PAD + N_META + seq
    meta = jnp.broadcast_to(meta_tokens[None].astype(x.dtype), (b, N_META, d))
    h0 = jnp.concatenate([jnp.zeros((b, FRONT_PAD, d), x.dtype), meta, x], axis=1).reshape(b * lp, d)
    proj_tm = 1024 if (b * lp) % 1024 == 0 else 128
    tok_tm = 256 if (b * lp) % 256 == 0 else 128
    l = 0
    h1, ic, dest1, dest2, yb = _layer(
        h0, lp, l, norm1_w[l], w_in[l], rwkv_mu[l], rwkv_w0[l], rwkv_w2[l], rwkv_a0[l], rwkv_a2[l],
        rwkv_g2[l], rwkv_k_k[l], rwkv_k_a[l], rwkv_r_k[l], rwkv_ln_w[l], rwkv_ln_b[l], q_norm_w[l],
        k_norm_w[l], lambda_q1[l], lambda_k1[l], lambda_q2[l], lambda_k2[l], diff_subln_w[l],
        w_branch_rwkv[l], w_branch_diff[l], w_out[l], norm2_w[l], router_group_w[l], router_group_b[l],
        router_expert_w[l], router_expert_b[l], expert_w_gate[l], expert_w_up[l], expert_w_down[l],
        proj_tm, tok_tm)
    return _combine(dest1, dest2, h1, ic, yb, b, lp, OUT_TILE)
```

```python
import functools
import math

import jax
import jax.numpy as jnp
from jax import lax
from jax.experimental import pallas as pl
from jax.experimental.pallas import tpu as pltpu

F32 = jnp.float32
BF16 = jnp.bfloat16

D_MODEL = 1024
N_META = 16
NORM_EPS = 1e-6
RWKV_HEADS = 16
RWKV_HEAD_DIM = 64
RWKV_WIDTH = RWKV_HEADS * RWKV_HEAD_DIM
DECAY_LORA = 64
AAA_LORA = 64
GATE_LORA = 128
LORA_COLS = DECAY_LORA + AAA_LORA + GATE_LORA
RWKV_GN_EPS = 64e-5
DIFF_HEADS = 8
DIFF_HEAD_DIM = 64
DIFF_V_DIM = 2 * DIFF_HEAD_DIM
DIFF_QK_WIDTH = DIFF_HEADS * 2 * DIFF_HEAD_DIM
DIFF_V_WIDTH = DIFF_HEADS * DIFF_V_DIM
ROPE_THETA = 500000.0
ROPE_DIM = DIFF_HEAD_DIM // 4
N_GROUPS = 4
EXPERTS_PER_GROUP = 8
N_EXPERTS = N_GROUPS * EXPERTS_PER_GROUP
EXPERT_FF = 512

LANES = 128
ATTN_BLOCK = 256
ATTN_KEY_BLOCK = 512
ATTN_HEADS_PER_STEP = 2
FRONT_PAD = ATTN_BLOCK - N_META
RWKV_CHUNK = 64
RWKV_SUBCHUNKS = 2
OUT_TILE = 128
MOE_BLOCK = 256
NEG_BIG = -1e30
VMEM_LIMIT = 48 * 1024 * 1024

COL_GATE = 0
COL_DIFF = 2 * D_MODEL
COL_RWKV = COL_DIFF + 2 * DIFF_QK_WIDTH + DIFF_V_WIDTH
COL_LORA = COL_RWKV + 3 * RWKV_WIDTH
IN_COLS = COL_LORA + LORA_COLS


def _dot(a, b):
    return jnp.dot(a, b, preferred_element_type=F32)


def _dot_nt(a, b):
    return lax.dot_general(a, b, (((1,), (1,)), ((), ())), preferred_element_type=F32)


def _split2(x):
    hi = x.astype(BF16)
    lo = (x - hi.astype(F32)).astype(BF16)
    return hi, lo


def _mm3(a, b_ref):
    ah, al = _split2(a)
    return _dot(ah, b_ref[0]) + _dot(ah, b_ref[1]) + _dot(al, b_ref[0])


def _hi_lo(w):
    hi = w.astype(BF16)
    return jnp.stack([hi, (w - hi.astype(F32)).astype(BF16)])


def _sigmoid(x):
    return 1.0 / (1.0 + jnp.exp(-x))


def _head_ones(width=LANES, head=RWKV_HEAD_DIM):
    r = lax.broadcasted_iota(jnp.int32, (width, width), 0) // head
    c = lax.broadcasted_iota(jnp.int32, (width, width), 1) // head
    return jnp.where(r == c, 1.0, 0.0).astype(BF16)


def _seg_sum(x, ones_bd):
    hi, lo = _split2(x)
    return _dot(hi, ones_bd) + _dot(lo, ones_bd)


def _seg_sum_wide(x, ones_bd):
    rows, width = x.shape
    n = width // LANES
    xs = jnp.concatenate([x[:, i * LANES:(i + 1) * LANES] for i in range(n)], axis=0)
    ys = _seg_sum(xs, ones_bd)
    return jnp.concatenate([ys[i * rows:(i + 1) * rows] for i in range(n)], axis=1)


def _proj_kernel(h_ref, nw_ref, w_ref, o_ref, u_ref):
    @pl.when(pl.program_id(1) == 0)
    def _():
        x = h_ref[...]
        ms = jnp.mean(x * x, axis=-1, keepdims=True)
        u_ref[...] = (x * lax.rsqrt(ms + NORM_EPS) * nw_ref[...]).astype(BF16)

    o_ref[...] = _dot(u_ref[...], w_ref[...]).astype(o_ref.dtype)


def _proj(h, norm_w, w_bf16, tm, tn):
    t, d = h.shape
    n = w_bf16.shape[1]
    return pl.pallas_call(
        _proj_kernel,
        out_shape=jax.ShapeDtypeStruct((t, n), BF16),
        grid=(t // tm, n // tn),
        in_specs=[
            pl.BlockSpec((tm, d), lambda i, j: (i, 0)),
            pl.BlockSpec((1, d), lambda i, j: (0, 0)),
            pl.BlockSpec((d, tn), lambda i, j: (0, j)),
        ],
        out_specs=pl.BlockSpec((tm, tn), lambda i, j: (i, j)),
        scratch_shapes=[pltpu.VMEM((tm, d), BF16)],
        compiler_params=pltpu.CompilerParams(
            dimension_semantics=("parallel", "arbitrary"), vmem_limit_bytes=VMEM_LIMIT),
        name="norm_proj",
    )(h, norm_w.reshape(1, d), w_bf16)


def _rwkv_kernel(*refs):
    o_ref = refs[16]
    xr, xk, xv, xl, s_ref = refs[17:22]
    c = pl.program_id(1)
    n_skip = FRONT_PAD // o_ref.shape[1]

    @pl.when(c == 0)
    def _init():
        for xs in (xr, xk, xv, xl):
            xs[0:8, :] = jnp.zeros((8, xs.shape[1]), F32)
        s_ref[...] = jnp.zeros(s_ref.shape, F32)

    @pl.when(c < n_skip)
    def _pad():
        o_ref[...] = jnp.zeros(o_ref.shape, o_ref.dtype)

    @pl.when(c >= n_skip)
    def _chunk():
        _rwkv_chunk(*refs)


def _rwkv_chunk(r_ref, k_ref, v_ref, lo_ref, mu_ref, mul_ref, w0_ref, w2_ref, a0_ref, a2_ref, g2_ref,
                kkw_ref, kaw_ref, rkw_ref, lnw_ref, lnb_ref, o_ref,
                xr, xk, xv, xl, s_ref, kt_s, bt_s, kn_s, rt_s, v_s, y_s, gc_s, bon_s, g_s):
    R = r_ref.shape[1]
    C = RWKV_CHUNK
    n_sub = R // C
    W = RWKV_WIDTH
    n_pairs = W // LANES

    def shift_mix(in_ref, xs, mu):
        x = in_ref[0].astype(F32)
        xs[8:8 + R, :] = x
        prev = xs[7:7 + R, :]
        xs[7:8, :] = x[R - 1:R, :]
        return x + (prev - x) * mu

    r = shift_mix(r_ref, xr, mu_ref[:, 0:W])
    k = shift_mix(k_ref, xk, mu_ref[:, W:2 * W])
    v = shift_mix(v_ref, xv, mu_ref[:, 2 * W:3 * W])
    lo = shift_mix(lo_ref, xl, mul_ref[...])
    xw = jnp.tanh(lo[:, 0:DECAY_LORA])
    xa = lo[:, DECAY_LORA:DECAY_LORA + AAA_LORA]
    xg = _sigmoid(lo[:, DECAY_LORA + AAA_LORA:LORA_COLS])

    z = -(w0_ref[...] + _mm3(xw, w2_ref))
    softplus = jnp.maximum(z, 0.0) + jnp.log(1.0 + jnp.exp(-jnp.abs(z)))
    lw = -jnp.exp(-softplus - 0.5)
    a = _sigmoid(a0_ref[...] + _mm3(xa, a2_ref))
    g_s[...] = _mm3(xg, g2_ref)

    ones_bd = _head_ones()
    kk = k * kkw_ref[...]
    kkn = kk / jnp.maximum(jnp.sqrt(_seg_sum_wide(kk * kk, ones_bd)), 1e-12)
    k2 = k * (1.0 + (a - 1.0) * kaw_ref[...])
    bon_s[...] = _seg_sum_wide(r * k2 * rkw_ref[...], ones_bd) * v

    ti = lax.broadcasted_iota(jnp.int32, (R, R), 0)
    tj = lax.broadcasted_iota(jnp.int32, (R, R), 1)
    ltri = jnp.where((ti >= tj) & (ti // C == tj // C), 1.0, 0.0).astype(BF16)
    l1 = lw.astype(BF16)
    rem = lw - l1.astype(F32)
    l2 = rem.astype(BF16)
    l3 = (rem - l2.astype(F32)).astype(BF16)
    cum = _dot(ltri, l1) + _dot(ltri, l2) + _dot(ltri, l3)
    e_pos = jnp.exp(cum)
    e_neg = jnp.exp(-cum)
    kt_s[...] = kkn * jnp.exp(cum - lw)
    bt_s[...] = kkn * a * e_neg
    kn_s[...] = k2 * e_neg
    rt_s[...] = r * e_pos
    v_s[...] = v
    for s in range(n_sub):
        gc_s[s:s + 1, :] = jnp.exp(cum[(s + 1) * C - 1:(s + 1) * C, :])

    lane = lax.broadcasted_iota(jnp.int32, (1, LANES), 1)
    m0 = jnp.where(lane < RWKV_HEAD_DIM, 1.0, 0.0)
    m1 = 1.0 - m0
    trow = lax.broadcasted_iota(jnp.int32, (C, 2 * C), 0)
    tcol = lax.broadcasted_iota(jnp.int32, (C, 2 * C), 1) % C
    strict = trow > tcol
    incl = trow >= tcol
    eye2 = jnp.where(trow == tcol, 1.0, 0.0)
    br = lax.broadcasted_iota(jnp.int32, (LANES, LANES), 0) // RWKV_HEAD_DIM
    bc = lax.broadcasted_iota(jnp.int32, (LANES, LANES), 1) // RWKV_HEAD_DIM
    bdmask = jnp.where(br == bc, 1.0, 0.0)

    m0b = m0.astype(BF16)
    m1b = m1.astype(BF16)

    def stack2(y):
        yb = y.astype(BF16)
        return jnp.concatenate([yb * m0b, yb * m1b], axis=0)

    pairs = range(n_pairs)
    units = [(s, p) for s in range(n_sub) for p in pairs]
    win = {(s, p): (slice(s * C, (s + 1) * C), slice(p * LANES, (p + 1) * LANES)) for s, p in units}
    kt = {q: kt_s[win[q]] for q in units}
    bt = {q: bt_s[win[q]] for q in units}
    kn = {q: kn_s[win[q]] for q in units}
    rt = {q: rt_s[win[q]] for q in units}
    vv = {q: v_s[win[q]] for q in units}
    gc = {q: gc_s[q[0]:q[0] + 1, win[q][1]] for q in units}
    m_all = {q: _dot_nt(jnp.concatenate([kt[q], rt[q]], axis=0).astype(BF16),
                        jnp.concatenate([stack2(bt[q]), stack2(kn[q])], axis=0))
             for q in units}
    b_mat = {q: jnp.where(strict, m_all[q][0:C, 2 * C:4 * C], 0.0).astype(BF16) for q in units}
    pq_mat = {q: jnp.concatenate([jnp.where(incl, m_all[q][C:2 * C, 0:2 * C], 0.0),
                                  jnp.where(incl, m_all[q][C:2 * C, 2 * C:4 * C], 0.0)], axis=1).astype(BF16)
              for q in units}

    pw = {q: -jnp.where(strict, m_all[q][0:C, 0:2 * C], 0.0) for q in units}
    t_inv = {q: eye2 + pw[q] for q in units}
    for _ in range(int(math.log2(C)) - 1):
        pw = {q: _dot(pw[q].astype(BF16), stack2(pw[q])) for q in units}
        t_inv = {q: t_inv[q] + _dot(t_inv[q].astype(BF16), stack2(pw[q])) for q in units}
    t_inv = {q: t_inv[q].astype(BF16) for q in units}
    v_st = {q: stack2(vv[q]) for q in units}
    xc = {q: jnp.concatenate([bt[q] * gc[q], kn[q] * gc[q]], axis=0).astype(BF16) for q in units}

    state = [s_ref[p] for p in pairs]
    for s in range(n_sub):
        qs = [(s, p) for p in pairs]
        s_bf = [state[p].astype(BF16) for p in pairs]
        rhs = [_dot_nt(kt[q].astype(BF16), s_bf[q[1]]) + _dot(b_mat[q], v_st[q]) for q in qs]
        u = [-_dot(t_inv[q], stack2(rhs[q[1]])) for q in qs]
        for q in qs:
            y_s[win[q]] = _dot_nt(rt[q].astype(BF16), s_bf[q[1]]) + _dot(
                pq_mat[q], jnp.concatenate([stack2(u[q[1]]), v_st[q]], axis=0))
        for q in qs:
            uc = jnp.concatenate([u[q[1]], vv[q]], axis=0)
            state[q[1]] = state[q[1]] * gc[q] + bdmask * _dot(jnp.transpose(uc).astype(BF16), xc[q])
    for p in pairs:
        s_ref[p] = state[p]

    y = y_s[...]
    inv_n = 1.0 / RWKV_HEAD_DIM
    mean = _seg_sum_wide(y, ones_bd) * inv_n
    dlt = y - mean
    var = _seg_sum_wide(dlt * dlt, ones_bd) * inv_n
    yn = dlt * lax.rsqrt(var + RWKV_GN_EPS) * lnw_ref[...] + lnb_ref[...]
    o_ref[0] = ((yn + bon_s[...]) * g_s[...]).astype(o_ref.dtype)


def _rwkv(proj3, mu_rkv, mu_lo, w0, w2, a0, a2, g2, k_k, k_a, r_k, ln_w, ln_b):
    b, lp, _ = proj3.shape
    C = RWKV_CHUNK * RWKV_SUBCHUNKS
    W = RWKV_WIDTH
    cb = COL_RWKV // W
    lb = COL_LORA // LORA_COLS
    row = lambda x: x.reshape(1, -1)
    full = lambda shape: pl.BlockSpec(shape, lambda i, c: (0,) * len(shape))
    wide = pltpu.VMEM((C, W), F32)
    return pl.pallas_call(
        _rwkv_kernel,
        out_shape=jax.ShapeDtypeStruct((b, lp, W), BF16),
        grid=(b, lp // C),
        in_specs=[
            pl.BlockSpec((1, C, W), lambda i, c: (i, c, cb)),
            pl.BlockSpec((1, C, W), lambda i, c: (i, c, cb + 1)),
            pl.BlockSpec((1, C, W), lambda i, c: (i, c, cb + 2)),
            pl.BlockSpec((1, C, LORA_COLS), lambda i, c: (i, c, lb)),
            full((1, 3 * W)), full((1, LORA_COLS)),
            full((1, W)), full((2, DECAY_LORA, W)), full((1, W)), full((2, AAA_LORA, W)), full((2, GATE_LORA, W)),
            full((1, W)), full((1, W)), full((1, W)), full((1, W)), full((1, W)),
        ],
        out_specs=pl.BlockSpec((1, C, W), lambda i, c: (i, c, 0)),
        scratch_shapes=[
            pltpu.VMEM((C + 8, W), F32), pltpu.VMEM((C + 8, W), F32), pltpu.VMEM((C + 8, W), F32),
            pltpu.VMEM((C + 8, LORA_COLS), F32),
            pltpu.VMEM((W // LANES, LANES, LANES), F32),
            wide, wide, wide, wide, wide, wide, pltpu.VMEM((8, W), F32), wide, wide,
        ],
        compiler_params=pltpu.CompilerParams(
            dimension_semantics=("parallel", "arbitrary"), vmem_limit_bytes=VMEM_LIMIT),
        name="rwkv7_time_mix",
    )(proj3, proj3, proj3, proj3, row(mu_rkv), row(mu_lo), row(w0), _hi_lo(w2), row(a0), _hi_lo(a2), _hi_lo(g2),
      row(k_k), row(k_a), row(r_k), row(ln_w), row(ln_b))


def _attn_kernel(q_ref, k_ref, v_ref, cos_ref, s1_ref, s2_ref, qw_ref, kw_ref, lam_ref, sw_ref, o_ref,
                 kp_s, vp_s, m_s, acc_s, *, tq, tk, lambda_init):
    qi = pl.program_id(2)
    lp = k_ref.shape[1]
    n_real = lp - FRONT_PAD
    lk = kp_s.shape[0]
    nh = k_ref.shape[2] // LANES
    vw = 2 * LANES
    heads = range(nh)
    hs = [slice(h * LANES, (h + 1) * LANES) for h in heads]
    ones_bd = _head_ones(LANES, DIFF_HEAD_DIM)
    shift = ROPE_DIM // 2

    def norm_rope(x, w, rows):
        ms = _seg_sum(x * x, ones_bd) * (1.0 / DIFF_HEAD_DIM)
        xn = x * lax.rsqrt(ms + NORM_EPS) * w
        return (xn * cos_ref[rows, :] + pltpu.roll(xn, shift, 1) * s1_ref[rows, :]
                + pltpu.roll(xn, LANES - shift, 1) * s2_ref[rows, :])

    @pl.when(qi == 0)
    def _prep():
        def put(dst, n):
            src = pl.ds(FRONT_PAD + dst, n)
            rows = pl.ds(dst, n)
            for h in heads:
                kp_s[rows, hs[h]] = norm_rope(k_ref[0, src, hs[h]].astype(F32), kw_ref[...], src).astype(BF16)
                vp_s[rows, h * vw:h * vw + LANES] = v_ref[0, src, hs[h]].astype(BF16)
                vp_s[rows, h * vw + LANES:(h + 1) * vw] = jnp.ones((n, LANES), BF16)

        def body(i, carry):
            put(pl.multiple_of(i * LANES, LANES), LANES)
            return carry
        lax.fori_loop(0, n_real // LANES, body, 0, unroll=2)
        tail = n_real % LANES
        if tail:
            put(n_real - tail, tail)
        if lk > n_real:
            kp_s[n_real:lk, :] = jnp.zeros((lk - n_real, nh * LANES), BF16)
            vp_s[n_real:lk, :] = jnp.zeros((lk - n_real, nh * vw), BF16)

    lane = lax.broadcasted_iota(jnp.int32, (1, LANES), 1)
    m0 = jnp.where(lane < DIFF_HEAD_DIM, 1.0, 0.0)
    m1 = 1.0 - m0
    rows_q = pl.ds(pl.multiple_of(qi * tq, tq), tq)
    qs = []
    for h in heads:
        qn = norm_rope(q_ref[0, :, hs[h]].astype(F32), qw_ref[...], rows_q) * (DIFF_HEAD_DIM ** -0.5)
        qs.append(jnp.concatenate([qn * m0, qn * m1], axis=0).astype(BF16))
    m_s[...] = jnp.full(m_s.shape, NEG_BIG, F32)
    acc_s[...] = jnp.zeros(acc_s.shape, F32)
    first_row = qi * tq - FRONT_PAD
    row = first_row + lax.broadcasted_iota(jnp.int32, (2 * tq, LANES), 0) % tq
    col0 = lax.broadcasted_iota(jnp.int32, (2 * tq, LANES), 1)
    n_sub = tk // LANES

    def step(j, causal):
        start = j * tk
        if not isinstance(j, int):
            start = pl.multiple_of(start, tk)
        ks = pl.ds(start, tk)
        s = [_dot_nt(qs[h], kp_s[ks, hs[h]]) for h in heads]
        sub = [[s[h][:, c * LANES:(c + 1) * LANES] for c in range(n_sub)] for h in heads]
        if causal:
            sub = [[jnp.where(start + c * LANES + col0 <= row, sub[h][c], NEG_BIG) for c in range(n_sub)]
                   for h in heads]
        m_old = [m_s[h] for h in heads]
        m_new = []
        for h in heads:
            mx = functools.reduce(jnp.maximum, sub[h])
            m_new.append(jnp.maximum(m_old[h], jnp.broadcast_to(
                jnp.max(mx, axis=1, keepdims=True), mx.shape)))
        p = [jnp.concatenate([jnp.exp(sub[h][c] - m_new[h]) for c in range(n_sub)], axis=1).astype(BF16)
             for h in heads]
        pv = [_dot(p[h], vp_s[ks, h * vw:(h + 1) * vw]) for h in heads]
        for h in heads:
            alpha = jnp.exp(m_old[h] - m_new[h])
            acc_s[h] = jnp.concatenate([alpha, alpha], axis=1) * acc_s[h] + pv[h]
            m_s[h] = m_new[h]

    n_full = jnp.maximum(first_row + 1, 0) // tk
    last = jnp.maximum(first_row + tq - 1, 0) // tk

    def mid(j, carry):
        step(j, False)
        return carry
    lax.fori_loop(0, n_full, mid, 0)
    step(n_full, True)

    @pl.when(last > n_full)
    def _diag():
        step(last, True)

    lam = (jnp.exp(jnp.sum(lam_ref[0:1, :] * lam_ref[1:2, :], axis=1, keepdims=True))
           - jnp.exp(jnp.sum(lam_ref[2:3, :] * lam_ref[3:4, :], axis=1, keepdims=True)) + lambda_init)
    for h in heads:
        acc = acc_s[h]
        o = acc[:, 0:LANES] / acc[:, LANES:vw]
        od = o[0:tq] - lam * o[tq:2 * tq]
        ms = jnp.mean(od * od, axis=1, keepdims=True)
        o_ref[0, :, hs[h]] = (od * lax.rsqrt(ms + NORM_EPS) * sw_ref[...]
                              * (1.0 - lambda_init)).astype(o_ref.dtype)


def _rope_tables(lp):
    half = ROPE_DIM // 2
    inv_freq = jnp.exp(-math.log(ROPE_THETA) * jnp.arange(half, dtype=F32) * 2.0 / ROPE_DIM)
    pos = (jnp.arange(lp) - FRONT_PAD).astype(F32)
    ang = pos[:, None] * inv_freq[None, :]
    cos, sin = jnp.cos(ang), jnp.sin(ang)
    one = jnp.ones((lp, DIFF_HEAD_DIM - ROPE_DIM), F32)
    zero = jnp.zeros((lp, DIFF_HEAD_DIM - ROPE_DIM), F32)
    zh = jnp.zeros((lp, half), F32)
    c = jnp.concatenate([cos, cos, one], axis=1)
    s1 = jnp.concatenate([zh, sin, zero], axis=1)
    s2 = jnp.concatenate([-sin, zh, zero], axis=1)
    dup = lambda t: jnp.concatenate([t, t], axis=1)
    return dup(c), dup(s1), dup(s2)


def _attention(proj3, q_norm_w, k_norm_w, lam4, subln_w, lambda_init):
    b, lp, _ = proj3.shape
    tq, tk, nh = ATTN_BLOCK, ATTN_KEY_BLOCK, ATTN_HEADS_PER_STEP
    lk = -(-(lp - FRONT_PAD) // tk) * tk
    hw = nh * LANES
    qb = COL_DIFF // hw
    kb = qb + DIFF_QK_WIDTH // hw
    vb = kb + DIFF_QK_WIDTH // hw
    cos, s1, s2 = _rope_tables(lp)
    dup = lambda w: jnp.concatenate([w, w]).reshape(1, LANES)
    full = lambda shape: pl.BlockSpec(shape, lambda i, h, q: (0,) * len(shape))
    return pl.pallas_call(
        functools.partial(_attn_kernel, tq=tq, tk=tk, lambda_init=lambda_init),
        out_shape=jax.ShapeDtypeStruct((b, lp, DIFF_V_WIDTH), BF16),
        grid=(b, DIFF_HEADS // nh, lp // tq),
        in_specs=[
            pl.BlockSpec((1, tq, hw), lambda i, h, q: (i, q, qb + h)),
            pl.BlockSpec((1, lp, hw), lambda i, h, q: (i, 0, kb + h)),
            pl.BlockSpec((1, lp, hw), lambda i, h, q: (i, 0, vb + h)),
            full((lp, LANES)), full((lp, LANES)), full((lp, LANES)),
            full((1, LANES)), full((1, LANES)), full((4, DIFF_HEAD_DIM)), full((1, LANES)),
        ],
        out_specs=pl.BlockSpec((1, tq, hw), lambda i, h, q: (i, q, h)),
        scratch_shapes=[pltpu.VMEM((lk, hw), BF16), pltpu.VMEM((lk, 2 * hw), BF16),
                        pltpu.VMEM((nh, 2 * tq, LANES), F32), pltpu.VMEM((nh, 2 * tq, 2 * LANES), F32)],
        compiler_params=pltpu.CompilerParams(
            dimension_semantics=("parallel", "parallel", "arbitrary"), vmem_limit_bytes=VMEM_LIMIT),
        name="diff_attention",
    )(proj3, proj3, proj3, cos, s1, s2, dup(q_norm_w), dup(k_norm_w), lam4, subln_w.reshape(1, LANES))


def _merge_kernel(rw_ref, da_ref, g1_ref, g2_ref, h_ref, wbr_ref, wbd_ref, wo_ref, n2_ref, wr_ref, br_ref,
                  h1_ref, u2_ref, il_ref, ic_ref, cnt_ref, base_s, *, lp):
    i = pl.program_id(0)
    tm = rw_ref.shape[0]

    @pl.when(i == 0)
    def _():
        base_s[...] = jnp.zeros(base_s.shape, F32)

    y1 = _dot(rw_ref[...], wbr_ref[...])
    y2 = _dot(da_ref[...], wbd_ref[...])
    merged = _sigmoid(g1_ref[...].astype(F32)) * y1 + _sigmoid(g2_ref[...].astype(F32)) * y2
    h1 = h_ref[...] + _dot(merged.astype(BF16), wo_ref[...])
    h1_ref[...] = h1
    u2 = h1 * lax.rsqrt(jnp.mean(h1 * h1, axis=-1, keepdims=True) + NORM_EPS) * n2_ref[...]
    u2_ref[...] = u2

    uh, ul = _split2(u2)
    wh, wl = _split2(wr_ref[...])
    lt = _dot_nt(wh, uh) + _dot_nt(wh, ul) + _dot_nt(wl, uh) + br_ref[...]

    gi8 = lax.broadcasted_iota(jnp.int32, (8, tm), 0)
    lg = lt[0:8]
    ge = jnp.exp(lg - jnp.max(lg, axis=0, keepdims=True))
    gp = ge / jnp.sum(ge, axis=0, keepdims=True)
    gv = jnp.max(gp, axis=0, keepdims=True)
    gidx = jnp.min(jnp.where(gp == gv, gi8, N_EXPERTS), axis=0, keepdims=True)

    ei = lax.broadcasted_iota(jnp.int32, (N_EXPERTS, tm), 0)
    sel = (ei // EXPERTS_PER_GROUP) == gidx
    le = jnp.where(sel, lt[8:8 + N_EXPERTS], NEG_BIG)
    ee = jnp.where(sel, jnp.exp(le - jnp.max(le, axis=0, keepdims=True)), 0.0)
    ep = jnp.where(sel, ee / jnp.sum(ee, axis=0, keepdims=True), -1.0)
    v1 = jnp.max(ep, axis=0, keepdims=True)
    i1 = jnp.min(jnp.where(ep == v1, ei, N_EXPERTS), axis=0, keepdims=True)
    ep2 = jnp.where(ei == i1, -1.0, ep)
    v2 = jnp.max(ep2, axis=0, keepdims=True)
    i2 = jnp.min(jnp.where(ep2 == v2, ei, N_EXPERTS), axis=0, keepdims=True)
    den = v1 + v2
    gate1 = gv * v1 / den
    gate2 = gv * v2 / den

    tok = (i * tm + lax.broadcasted_iota(jnp.int32, (1, tm), 1)).astype(F32)
    pos = tok - jnp.floor((tok + 0.5) / lp) * lp
    valid = pos > (FRONT_PAD - 0.5)

    oh1 = jnp.where((ei == i1) & valid, 1.0, 0.0)
    oh2 = jnp.where((ei == i2) & valid, 1.0, 0.0)
    oh = oh1 + oh2
    ur = lax.broadcasted_iota(jnp.int32, (tm, tm), 0)
    uc = lax.broadcasted_iota(jnp.int32, (tm, tm), 1)
    before = jnp.where(ur < uc, 1.0, 0.0).astype(BF16)
    tot = base_s[:, 0:1] + _dot(oh.astype(BF16), before)
    rank1 = jnp.sum(oh1 * tot, axis=0, keepdims=True)
    rank2 = jnp.sum(oh2 * tot, axis=0, keepdims=True)
    base_s[...] = base_s[...] + jnp.sum(oh, axis=1, keepdims=True)
    cnt_ref[...] = base_s[...]

    il = jnp.where(gi8 == 0, i1, jnp.where(gi8 == 1, i2, jnp.where(
        gi8 == 2, rank1.astype(jnp.int32), jnp.where(gi8 == 3, rank2.astype(jnp.int32), jnp.where(
            gi8 == 4, valid.astype(jnp.int32), 0)))))
    il_ref[...] = il
    ri = lax.broadcasted_iota(jnp.int32, (LANES, tm), 0)
    ic = jnp.where(ri == 0, gate1, jnp.where(ri == 1, gate2, 0.0))
    ic_ref[...] = jnp.transpose(ic)


def _merge(rw, da, proj, h0, wbr, wbd, wo, norm2_w, wr, br, lp, tm):
    t, d = h0.shape
    gb = COL_GATE // d
    full = lambda shape: pl.BlockSpec(shape, lambda i: (0,) * len(shape))
    tile = lambda c: pl.BlockSpec((tm, d), lambda i: (i, c))
    return pl.pallas_call(
        functools.partial(_merge_kernel, lp=lp),
        out_shape=(
            jax.ShapeDtypeStruct((t, d), F32),
            jax.ShapeDtypeStruct((t, d), F32),
            jax.ShapeDtypeStruct((8, t), jnp.int32),
            jax.ShapeDtypeStruct((t, LANES), F32),
            jax.ShapeDtypeStruct((N_EXPERTS, LANES), F32),
        ),
        grid=(t // tm,),
        in_specs=[tile(0), tile(0), tile(gb), tile(gb + 1), tile(0),
                  full((d, d)), full((d, d)), full((d, d)), full((1, d)), full((LANES, d)), full((LANES, 1))],
        out_specs=(
            tile(0), tile(0),
            pl.BlockSpec((8, tm), lambda i: (0, i)),
            pl.BlockSpec((tm, LANES), lambda i: (i, 0)),
            full((N_EXPERTS, LANES)),
        ),
        scratch_shapes=[pltpu.VMEM((N_EXPERTS, LANES), F32)],
        compiler_params=pltpu.CompilerParams(
            dimension_semantics=("arbitrary",), vmem_limit_bytes=VMEM_LIMIT),
        name="merge_router",
    )(rw, da, proj, proj, h0, wbr, wbd, wo, norm2_w.reshape(1, d), wr, br)


def _dispatch_kernel(d1_ref, d2_ref, u_ref, xin_hbm, xb_hbm, sem):
    del xin_hbm
    tm = d1_ref.shape[2]

    def start(r, carry):
        for prio, d_ref in enumerate((d1_ref, d2_ref)):
            pltpu.make_async_copy(u_ref.at[pl.ds(r, 1)], xb_hbm.at[pl.ds(d_ref[0, 0, r], 1)],
                                  sem).start(priority=prio)
        return carry

    lax.fori_loop(0, tm, start, 0, unroll=8)
    for _ in range(2):
        pltpu.make_async_copy(u_ref, xb_hbm.at[pl.ds(0, tm)], sem).wait()


def _dispatch(dest1, dest2, u2, cap, tm):
    t, d = u2.shape
    nt = t // tm
    smem = lambda: pl.BlockSpec((1, 1, tm), lambda i: (i, 0, 0), memory_space=pltpu.SMEM)
    return pl.pallas_call(
        _dispatch_kernel,
        out_shape=jax.ShapeDtypeStruct((cap, d), F32),
        grid=(nt,),
        in_specs=[smem(), smem(), pl.BlockSpec((tm, d), lambda i: (i, 0)), pl.BlockSpec(memory_space=pl.ANY)],
        out_specs=pl.BlockSpec(memory_space=pl.ANY),
        scratch_shapes=[pltpu.SemaphoreType.DMA(())],
        input_output_aliases={3: 0},
        compiler_params=pltpu.CompilerParams(
            dimension_semantics=("arbitrary",), vmem_limit_bytes=VMEM_LIMIT),
        name="moe_dispatch",
    )(dest1.reshape(nt, 1, tm), dest2.reshape(nt, 1, tm), u2, jnp.zeros((cap, d), F32))


def _moe_kernel(be_ref, nb_ref, x_ref, wg_ref, wu_ref, wd_ref, o_ref, wg_s, wu_s, wd_s):
    i = pl.program_id(0)
    used = i < nb_ref[0]

    @pl.when(used & ((i == 0) | (be_ref[i] != be_ref[jnp.maximum(i - 1, 0)])))
    def _():
        wg_s[...] = wg_ref[0].astype(BF16)
        wu_s[...] = wu_ref[0].astype(BF16)
        wd_s[...] = wd_ref[0].astype(BF16)

    @pl.when(used)
    def _():
        x = x_ref[...].astype(BF16)
        hg = _dot(x, wg_s[...])
        hu = _dot(x, wu_s[...])
        hid = hg * _sigmoid(hg) * hu
        o_ref[...] = _dot(hid.astype(BF16), wd_s[...])

    @pl.when(i >= nb_ref[0])
    def _():
        o_ref[...] = jnp.zeros(o_ref.shape, F32)


def _moe(block_e, n_used, xb, wg, wu, wd):
    d = xb.shape[1]
    ff = wg.shape[2]
    bm = MOE_BLOCK
    cap = block_e.shape[0] * bm
    return pl.pallas_call(
        _moe_kernel,
        out_shape=jax.ShapeDtypeStruct((cap, d), F32),
        grid_spec=pltpu.PrefetchScalarGridSpec(
            num_scalar_prefetch=2,
            grid=(cap // bm,),
            in_specs=[
                pl.BlockSpec((bm, d), lambda i, be, nb: (i, 0)),
                pl.BlockSpec((1, d, ff), lambda i, be, nb: (be[i], 0, 0)),
                pl.BlockSpec((1, d, ff), lambda i, be, nb: (be[i], 0, 0)),
                pl.BlockSpec((1, ff, d), lambda i, be, nb: (be[i], 0, 0)),
            ],
            out_specs=pl.BlockSpec((bm, d), lambda i, be, nb: (i, 0)),
            scratch_shapes=[pltpu.VMEM((d, ff), BF16), pltpu.VMEM((d, ff), BF16), pltpu.VMEM((ff, d), BF16)],
        ),
        compiler_params=pltpu.CompilerParams(
            dimension_semantics=("arbitrary",), vmem_limit_bytes=VMEM_LIMIT),
        name="moe_experts",
    )(block_e, n_used, xb, wg, wu, wd)


def _combine_kernel(d1_ref, d2_ref, h_ref, ic_ref, yb_hbm, o_ref, ga, gb, sem):
    tm = h_ref.shape[0]

    def start(r, carry):
        for prio, (d_ref, buf) in enumerate(((d1_ref, ga), (d2_ref, gb))):
            pltpu.make_async_copy(yb_hbm.at[pl.ds(d_ref[0, 0, r], 1)], buf.at[pl.ds(r, 1)],
                                  sem).start(priority=prio)
        return carry

    lax.fori_loop(0, tm, start, 0, unroll=8)
    for buf in (ga, gb):
        pltpu.make_async_copy(yb_hbm.at[pl.ds(0, tm)], buf, sem).wait()
    ic = ic_ref[...]
    o_ref[0] = h_ref[...] + ic[:, 0:1] * ga[...] + ic[:, 1:2] * gb[...]


def _combine(dest1, dest2, h1, ic, yb, b, lp, tm):
    t, d = h1.shape
    per = lp // tm
    first = (FRONT_PAD + N_META) // tm
    nt = t // tm
    smem = lambda: pl.BlockSpec((1, 1, tm), lambda i, j: (i * per + j + first, 0, 0), memory_space=pltpu.SMEM)
    return pl.pallas_call(
        _combine_kernel,
        out_shape=jax.ShapeDtypeStruct((b, lp - FRONT_PAD - N_META, d), F32),
        grid=(b, per - first),
        in_specs=[
            smem(), smem(),
            pl.BlockSpec((tm, d), lambda i, j: (i * per + j + first, 0)),
            pl.BlockSpec((tm, LANES), lambda i, j: (i * per + j + first, 0)),
            pl.BlockSpec(memory_space=pl.ANY),
        ],
        out_specs=pl.BlockSpec((1, tm, d), lambda i, j: (i, j, 0)),
        scratch_shapes=[pltpu.VMEM((tm, d), F32), pltpu.VMEM((tm, d), F32), pltpu.SemaphoreType.DMA(())],
        compiler_params=pltpu.CompilerParams(
            dimension_semantics=("arbitrary", "arbitrary"), vmem_limit_bytes=VMEM_LIMIT),
        name="moe_combine",
    )(dest1.reshape(nt, 1, tm), dest2.reshape(nt, 1, tm), h1, ic, yb)


def _routing_tables(il, cnt, n_blocks, tm):
    bm = MOE_BLOCK
    counts = cnt[:, 0].astype(jnp.int32)
    padded = (counts + bm - 1) // bm * bm
    pad_end = jnp.cumsum(padded)
    pad_start = pad_end - padded
    valid = il[4] > 0
    spare = n_blocks * bm + jnp.arange(il.shape[1], dtype=jnp.int32) % tm
    dest1 = jnp.where(valid, pad_start[il[0]] + il[2], spare).astype(jnp.int32)
    dest2 = jnp.where(valid, pad_start[il[1]] + il[3], spare + tm).astype(jnp.int32)
    starts = jnp.arange(n_blocks, dtype=jnp.int32) * bm
    block_e = jnp.minimum(jnp.sum((pad_end[None, :] <= starts[:, None]).astype(jnp.int32), axis=1),
                          N_EXPERTS - 1)
    n_used = (pad_end[-1:] // bm).astype(jnp.int32)
    return dest1, dest2, block_e, n_used


def _layer(h0, lp, l, norm1_w, w_in, rwkv_mu, rwkv_w0, rwkv_w2, rwkv_a0, rwkv_a2, rwkv_g2,
           rwkv_k_k, rwkv_k_a, rwkv_r_k, rwkv_ln_w, rwkv_ln_b, q_norm_w, k_norm_w,
           lambda_q1, lambda_k1, lambda_q2, lambda_k2, diff_subln_w, w_branch_rwkv, w_branch_diff,
           w_out, norm2_w, router_group_w, router_group_b, router_expert_w, router_expert_b,
           expert_w_gate, expert_w_up, expert_w_down, proj_tm, tok_tm):
    t, d = h0.shape
    b = t // lp
    lambda_init = 0.8 - 0.6 * math.exp(-0.3 * l)
    rw_cols = 3 * RWKV_WIDTH
    diff_cols = 2 * DIFF_QK_WIDTH + DIFF_V_WIDTH
    w_perm = jnp.concatenate([
        w_in[:, rw_cols + LORA_COLS + diff_cols:],
        w_in[:, rw_cols + LORA_COLS:rw_cols + LORA_COLS + diff_cols],
        w_in[:, :rw_cols + LORA_COLS],
    ], axis=1).astype(BF16)
    proj = _proj(h0, norm1_w, w_perm, proj_tm, 768)
    proj3 = proj.reshape(b, lp, IN_COLS)

    rw = _rwkv(proj3, rwkv_mu[:rw_cols], rwkv_mu[rw_cols:], rwkv_w0, rwkv_w2, rwkv_a0, rwkv_a2, rwkv_g2,
               rwkv_k_k, rwkv_k_a, rwkv_r_k.reshape(-1), rwkv_ln_w, rwkv_ln_b)
    lam4 = jnp.stack([lambda_q1, lambda_k1, lambda_q2, lambda_k2])
    da = _attention(proj3, q_norm_w, k_norm_w, lam4, diff_subln_w, lambda_init)

    wr = jnp.zeros((LANES, d), F32).at[0:N_GROUPS].set(router_group_w.T).at[8:8 + N_EXPERTS].set(router_expert_w.T)
    br = jnp.zeros((LANES,), F32).at[0:N_GROUPS].set(router_group_b).at[N_GROUPS:8].set(NEG_BIG)
    br = br.at[8:8 + N_EXPERTS].set(router_expert_b).reshape(LANES, 1)
    h1, u2, il, ic, cnt = _merge(
        rw.reshape(t, RWKV_WIDTH), da.reshape(t, DIFF_V_WIDTH), proj, h0,
        w_branch_rwkv.astype(BF16), w_branch_diff.astype(BF16), w_out.astype(BF16), norm2_w, wr, br, lp, tok_tm)

    n_real = b * (lp - FRONT_PAD)
    n_blocks = -(-(2 * n_real) // MOE_BLOCK) + N_EXPERTS
    dest1, dest2, block_e, n_used = _routing_tables(il, cnt, n_blocks, tok_tm)
    xb = _dispatch(dest1, dest2, u2, n_blocks * MOE_BLOCK + 2 * tok_tm, tok_tm)
    yb = _moe(block_e, n_used, xb, expert_w_gate, expert_w_up, expert_w_down)
    return h1, ic, dest1, dest2, yb


def kernel(x, meta_tokens, norm1_w, w_in, rwkv_mu, rwkv_w0, rwkv_w2, rwkv_a0, rwkv_a2, rwkv_g2, rwkv_k_k, rwkv_k_a, rwkv_r_k, rwkv_ln_w, rwkv_ln_b, q_norm_w, k_norm_w, lambda_q1, lambda_k1, lambda_q2, lambda_k2, diff_subln_w, w_branch_rwkv, w_branch_diff, w_out, norm2_w, router_group_w, router_group_b, router_expert_w, router_expert_b, expert_w_gate, expert_w_up, expert_w_down):
    b, seq, d = x.shape
    depth = norm1_w.shape[0]
    assert depth == 1, "the combine step emits the final output; deeper stacks need an intermediate form"
    lp = FRONT_PAD + N_META + seq
    meta = jnp.broadcast_to(meta_tokens[None].astype(x.dtype), (b, N_META, d))
    h0 = jnp.concatenate([jnp.zeros((b, FRONT_PAD, d), x.dtype), meta, x], axis=1).reshape(b * lp, d)
    proj_tm = 2048 if (b * lp) % 2048 == 0 else 128
    tok_tm = 512 if (b * lp) % 512 == 0 else 128
    l = 0
    h1, ic, dest1, dest2, yb = _layer(
        h0, lp, l, norm1_w[l], w_in[l], rwkv_mu[l], rwkv_w0[l], rwkv_w2[l], rwkv_a0[l], rwkv_a2[l],
        rwkv_g2[l], rwkv_k_k[l], rwkv_k_a[l], rwkv_r_k[l], rwkv_ln_w[l], rwkv_ln_b[l], q_norm_w[l],
        k_norm_w[l], lambda_q1[l], lambda_k1[l], lambda_q2[l], lambda_k2[l], diff_subln_w[l],
        w_branch_rwkv[l], w_branch_diff[l], w_out[l], norm2_w[l], router_group_w[l], router_group_b[l],
        router_expert_w[l], router_expert_b[l], expert_w_gate[l], expert_w_up[l], expert_w_down[l],
        proj_tm, tok_tm)
    return _combine(dest1, dest2, h1, ic, yb, b, lp, OUT_TILE)
```

```python
import functools
import math

import jax
import jax.numpy as jnp
from jax import lax
from jax.experimental import pallas as pl
from jax.experimental.pallas import tpu as pltpu

F32 = jnp.float32
BF16 = jnp.bfloat16

D_MODEL = 1024
N_META = 16
NORM_EPS = 1e-6
RWKV_HEADS = 16
RWKV_HEAD_DIM = 64
RWKV_WIDTH = RWKV_HEADS * RWKV_HEAD_DIM
DECAY_LORA = 64
AAA_LORA = 64
GATE_LORA = 128
LORA_COLS = DECAY_LORA + AAA_LORA + GATE_LORA
RWKV_GN_EPS = 64e-5
DIFF_HEADS = 8
DIFF_HEAD_DIM = 64
DIFF_V_DIM = 2 * DIFF_HEAD_DIM
DIFF_QK_WIDTH = DIFF_HEADS * 2 * DIFF_HEAD_DIM
DIFF_V_WIDTH = DIFF_HEADS * DIFF_V_DIM
ROPE_THETA = 500000.0
ROPE_DIM = DIFF_HEAD_DIM // 4
N_GROUPS = 4
EXPERTS_PER_GROUP = 8
N_EXPERTS = N_GROUPS * EXPERTS_PER_GROUP
EXPERT_FF = 512

LANES = 128
ATTN_BLOCK = 256
ATTN_KEY_BLOCK = 512
ATTN_HEADS_PER_STEP = 4
FRONT_PAD = ATTN_BLOCK - N_META
RWKV_CHUNK = 64
RWKV_SUBCHUNKS = 3
OUT_TILE = 128
MOE_BLOCK = 256
NEG_BIG = -1e30
VMEM_LIMIT = 48 * 1024 * 1024

COL_GATE = 0
COL_DIFF = 2 * D_MODEL
COL_RWKV = COL_DIFF + 2 * DIFF_QK_WIDTH + DIFF_V_WIDTH
COL_LORA = COL_RWKV + 3 * RWKV_WIDTH
IN_COLS = COL_LORA + LORA_COLS


def _dot(a, b):
    return jnp.dot(a, b, preferred_element_type=F32)


def _dot_nt(a, b):
    return lax.dot_general(a, b, (((1,), (1,)), ((), ())), preferred_element_type=F32)


def _split2(x):
    hi = x.astype(BF16)
    lo = (x - hi.astype(F32)).astype(BF16)
    return hi, lo


def _mm3(a, b_ref):
    ah, al = _split2(a)
    return _dot(ah, b_ref[0]) + _dot(ah, b_ref[1]) + _dot(al, b_ref[0])


def _hi_lo(w):
    hi = w.astype(BF16)
    return jnp.stack([hi, (w - hi.astype(F32)).astype(BF16)])


def _sigmoid(x):
    return 1.0 / (1.0 + jnp.exp(-x))


def _head_ones(width=LANES, head=RWKV_HEAD_DIM):
    r = lax.broadcasted_iota(jnp.int32, (width, width), 0) // head
    c = lax.broadcasted_iota(jnp.int32, (width, width), 1) // head
    return jnp.where(r == c, 1.0, 0.0).astype(BF16)


def _seg_sum(x, ones_bd):
    hi, lo = _split2(x)
    return _dot(hi, ones_bd) + _dot(lo, ones_bd)


def _seg_sum_wide(x, ones_bd):
    rows, width = x.shape
    n = width // LANES
    xs = jnp.concatenate([x[:, i * LANES:(i + 1) * LANES] for i in range(n)], axis=0)
    ys = _seg_sum(xs, ones_bd)
    return jnp.concatenate([ys[i * rows:(i + 1) * rows] for i in range(n)], axis=1)


def _proj_kernel(h_ref, nw_ref, w_ref, o_ref, u_ref):
    @pl.when(pl.program_id(1) == 0)
    def _():
        x = h_ref[...]
        ms = jnp.mean(x * x, axis=-1, keepdims=True)
        u_ref[...] = (x * lax.rsqrt(ms + NORM_EPS) * nw_ref[...]).astype(BF16)

    o_ref[...] = _dot(u_ref[...], w_ref[...]).astype(o_ref.dtype)


def _proj(h, norm_w, w_bf16, tm, tn):
    t, d = h.shape
    n = w_bf16.shape[1]
    return pl.pallas_call(
        _proj_kernel,
        out_shape=jax.ShapeDtypeStruct((t, n), BF16),
        grid=(t // tm, n // tn),
        in_specs=[
            pl.BlockSpec((tm, d), lambda i, j: (i, 0)),
            pl.BlockSpec((1, d), lambda i, j: (0, 0)),
            pl.BlockSpec((d, tn), lambda i, j: (0, j)),
        ],
        out_specs=pl.BlockSpec((tm, tn), lambda i, j: (i, j)),
        scratch_shapes=[pltpu.VMEM((tm, d), BF16)],
        compiler_params=pltpu.CompilerParams(
            dimension_semantics=("parallel", "arbitrary"), vmem_limit_bytes=VMEM_LIMIT),
        name="norm_proj",
    )(h, norm_w.reshape(1, d), w_bf16)


def _rwkv_kernel(*refs):
    o_ref = refs[16]
    xr, xk, xv, xl, s_ref = refs[17:22]
    c = pl.program_id(1)
    n_skip = FRONT_PAD // o_ref.shape[1]

    @pl.when(c == 0)
    def _init():
        for xs in (xr, xk, xv, xl):
            xs[0:8, :] = jnp.zeros((8, xs.shape[1]), F32)
        s_ref[...] = jnp.zeros(s_ref.shape, F32)

    @pl.when(c < n_skip)
    def _pad():
        o_ref[...] = jnp.zeros(o_ref.shape, o_ref.dtype)

    @pl.when(c >= n_skip)
    def _chunk():
        _rwkv_chunk(*refs)


def _rwkv_chunk(r_ref, k_ref, v_ref, lo_ref, mu_ref, mul_ref, w0_ref, w2_ref, a0_ref, a2_ref, g2_ref,
                kkw_ref, kaw_ref, rkw_ref, lnw_ref, lnb_ref, o_ref,
                xr, xk, xv, xl, s_ref, kt_s, bt_s, kn_s, rt_s, v_s, y_s, gc_s, bon_s, g_s):
    R = r_ref.shape[1]
    C = RWKV_CHUNK
    n_sub = R // C
    W = RWKV_WIDTH
    n_pairs = W // LANES

    def shift_mix(in_ref, xs, mu):
        x = in_ref[0].astype(F32)
        xs[8:8 + R, :] = x
        prev = xs[7:7 + R, :]
        xs[7:8, :] = x[R - 1:R, :]
        return x + (prev - x) * mu

    r = shift_mix(r_ref, xr, mu_ref[:, 0:W])
    k = shift_mix(k_ref, xk, mu_ref[:, W:2 * W])
    v = shift_mix(v_ref, xv, mu_ref[:, 2 * W:3 * W])
    lo = shift_mix(lo_ref, xl, mul_ref[...])
    xw = jnp.tanh(lo[:, 0:DECAY_LORA])
    xa = lo[:, DECAY_LORA:DECAY_LORA + AAA_LORA]
    xg = _sigmoid(lo[:, DECAY_LORA + AAA_LORA:LORA_COLS])

    z = -(w0_ref[...] + _mm3(xw, w2_ref))
    softplus = jnp.maximum(z, 0.0) + jnp.log(1.0 + jnp.exp(-jnp.abs(z)))
    lw = -jnp.exp(-softplus - 0.5)
    a = _sigmoid(a0_ref[...] + _mm3(xa, a2_ref))
    g_s[...] = _mm3(xg, g2_ref)

    ones_bd = _head_ones()
    kk = k * kkw_ref[...]
    kkn = kk / jnp.maximum(jnp.sqrt(_seg_sum_wide(kk * kk, ones_bd)), 1e-12)
    k2 = k * (1.0 + (a - 1.0) * kaw_ref[...])
    bon_s[...] = _seg_sum_wide(r * k2 * rkw_ref[...], ones_bd) * v

    ti = lax.broadcasted_iota(jnp.int32, (R, R), 0)
    tj = lax.broadcasted_iota(jnp.int32, (R, R), 1)
    ltri = jnp.where((ti >= tj) & (ti // C == tj // C), 1.0, 0.0).astype(BF16)
    l1 = lw.astype(BF16)
    rem = lw - l1.astype(F32)
    l2 = rem.astype(BF16)
    l3 = (rem - l2.astype(F32)).astype(BF16)
    cum = _dot(ltri, l1) + _dot(ltri, l2) + _dot(ltri, l3)
    e_pos = jnp.exp(cum)
    e_neg = jnp.exp(-cum)
    kt_s[...] = kkn * jnp.exp(cum - lw)
    bt_s[...] = kkn * a * e_neg
    kn_s[...] = k2 * e_neg
    rt_s[...] = r * e_pos
    v_s[...] = v
    for s in range(n_sub):
        gc_s[s:s + 1, :] = jnp.exp(cum[(s + 1) * C - 1:(s + 1) * C, :])

    lane = lax.broadcasted_iota(jnp.int32, (1, LANES), 1)
    m0 = jnp.where(lane < RWKV_HEAD_DIM, 1.0, 0.0)
    m1 = 1.0 - m0
    trow = lax.broadcasted_iota(jnp.int32, (C, 2 * C), 0)
    tcol = lax.broadcasted_iota(jnp.int32, (C, 2 * C), 1) % C
    strict = trow > tcol
    incl = trow >= tcol
    eye2 = jnp.where(trow == tcol, 1.0, 0.0)
    br = lax.broadcasted_iota(jnp.int32, (LANES, LANES), 0) // RWKV_HEAD_DIM
    bc = lax.broadcasted_iota(jnp.int32, (LANES, LANES), 1) // RWKV_HEAD_DIM
    bdmask = jnp.where(br == bc, 1.0, 0.0)

    m0b = m0.astype(BF16)
    m1b = m1.astype(BF16)

    def stack2(y):
        yb = y.astype(BF16)
        return jnp.concatenate([yb * m0b, yb * m1b], axis=0)

    pairs = range(n_pairs)
    units = [(s, p) for s in range(n_sub) for p in pairs]
    win = {(s, p): (slice(s * C, (s + 1) * C), slice(p * LANES, (p + 1) * LANES)) for s, p in units}
    kt = {q: kt_s[win[q]] for q in units}
    bt = {q: bt_s[win[q]] for q in units}
    kn = {q: kn_s[win[q]] for q in units}
    rt = {q: rt_s[win[q]] for q in units}
    vv = {q: v_s[win[q]] for q in units}
    gc = {q: gc_s[q[0]:q[0] + 1, win[q][1]] for q in units}
    m_all = {q: _dot_nt(jnp.concatenate([kt[q], rt[q]], axis=0).astype(BF16),
                        jnp.concatenate([stack2(bt[q]), stack2(kn[q])], axis=0))
             for q in units}
    b_mat = {q: jnp.where(strict, m_all[q][0:C, 2 * C:4 * C], 0.0).astype(BF16) for q in units}
    pq_mat = {q: jnp.concatenate([jnp.where(incl, m_all[q][C:2 * C, 0:2 * C], 0.0),
                                  jnp.where(incl, m_all[q][C:2 * C, 2 * C:4 * C], 0.0)], axis=1).astype(BF16)
              for q in units}

    pw = {q: -jnp.where(strict, m_all[q][0:C, 0:2 * C], 0.0) for q in units}
    t_inv = {q: eye2 + pw[q] for q in units}
    for _ in range(int(math.log2(C)) - 1):
        pw = {q: _dot(pw[q].astype(BF16), stack2(pw[q])) for q in units}
        t_inv = {q: t_inv[q] + _dot(t_inv[q].astype(BF16), stack2(pw[q])) for q in units}
    t_inv = {q: t_inv[q].astype(BF16) for q in units}
    v_st = {q: stack2(vv[q]) for q in units}
    xc = {q: jnp.concatenate([bt[q] * gc[q], kn[q] * gc[q]], axis=0).astype(BF16) for q in units}

    state = [s_ref[p] for p in pairs]
    for s in range(n_sub):
        qs = [(s, p) for p in pairs]
        s_bf = [state[p].astype(BF16) for p in pairs]
        rhs = [_dot_nt(kt[q].astype(BF16), s_bf[q[1]]) + _dot(b_mat[q], v_st[q]) for q in qs]
        u = [-_dot(t_inv[q], stack2(rhs[q[1]])) for q in qs]
        for q in qs:
            y_s[win[q]] = _dot_nt(rt[q].astype(BF16), s_bf[q[1]]) + _dot(
                pq_mat[q], jnp.concatenate([stack2(u[q[1]]), v_st[q]], axis=0))
        for q in qs:
            uc = jnp.concatenate([u[q[1]], vv[q]], axis=0)
            state[q[1]] = state[q[1]] * gc[q] + bdmask * _dot(jnp.transpose(uc).astype(BF16), xc[q])
    for p in pairs:
        s_ref[p] = state[p]

    y = y_s[...]
    inv_n = 1.0 / RWKV_HEAD_DIM
    mean = _seg_sum_wide(y, ones_bd) * inv_n
    dlt = y - mean
    var = _seg_sum_wide(dlt * dlt, ones_bd) * inv_n
    yn = dlt * lax.rsqrt(var + RWKV_GN_EPS) * lnw_ref[...] + lnb_ref[...]
    o_ref[0] = ((yn + bon_s[...]) * g_s[...]).astype(o_ref.dtype)


def _rwkv(proj3, mu_rkv, mu_lo, w0, w2, a0, a2, g2, k_k, k_a, r_k, ln_w, ln_b):
    b, lp, _ = proj3.shape
    C = RWKV_CHUNK * RWKV_SUBCHUNKS
    W = RWKV_WIDTH
    cb = COL_RWKV // W
    lb = COL_LORA // LORA_COLS
    row = lambda x: x.reshape(1, -1)
    full = lambda shape: pl.BlockSpec(shape, lambda i, c: (0,) * len(shape))
    wide = pltpu.VMEM((C, W), F32)
    return pl.pallas_call(
        _rwkv_kernel,
        out_shape=jax.ShapeDtypeStruct((b, lp, W), BF16),
        grid=(b, lp // C),
        in_specs=[
            pl.BlockSpec((1, C, W), lambda i, c: (i, c, cb)),
            pl.BlockSpec((1, C, W), lambda i, c: (i, c, cb + 1)),
            pl.BlockSpec((1, C, W), lambda i, c: (i, c, cb + 2)),
            pl.BlockSpec((1, C, LORA_COLS), lambda i, c: (i, c, lb)),
            full((1, 3 * W)), full((1, LORA_COLS)),
            full((1, W)), full((2, DECAY_LORA, W)), full((1, W)), full((2, AAA_LORA, W)), full((2, GATE_LORA, W)),
            full((1, W)), full((1, W)), full((1, W)), full((1, W)), full((1, W)),
        ],
        out_specs=pl.BlockSpec((1, C, W), lambda i, c: (i, c, 0)),
        scratch_shapes=[
            pltpu.VMEM((C + 8, W), F32), pltpu.VMEM((C + 8, W), F32), pltpu.VMEM((C + 8, W), F32),
            pltpu.VMEM((C + 8, LORA_COLS), F32),
            pltpu.VMEM((W // LANES, LANES, LANES), F32),
            wide, wide, wide, wide, wide, wide, pltpu.VMEM((8, W), F32), wide, wide,
        ],
        compiler_params=pltpu.CompilerParams(
            dimension_semantics=("parallel", "arbitrary"), vmem_limit_bytes=VMEM_LIMIT),
        name="rwkv7_time_mix",
    )(proj3, proj3, proj3, proj3, row(mu_rkv), row(mu_lo), row(w0), _hi_lo(w2), row(a0), _hi_lo(a2), _hi_lo(g2),
      row(k_k), row(k_a), row(r_k), row(ln_w), row(ln_b))


def _attn_kernel(q_ref, k_ref, v_ref, cos_ref, s1_ref, s2_ref, qw_ref, kw_ref, lam_ref, sw_ref, o_ref,
                 kp_s, vp_s, m_s, acc_s, *, tq, tk, lambda_init):
    qi = pl.program_id(2)
    lp = k_ref.shape[1]
    n_real = lp - FRONT_PAD
    lk = kp_s.shape[0]
    nh = k_ref.shape[2] // LANES
    vw = 2 * LANES
    heads = range(nh)
    hs = [slice(h * LANES, (h + 1) * LANES) for h in heads]
    ones_bd = _head_ones(LANES, DIFF_HEAD_DIM)
    shift = ROPE_DIM // 2

    def norm_rope(x, w, rows):
        ms = _seg_sum(x * x, ones_bd) * (1.0 / DIFF_HEAD_DIM)
        xn = x * lax.rsqrt(ms + NORM_EPS) * w
        return (xn * cos_ref[rows, :] + pltpu.roll(xn, shift, 1) * s1_ref[rows, :]
                + pltpu.roll(xn, LANES - shift, 1) * s2_ref[rows, :])

    @pl.when(qi == 0)
    def _prep():
        def put(dst, n):
            src = pl.ds(FRONT_PAD + dst, n)
            rows = pl.ds(dst, n)
            for h in heads:
                kp_s[rows, hs[h]] = norm_rope(k_ref[0, src, hs[h]].astype(F32), kw_ref[...], src).astype(BF16)
                vp_s[rows, h * vw:h * vw + LANES] = v_ref[0, src, hs[h]].astype(BF16)
                vp_s[rows, h * vw + LANES:(h + 1) * vw] = jnp.ones((n, LANES), BF16)

        def body(i, carry):
            put(pl.multiple_of(i * LANES, LANES), LANES)
            return carry
        lax.fori_loop(0, n_real // LANES, body, 0, unroll=2)
        tail = n_real % LANES
        if tail:
            put(n_real - tail, tail)
        if lk > n_real:
            kp_s[n_real:lk, :] = jnp.zeros((lk - n_real, nh * LANES), BF16)
            vp_s[n_real:lk, :] = jnp.zeros((lk - n_real, nh * vw), BF16)

    lane = lax.broadcasted_iota(jnp.int32, (1, LANES), 1)
    m0 = jnp.where(lane < DIFF_HEAD_DIM, 1.0, 0.0)
    m1 = 1.0 - m0
    rows_q = pl.ds(pl.multiple_of(qi * tq, tq), tq)
    qs = []
    for h in heads:
        qn = norm_rope(q_ref[0, :, hs[h]].astype(F32), qw_ref[...], rows_q) * (DIFF_HEAD_DIM ** -0.5)
        qs.append(jnp.concatenate([qn * m0, qn * m1], axis=0).astype(BF16))
    m_s[...] = jnp.full(m_s.shape, NEG_BIG, F32)
    acc_s[...] = jnp.zeros(acc_s.shape, F32)
    first_row = qi * tq - FRONT_PAD
    row = first_row + lax.broadcasted_iota(jnp.int32, (2 * tq, LANES), 0) % tq
    col0 = lax.broadcasted_iota(jnp.int32, (2 * tq, LANES), 1)
    n_sub = tk // LANES

    def step(j, causal):
        start = j * tk
        if not isinstance(j, int):
            start = pl.multiple_of(start, tk)
        ks = pl.ds(start, tk)
        s = [_dot_nt(qs[h], kp_s[ks, hs[h]]) for h in heads]
        sub = [[s[h][:, c * LANES:(c + 1) * LANES] for c in range(n_sub)] for h in heads]
        if causal:
            sub = [[jnp.where(start + c * LANES + col0 <= row, sub[h][c], NEG_BIG) for c in range(n_sub)]
                   for h in heads]
        m_old = [m_s[h] for h in heads]
        m_new = []
        for h in heads:
            mx = functools.reduce(jnp.maximum, sub[h])
            m_new.append(jnp.maximum(m_old[h], jnp.broadcast_to(
                jnp.max(mx, axis=1, keepdims=True), mx.shape)))
        p = [jnp.concatenate([jnp.exp(sub[h][c] - m_new[h]) for c in range(n_sub)], axis=1).astype(BF16)
             for h in heads]
        pv = [_dot(p[h], vp_s[ks, h * vw:(h + 1) * vw]) for h in heads]
        for h in heads:
            alpha = jnp.exp(m_old[h] - m_new[h])
            acc_s[h] = jnp.concatenate([alpha, alpha], axis=1) * acc_s[h] + pv[h]
            m_s[h] = m_new[h]

    n_full = jnp.maximum(first_row + 1, 0) // tk
    last = jnp.maximum(first_row + tq - 1, 0) // tk

    def mid(j, carry):
        step(j, False)
        return carry
    lax.fori_loop(0, n_full, mid, 0)
    step(n_full, True)

    @pl.when(last > n_full)
    def _diag():
        step(last, True)

    lam = (jnp.exp(jnp.sum(lam_ref[0:1, :] * lam_ref[1:2, :], axis=1, keepdims=True))
           - jnp.exp(jnp.sum(lam_ref[2:3, :] * lam_ref[3:4, :], axis=1, keepdims=True)) + lambda_init)
    for h in heads:
        acc = acc_s[h]
        o = acc[:, 0:LANES] / acc[:, LANES:vw]
        od = o[0:tq] - lam * o[tq:2 * tq]
        ms = jnp.mean(od * od, axis=1, keepdims=True)
        o_ref[0, :, hs[h]] = (od * lax.rsqrt(ms + NORM_EPS) * sw_ref[...]
                              * (1.0 - lambda_init)).astype(o_ref.dtype)


def _rope_tables(lp):
    half = ROPE_DIM // 2
    inv_freq = jnp.exp(-math.log(ROPE_THETA) * jnp.arange(half, dtype=F32) * 2.0 / ROPE_DIM)
    pos = (jnp.arange(lp) - FRONT_PAD).astype(F32)
    ang = pos[:, None] * inv_freq[None, :]
    cos, sin = jnp.cos(ang), jnp.sin(ang)
    one = jnp.ones((lp, DIFF_HEAD_DIM - ROPE_DIM), F32)
    zero = jnp.zeros((lp, DIFF_HEAD_DIM - ROPE_DIM), F32)
    zh = jnp.zeros((lp, half), F32)
    c = jnp.concatenate([cos, cos, one], axis=1)
    s1 = jnp.concatenate([zh, sin, zero], axis=1)
    s2 = jnp.concatenate([-sin, zh, zero], axis=1)
    dup = lambda t: jnp.concatenate([t, t], axis=1)
    return dup(c), dup(s1), dup(s2)


def _attention(proj3, q_norm_w, k_norm_w, lam4, subln_w, lambda_init):
    b, lp, _ = proj3.shape
    tq, tk, nh = ATTN_BLOCK, ATTN_KEY_BLOCK, ATTN_HEADS_PER_STEP
    lk = -(-(lp - FRONT_PAD) // tk) * tk
    hw = nh * LANES
    qb = COL_DIFF // hw
    kb = qb + DIFF_QK_WIDTH // hw
    vb = kb + DIFF_QK_WIDTH // hw
    cos, s1, s2 = _rope_tables(lp)
    dup = lambda w: jnp.concatenate([w, w]).reshape(1, LANES)
    full = lambda shape: pl.BlockSpec(shape, lambda i, h, q: (0,) * len(shape))
    return pl.pallas_call(
        functools.partial(_attn_kernel, tq=tq, tk=tk, lambda_init=lambda_init),
        out_shape=jax.ShapeDtypeStruct((b, lp, DIFF_V_WIDTH), BF16),
        grid=(b, DIFF_HEADS // nh, lp // tq),
        in_specs=[
            pl.BlockSpec((1, tq, hw), lambda i, h, q: (i, q, qb + h)),
            pl.BlockSpec((1, lp, hw), lambda i, h, q: (i, 0, kb + h)),
            pl.BlockSpec((1, lp, hw), lambda i, h, q: (i, 0, vb + h)),
            full((lp, LANES)), full((lp, LANES)), full((lp, LANES)),
            full((1, LANES)), full((1, LANES)), full((4, DIFF_HEAD_DIM)), full((1, LANES)),
        ],
        out_specs=pl.BlockSpec((1, tq, hw), lambda i, h, q: (i, q, h)),
        scratch_shapes=[pltpu.VMEM((lk, hw), BF16), pltpu.VMEM((lk, 2 * hw), BF16),
                        pltpu.VMEM((nh, 2 * tq, LANES), F32), pltpu.VMEM((nh, 2 * tq, 2 * LANES), F32)],
        compiler_params=pltpu.CompilerParams(
            dimension_semantics=("parallel", "parallel", "arbitrary"), vmem_limit_bytes=VMEM_LIMIT),
        name="diff_attention",
    )(proj3, proj3, proj3, cos, s1, s2, dup(q_norm_w), dup(k_norm_w), lam4, subln_w.reshape(1, LANES))


def _merge_kernel(rw_ref, da_ref, g1_ref, g2_ref, h_ref, wbr_ref, wbd_ref, wo_ref, n2_ref, wr_ref, br_ref,
                  h1_ref, u2_ref, il_ref, ic_ref, cnt_ref, base_s, *, lp):
    i = pl.program_id(0)
    tm = rw_ref.shape[0]

    @pl.when(i == 0)
    def _():
        base_s[...] = jnp.zeros(base_s.shape, F32)

    y1 = _dot(rw_ref[...], wbr_ref[...])
    y2 = _dot(da_ref[...], wbd_ref[...])
    merged = _sigmoid(g1_ref[...].astype(F32)) * y1 + _sigmoid(g2_ref[...].astype(F32)) * y2
    h1 = h_ref[...] + _dot(merged.astype(BF16), wo_ref[...])
    h1_ref[...] = h1
    u2 = h1 * lax.rsqrt(jnp.mean(h1 * h1, axis=-1, keepdims=True) + NORM_EPS) * n2_ref[...]
    u2_ref[...] = u2

    uh, ul = _split2(u2)
    wh, wl = _split2(wr_ref[...])
    lt = _dot_nt(wh, uh) + _dot_nt(wh, ul) + _dot_nt(wl, uh) + br_ref[...]

    gi8 = lax.broadcasted_iota(jnp.int32, (8, tm), 0)
    lg = lt[0:8]
    ge = jnp.exp(lg - jnp.max(lg, axis=0, keepdims=True))
    gp = ge / jnp.sum(ge, axis=0, keepdims=True)
    gv = jnp.max(gp, axis=0, keepdims=True)
    gidx = jnp.min(jnp.where(gp == gv, gi8, N_EXPERTS), axis=0, keepdims=True)

    ei = lax.broadcasted_iota(jnp.int32, (N_EXPERTS, tm), 0)
    sel = (ei // EXPERTS_PER_GROUP) == gidx
    le = jnp.where(sel, lt[8:8 + N_EXPERTS], NEG_BIG)
    ee = jnp.where(sel, jnp.exp(le - jnp.max(le, axis=0, keepdims=True)), 0.0)
    ep = jnp.where(sel, ee / jnp.sum(ee, axis=0, keepdims=True), -1.0)
    v1 = jnp.max(ep, axis=0, keepdims=True)
    i1 = jnp.min(jnp.where(ep == v1, ei, N_EXPERTS), axis=0, keepdims=True)
    ep2 = jnp.where(ei == i1, -1.0, ep)
    v2 = jnp.max(ep2, axis=0, keepdims=True)
    i2 = jnp.min(jnp.where(ep2 == v2, ei, N_EXPERTS), axis=0, keepdims=True)
    den = v1 + v2
    gate1 = gv * v1 / den
    gate2 = gv * v2 / den

    tok = (i * tm + lax.broadcasted_iota(jnp.int32, (1, tm), 1)).astype(F32)
    pos = tok - jnp.floor((tok + 0.5) / lp) * lp
    valid = pos > (FRONT_PAD - 0.5)

    oh1 = jnp.where((ei == i1) & valid, 1.0, 0.0)
    oh2 = jnp.where((ei == i2) & valid, 1.0, 0.0)
    oh = oh1 + oh2
    ur = lax.broadcasted_iota(jnp.int32, (tm, tm), 0)
    uc = lax.broadcasted_iota(jnp.int32, (tm, tm), 1)
    before = jnp.where(ur < uc, 1.0, 0.0).astype(BF16)
    tot = base_s[:, 0:1] + _dot(oh.astype(BF16), before)
    rank1 = jnp.sum(oh1 * tot, axis=0, keepdims=True)
    rank2 = jnp.sum(oh2 * tot, axis=0, keepdims=True)
    base_s[...] = base_s[...] + jnp.sum(oh, axis=1, keepdims=True)
    cnt_ref[...] = base_s[...]

    il = jnp.where(gi8 == 0, i1, jnp.where(gi8 == 1, i2, jnp.where(
        gi8 == 2, rank1.astype(jnp.int32), jnp.where(gi8 == 3, rank2.astype(jnp.int32), jnp.where(
            gi8 == 4, valid.astype(jnp.int32), 0)))))
    il_ref[...] = il
    ri = lax.broadcasted_iota(jnp.int32, (LANES, tm), 0)
    ic = jnp.where(ri == 0, gate1, jnp.where(ri == 1, gate2, 0.0))
    ic_ref[...] = jnp.transpose(ic)


def _merge(rw, da, proj, h0, wbr, wbd, wo, norm2_w, wr, br, lp, tm):
    t, d = h0.shape
    gb = COL_GATE // d
    full = lambda shape: pl.BlockSpec(shape, lambda i: (0,) * len(shape))
    tile = lambda c: pl.BlockSpec((tm, d), lambda i: (i, c))
    return pl.pallas_call(
        functools.partial(_merge_kernel, lp=lp),
        out_shape=(
            jax.ShapeDtypeStruct((t, d), F32),
            jax.ShapeDtypeStruct((t, d), F32),
            jax.ShapeDtypeStruct((8, t), jnp.int32),
            jax.ShapeDtypeStruct((t, LANES), F32),
            jax.ShapeDtypeStruct((N_EXPERTS, LANES), F32),
        ),
        grid=(t // tm,),
        in_specs=[tile(0), tile(0), tile(gb), tile(gb + 1), tile(0),
                  full((d, d)), full((d, d)), full((d, d)), full((1, d)), full((LANES, d)), full((LANES, 1))],
        out_specs=(
            tile(0), tile(0),
            pl.BlockSpec((8, tm), lambda i: (0, i)),
            pl.BlockSpec((tm, LANES), lambda i: (i, 0)),
            full((N_EXPERTS, LANES)),
        ),
        scratch_shapes=[pltpu.VMEM((N_EXPERTS, LANES), F32)],
        compiler_params=pltpu.CompilerParams(
            dimension_semantics=("arbitrary",), vmem_limit_bytes=VMEM_LIMIT),
        name="merge_router",
    )(rw, da, proj, proj, h0, wbr, wbd, wo, norm2_w.reshape(1, d), wr, br)


def _dispatch_kernel(d1_ref, d2_ref, u_ref, xin_hbm, xb_hbm, sem):
    del xin_hbm
    tm = d1_ref.shape[2]

    def start(r, carry):
        for prio, d_ref in enumerate((d1_ref, d2_ref)):
            pltpu.make_async_copy(u_ref.at[pl.ds(r, 1)], xb_hbm.at[pl.ds(d_ref[0, 0, r], 1)],
                                  sem).start(priority=prio)
        return carry

    lax.fori_loop(0, tm, start, 0, unroll=8)
    for _ in range(2):
        pltpu.make_async_copy(u_ref, xb_hbm.at[pl.ds(0, tm)], sem).wait()


def _dispatch(dest1, dest2, u2, cap, tm):
    t, d = u2.shape
    nt = t // tm
    smem = lambda: pl.BlockSpec((1, 1, tm), lambda i: (i, 0, 0), memory_space=pltpu.SMEM)
    return pl.pallas_call(
        _dispatch_kernel,
        out_shape=jax.ShapeDtypeStruct((cap, d), F32),
        grid=(nt,),
        in_specs=[smem(), smem(), pl.BlockSpec((tm, d), lambda i: (i, 0)), pl.BlockSpec(memory_space=pl.ANY)],
        out_specs=pl.BlockSpec(memory_space=pl.ANY),
        scratch_shapes=[pltpu.SemaphoreType.DMA(())],
        input_output_aliases={3: 0},
        compiler_params=pltpu.CompilerParams(
            dimension_semantics=("arbitrary",), vmem_limit_bytes=VMEM_LIMIT),
        name="moe_dispatch",
    )(dest1.reshape(nt, 1, tm), dest2.reshape(nt, 1, tm), u2, jnp.zeros((cap, d), F32))


def _moe_kernel(be_ref, nb_ref, x_ref, wg_ref, wu_ref, wd_ref, o_ref, wg_s, wu_s, wd_s):
    i = pl.program_id(0)
    used = i < nb_ref[0]

    @pl.when(used & ((i == 0) | (be_ref[i] != be_ref[jnp.maximum(i - 1, 0)])))
    def _():
        wg_s[...] = wg_ref[0].astype(BF16)
        wu_s[...] = wu_ref[0].astype(BF16)
        wd_s[...] = wd_ref[0].astype(BF16)

    @pl.when(used)
    def _():
        x = x_ref[...].astype(BF16)
        hg = _dot(x, wg_s[...])
        hu = _dot(x, wu_s[...])
        hid = hg * _sigmoid(hg) * hu
        o_ref[...] = _dot(hid.astype(BF16), wd_s[...])

    @pl.when(i >= nb_ref[0])
    def _():
        o_ref[...] = jnp.zeros(o_ref.shape, F32)


def _moe(block_e, n_used, xb, wg, wu, wd):
    d = xb.shape[1]
    ff = wg.shape[2]
    bm = MOE_BLOCK
    cap = block_e.shape[0] * bm
    return pl.pallas_call(
        _moe_kernel,
        out_shape=jax.ShapeDtypeStruct((cap, d), F32),
        grid_spec=pltpu.PrefetchScalarGridSpec(
            num_scalar_prefetch=2,
            grid=(cap // bm,),
            in_specs=[
                pl.BlockSpec((bm, d), lambda i, be, nb: (i, 0)),
                pl.BlockSpec((1, d, ff), lambda i, be, nb: (be[i], 0, 0)),
                pl.BlockSpec((1, d, ff), lambda i, be, nb: (be[i], 0, 0)),
                pl.BlockSpec((1, ff, d), lambda i, be, nb: (be[i], 0, 0)),
            ],
            out_specs=pl.BlockSpec((bm, d), lambda i, be, nb: (i, 0)),
            scratch_shapes=[pltpu.VMEM((d, ff), BF16), pltpu.VMEM((d, ff), BF16), pltpu.VMEM((ff, d), BF16)],
        ),
        compiler_params=pltpu.CompilerParams(
            dimension_semantics=("arbitrary",), vmem_limit_bytes=VMEM_LIMIT),
        name="moe_experts",
    )(block_e, n_used, xb, wg, wu, wd)


def _combine_kernel(d1_ref, d2_ref, n1_ref, n2_ref, h_ref, ic_ref, yb_hbm, o_ref, ga, gb, sem):
    tm = h_ref.shape[0]
    n = pl.program_id(0) * pl.num_programs(1) + pl.program_id(1)
    total = pl.num_programs(0) * pl.num_programs(1)
    slot = lax.rem(n, 2)

    def issue(da_ref, db_ref, sl):
        def start(r, carry):
            for prio, (d_ref, buf) in enumerate(((da_ref, ga), (db_ref, gb))):
                pltpu.make_async_copy(yb_hbm.at[pl.ds(d_ref[0, 0, r], 1)], buf.at[sl, pl.ds(r, 1)],
                                      sem.at[sl]).start(priority=prio)
            return carry
        lax.fori_loop(0, tm, start, 0, unroll=8)

    @pl.when(n == 0)
    def _first():
        issue(d1_ref, d2_ref, 0)

    @pl.when(n + 1 < total)
    def _next():
        issue(n1_ref, n2_ref, 1 - slot)

    for buf in (ga, gb):
        pltpu.make_async_copy(yb_hbm.at[pl.ds(0, tm)], buf.at[slot], sem.at[slot]).wait()
    ic = ic_ref[...]
    o_ref[0] = h_ref[...] + ic[:, 0:1] * ga[slot] + ic[:, 1:2] * gb[slot]


def _combine(dest1, dest2, h1, ic, yb, b, lp, tm):
    t, d = h1.shape
    per = lp // tm
    first = (FRONT_PAD + N_META) // tm
    steps = per - first
    nt = t // tm
    tile = lambda i, j: i * per + j + first

    def next_tile(i, j):
        nxt = jnp.minimum(i * steps + j + 1, b * steps - 1)
        return tile(nxt // steps, nxt % steps)

    cur = lambda: pl.BlockSpec((1, 1, tm), lambda i, j: (tile(i, j), 0, 0), memory_space=pltpu.SMEM)
    nxt = lambda: pl.BlockSpec((1, 1, tm), lambda i, j: (next_tile(i, j), 0, 0), memory_space=pltpu.SMEM)
    d1 = dest1.reshape(nt, 1, tm)
    d2 = dest2.reshape(nt, 1, tm)
    return pl.pallas_call(
        _combine_kernel,
        out_shape=jax.ShapeDtypeStruct((b, lp - FRONT_PAD - N_META, d), F32),
        grid=(b, steps),
        in_specs=[
            cur(), cur(), nxt(), nxt(),
            pl.BlockSpec((tm, d), lambda i, j: (tile(i, j), 0)),
            pl.BlockSpec((tm, LANES), lambda i, j: (tile(i, j), 0)),
            pl.BlockSpec(memory_space=pl.ANY),
        ],
        out_specs=pl.BlockSpec((1, tm, d), lambda i, j: (i, j, 0)),
        scratch_shapes=[pltpu.VMEM((2, tm, d), F32), pltpu.VMEM((2, tm, d), F32),
                        pltpu.SemaphoreType.DMA((2,))],
        compiler_params=pltpu.CompilerParams(
            dimension_semantics=("arbitrary", "arbitrary"), vmem_limit_bytes=VMEM_LIMIT),
        name="moe_combine",
    )(d1, d2, d1, d2, h1, ic, yb)


def _routing_tables(il, cnt, n_blocks, tm):
    bm = MOE_BLOCK
    counts = cnt[:, 0].astype(jnp.int32)
    padded = (counts + bm - 1) // bm * bm
    pad_end = jnp.cumsum(padded)
    pad_start = pad_end - padded
    valid = il[4] > 0
    spare = n_blocks * bm + jnp.arange(il.shape[1], dtype=jnp.int32) % tm
    dest1 = jnp.where(valid, pad_start[il[0]] + il[2], spare).astype(jnp.int32)
    dest2 = jnp.where(valid, pad_start[il[1]] + il[3], spare + tm).astype(jnp.int32)
    starts = jnp.arange(n_blocks, dtype=jnp.int32) * bm
    block_e = jnp.minimum(jnp.sum((pad_end[None, :] <= starts[:, None]).astype(jnp.int32), axis=1),
                          N_EXPERTS - 1)
    n_used = (pad_end[-1:] // bm).astype(jnp.int32)
    return dest1, dest2, block_e, n_used


def _layer(h0, lp, l, norm1_w, w_in, rwkv_mu, rwkv_w0, rwkv_w2, rwkv_a0, rwkv_a2, rwkv_g2,
           rwkv_k_k, rwkv_k_a, rwkv_r_k, rwkv_ln_w, rwkv_ln_b, q_norm_w, k_norm_w,
           lambda_q1, lambda_k1, lambda_q2, lambda_k2, diff_subln_w, w_branch_rwkv, w_branch_diff,
           w_out, norm2_w, router_group_w, router_group_b, router_expert_w, router_expert_b,
           expert_w_gate, expert_w_up, expert_w_down, proj_tm, tok_tm):
    t, d = h0.shape
    b = t // lp
    lambda_init = 0.8 - 0.6 * math.exp(-0.3 * l)
    rw_cols = 3 * RWKV_WIDTH
    diff_cols = 2 * DIFF_QK_WIDTH + DIFF_V_WIDTH
    w_perm = jnp.concatenate([
        w_in[:, rw_cols + LORA_COLS + diff_cols:],
        w_in[:, rw_cols + LORA_COLS:rw_cols + LORA_COLS + diff_cols],
        w_in[:, :rw_cols + LORA_COLS],
    ], axis=1).astype(BF16)
    proj = _proj(h0, norm1_w, w_perm, proj_tm, 768)
    proj3 = proj.reshape(b, lp, IN_COLS)

    rw = _rwkv(proj3, rwkv_mu[:rw_cols], rwkv_mu[rw_cols:], rwkv_w0, rwkv_w2, rwkv_a0, rwkv_a2, rwkv_g2,
               rwkv_k_k, rwkv_k_a, rwkv_r_k.reshape(-1), rwkv_ln_w, rwkv_ln_b)
    lam4 = jnp.stack([lambda_q1, lambda_k1, lambda_q2, lambda_k2])
    da = _attention(proj3, q_norm_w, k_norm_w, lam4, diff_subln_w, lambda_init)

    wr = jnp.zeros((LANES, d), F32).at[0:N_GROUPS].set(router_group_w.T).at[8:8 + N_EXPERTS].set(router_expert_w.T)
    br = jnp.zeros((LANES,), F32).at[0:N_GROUPS].set(router_group_b).at[N_GROUPS:8].set(NEG_BIG)
    br = br.at[8:8 + N_EXPERTS].set(router_expert_b).reshape(LANES, 1)
    h1, u2, il, ic, cnt = _merge(
        rw.reshape(t, RWKV_WIDTH), da.reshape(t, DIFF_V_WIDTH), proj, h0,
        w_branch_rwkv.astype(BF16), w_branch_diff.astype(BF16), w_out.astype(BF16), norm2_w, wr, br, lp, tok_tm)

    n_real = b * (lp - FRONT_PAD)
    n_blocks = -(-(2 * n_real) // MOE_BLOCK) + N_EXPERTS
    dest1, dest2, block_e, n_used = _routing_tables(il, cnt, n_blocks, tok_tm)
    xb = _dispatch(dest1, dest2, u2, n_blocks * MOE_BLOCK + 2 * tok_tm, tok_tm)
    yb = _moe(block_e, n_used, xb, expert_w_gate, expert_w_up, expert_w_down)
    return h1, ic, dest1, dest2, yb


def kernel(x, meta_tokens, norm1_w, w_in, rwkv_mu, rwkv_w0, rwkv_w2, rwkv_a0, rwkv_a2, rwkv_g2, rwkv_k_k, rwkv_k_a, rwkv_r_k, rwkv_ln_w, rwkv_ln_b, q_norm_w, k_norm_w, lambda_q1, lambda_k1, lambda_q2, lambda_k2, diff_subln_w, w_branch_rwkv, w_branch_diff, w_out, norm2_w, router_group_w, router_group_b, router_expert_w, router_expert_b, expert_w_gate, expert_w_up, expert_w_down):
    b, seq, d = x.shape
    depth = norm1_w.shape[0]
    assert depth == 1, "the combine step emits the final output; deeper stacks need an intermediate form"
    lp = FRONT_PAD + N_META + seq
    meta = jnp.broadcast_to(meta_tokens[None].astype(x.dtype), (b, N_META, d))
    h0 = jnp.concatenate([jnp.zeros((b, FRONT_PAD, d), x.dtype), meta, x], axis=1).reshape(b * lp, d)
    proj_tm = 2048 if (b * lp) % 2048 == 0 else 128
    tok_tm = 512 if (b * lp) % 512 == 0 else 128
    l = 0
    h1, ic, dest1, dest2, yb = _layer(
        h0, lp, l, norm1_w[l], w_in[l], rwkv_mu[l], rwkv_w0[l], rwkv_w2[l], rwkv_a0[l], rwkv_a2[l],
        rwkv_g2[l], rwkv_k_k[l], rwkv_k_a[l], rwkv_r_k[l], rwkv_ln_w[l], rwkv_ln_b[l], q_norm_w[l],
        k_norm_w[l], lambda_q1[l], lambda_k1[l], lambda_q2[l], lambda_k2[l], diff_subln_w[l],
        w_branch_rwkv[l], w_branch_diff[l], w_out[l], norm2_w[l], router_group_w[l], router_group_b[l],
        router_expert_w[l], router_expert_b[l], expert_w_gate[l], expert_w_up[l], expert_w_down[l],
        proj_tm, tok_tm)
    return _combine(dest1, dest2, h1, ic, yb, b, lp, OUT_TILE)
```

```python
import functools
import math

import jax
import jax.numpy as jnp
from jax import lax
from jax.experimental import pallas as pl
from jax.experimental.pallas import tpu as pltpu

F32 = jnp.float32
BF16 = jnp.bfloat16

D_MODEL = 1024
N_META = 16
NORM_EPS = 1e-6
RWKV_HEADS = 16
RWKV_HEAD_DIM = 64
RWKV_WIDTH = RWKV_HEADS * RWKV_HEAD_DIM
DECAY_LORA = 64
AAA_LORA = 64
GATE_LORA = 128
LORA_COLS = DECAY_LORA + AAA_LORA + GATE_LORA
RWKV_GN_EPS = 64e-5
DIFF_HEADS = 8
DIFF_HEAD_DIM = 64
DIFF_V_DIM = 2 * DIFF_HEAD_DIM
DIFF_QK_WIDTH = DIFF_HEADS * 2 * DIFF_HEAD_DIM
DIFF_V_WIDTH = DIFF_HEADS * DIFF_V_DIM
ROPE_THETA = 500000.0
ROPE_DIM = DIFF_HEAD_DIM // 4
N_GROUPS = 4
EXPERTS_PER_GROUP = 8
N_EXPERTS = N_GROUPS * EXPERTS_PER_GROUP
EXPERT_FF = 512

LANES = 128
ATTN_BLOCK = 256
ATTN_KEY_BLOCK = 512
ATTN_HEADS_PER_STEP = 4
FRONT_PAD = ATTN_BLOCK - N_META
RWKV_CHUNK = 64
RWKV_SUBCHUNKS = 3
RWKV_HEAD_GROUP = 2
OUT_TILE = 128
MOE_BLOCK = 256
NEG_BIG = -1e30
VMEM_LIMIT = 48 * 1024 * 1024

COL_GATE = 0
COL_DIFF = 2 * D_MODEL
COL_RWKV = COL_DIFF + 2 * DIFF_QK_WIDTH + DIFF_V_WIDTH
COL_LORA = COL_RWKV + 3 * RWKV_WIDTH
IN_COLS = COL_LORA + LORA_COLS


def _dot(a, b):
    return jnp.dot(a, b, preferred_element_type=F32)


def _dot_nt(a, b):
    return lax.dot_general(a, b, (((1,), (1,)), ((), ())), preferred_element_type=F32)


def _split2(x):
    hi = x.astype(BF16)
    lo = (x - hi.astype(F32)).astype(BF16)
    return hi, lo


def _mm3(a, b_ref):
    ah, al = _split2(a)
    return _dot(ah, b_ref[0]) + _dot(ah, b_ref[1]) + _dot(al, b_ref[0])


def _hi_lo(w):
    hi = w.astype(BF16)
    return jnp.stack([hi, (w - hi.astype(F32)).astype(BF16)])


def _sigmoid(x):
    return 1.0 / (1.0 + jnp.exp(-x))


def _head_ones(width=LANES, head=RWKV_HEAD_DIM):
    r = lax.broadcasted_iota(jnp.int32, (width, width), 0) // head
    c = lax.broadcasted_iota(jnp.int32, (width, width), 1) // head
    return jnp.where(r == c, 1.0, 0.0).astype(BF16)


def _seg_sum(x, ones_bd):
    hi, lo = _split2(x)
    return _dot(hi, ones_bd) + _dot(lo, ones_bd)


def _seg_sum_wide(x, ones_bd):
    rows, width = x.shape
    n = width // LANES
    xs = jnp.concatenate([x[:, i * LANES:(i + 1) * LANES] for i in range(n)], axis=0)
    ys = _seg_sum(xs, ones_bd)
    return jnp.concatenate([ys[i * rows:(i + 1) * rows] for i in range(n)], axis=1)


def _proj_kernel(h_ref, nw_ref, w_ref, o_ref, u_ref):
    @pl.when(pl.program_id(1) == 0)
    def _():
        x = h_ref[...]
        ms = jnp.mean(x * x, axis=-1, keepdims=True)
        u_ref[...] = (x * lax.rsqrt(ms + NORM_EPS) * nw_ref[...]).astype(BF16)

    o_ref[...] = _dot(u_ref[...], w_ref[...]).astype(o_ref.dtype)


def _proj(h, norm_w, w_bf16, tm, tn):
    t, d = h.shape
    n = w_bf16.shape[1]
    return pl.pallas_call(
        _proj_kernel,
        out_shape=jax.ShapeDtypeStruct((t, n), BF16),
        grid=(t // tm, n // tn),
        in_specs=[
            pl.BlockSpec((tm, d), lambda i, j: (i, 0)),
            pl.BlockSpec((1, d), lambda i, j: (0, 0)),
            pl.BlockSpec((d, tn), lambda i, j: (0, j)),
        ],
        out_specs=pl.BlockSpec((tm, tn), lambda i, j: (i, j)),
        scratch_shapes=[pltpu.VMEM((tm, d), BF16)],
        compiler_params=pltpu.CompilerParams(
            dimension_semantics=("parallel", "arbitrary"), vmem_limit_bytes=VMEM_LIMIT),
        name="norm_proj",
    )(h, norm_w.reshape(1, d), w_bf16)


def _rwkv_kernel(*refs):
    o_ref = refs[16]
    xr, xk, xv, xl, s_ref = refs[17:22]
    c = pl.program_id(1)
    n_skip = FRONT_PAD // o_ref.shape[1]

    @pl.when(c == 0)
    def _init():
        for xs in (xr, xk, xv, xl):
            xs[0:8, :] = jnp.zeros((8, xs.shape[1]), F32)
        s_ref[...] = jnp.zeros(s_ref.shape, F32)

    @pl.when(c < n_skip)
    def _pad():
        o_ref[...] = jnp.zeros(o_ref.shape, o_ref.dtype)

    @pl.when(c >= n_skip)
    def _chunk():
        _rwkv_chunk(*refs)


def _rwkv_chunk(r_ref, k_ref, v_ref, lo_ref, mu_ref, mul_ref, w0_ref, w2_ref, a0_ref, a2_ref, g2_ref,
                kkw_ref, kaw_ref, rkw_ref, lnw_ref, lnb_ref, o_ref,
                xr, xk, xv, xl, s_ref, kt_s, bt_s, kn_s, rt_s, v_s, y_s, gc_s, bon_s, g_s):
    R = r_ref.shape[1]
    C = RWKV_CHUNK
    n_sub = R // C
    W = RWKV_WIDTH

    def shift_mix(in_ref, xs, mu):
        x = in_ref[0].astype(F32)
        xs[8:8 + R, :] = x
        prev = xs[7:7 + R, :]
        xs[7:8, :] = x[R - 1:R, :]
        return x + (prev - x) * mu

    r = shift_mix(r_ref, xr, mu_ref[:, 0:W])
    k = shift_mix(k_ref, xk, mu_ref[:, W:2 * W])
    v = shift_mix(v_ref, xv, mu_ref[:, 2 * W:3 * W])
    lo = shift_mix(lo_ref, xl, mul_ref[...])
    xw = jnp.tanh(lo[:, 0:DECAY_LORA])
    xa = lo[:, DECAY_LORA:DECAY_LORA + AAA_LORA]
    xg = _sigmoid(lo[:, DECAY_LORA + AAA_LORA:LORA_COLS])

    lw = (-math.exp(-0.5)) * _sigmoid(w0_ref[...] + _mm3(xw, w2_ref))
    a = _sigmoid(a0_ref[...] + _mm3(xa, a2_ref))
    g_s[...] = _mm3(xg, g2_ref)

    ones_bd = _head_ones()
    kk = k * kkw_ref[...]
    kkn = kk * lax.rsqrt(jnp.maximum(_seg_sum_wide(kk * kk, ones_bd), 1e-24))
    k2 = k * (1.0 + (a - 1.0) * kaw_ref[...])
    bon_s[...] = _seg_sum_wide(r * k2 * rkw_ref[...], ones_bd) * v

    ti = lax.broadcasted_iota(jnp.int32, (R, R), 0)
    tj = lax.broadcasted_iota(jnp.int32, (R, R), 1)
    ltri = jnp.where((ti >= tj) & (ti // C == tj // C), 1.0, 0.0).astype(BF16)
    l1 = lw.astype(BF16)
    rem = lw - l1.astype(F32)
    l2 = rem.astype(BF16)
    l3 = (rem - l2.astype(F32)).astype(BF16)
    cum = _dot(ltri, l1) + _dot(ltri, l2) + _dot(ltri, l3)
    e_pos = jnp.exp(cum)
    e_neg = jnp.exp(-cum)
    kt_s[...] = kkn * jnp.exp(cum - lw)
    bt_s[...] = kkn * a * e_neg
    kn_s[...] = k2 * e_neg
    rt_s[...] = r * e_pos
    v_s[...] = v
    for s in range(n_sub):
        gc_s[s:s + 1, :] = jnp.exp(cum[(s + 1) * C - 1:(s + 1) * C, :])

    G = RWKV_HEAD_GROUP
    GL = G * RWKV_HEAD_DIM
    GC = G * C
    n_groups = W // GL
    lane_head = lax.broadcasted_iota(jnp.int32, (1, GL), 1) // RWKV_HEAD_DIM
    head_mask = [jnp.where(lane_head == h, 1.0, 0.0).astype(BF16) for h in range(G)]
    trow = lax.broadcasted_iota(jnp.int32, (C, GC), 0)
    tcol = lax.broadcasted_iota(jnp.int32, (C, GC), 1) % C
    strict = trow > tcol
    incl = trow >= tcol
    eye = jnp.where(trow == tcol, 1.0, 0.0)
    br = lax.broadcasted_iota(jnp.int32, (GL, GL), 0) // RWKV_HEAD_DIM
    bc = lax.broadcasted_iota(jnp.int32, (GL, GL), 1) // RWKV_HEAD_DIM
    bdmask = jnp.where(br == bc, 1.0, 0.0)

    def stack(y):
        yb = y.astype(BF16)
        return jnp.concatenate([yb * m for m in head_mask], axis=0)

    groups = range(n_groups)
    units = [(s, p) for s in range(n_sub) for p in groups]
    win = {(s, p): (slice(s * C, (s + 1) * C), slice(p * GL, (p + 1) * GL)) for s, p in units}
    kt = {q: kt_s[win[q]] for q in units}
    bt = {q: bt_s[win[q]] for q in units}
    kn = {q: kn_s[win[q]] for q in units}
    rt = {q: rt_s[win[q]] for q in units}
    vv = {q: v_s[win[q]] for q in units}
    gc = {q: gc_s[q[0]:q[0] + 1, win[q][1]] for q in units}
    kr = {q: jnp.concatenate([kt[q], rt[q]], axis=0).astype(BF16) for q in units}
    m_all = {q: _dot_nt(kr[q], jnp.concatenate([stack(bt[q]), stack(kn[q])], axis=0))
             for q in units}
    p_mat = {q: jnp.where(incl, m_all[q][C:2 * C, 0:GC], 0.0).astype(BF16) for q in units}
    bq_mat = {q: jnp.concatenate([jnp.where(strict, m_all[q][0:C, GC:2 * GC], 0.0),
                                  jnp.where(incl, m_all[q][C:2 * C, GC:2 * GC], 0.0)], axis=0).astype(BF16)
              for q in units}
    v_term = {q: _dot(bq_mat[q], stack(vv[q])) for q in units}

    pw = {q: -jnp.where(strict, m_all[q][0:C, 0:GC], 0.0) for q in units}
    t_inv = {q: eye + pw[q] for q in units}
    pw = {q: _dot(pw[q].astype(BF16), stack(pw[q])) for q in units}
    for _ in range(int(math.log2(C)) - 2):
        both = {q: _dot(jnp.concatenate([t_inv[q], pw[q]], axis=0).astype(BF16), stack(pw[q]))
                for q in units}
        t_inv = {q: t_inv[q] + both[q][0:C] for q in units}
        pw = {q: both[q][C:2 * C] for q in units}
    t_inv = {q: (t_inv[q] + _dot(t_inv[q].astype(BF16), stack(pw[q]))).astype(BF16) for q in units}
    xc = {q: jnp.concatenate([bt[q] * gc[q], kn[q] * gc[q]], axis=0).astype(BF16) for q in units}

    state = [s_ref[p] for p in groups]
    for s in range(n_sub):
        qs = [(s, p) for p in groups]
        ks = [_dot_nt(kr[q], state[q[1]].astype(BF16)) for q in qs]
        u = [-_dot(t_inv[q], stack(ks[q[1]][0:C] + v_term[q][0:C])) for q in qs]
        for q in qs:
            y_s[win[q]] = ks[q[1]][C:2 * C] + v_term[q][C:2 * C] + _dot(p_mat[q], stack(u[q[1]]))
        for q in qs:
            uc = jnp.concatenate([u[q[1]], vv[q]], axis=0)
            state[q[1]] = state[q[1]] * gc[q] + bdmask * _dot(jnp.transpose(uc).astype(BF16), xc[q])
    for p in groups:
        s_ref[p] = state[p]

    y = y_s[...]
    inv_n = 1.0 / RWKV_HEAD_DIM
    mean = _seg_sum_wide(y, ones_bd) * inv_n
    dlt = y - mean
    var = _seg_sum_wide(dlt * dlt, ones_bd) * inv_n
    yn = dlt * lax.rsqrt(var + RWKV_GN_EPS) * lnw_ref[...] + lnb_ref[...]
    o_ref[0] = ((yn + bon_s[...]) * g_s[...]).astype(o_ref.dtype)


def _rwkv(proj3, mu_rkv, mu_lo, w0, w2, a0, a2, g2, k_k, k_a, r_k, ln_w, ln_b):
    b, lp, _ = proj3.shape
    C = RWKV_CHUNK * RWKV_SUBCHUNKS
    W = RWKV_WIDTH
    cb = COL_RWKV // W
    lb = COL_LORA // LORA_COLS
    row = lambda x: x.reshape(1, -1)
    full = lambda shape: pl.BlockSpec(shape, lambda i, c: (0,) * len(shape))
    wide = pltpu.VMEM((C, W), F32)
    gl = RWKV_HEAD_GROUP * RWKV_HEAD_DIM
    return pl.pallas_call(
        _rwkv_kernel,
        out_shape=jax.ShapeDtypeStruct((b, lp, W), BF16),
        grid=(b, lp // C),
        in_specs=[
            pl.BlockSpec((1, C, W), lambda i, c: (i, c, cb)),
            pl.BlockSpec((1, C, W), lambda i, c: (i, c, cb + 1)),
            pl.BlockSpec((1, C, W), lambda i, c: (i, c, cb + 2)),
            pl.BlockSpec((1, C, LORA_COLS), lambda i, c: (i, c, lb)),
            full((1, 3 * W)), full((1, LORA_COLS)),
            full((1, W)), full((2, DECAY_LORA, W)), full((1, W)), full((2, AAA_LORA, W)), full((2, GATE_LORA, W)),
            full((1, W)), full((1, W)), full((1, W)), full((1, W)), full((1, W)),
        ],
        out_specs=pl.BlockSpec((1, C, W), lambda i, c: (i, c, 0)),
        scratch_shapes=[
            pltpu.VMEM((C + 8, W), F32), pltpu.VMEM((C + 8, W), F32), pltpu.VMEM((C + 8, W), F32),
            pltpu.VMEM((C + 8, LORA_COLS), F32),
            pltpu.VMEM((W // gl, gl, gl), F32),
            wide, wide, wide, wide, wide, wide, pltpu.VMEM((8, W), F32), wide, wide,
        ],
        compiler_params=pltpu.CompilerParams(
            dimension_semantics=("parallel", "arbitrary"), vmem_limit_bytes=VMEM_LIMIT),
        name="rwkv7_time_mix",
    )(proj3, proj3, proj3, proj3, row(mu_rkv), row(mu_lo), row(w0), _hi_lo(w2), row(a0), _hi_lo(a2), _hi_lo(g2),
      row(k_k), row(k_a), row(r_k), row(ln_w), row(ln_b))


def _attn_kernel(q_ref, k_ref, v_ref, cos_ref, s1_ref, s2_ref, qw_ref, kw_ref, lam_ref, sw_ref, o_ref,
                 kp_s, vp_s, m_s, acc_s, *, tq, tk, lambda_init):
    qi = pl.program_id(2)
    lp = k_ref.shape[1]
    n_real = lp - FRONT_PAD
    lk = kp_s.shape[0]
    nh = k_ref.shape[2] // LANES
    vw = 2 * LANES
    heads = range(nh)
    hs = [slice(h * LANES, (h + 1) * LANES) for h in heads]
    ones_bd = _head_ones(LANES, DIFF_HEAD_DIM)
    shift = ROPE_DIM // 2

    def norm_rope(x, w, rows):
        ms = _seg_sum(x * x, ones_bd) * (1.0 / DIFF_HEAD_DIM)
        xn = x * lax.rsqrt(ms + NORM_EPS) * w
        return (xn * cos_ref[rows, :] + pltpu.roll(xn, shift, 1) * s1_ref[rows, :]
                + pltpu.roll(xn, LANES - shift, 1) * s2_ref[rows, :])

    @pl.when(qi == 0)
    def _prep():
        def put(dst, n):
            src = pl.ds(FRONT_PAD + dst, n)
            rows = pl.ds(dst, n)
            for h in heads:
                kp_s[rows, hs[h]] = norm_rope(k_ref[0, src, hs[h]].astype(F32), kw_ref[...], src).astype(BF16)
                vp_s[rows, h * vw:h * vw + LANES] = v_ref[0, src, hs[h]].astype(BF16)
                vp_s[rows, h * vw + LANES:(h + 1) * vw] = jnp.ones((n, LANES), BF16)

        def body(i, carry):
            put(pl.multiple_of(i * LANES, LANES), LANES)
            return carry
        lax.fori_loop(0, n_real // LANES, body, 0, unroll=2)
        tail = n_real % LANES
        if tail:
            put(n_real - tail, tail)
        if lk > n_real:
            kp_s[n_real:lk, :] = jnp.zeros((lk - n_real, nh * LANES), BF16)
            vp_s[n_real:lk, :] = jnp.zeros((lk - n_real, nh * vw), BF16)

    lane = lax.broadcasted_iota(jnp.int32, (1, LANES), 1)
    m0 = jnp.where(lane < DIFF_HEAD_DIM, 1.0, 0.0)
    m1 = 1.0 - m0
    rows_q = pl.ds(pl.multiple_of(qi * tq, tq), tq)
    qs = []
    for h in heads:
        qn = norm_rope(q_ref[0, :, hs[h]].astype(F32), qw_ref[...], rows_q) * (DIFF_HEAD_DIM ** -0.5)
        qs.append(jnp.concatenate([qn * m0, qn * m1], axis=0).astype(BF16))
    m_s[...] = jnp.full(m_s.shape, NEG_BIG, F32)
    acc_s[...] = jnp.zeros(acc_s.shape, F32)
    first_row = qi * tq - FRONT_PAD
    row = first_row + lax.broadcasted_iota(jnp.int32, (2 * tq, LANES), 0) % tq
    col0 = lax.broadcasted_iota(jnp.int32, (2 * tq, LANES), 1)
    n_sub = tk // LANES

    def step(j, causal):
        start = j * tk
        if not isinstance(j, int):
            start = pl.multiple_of(start, tk)
        ks = pl.ds(start, tk)
        s = [_dot_nt(qs[h], kp_s[ks, hs[h]]) for h in heads]
        sub = [[s[h][:, c * LANES:(c + 1) * LANES] for c in range(n_sub)] for h in heads]
        if causal:
            sub = [[jnp.where(start + c * LANES + col0 <= row, sub[h][c], NEG_BIG) for c in range(n_sub)]
                   for h in heads]
        m_old = [m_s[h] for h in heads]
        m_new = []
        for h in heads:
            mx = functools.reduce(jnp.maximum, sub[h])
            m_new.append(jnp.maximum(m_old[h], jnp.broadcast_to(
                jnp.max(mx, axis=1, keepdims=True), mx.shape)))
        p = [jnp.concatenate([jnp.exp(sub[h][c] - m_new[h]) for c in range(n_sub)], axis=1).astype(BF16)
             for h in heads]
        pv = [_dot(p[h], vp_s[ks, h * vw:(h + 1) * vw]) for h in heads]
        for h in heads:
            alpha = jnp.exp(m_old[h] - m_new[h])
            acc_s[h] = jnp.concatenate([alpha, alpha], axis=1) * acc_s[h] + pv[h]
            m_s[h] = m_new[h]

    n_full = jnp.maximum(first_row + 1, 0) // tk
    last = jnp.maximum(first_row + tq - 1, 0) // tk

    def mid(j, carry):
        step(j, False)
        return carry
    lax.fori_loop(0, n_full, mid, 0)
    step(n_full, True)

    @pl.when(last > n_full)
    def _diag():
        step(last, True)

    lam = (jnp.exp(jnp.sum(lam_ref[0:1, :] * lam_ref[1:2, :], axis=1, keepdims=True))
           - jnp.exp(jnp.sum(lam_ref[2:3, :] * lam_ref[3:4, :], axis=1, keepdims=True)) + lambda_init)
    for h in heads:
        acc = acc_s[h]
        o = acc[:, 0:LANES] / acc[:, LANES:vw]
        od = o[0:tq] - lam * o[tq:2 * tq]
        ms = jnp.mean(od * od, axis=1, keepdims=True)
        o_ref[0, :, hs[h]] = (od * lax.rsqrt(ms + NORM_EPS) * sw_ref[...]
                              * (1.0 - lambda_init)).astype(o_ref.dtype)


def _rope_tables(lp):
    half = ROPE_DIM // 2
    inv_freq = jnp.exp(-math.log(ROPE_THETA) * jnp.arange(half, dtype=F32) * 2.0 / ROPE_DIM)
    pos = (jnp.arange(lp) - FRONT_PAD).astype(F32)
    ang = pos[:, None] * inv_freq[None, :]
    cos, sin = jnp.cos(ang), jnp.sin(ang)
    one = jnp.ones((lp, DIFF_HEAD_DIM - ROPE_DIM), F32)
    zero = jnp.zeros((lp, DIFF_HEAD_DIM - ROPE_DIM), F32)
    zh = jnp.zeros((lp, half), F32)
    c = jnp.concatenate([cos, cos, one], axis=1)
    s1 = jnp.concatenate([zh, sin, zero], axis=1)
    s2 = jnp.concatenate([-sin, zh, zero], axis=1)
    dup = lambda t: jnp.concatenate([t, t], axis=1)
    return dup(c), dup(s1), dup(s2)


def _attention(proj3, q_norm_w, k_norm_w, lam4, subln_w, lambda_init):
    b, lp, _ = proj3.shape
    tq, tk, nh = ATTN_BLOCK, ATTN_KEY_BLOCK, ATTN_HEADS_PER_STEP
    lk = -(-(lp - FRONT_PAD) // tk) * tk
    hw = nh * LANES
    qb = COL_DIFF // hw
    kb = qb + DIFF_QK_WIDTH // hw
    vb = kb + DIFF_QK_WIDTH // hw
    cos, s1, s2 = _rope_tables(lp)
    dup = lambda w: jnp.concatenate([w, w]).reshape(1, LANES)
    full = lambda shape: pl.BlockSpec(shape, lambda i, h, q: (0,) * len(shape))
    return pl.pallas_call(
        functools.partial(_attn_kernel, tq=tq, tk=tk, lambda_init=lambda_init),
        out_shape=jax.ShapeDtypeStruct((b, lp, DIFF_V_WIDTH), BF16),
        grid=(b, DIFF_HEADS // nh, lp // tq),
        in_specs=[
            pl.BlockSpec((1, tq, hw), lambda i, h, q: (i, q, qb + h)),
            pl.BlockSpec((1, lp, hw), lambda i, h, q: (i, 0, kb + h)),
            pl.BlockSpec((1, lp, hw), lambda i, h, q: (i, 0, vb + h)),
            full((lp, LANES)), full((lp, LANES)), full((lp, LANES)),
            full((1, LANES)), full((1, LANES)), full((4, DIFF_HEAD_DIM)), full((1, LANES)),
        ],
        out_specs=pl.BlockSpec((1, tq, hw), lambda i, h, q: (i, q, h)),
        scratch_shapes=[pltpu.VMEM((lk, hw), BF16), pltpu.VMEM((lk, 2 * hw), BF16),
                        pltpu.VMEM((nh, 2 * tq, LANES), F32), pltpu.VMEM((nh, 2 * tq, 2 * LANES), F32)],
        compiler_params=pltpu.CompilerParams(
            dimension_semantics=("parallel", "parallel", "arbitrary"), vmem_limit_bytes=VMEM_LIMIT),
        name="diff_attention",
    )(proj3, proj3, proj3, cos, s1, s2, dup(q_norm_w), dup(k_norm_w), lam4, subln_w.reshape(1, LANES))


def _merge_kernel(rw_ref, da_ref, g1_ref, g2_ref, h_ref, wbr_ref, wbd_ref, wo_ref, n2_ref, wr_ref, br_ref,
                  h1_ref, u2_ref, il_ref, ic_ref, cnt_ref, base_s, *, lp):
    i = pl.program_id(0)
    tm = rw_ref.shape[0]

    @pl.when(i == 0)
    def _():
        base_s[...] = jnp.zeros(base_s.shape, F32)

    y1 = _dot(rw_ref[...], wbr_ref[...])
    y2 = _dot(da_ref[...], wbd_ref[...])
    merged = _sigmoid(g1_ref[...].astype(F32)) * y1 + _sigmoid(g2_ref[...].astype(F32)) * y2
    h1 = h_ref[...] + _dot(merged.astype(BF16), wo_ref[...])
    h1_ref[...] = h1
    u2 = h1 * lax.rsqrt(jnp.mean(h1 * h1, axis=-1, keepdims=True) + NORM_EPS) * n2_ref[...]
    u2_ref[...] = u2

    uh, ul = _split2(u2)
    wh, wl = _split2(wr_ref[...])
    lt = _dot_nt(wh, uh) + _dot_nt(wh, ul) + _dot_nt(wl, uh) + br_ref[...]

    gi8 = lax.broadcasted_iota(jnp.int32, (8, tm), 0)
    lg = lt[0:8]
    ge = jnp.exp(lg - jnp.max(lg, axis=0, keepdims=True))
    gp = ge / jnp.sum(ge, axis=0, keepdims=True)
    gv = jnp.max(gp, axis=0, keepdims=True)
    gidx = jnp.min(jnp.where(gp == gv, gi8, N_EXPERTS), axis=0, keepdims=True)

    ei = lax.broadcasted_iota(jnp.int32, (N_EXPERTS, tm), 0)
    sel = (ei // EXPERTS_PER_GROUP) == gidx
    le = jnp.where(sel, lt[8:8 + N_EXPERTS], NEG_BIG)
    ee = jnp.where(sel, jnp.exp(le - jnp.max(le, axis=0, keepdims=True)), 0.0)
    ep = jnp.where(sel, ee / jnp.sum(ee, axis=0, keepdims=True), -1.0)
    v1 = jnp.max(ep, axis=0, keepdims=True)
    i1 = jnp.min(jnp.where(ep == v1, ei, N_EXPERTS), axis=0, keepdims=True)
    ep2 = jnp.where(ei == i1, -1.0, ep)
    v2 = jnp.max(ep2, axis=0, keepdims=True)
    i2 = jnp.min(jnp.where(ep2 == v2, ei, N_EXPERTS), axis=0, keepdims=True)
    den = v1 + v2
    gate1 = gv * v1 / den
    gate2 = gv * v2 / den

    tok = (i * tm + lax.broadcasted_iota(jnp.int32, (1, tm), 1)).astype(F32)
    pos = tok - jnp.floor((tok + 0.5) / lp) * lp
    valid = pos > (FRONT_PAD - 0.5)

    oh1 = jnp.where((ei == i1) & valid, 1.0, 0.0)
    oh2 = jnp.where((ei == i2) & valid, 1.0, 0.0)
    oh = oh1 + oh2
    ur = lax.broadcasted_iota(jnp.int32, (tm, tm), 0)
    uc = lax.broadcasted_iota(jnp.int32, (tm, tm), 1)
    before = jnp.where(ur < uc, 1.0, 0.0).astype(BF16)
    tot = base_s[:, 0:1] + _dot(oh.astype(BF16), before)
    rank1 = jnp.sum(oh1 * tot, axis=0, keepdims=True)
    rank2 = jnp.sum(oh2 * tot, axis=0, keepdims=True)
    base_s[...] = base_s[...] + jnp.sum(oh, axis=1, keepdims=True)
    cnt_ref[...] = base_s[...]

    il = jnp.where(gi8 == 0, i1, jnp.where(gi8 == 1, i2, jnp.where(
        gi8 == 2, rank1.astype(jnp.int32), jnp.where(gi8 == 3, rank2.astype(jnp.int32), jnp.where(
            gi8 == 4, valid.astype(jnp.int32), 0)))))
    il_ref[...] = il
    ri = lax.broadcasted_iota(jnp.int32, (LANES, tm), 0)
    ic = jnp.where(ri == 0, gate1, jnp.where(ri == 1, gate2, 0.0))
    ic_ref[...] = jnp.transpose(ic)


def _merge(rw, da, proj, h0, wbr, wbd, wo, norm2_w, wr, br, lp, tm):
    t, d = h0.shape
    gb = COL_GATE // d
    full = lambda shape: pl.BlockSpec(shape, lambda i: (0,) * len(shape))
    tile = lambda c: pl.BlockSpec((tm, d), lambda i: (i, c))
    return pl.pallas_call(
        functools.partial(_merge_kernel, lp=lp),
        out_shape=(
            jax.ShapeDtypeStruct((t, d), F32),
            jax.ShapeDtypeStruct((t, d), F32),
            jax.ShapeDtypeStruct((8, t), jnp.int32),
            jax.ShapeDtypeStruct((t, LANES), F32),
            jax.ShapeDtypeStruct((N_EXPERTS, LANES), F32),
        ),
        grid=(t // tm,),
        in_specs=[tile(0), tile(0), tile(gb), tile(gb + 1), tile(0),
                  full((d, d)), full((d, d)), full((d, d)), full((1, d)), full((LANES, d)), full((LANES, 1))],
        out_specs=(
            tile(0), tile(0),
            pl.BlockSpec((8, tm), lambda i: (0, i)),
            pl.BlockSpec((tm, LANES), lambda i: (i, 0)),
            full((N_EXPERTS, LANES)),
        ),
        scratch_shapes=[pltpu.VMEM((N_EXPERTS, LANES), F32)],
        compiler_params=pltpu.CompilerParams(
            dimension_semantics=("arbitrary",), vmem_limit_bytes=VMEM_LIMIT),
        name="merge_router",
    )(rw, da, proj, proj, h0, wbr, wbd, wo, norm2_w.reshape(1, d), wr, br)


def _dispatch_kernel(d1_ref, d2_ref, u_ref, xin_hbm, xb_hbm, sem):
    del xin_hbm
    tm = d1_ref.shape[2]

    def start(r, carry):
        for prio, d_ref in enumerate((d1_ref, d2_ref)):
            pltpu.make_async_copy(u_ref.at[pl.ds(r, 1)], xb_hbm.at[pl.ds(d_ref[0, 0, r], 1)],
                                  sem).start(priority=prio)
        return carry

    lax.fori_loop(0, tm, start, 0, unroll=8)
    for _ in range(2):
        pltpu.make_async_copy(u_ref, xb_hbm.at[pl.ds(0, tm)], sem).wait()


def _dispatch(dest1, dest2, u2, cap, tm):
    t, d = u2.shape
    nt = t // tm
    smem = lambda: pl.BlockSpec((1, 1, tm), lambda i: (i, 0, 0), memory_space=pltpu.SMEM)
    return pl.pallas_call(
        _dispatch_kernel,
        out_shape=jax.ShapeDtypeStruct((cap, d), F32),
        grid=(nt,),
        in_specs=[smem(), smem(), pl.BlockSpec((tm, d), lambda i: (i, 0)), pl.BlockSpec(memory_space=pl.ANY)],
        out_specs=pl.BlockSpec(memory_space=pl.ANY),
        scratch_shapes=[pltpu.SemaphoreType.DMA(())],
        input_output_aliases={3: 0},
        compiler_params=pltpu.CompilerParams(
            dimension_semantics=("arbitrary",), vmem_limit_bytes=VMEM_LIMIT),
        name="moe_dispatch",
    )(dest1.reshape(nt, 1, tm), dest2.reshape(nt, 1, tm), u2, jnp.zeros((cap, d), F32))


def _moe_kernel(be_ref, nb_ref, x_ref, wg_ref, wu_ref, wd_ref, o_ref, wg_s, wu_s, wd_s):
    i = pl.program_id(0)
    used = i < nb_ref[0]

    @pl.when(used & ((i == 0) | (be_ref[i] != be_ref[jnp.maximum(i - 1, 0)])))
    def _():
        wg_s[...] = wg_ref[0].astype(BF16)
        wu_s[...] = wu_ref[0].astype(BF16)
        wd_s[...] = wd_ref[0].astype(BF16)

    @pl.when(used)
    def _():
        x = x_ref[...].astype(BF16)
        hg = _dot(x, wg_s[...])
        hu = _dot(x, wu_s[...])
        hid = hg * _sigmoid(hg) * hu
        o_ref[...] = _dot(hid.astype(BF16), wd_s[...])

    @pl.when(i >= nb_ref[0])
    def _():
        o_ref[...] = jnp.zeros(o_ref.shape, F32)


def _moe(block_e, n_used, xb, wg, wu, wd):
    d = xb.shape[1]
    ff = wg.shape[2]
    bm = MOE_BLOCK
    cap = block_e.shape[0] * bm
    return pl.pallas_call(
        _moe_kernel,
        out_shape=jax.ShapeDtypeStruct((cap, d), F32),
        grid_spec=pltpu.PrefetchScalarGridSpec(
            num_scalar_prefetch=2,
            grid=(cap // bm,),
            in_specs=[
                pl.BlockSpec((bm, d), lambda i, be, nb: (i, 0)),
                pl.BlockSpec((1, d, ff), lambda i, be, nb: (be[i], 0, 0)),
                pl.BlockSpec((1, d, ff), lambda i, be, nb: (be[i], 0, 0)),
                pl.BlockSpec((1, ff, d), lambda i, be, nb: (be[i], 0, 0)),
            ],
            out_specs=pl.BlockSpec((bm, d), lambda i, be, nb: (i, 0)),
            scratch_shapes=[pltpu.VMEM((d, ff), BF16), pltpu.VMEM((d, ff), BF16), pltpu.VMEM((ff, d), BF16)],
        ),
        compiler_params=pltpu.CompilerParams(
            dimension_semantics=("arbitrary",), vmem_limit_bytes=VMEM_LIMIT),
        name="moe_experts",
    )(block_e, n_used, xb, wg, wu, wd)


def _combine_kernel(d1_ref, d2_ref, n1_ref, n2_ref, h_ref, ic_ref, yb_hbm, o_ref, ga, gb, sem):
    tm = h_ref.shape[0]
    n = pl.program_id(0) * pl.num_programs(1) + pl.program_id(1)
    total = pl.num_programs(0) * pl.num_programs(1)
    slot = lax.rem(n, 2)

    def issue(da_ref, db_ref, sl):
        def start(r, carry):
            for prio, (d_ref, buf) in enumerate(((da_ref, ga), (db_ref, gb))):
                pltpu.make_async_copy(yb_hbm.at[pl.ds(d_ref[0, 0, r], 1)], buf.at[sl, pl.ds(r, 1)],
                                      sem.at[sl]).start(priority=prio)
            return carry
        lax.fori_loop(0, tm, start, 0, unroll=8)

    @pl.when(n == 0)
    def _first():
        issue(d1_ref, d2_ref, 0)

    @pl.when(n + 1 < total)
    def _next():
        issue(n1_ref, n2_ref, 1 - slot)

    for buf in (ga, gb):
        pltpu.make_async_copy(yb_hbm.at[pl.ds(0, tm)], buf.at[slot], sem.at[slot]).wait()
    ic = ic_ref[...]
    o_ref[0] = h_ref[...] + ic[:, 0:1] * ga[slot] + ic[:, 1:2] * gb[slot]


def _combine(dest1, dest2, h1, ic, yb, b, lp, tm):
    t, d = h1.shape
    per = lp // tm
    first = (FRONT_PAD + N_META) // tm
    steps = per - first
    nt = t // tm
    tile = lambda i, j: i * per + j + first

    def next_tile(i, j):
        nxt = jnp.minimum(i * steps + j + 1, b * steps - 1)
        return tile(nxt // steps, nxt % steps)

    cur = lambda: pl.BlockSpec((1, 1, tm), lambda i, j: (tile(i, j), 0, 0), memory_space=pltpu.SMEM)
    nxt = lambda: pl.BlockSpec((1, 1, tm), lambda i, j: (next_tile(i, j), 0, 0), memory_space=pltpu.SMEM)
    d1 = dest1.reshape(nt, 1, tm)
    d2 = dest2.reshape(nt, 1, tm)
    return pl.pallas_call(
        _combine_kernel,
        out_shape=jax.ShapeDtypeStruct((b, lp - FRONT_PAD - N_META, d), F32),
        grid=(b, steps),
        in_specs=[
            cur(), cur(), nxt(), nxt(),
            pl.BlockSpec((tm, d), lambda i, j: (tile(i, j), 0)),
            pl.BlockSpec((tm, LANES), lambda i, j: (tile(i, j), 0)),
            pl.BlockSpec(memory_space=pl.ANY),
        ],
        out_specs=pl.BlockSpec((1, tm, d), lambda i, j: (i, j, 0)),
        scratch_shapes=[pltpu.VMEM((2, tm, d), F32), pltpu.VMEM((2, tm, d), F32),
                        pltpu.SemaphoreType.DMA((2,))],
        compiler_params=pltpu.CompilerParams(
            dimension_semantics=("arbitrary", "arbitrary"), vmem_limit_bytes=VMEM_LIMIT),
        name="moe_combine",
    )(d1, d2, d1, d2, h1, ic, yb)


def _routing_tables(il, cnt, n_blocks, tm):
    bm = MOE_BLOCK
    counts = cnt[:, 0].astype(jnp.int32)
    padded = (counts + bm - 1) // bm * bm
    pad_end = jnp.cumsum(padded)
    pad_start = pad_end - padded
    valid = il[4] > 0
    spare = n_blocks * bm + jnp.arange(il.shape[1], dtype=jnp.int32) % tm
    dest1 = jnp.where(valid, pad_start[il[0]] + il[2], spare).astype(jnp.int32)
    dest2 = jnp.where(valid, pad_start[il[1]] + il[3], spare + tm).astype(jnp.int32)
    starts = jnp.arange(n_blocks, dtype=jnp.int32) * bm
    block_e = jnp.minimum(jnp.sum((pad_end[None, :] <= starts[:, None]).astype(jnp.int32), axis=1),
                          N_EXPERTS - 1)
    n_used = (pad_end[-1:] // bm).astype(jnp.int32)
    return dest1, dest2, block_e, n_used


def _layer(h0, lp, l, norm1_w, w_in, rwkv_mu, rwkv_w0, rwkv_w2, rwkv_a0, rwkv_a2, rwkv_g2,
           rwkv_k_k, rwkv_k_a, rwkv_r_k, rwkv_ln_w, rwkv_ln_b, q_norm_w, k_norm_w,
           lambda_q1, lambda_k1, lambda_q2, lambda_k2, diff_subln_w, w_branch_rwkv, w_branch_diff,
           w_out, norm2_w, router_group_w, router_group_b, router_expert_w, router_expert_b,
           expert_w_gate, expert_w_up, expert_w_down, proj_tm, tok_tm):
    t, d = h0.shape
    b = t // lp
    lambda_init = 0.8 - 0.6 * math.exp(-0.3 * l)
    rw_cols = 3 * RWKV_WIDTH
    diff_cols = 2 * DIFF_QK_WIDTH + DIFF_V_WIDTH
    w_perm = jnp.concatenate([
        w_in[:, rw_cols + LORA_COLS + diff_cols:],
        w_in[:, rw_cols + LORA_COLS:rw_cols + LORA_COLS + diff_cols],
        w_in[:, :rw_cols + LORA_COLS],
    ], axis=1).astype(BF16)
    proj = _proj(h0, norm1_w, w_perm, proj_tm, 768)
    proj3 = proj.reshape(b, lp, IN_COLS)

    rw = _rwkv(proj3, rwkv_mu[:rw_cols], rwkv_mu[rw_cols:], rwkv_w0, rwkv_w2, rwkv_a0, rwkv_a2, rwkv_g2,
               rwkv_k_k, rwkv_k_a, rwkv_r_k.reshape(-1), rwkv_ln_w, rwkv_ln_b)
    lam4 = jnp.stack([lambda_q1, lambda_k1, lambda_q2, lambda_k2])
    da = _attention(proj3, q_norm_w, k_norm_w, lam4, diff_subln_w, lambda_init)

    wr = jnp.zeros((LANES, d), F32).at[0:N_GROUPS].set(router_group_w.T).at[8:8 + N_EXPERTS].set(router_expert_w.T)
    br = jnp.zeros((LANES,), F32).at[0:N_GROUPS].set(router_group_b).at[N_GROUPS:8].set(NEG_BIG)
    br = br.at[8:8 + N_EXPERTS].set(router_expert_b).reshape(LANES, 1)
    h1, u2, il, ic, cnt = _merge(
        rw.reshape(t, RWKV_WIDTH), da.reshape(t, DIFF_V_WIDTH), proj, h0,
        w_branch_rwkv.astype(BF16), w_branch_diff.astype(BF16), w_out.astype(BF16), norm2_w, wr, br, lp, tok_tm)

    n_real = b * (lp - FRONT_PAD)
    n_blocks = -(-(2 * n_real) // MOE_BLOCK) + N_EXPERTS
    dest1, dest2, block_e, n_used = _routing_tables(il, cnt, n_blocks, tok_tm)
    xb = _dispatch(dest1, dest2, u2, n_blocks * MOE_BLOCK + 2 * tok_tm, tok_tm)
    yb = _moe(block_e, n_used, xb, expert_w_gate, expert_w_up, expert_w_down)
    return h1, ic, dest1, dest2, yb


def kernel(x, meta_tokens, norm1_w, w_in, rwkv_mu, rwkv_w0, rwkv_w2, rwkv_a0, rwkv_a2, rwkv_g2, rwkv_k_k, rwkv_k_a, rwkv_r_k, rwkv_ln_w, rwkv_ln_b, q_norm_w, k_norm_w, lambda_q1, lambda_k1, lambda_q2, lambda_k2, diff_subln_w, w_branch_rwkv, w_branch_diff, w_out, norm2_w, router_group_w, router_group_b, router_expert_w, router_expert_b, expert_w_gate, expert_w_up, expert_w_down):
    b, seq, d = x.shape
    depth = norm1_w.shape[0]
    assert depth == 1, "the combine step emits the final output; deeper stacks need an intermediate form"
    lp = FRONT_PAD + N_META + seq
    meta = jnp.broadcast_to(meta_tokens[None].astype(x.dtype), (b, N_META, d))
    h0 = jnp.concatenate([jnp.zeros((b, FRONT_PAD, d), x.dtype), meta, x], axis=1).reshape(b * lp, d)
    proj_tm = 2048 if (b * lp) % 2048 == 0 else 128
    tok_tm = 512 if (b * lp) % 512 == 0 else 128
    l = 0
    h1, ic, dest1, dest2, yb = _layer(
        h0, lp, l, norm1_w[l], w_in[l], rwkv_mu[l], rwkv_w0[l], rwkv_w2[l], rwkv_a0[l], rwkv_a2[l],
        rwkv_g2[l], rwkv_k_k[l], rwkv_k_a[l], rwkv_r_k[l], rwkv_ln_w[l], rwkv_ln_b[l], q_norm_w[l],
        k_norm_w[l], lambda_q1[l], lambda_k1[l], lambda_q2[l], lambda_k2[l], diff_subln_w[l],
        w_branch_rwkv[l], w_branch_diff[l], w_out[l], norm2_w[l], router_group_w[l], router_group_b[l],
        router_expert_w[l], router_expert_b[l], expert_w_gate[l], expert_w_up[l], expert_w_down[l],
        proj_tm, tok_tm)
    return _combine(dest1, dest2, h1, ic, yb, b, lp, OUT_TILE)
```

```python
import functools
import math

import jax
import jax.numpy as jnp
from jax import lax
from jax.experimental import pallas as pl
from jax.experimental.pallas import tpu as pltpu

F32 = jnp.float32
BF16 = jnp.bfloat16

D_MODEL = 1024
N_META = 16
NORM_EPS = 1e-6
RWKV_HEADS = 16
RWKV_HEAD_DIM = 64
RWKV_WIDTH = RWKV_HEADS * RWKV_HEAD_DIM
DECAY_LORA = 64
AAA_LORA = 64
GATE_LORA = 128
LORA_COLS = DECAY_LORA + AAA_LORA + GATE_LORA
RWKV_GN_EPS = 64e-5
DIFF_HEADS = 8
DIFF_HEAD_DIM = 64
DIFF_V_DIM = 2 * DIFF_HEAD_DIM
DIFF_QK_WIDTH = DIFF_HEADS * 2 * DIFF_HEAD_DIM
DIFF_V_WIDTH = DIFF_HEADS * DIFF_V_DIM
ROPE_THETA = 500000.0
ROPE_DIM = DIFF_HEAD_DIM // 4
N_GROUPS = 4
EXPERTS_PER_GROUP = 8
N_EXPERTS = N_GROUPS * EXPERTS_PER_GROUP
EXPERT_FF = 512

LANES = 128
ATTN_BLOCK = 256
ATTN_KEY_BLOCK = 512
ATTN_HEADS_PER_STEP = 4
FRONT_PAD = ATTN_BLOCK - N_META
RWKV_CHUNK = 64
RWKV_SUBCHUNKS = 3
RWKV_HEAD_GROUP = 2
OUT_TILE = 128
MOE_BLOCK = 256
NEG_BIG = -1e30
VMEM_LIMIT = 48 * 1024 * 1024

COL_GATE = 0
COL_DIFF = 2 * D_MODEL
COL_RWKV = COL_DIFF + 2 * DIFF_QK_WIDTH + DIFF_V_WIDTH
COL_LORA = COL_RWKV + 3 * RWKV_WIDTH
IN_COLS = COL_LORA + LORA_COLS


def _dot(a, b):
    return jnp.dot(a, b, preferred_element_type=F32)


def _dot_nt(a, b):
    return lax.dot_general(a, b, (((1,), (1,)), ((), ())), preferred_element_type=F32)


def _split2(x):
    hi = x.astype(BF16)
    lo = (x - hi.astype(F32)).astype(BF16)
    return hi, lo


def _mm3(a, b_ref):
    ah, al = _split2(a)
    return _dot(ah, b_ref[0]) + _dot(ah, b_ref[1]) + _dot(al, b_ref[0])


def _hi_lo(w):
    hi = w.astype(BF16)
    return jnp.stack([hi, (w - hi.astype(F32)).astype(BF16)])


def _sigmoid(x):
    return 1.0 / (1.0 + jnp.exp(-x))


def _head_ones(width=LANES, head=RWKV_HEAD_DIM):
    r = lax.broadcasted_iota(jnp.int32, (width, width), 0) // head
    c = lax.broadcasted_iota(jnp.int32, (width, width), 1) // head
    return jnp.where(r == c, 1.0, 0.0).astype(BF16)


def _seg_sum(x, ones_bd):
    hi, lo = _split2(x)
    return _dot(hi, ones_bd) + _dot(lo, ones_bd)


def _seg_sum_wide(x, ones_bd):
    rows, width = x.shape
    n = width // LANES
    xs = jnp.concatenate([x[:, i * LANES:(i + 1) * LANES] for i in range(n)], axis=0)
    ys = _seg_sum(xs, ones_bd)
    return jnp.concatenate([ys[i * rows:(i + 1) * rows] for i in range(n)], axis=1)


def _proj_kernel(h_ref, nw_ref, w_ref, o_ref, u_ref):
    @pl.when(pl.program_id(1) == 0)
    def _():
        x = h_ref[...]
        ms = jnp.mean(x * x, axis=-1, keepdims=True)
        u_ref[...] = (x * lax.rsqrt(ms + NORM_EPS) * nw_ref[...]).astype(BF16)

    o_ref[...] = _dot(u_ref[...], w_ref[...]).astype(o_ref.dtype)


def _proj(h, norm_w, w_bf16, tm, tn):
    t, d = h.shape
    n = w_bf16.shape[1]
    return pl.pallas_call(
        _proj_kernel,
        out_shape=jax.ShapeDtypeStruct((t, n), BF16),
        grid=(t // tm, n // tn),
        in_specs=[
            pl.BlockSpec((tm, d), lambda i, j: (i, 0)),
            pl.BlockSpec((1, d), lambda i, j: (0, 0)),
            pl.BlockSpec((d, tn), lambda i, j: (0, j)),
        ],
        out_specs=pl.BlockSpec((tm, tn), lambda i, j: (i, j)),
        scratch_shapes=[pltpu.VMEM((tm, d), BF16)],
        compiler_params=pltpu.CompilerParams(
            dimension_semantics=("parallel", "arbitrary"), vmem_limit_bytes=VMEM_LIMIT),
        name="norm_proj",
    )(h, norm_w.reshape(1, d), w_bf16)


def _rwkv_kernel(*refs):
    o_ref = refs[16]
    xr, xk, xv, xl, s_ref = refs[17:22]
    c = pl.program_id(1)
    n_skip = FRONT_PAD // o_ref.shape[1]

    @pl.when(c == 0)
    def _init():
        for xs in (xr, xk, xv, xl):
            xs[0:8, :] = jnp.zeros((8, xs.shape[1]), F32)
        s_ref[...] = jnp.zeros(s_ref.shape, F32)

    @pl.when(c < n_skip)
    def _pad():
        o_ref[...] = jnp.zeros(o_ref.shape, o_ref.dtype)

    @pl.when(c >= n_skip)
    def _chunk():
        _rwkv_chunk(*refs)


def _rwkv_chunk(r_ref, k_ref, v_ref, lo_ref, mu_ref, mul_ref, w0_ref, w2_ref, a0_ref, a2_ref, g2_ref,
                kkw_ref, kaw_ref, rkw_ref, lnw_ref, lnb_ref, o_ref,
                xr, xk, xv, xl, s_ref, kt_s, bt_s, kn_s, rt_s, v_s, y_s, gc_s, bon_s, g_s):
    R = r_ref.shape[1]
    C = RWKV_CHUNK
    n_sub = R // C
    W = RWKV_WIDTH

    def shift_mix(in_ref, xs, mu):
        x = in_ref[0].astype(F32)
        xs[8:8 + R, :] = x
        prev = xs[7:7 + R, :]
        xs[7:8, :] = x[R - 1:R, :]
        return x + (prev - x) * mu

    r = shift_mix(r_ref, xr, mu_ref[:, 0:W])
    k = shift_mix(k_ref, xk, mu_ref[:, W:2 * W])
    v = shift_mix(v_ref, xv, mu_ref[:, 2 * W:3 * W])
    lo = shift_mix(lo_ref, xl, mul_ref[...])
    xw = jnp.tanh(lo[:, 0:DECAY_LORA])
    xa = lo[:, DECAY_LORA:DECAY_LORA + AAA_LORA]
    xg = _sigmoid(lo[:, DECAY_LORA + AAA_LORA:LORA_COLS])

    lw = (-math.exp(-0.5)) * _sigmoid(w0_ref[...] + _mm3(xw, w2_ref))
    a = _sigmoid(a0_ref[...] + _mm3(xa, a2_ref))
    g_s[...] = _mm3(xg, g2_ref)

    ones_bd = _head_ones()
    kk = k * kkw_ref[...]
    kkn = kk * lax.rsqrt(jnp.maximum(_seg_sum_wide(kk * kk, ones_bd), 1e-24))
    k2 = k * (1.0 + (a - 1.0) * kaw_ref[...])
    bon_s[...] = _seg_sum_wide(r * k2 * rkw_ref[...], ones_bd) * v

    ti = lax.broadcasted_iota(jnp.int32, (R, R), 0)
    tj = lax.broadcasted_iota(jnp.int32, (R, R), 1)
    ltri = jnp.where((ti >= tj) & (ti // C == tj // C), 1.0, 0.0).astype(BF16)
    l1 = lw.astype(BF16)
    rem = lw - l1.astype(F32)
    l2 = rem.astype(BF16)
    l3 = (rem - l2.astype(F32)).astype(BF16)
    cum = _dot(ltri, l1) + _dot(ltri, l2) + _dot(ltri, l3)
    e_pos = jnp.exp(cum)
    e_neg = jnp.exp(-cum)
    kt_s[...] = kkn * jnp.exp(cum - lw)
    bt_s[...] = kkn * a * e_neg
    kn_s[...] = k2 * e_neg
    rt_s[...] = r * e_pos
    v_s[...] = v
    for s in range(n_sub):
        gc_s[s:s + 1, :] = jnp.exp(cum[(s + 1) * C - 1:(s + 1) * C, :])

    G = RWKV_HEAD_GROUP
    GL = G * RWKV_HEAD_DIM
    GC = G * C
    n_groups = W // GL
    lane_head = lax.broadcasted_iota(jnp.int32, (1, GL), 1) // RWKV_HEAD_DIM
    head_mask = [jnp.where(lane_head == h, 1.0, 0.0).astype(BF16) for h in range(G)]
    trow = lax.broadcasted_iota(jnp.int32, (C, GC), 0)
    tcol = lax.broadcasted_iota(jnp.int32, (C, GC), 1) % C
    strict = trow > tcol
    incl = trow >= tcol
    eye = jnp.where(trow == tcol, 1.0, 0.0)
    br = lax.broadcasted_iota(jnp.int32, (GL, GL), 0) // RWKV_HEAD_DIM
    bc = lax.broadcasted_iota(jnp.int32, (GL, GL), 1) // RWKV_HEAD_DIM
    bdmask = jnp.where(br == bc, 1.0, 0.0)

    def stack(y):
        yb = y.astype(BF16)
        return jnp.concatenate([yb * m for m in head_mask], axis=0)

    groups = range(n_groups)
    units = [(s, p) for s in range(n_sub) for p in groups]
    win = {(s, p): (slice(s * C, (s + 1) * C), slice(p * GL, (p + 1) * GL)) for s, p in units}
    kt = {q: kt_s[win[q]] for q in units}
    bt = {q: bt_s[win[q]] for q in units}
    kn = {q: kn_s[win[q]] for q in units}
    rt = {q: rt_s[win[q]] for q in units}
    vv = {q: v_s[win[q]] for q in units}
    gc = {q: gc_s[q[0]:q[0] + 1, win[q][1]] for q in units}
    kr = {q: jnp.concatenate([kt[q], rt[q]], axis=0).astype(BF16) for q in units}
    m_all = {q: _dot_nt(kr[q], jnp.concatenate([stack(bt[q]), stack(kn[q])], axis=0))
             for q in units}
    p_mat = {q: jnp.where(incl, m_all[q][C:2 * C, 0:GC], 0.0).astype(BF16) for q in units}
    bq_mat = {q: jnp.concatenate([jnp.where(strict, m_all[q][0:C, GC:2 * GC], 0.0),
                                  jnp.where(incl, m_all[q][C:2 * C, GC:2 * GC], 0.0)], axis=0).astype(BF16)
              for q in units}
    v_term = {q: _dot(bq_mat[q], stack(vv[q])) for q in units}

    pw = {q: -jnp.where(strict, m_all[q][0:C, 0:GC], 0.0) for q in units}
    t_inv = {q: eye + pw[q] for q in units}
    pw = {q: _dot(pw[q].astype(BF16), stack(pw[q])) for q in units}
    for _ in range(int(math.log2(C)) - 2):
        both = {q: _dot(jnp.concatenate([t_inv[q], pw[q]], axis=0).astype(BF16), stack(pw[q]))
                for q in units}
        t_inv = {q: t_inv[q] + both[q][0:C] for q in units}
        pw = {q: both[q][C:2 * C] for q in units}
    t_inv = {q: (t_inv[q] + _dot(t_inv[q].astype(BF16), stack(pw[q]))).astype(BF16) for q in units}
    xc = {q: jnp.concatenate([bt[q] * gc[q], kn[q] * gc[q]], axis=0).astype(BF16) for q in units}

    state = [s_ref[p] for p in groups]
    for s in range(n_sub):
        qs = [(s, p) for p in groups]
        ks = [_dot_nt(kr[q], state[q[1]].astype(BF16)) for q in qs]
        u = [-_dot(t_inv[q], stack(ks[q[1]][0:C] + v_term[q][0:C])) for q in qs]
        for q in qs:
            y_s[win[q]] = ks[q[1]][C:2 * C] + v_term[q][C:2 * C] + _dot(p_mat[q], stack(u[q[1]]))
        for q in qs:
            uc = jnp.concatenate([u[q[1]], vv[q]], axis=0)
            state[q[1]] = state[q[1]] * gc[q] + bdmask * _dot(jnp.transpose(uc).astype(BF16), xc[q])
    for p in groups:
        s_ref[p] = state[p]

    y = y_s[...]
    inv_n = 1.0 / RWKV_HEAD_DIM
    mean = _seg_sum_wide(y, ones_bd) * inv_n
    dlt = y - mean
    var = _seg_sum_wide(dlt * dlt, ones_bd) * inv_n
    yn = dlt * lax.rsqrt(var + RWKV_GN_EPS) * lnw_ref[...] + lnb_ref[...]
    o_ref[0] = ((yn + bon_s[...]) * g_s[...]).astype(o_ref.dtype)


def _rwkv(proj3, mu_rkv, mu_lo, w0, w2, a0, a2, g2, k_k, k_a, r_k, ln_w, ln_b):
    b, lp, _ = proj3.shape
    C = RWKV_CHUNK * RWKV_SUBCHUNKS
    W = RWKV_WIDTH
    cb = COL_RWKV // W
    lb = COL_LORA // LORA_COLS
    row = lambda x: x.reshape(1, -1)
    full = lambda shape: pl.BlockSpec(shape, lambda i, c: (0,) * len(shape))
    wide = pltpu.VMEM((C, W), F32)
    gl = RWKV_HEAD_GROUP * RWKV_HEAD_DIM
    return pl.pallas_call(
        _rwkv_kernel,
        out_shape=jax.ShapeDtypeStruct((b, lp, W), BF16),
        grid=(b, lp // C),
        in_specs=[
            pl.BlockSpec((1, C, W), lambda i, c: (i, c, cb)),
            pl.BlockSpec((1, C, W), lambda i, c: (i, c, cb + 1)),
            pl.BlockSpec((1, C, W), lambda i, c: (i, c, cb + 2)),
            pl.BlockSpec((1, C, LORA_COLS), lambda i, c: (i, c, lb)),
            full((1, 3 * W)), full((1, LORA_COLS)),
            full((1, W)), full((2, DECAY_LORA, W)), full((1, W)), full((2, AAA_LORA, W)), full((2, GATE_LORA, W)),
            full((1, W)), full((1, W)), full((1, W)), full((1, W)), full((1, W)),
        ],
        out_specs=pl.BlockSpec((1, C, W), lambda i, c: (i, c, 0)),
        scratch_shapes=[
            pltpu.VMEM((C + 8, W), F32), pltpu.VMEM((C + 8, W), F32), pltpu.VMEM((C + 8, W), F32),
            pltpu.VMEM((C + 8, LORA_COLS), F32),
            pltpu.VMEM((W // gl, gl, gl), F32),
            wide, wide, wide, wide, wide, wide, pltpu.VMEM((8, W), F32), wide, wide,
        ],
        compiler_params=pltpu.CompilerParams(
            dimension_semantics=("parallel", "arbitrary"), vmem_limit_bytes=VMEM_LIMIT),
        name="rwkv7_time_mix",
    )(proj3, proj3, proj3, proj3, row(mu_rkv), row(mu_lo), row(w0), _hi_lo(w2), row(a0), _hi_lo(a2), _hi_lo(g2),
      row(k_k), row(k_a), row(r_k), row(ln_w), row(ln_b))


def _attn_kernel(q_ref, k_ref, v_ref, cos_ref, s1_ref, s2_ref, qw_ref, kw_ref, lam_ref, sw_ref, o_ref,
                 kp_s, vp_s, m_s, acc_s, *, tq, tk, lambda_init):
    qi = pl.program_id(2)
    lp = k_ref.shape[1]
    n_real = lp - FRONT_PAD
    lk = kp_s.shape[0]
    nh = k_ref.shape[2] // LANES
    vw = 2 * LANES
    heads = range(nh)
    hs = [slice(h * LANES, (h + 1) * LANES) for h in heads]
    ones_bd = _head_ones(LANES, DIFF_HEAD_DIM)
    shift = ROPE_DIM // 2

    def norm_rope(x, w, rows):
        ms = _seg_sum(x * x, ones_bd) * (1.0 / DIFF_HEAD_DIM)
        xn = x * lax.rsqrt(ms + NORM_EPS) * w
        return (xn * cos_ref[rows, :] + pltpu.roll(xn, shift, 1) * s1_ref[rows, :]
                + pltpu.roll(xn, LANES - shift, 1) * s2_ref[rows, :])

    @pl.when(qi == 0)
    def _prep():
        def put(dst, n):
            src = pl.ds(FRONT_PAD + dst, n)
            rows = pl.ds(dst, n)
            for h in heads:
                kp_s[rows, hs[h]] = norm_rope(k_ref[0, src, hs[h]].astype(F32), kw_ref[...], src).astype(BF16)
                vp_s[rows, h * vw:h * vw + LANES] = v_ref[0, src, hs[h]].astype(BF16)
                vp_s[rows, h * vw + LANES:(h + 1) * vw] = jnp.ones((n, LANES), BF16)

        def body(i, carry):
            put(pl.multiple_of(i * LANES, LANES), LANES)
            return carry
        lax.fori_loop(0, n_real // LANES, body, 0, unroll=2)
        tail = n_real % LANES
        if tail:
            put(n_real - tail, tail)
        if lk > n_real:
            kp_s[n_real:lk, :] = jnp.zeros((lk - n_real, nh * LANES), BF16)
            vp_s[n_real:lk, :] = jnp.zeros((lk - n_real, nh * vw), BF16)

    lane = lax.broadcasted_iota(jnp.int32, (1, LANES), 1)
    m0 = jnp.where(lane < DIFF_HEAD_DIM, 1.0, 0.0)
    m1 = 1.0 - m0
    rows_q = pl.ds(pl.multiple_of(qi * tq, tq), tq)
    qs = []
    for h in heads:
        qn = norm_rope(q_ref[0, :, hs[h]].astype(F32), qw_ref[...], rows_q) * (DIFF_HEAD_DIM ** -0.5)
        qs.append(jnp.concatenate([qn * m0, qn * m1], axis=0).astype(BF16))
    first_row = qi * tq - FRONT_PAD
    row = first_row + lax.broadcasted_iota(jnp.int32, (2 * tq, LANES), 0) % tq
    col0 = lax.broadcasted_iota(jnp.int32, (2 * tq, LANES), 1)

    def step(j, width, causal, init=False):
        start = j * tk
        if not isinstance(j, int):
            start = pl.multiple_of(start, tk)
        ks = pl.ds(start, width)
        n_sub = width // LANES
        s = [_dot_nt(qs[h], kp_s[ks, hs[h]]) for h in heads]
        sub = [[s[h][:, c * LANES:(c + 1) * LANES] for c in range(n_sub)] for h in heads]
        if causal:
            sub = [[jnp.where(start + c * LANES + col0 <= row, sub[h][c], NEG_BIG) for c in range(n_sub)]
                   for h in heads]
        m_new = []
        for h in heads:
            mx = functools.reduce(jnp.maximum, sub[h])
            mx = jnp.broadcast_to(jnp.max(mx, axis=1, keepdims=True), mx.shape)
            m_new.append(mx if init else jnp.maximum(m_s[h], mx))
        p = [jnp.concatenate([jnp.exp(sub[h][c] - m_new[h]) for c in range(n_sub)], axis=1).astype(BF16)
             for h in heads]
        pv = [_dot(p[h], vp_s[ks, h * vw:(h + 1) * vw]) for h in heads]
        for h in heads:
            if init:
                acc_s[h] = pv[h]
            else:
                alpha = jnp.exp(m_s[h] - m_new[h])
                acc_s[h] = jnp.concatenate([alpha, alpha], axis=1) * acc_s[h] + pv[h]
            m_s[h] = m_new[h]

    assert FRONT_PAD + N_META == tq and tk % tq == 0 and N_META <= LANES
    n_full = jnp.maximum(first_row + 1, 0) // tk
    last = jnp.maximum(first_row + tq - 1, 0) // tk
    step(n_full, tk, True, init=True)

    def mid(j, carry):
        step(j, tk, False)
        return carry
    lax.fori_loop(0, n_full, mid, 0)

    @pl.when(last > n_full)
    def _diag():
        step(last, LANES, True)

    lam = (jnp.exp(jnp.sum(lam_ref[0:1, :] * lam_ref[1:2, :], axis=1, keepdims=True))
           - jnp.exp(jnp.sum(lam_ref[2:3, :] * lam_ref[3:4, :], axis=1, keepdims=True)) + lambda_init)
    for h in heads:
        acc = acc_s[h]
        o = acc[:, 0:LANES] / acc[:, LANES:vw]
        od = o[0:tq] - lam * o[tq:2 * tq]
        ms = jnp.mean(od * od, axis=1, keepdims=True)
        o_ref[0, :, hs[h]] = (od * lax.rsqrt(ms + NORM_EPS) * sw_ref[...]
                              * (1.0 - lambda_init)).astype(o_ref.dtype)


def _rope_tables(lp):
    half = ROPE_DIM // 2
    inv_freq = jnp.exp(-math.log(ROPE_THETA) * jnp.arange(half, dtype=F32) * 2.0 / ROPE_DIM)
    pos = (jnp.arange(lp) - FRONT_PAD).astype(F32)
    ang = pos[:, None] * inv_freq[None, :]
    cos, sin = jnp.cos(ang), jnp.sin(ang)
    one = jnp.ones((lp, DIFF_HEAD_DIM - ROPE_DIM), F32)
    zero = jnp.zeros((lp, DIFF_HEAD_DIM - ROPE_DIM), F32)
    zh = jnp.zeros((lp, half), F32)
    c = jnp.concatenate([cos, cos, one], axis=1)
    s1 = jnp.concatenate([zh, sin, zero], axis=1)
    s2 = jnp.concatenate([-sin, zh, zero], axis=1)
    dup = lambda t: jnp.concatenate([t, t], axis=1)
    return dup(c), dup(s1), dup(s2)


def _attention(proj3, q_norm_w, k_norm_w, lam4, subln_w, lambda_init):
    b, lp, _ = proj3.shape
    tq, tk, nh = ATTN_BLOCK, ATTN_KEY_BLOCK, ATTN_HEADS_PER_STEP
    lk = -(-(lp - FRONT_PAD) // tk) * tk
    hw = nh * LANES
    qb = COL_DIFF // hw
    kb = qb + DIFF_QK_WIDTH // hw
    vb = kb + DIFF_QK_WIDTH // hw
    cos, s1, s2 = _rope_tables(lp)
    dup = lambda w: jnp.concatenate([w, w]).reshape(1, LANES)
    full = lambda shape: pl.BlockSpec(shape, lambda i, h, q: (0,) * len(shape))
    return pl.pallas_call(
        functools.partial(_attn_kernel, tq=tq, tk=tk, lambda_init=lambda_init),
        out_shape=jax.ShapeDtypeStruct((b, lp, DIFF_V_WIDTH), BF16),
        grid=(b, DIFF_HEADS // nh, lp // tq),
        in_specs=[
            pl.BlockSpec((1, tq, hw), lambda i, h, q: (i, q, qb + h)),
            pl.BlockSpec((1, lp, hw), lambda i, h, q: (i, 0, kb + h)),
            pl.BlockSpec((1, lp, hw), lambda i, h, q: (i, 0, vb + h)),
            full((lp, LANES)), full((lp, LANES)), full((lp, LANES)),
            full((1, LANES)), full((1, LANES)), full((4, DIFF_HEAD_DIM)), full((1, LANES)),
        ],
        out_specs=pl.BlockSpec((1, tq, hw), lambda i, h, q: (i, q, h)),
        scratch_shapes=[pltpu.VMEM((lk, hw), BF16), pltpu.VMEM((lk, 2 * hw), BF16),
                        pltpu.VMEM((nh, 2 * tq, LANES), F32), pltpu.VMEM((nh, 2 * tq, 2 * LANES), F32)],
        compiler_params=pltpu.CompilerParams(
            dimension_semantics=("parallel", "parallel", "arbitrary"), vmem_limit_bytes=VMEM_LIMIT),
        name="diff_attention",
    )(proj3, proj3, proj3, cos, s1, s2, dup(q_norm_w), dup(k_norm_w), lam4, subln_w.reshape(1, LANES))


def _merge_kernel(rw_ref, da_ref, g1_ref, g2_ref, h_ref, wbr_ref, wbd_ref, wo_ref, n2_ref, wr_ref, br_ref,
                  h1_ref, u2_ref, il_ref, ic_ref, cnt_ref, base_s, *, lp):
    i = pl.program_id(0)
    tm = rw_ref.shape[0]

    @pl.when(i == 0)
    def _():
        base_s[...] = jnp.zeros(base_s.shape, F32)

    y1 = _dot(rw_ref[...], wbr_ref[...])
    y2 = _dot(da_ref[...], wbd_ref[...])
    merged = _sigmoid(g1_ref[...].astype(F32)) * y1 + _sigmoid(g2_ref[...].astype(F32)) * y2
    h1 = h_ref[...] + _dot(merged.astype(BF16), wo_ref[...])
    h1_ref[...] = h1
    u2 = h1 * lax.rsqrt(jnp.mean(h1 * h1, axis=-1, keepdims=True) + NORM_EPS) * n2_ref[...]
    u2_ref[...] = u2

    uh, ul = _split2(u2)
    wh, wl = _split2(wr_ref[...])
    lt = _dot_nt(wh, uh) + _dot_nt(wh, ul) + _dot_nt(wl, uh) + br_ref[...]

    gi8 = lax.broadcasted_iota(jnp.int32, (8, tm), 0)
    lg = lt[0:8]
    ge = jnp.exp(lg - jnp.max(lg, axis=0, keepdims=True))
    gp = ge / jnp.sum(ge, axis=0, keepdims=True)
    gv = jnp.max(gp, axis=0, keepdims=True)
    gidx = jnp.min(jnp.where(gp == gv, gi8, N_EXPERTS), axis=0, keepdims=True)

    ei = lax.broadcasted_iota(jnp.int32, (N_EXPERTS, tm), 0)
    sel = (ei // EXPERTS_PER_GROUP) == gidx
    le = jnp.where(sel, lt[8:8 + N_EXPERTS], NEG_BIG)
    ee = jnp.where(sel, jnp.exp(le - jnp.max(le, axis=0, keepdims=True)), 0.0)
    ep = jnp.where(sel, ee / jnp.sum(ee, axis=0, keepdims=True), -1.0)
    v1 = jnp.max(ep, axis=0, keepdims=True)
    i1 = jnp.min(jnp.where(ep == v1, ei, N_EXPERTS), axis=0, keepdims=True)
    ep2 = jnp.where(ei == i1, -1.0, ep)
    v2 = jnp.max(ep2, axis=0, keepdims=True)
    i2 = jnp.min(jnp.where(ep2 == v2, ei, N_EXPERTS), axis=0, keepdims=True)
    den = v1 + v2
    gate1 = gv * v1 / den
    gate2 = gv * v2 / den

    tok = (i * tm + lax.broadcasted_iota(jnp.int32, (1, tm), 1)).astype(F32)
    pos = tok - jnp.floor((tok + 0.5) / lp) * lp
    valid = pos > (FRONT_PAD - 0.5)

    oh1 = jnp.where((ei == i1) & valid, 1.0, 0.0)
    oh2 = jnp.where((ei == i2) & valid, 1.0, 0.0)
    oh = oh1 + oh2
    ur = lax.broadcasted_iota(jnp.int32, (tm, tm), 0)
    uc = lax.broadcasted_iota(jnp.int32, (tm, tm), 1)
    before = jnp.where(ur < uc, 1.0, 0.0).astype(BF16)
    tot = base_s[:, 0:1] + _dot(oh.astype(BF16), before)
    rank1 = jnp.sum(oh1 * tot, axis=0, keepdims=True)
    rank2 = jnp.sum(oh2 * tot, axis=0, keepdims=True)
    base_s[...] = base_s[...] + jnp.sum(oh, axis=1, keepdims=True)
    cnt_ref[...] = base_s[...]

    il = jnp.where(gi8 == 0, i1, jnp.where(gi8 == 1, i2, jnp.where(
        gi8 == 2, rank1.astype(jnp.int32), jnp.where(gi8 == 3, rank2.astype(jnp.int32), jnp.where(
            gi8 == 4, valid.astype(jnp.int32), 0)))))
    il_ref[...] = il
    ri = lax.broadcasted_iota(jnp.int32, (LANES, tm), 0)
    ic = jnp.where(ri == 0, gate1, jnp.where(ri == 1, gate2, 0.0))
    ic_ref[...] = jnp.transpose(ic)


def _merge(rw, da, proj, h0, wbr, wbd, wo, norm2_w, wr, br, lp, tm):
    t, d = h0.shape
    gb = COL_GATE // d
    full = lambda shape: pl.BlockSpec(shape, lambda i: (0,) * len(shape))
    tile = lambda c: pl.BlockSpec((tm, d), lambda i: (i, c))
    return pl.pallas_call(
        functools.partial(_merge_kernel, lp=lp),
        out_shape=(
            jax.ShapeDtypeStruct((t, d), F32),
            jax.ShapeDtypeStruct((t, d), F32),
            jax.ShapeDtypeStruct((8, t), jnp.int32),
            jax.ShapeDtypeStruct((t, LANES), F32),
            jax.ShapeDtypeStruct((N_EXPERTS, LANES), F32),
        ),
        grid=(t // tm,),
        in_specs=[tile(0), tile(0), tile(gb), tile(gb + 1), tile(0),
                  full((d, d)), full((d, d)), full((d, d)), full((1, d)), full((LANES, d)), full((LANES, 1))],
        out_specs=(
            tile(0), tile(0),
            pl.BlockSpec((8, tm), lambda i: (0, i)),
            pl.BlockSpec((tm, LANES), lambda i: (i, 0)),
            full((N_EXPERTS, LANES)),
        ),
        scratch_shapes=[pltpu.VMEM((N_EXPERTS, LANES), F32)],
        compiler_params=pltpu.CompilerParams(
            dimension_semantics=("arbitrary",), vmem_limit_bytes=VMEM_LIMIT),
        name="merge_router",
    )(rw, da, proj, proj, h0, wbr, wbd, wo, norm2_w.reshape(1, d), wr, br)


def _dispatch_kernel(d1_ref, d2_ref, u_ref, xin_hbm, xb_hbm, sem):
    del xin_hbm
    tm = d1_ref.shape[2]

    def start(r, carry):
        for prio, d_ref in enumerate((d1_ref, d2_ref)):
            pltpu.make_async_copy(u_ref.at[pl.ds(r, 1)], xb_hbm.at[pl.ds(d_ref[0, 0, r], 1)],
                                  sem).start(priority=prio)
        return carry

    lax.fori_loop(0, tm, start, 0, unroll=8)
    for _ in range(2):
        pltpu.make_async_copy(u_ref, xb_hbm.at[pl.ds(0, tm)], sem).wait()


def _dispatch(dest1, dest2, u2, cap, tm):
    t, d = u2.shape
    nt = t // tm
    smem = lambda: pl.BlockSpec((1, 1, tm), lambda i: (i, 0, 0), memory_space=pltpu.SMEM)
    return pl.pallas_call(
        _dispatch_kernel,
        out_shape=jax.ShapeDtypeStruct((cap, d), F32),
        grid=(nt,),
        in_specs=[smem(), smem(), pl.BlockSpec((tm, d), lambda i: (i, 0)), pl.BlockSpec(memory_space=pl.ANY)],
        out_specs=pl.BlockSpec(memory_space=pl.ANY),
        scratch_shapes=[pltpu.SemaphoreType.DMA(())],
        input_output_aliases={3: 0},
        compiler_params=pltpu.CompilerParams(
            dimension_semantics=("arbitrary",), vmem_limit_bytes=VMEM_LIMIT),
        name="moe_dispatch",
    )(dest1.reshape(nt, 1, tm), dest2.reshape(nt, 1, tm), u2, jnp.zeros((cap, d), F32))


def _moe_kernel(be_ref, nb_ref, grp_ref, nxt_ref, x_ref, wg_hbm, wu_hbm, wd_hbm, o_ref,
                wg_f, wu_f, wd_f, wg_s, wu_s, wd_s, sem):
    i = pl.program_id(0)
    used = i < nb_ref[0]
    slot = lax.rem(grp_ref[i], 2)

    def fetch(expert, sl):
        return [pltpu.make_async_copy(w_hbm.at[expert], w_f.at[sl], sem.at[sl])
                for w_hbm, w_f in ((wg_hbm, wg_f), (wu_hbm, wu_f), (wd_hbm, wd_f))]

    @pl.when(used & ((i == 0) | (grp_ref[i] != grp_ref[jnp.maximum(i - 1, 0)])))
    def _():
        @pl.when(i == 0)
        def _():
            for cp in fetch(be_ref[0], 0):
                cp.start()

        @pl.when(nxt_ref[i] >= 0)
        def _():
            for cp in fetch(nxt_ref[i], 1 - slot):
                cp.start()

        for cp in fetch(be_ref[i], slot):
            cp.wait()
        wg_s[...] = wg_f[slot].astype(BF16)
        wu_s[...] = wu_f[slot].astype(BF16)
        wd_s[...] = wd_f[slot].astype(BF16)

    @pl.when(used)
    def _():
        x = x_ref[...].astype(BF16)
        hg = _dot(x, wg_s[...])
        hu = _dot(x, wu_s[...])
        hid = hg * _sigmoid(hg) * hu
        o_ref[...] = _dot(hid.astype(BF16), wd_s[...])

    @pl.when(i >= nb_ref[0])
    def _():
        o_ref[...] = jnp.zeros(o_ref.shape, F32)


def _moe(block_e, n_used, block_grp, block_nxt, xb, wg, wu, wd):
    d = xb.shape[1]
    ff = wg.shape[2]
    bm = MOE_BLOCK
    cap = block_e.shape[0] * bm
    hbm = pl.BlockSpec(memory_space=pl.ANY)
    return pl.pallas_call(
        _moe_kernel,
        out_shape=jax.ShapeDtypeStruct((cap, d), F32),
        grid_spec=pltpu.PrefetchScalarGridSpec(
            num_scalar_prefetch=4,
            grid=(cap // bm,),
            in_specs=[pl.BlockSpec((bm, d), lambda i, *_: (i, 0)), hbm, hbm, hbm],
            out_specs=pl.BlockSpec((bm, d), lambda i, *_: (i, 0)),
            scratch_shapes=[pltpu.VMEM((2, d, ff), F32), pltpu.VMEM((2, d, ff), F32), pltpu.VMEM((2, ff, d), F32),
                            pltpu.VMEM((d, ff), BF16), pltpu.VMEM((d, ff), BF16), pltpu.VMEM((ff, d), BF16),
                            pltpu.SemaphoreType.DMA((2,))],
        ),
        compiler_params=pltpu.CompilerParams(
            dimension_semantics=("arbitrary",), vmem_limit_bytes=VMEM_LIMIT),
        name="moe_experts",
    )(block_e, n_used, block_grp, block_nxt, xb, wg, wu, wd)


def _combine_kernel(d1_ref, d2_ref, n1_ref, n2_ref, h_ref, ic_ref, yb_hbm, o_ref, ga, gb, sem):
    tm = h_ref.shape[0]
    n = pl.program_id(0) * pl.num_programs(1) + pl.program_id(1)
    total = pl.num_programs(0) * pl.num_programs(1)
    slot = lax.rem(n, 2)

    def issue(da_ref, db_ref, sl):
        def start(r, carry):
            for prio, (d_ref, buf) in enumerate(((da_ref, ga), (db_ref, gb))):
                pltpu.make_async_copy(yb_hbm.at[pl.ds(d_ref[0, 0, r], 1)], buf.at[sl, pl.ds(r, 1)],
                                      sem.at[sl]).start(priority=prio)
            return carry
        lax.fori_loop(0, tm, start, 0, unroll=8)

    @pl.when(n == 0)
    def _first():
        issue(d1_ref, d2_ref, 0)

    @pl.when(n + 1 < total)
    def _next():
        issue(n1_ref, n2_ref, 1 - slot)

    for buf in (ga, gb):
        pltpu.make_async_copy(yb_hbm.at[pl.ds(0, tm)], buf.at[slot], sem.at[slot]).wait()
    ic = ic_ref[...]
    o_ref[0] = h_ref[...] + ic[:, 0:1] * ga[slot] + ic[:, 1:2] * gb[slot]


def _combine(dest1, dest2, h1, ic, yb, b, lp, tm):
    t, d = h1.shape
    per = lp // tm
    first = (FRONT_PAD + N_META) // tm
    steps = per - first
    nt = t // tm
    tile = lambda i, j: i * per + j + first

    def next_tile(i, j):
        nxt = jnp.minimum(i * steps + j + 1, b * steps - 1)
        return tile(nxt // steps, nxt % steps)

    cur = lambda: pl.BlockSpec((1, 1, tm), lambda i, j: (tile(i, j), 0, 0), memory_space=pltpu.SMEM)
    nxt = lambda: pl.BlockSpec((1, 1, tm), lambda i, j: (next_tile(i, j), 0, 0), memory_space=pltpu.SMEM)
    d1 = dest1.reshape(nt, 1, tm)
    d2 = dest2.reshape(nt, 1, tm)
    return pl.pallas_call(
        _combine_kernel,
        out_shape=jax.ShapeDtypeStruct((b, lp - FRONT_PAD - N_META, d), F32),
        grid=(b, steps),
        in_specs=[
            cur(), cur(), nxt(), nxt(),
            pl.BlockSpec((tm, d), lambda i, j: (tile(i, j), 0)),
            pl.BlockSpec((tm, LANES), lambda i, j: (tile(i, j), 0)),
            pl.BlockSpec(memory_space=pl.ANY),
        ],
        out_specs=pl.BlockSpec((1, tm, d), lambda i, j: (i, j, 0)),
        scratch_shapes=[pltpu.VMEM((2, tm, d), F32), pltpu.VMEM((2, tm, d), F32),
                        pltpu.SemaphoreType.DMA((2,))],
        compiler_params=pltpu.CompilerParams(
            dimension_semantics=("arbitrary", "arbitrary"), vmem_limit_bytes=VMEM_LIMIT),
        name="moe_combine",
    )(d1, d2, d1, d2, h1, ic, yb)


def _routing_tables(il, cnt, n_blocks, tm):
    bm = MOE_BLOCK
    counts = cnt[:, 0].astype(jnp.int32)
    padded = (counts + bm - 1) // bm * bm
    pad_end = jnp.cumsum(padded)
    pad_start = pad_end - padded
    valid = il[4] > 0
    spare = n_blocks * bm + jnp.arange(il.shape[1], dtype=jnp.int32) % tm
    dest1 = jnp.where(valid, pad_start[il[0]] + il[2], spare).astype(jnp.int32)
    dest2 = jnp.where(valid, pad_start[il[1]] + il[3], spare + tm).astype(jnp.int32)
    starts = jnp.arange(n_blocks, dtype=jnp.int32) * bm
    block_e = jnp.minimum(jnp.sum((pad_end[None, :] <= starts[:, None]).astype(jnp.int32), axis=1),
                          N_EXPERTS - 1)
    n_used = (pad_end[-1:] // bm).astype(jnp.int32)
    owns = counts > 0
    eid = jnp.arange(N_EXPERTS, dtype=jnp.int32)
    ordinal = jnp.cumsum(owns.astype(jnp.int32)) - 1
    later = (eid[None, :] > eid[:, None]) & owns[None, :]
    nxt = jnp.min(jnp.where(later, eid[None, :], N_EXPERTS), axis=1)
    nxt = jnp.where(nxt < N_EXPERTS, nxt, -1)
    return dest1, dest2, block_e, n_used, ordinal[block_e].astype(jnp.int32), nxt[block_e].astype(jnp.int32)


def _layer(h0, lp, l, norm1_w, w_in, rwkv_mu, rwkv_w0, rwkv_w2, rwkv_a0, rwkv_a2, rwkv_g2,
           rwkv_k_k, rwkv_k_a, rwkv_r_k, rwkv_ln_w, rwkv_ln_b, q_norm_w, k_norm_w,
           lambda_q1, lambda_k1, lambda_q2, lambda_k2, diff_subln_w, w_branch_rwkv, w_branch_diff,
           w_out, norm2_w, router_group_w, router_group_b, router_expert_w, router_expert_b,
           expert_w_gate, expert_w_up, expert_w_down, proj_tm, tok_tm):
    t, d = h0.shape
    b = t // lp
    lambda_init = 0.8 - 0.6 * math.exp(-0.3 * l)
    rw_cols = 3 * RWKV_WIDTH
    diff_cols = 2 * DIFF_QK_WIDTH + DIFF_V_WIDTH
    w_perm = jnp.concatenate([
        w_in[:, rw_cols + LORA_COLS + diff_cols:],
        w_in[:, rw_cols + LORA_COLS:rw_cols + LORA_COLS + diff_cols],
        w_in[:, :rw_cols + LORA_COLS],
    ], axis=1).astype(BF16)
    proj = _proj(h0, norm1_w, w_perm, proj_tm, 768)
    proj3 = proj.reshape(b, lp, IN_COLS)

    rw = _rwkv(proj3, rwkv_mu[:rw_cols], rwkv_mu[rw_cols:], rwkv_w0, rwkv_w2, rwkv_a0, rwkv_a2, rwkv_g2,
               rwkv_k_k, rwkv_k_a, rwkv_r_k.reshape(-1), rwkv_ln_w, rwkv_ln_b)
    lam4 = jnp.stack([lambda_q1, lambda_k1, lambda_q2, lambda_k2])
    da = _attention(proj3, q_norm_w, k_norm_w, lam4, diff_subln_w, lambda_init)

    wr = jnp.zeros((LANES, d), F32).at[0:N_GROUPS].set(router_group_w.T).at[8:8 + N_EXPERTS].set(router_expert_w.T)
    br = jnp.zeros((LANES,), F32).at[0:N_GROUPS].set(router_group_b).at[N_GROUPS:8].set(NEG_BIG)
    br = br.at[8:8 + N_EXPERTS].set(router_expert_b).reshape(LANES, 1)
    h1, u2, il, ic, cnt = _merge(
        rw.reshape(t, RWKV_WIDTH), da.reshape(t, DIFF_V_WIDTH), proj, h0,
        w_branch_rwkv.astype(BF16), w_branch_diff.astype(BF16), w_out.astype(BF16), norm2_w, wr, br, lp, tok_tm)

    n_real = b * (lp - FRONT_PAD)
    n_blocks = -(-(2 * n_real) // MOE_BLOCK) + N_EXPERTS
    dest1, dest2, block_e, n_used, block_grp, block_nxt = _routing_tables(il, cnt, n_blocks, tok_tm)
    xb = _dispatch(dest1, dest2, u2, n_blocks * MOE_BLOCK + 2 * tok_tm, tok_tm)
    yb = _moe(block_e, n_used, block_grp, block_nxt, xb, expert_w_gate, expert_w_up, expert_w_down)
    return h1, ic, dest1, dest2, yb


def kernel(x, meta_tokens, norm1_w, w_in, rwkv_mu, rwkv_w0, rwkv_w2, rwkv_a0, rwkv_a2, rwkv_g2, rwkv_k_k, rwkv_k_a, rwkv_r_k, rwkv_ln_w, rwkv_ln_b, q_norm_w, k_norm_w, lambda_q1, lambda_k1, lambda_q2, lambda_k2, diff_subln_w, w_branch_rwkv, w_branch_diff, w_out, norm2_w, router_group_w, router_group_b, router_expert_w, router_expert_b, expert_w_gate, expert_w_up, expert_w_down):
    b, seq, d = x.shape
    depth = norm1_w.shape[0]
    assert depth == 1, "the combine step emits the final output; deeper stacks need an intermediate form"
    lp = FRONT_PAD + N_META + seq
    meta = jnp.broadcast_to(meta_tokens[None].astype(x.dtype), (b, N_META, d))
    h0 = jnp.concatenate([jnp.zeros((b, FRONT_PAD, d), x.dtype), meta, x], axis=1).reshape(b * lp, d)
    proj_tm = 2048 if (b * lp) % 2048 == 0 else 128
    tok_tm = 512 if (b * lp) % 512 == 0 else 128
    l = 0
    h1, ic, dest1, dest2, yb = _layer(
        h0, lp, l, norm1_w[l], w_in[l], rwkv_mu[l], rwkv_w0[l], rwkv_w2[l], rwkv_a0[l], rwkv_a2[l],
        rwkv_g2[l], rwkv_k_k[l], rwkv_k_a[l], rwkv_r_k[l], rwkv_ln_w[l], rwkv_ln_b[l], q_norm_w[l],
        k_norm_w[l], lambda_q1[l], lambda_k1[l], lambda_q2[l], lambda_k2[l], diff_subln_w[l],
        w_branch_rwkv[l], w_branch_diff[l], w_out[l], norm2_w[l], router_group_w[l], router_group_b[l],
        router_expert_w[l], router_expert_b[l], expert_w_gate[l], expert_w_up[l], expert_w_down[l],
        proj_tm, tok_tm)
    return _combine(dest1, dest2, h1, ic, yb, b, lp, OUT_TILE)
```

```python
import functools
import math

import jax
import jax.numpy as jnp
from jax import lax
from jax.experimental import pallas as pl
from jax.experimental.pallas import tpu as pltpu

F32 = jnp.float32
BF16 = jnp.bfloat16

D_MODEL = 1024
N_META = 16
NORM_EPS = 1e-6
RWKV_HEADS = 16
RWKV_HEAD_DIM = 64
RWKV_WIDTH = RWKV_HEADS * RWKV_HEAD_DIM
DECAY_LORA = 64
AAA_LORA = 64
GATE_LORA = 128
LORA_COLS = DECAY_LORA + AAA_LORA + GATE_LORA
RWKV_GN_EPS = 64e-5
DIFF_HEADS = 8
DIFF_HEAD_DIM = 64
DIFF_V_DIM = 2 * DIFF_HEAD_DIM
DIFF_QK_WIDTH = DIFF_HEADS * 2 * DIFF_HEAD_DIM
DIFF_V_WIDTH = DIFF_HEADS * DIFF_V_DIM
ROPE_THETA = 500000.0
ROPE_DIM = DIFF_HEAD_DIM // 4
N_GROUPS = 4
EXPERTS_PER_GROUP = 8
N_EXPERTS = N_GROUPS * EXPERTS_PER_GROUP
EXPERT_FF = 512

LANES = 128
ATTN_BLOCK = 256
ATTN_KEY_BLOCK = 512
ATTN_HEADS_PER_STEP = 4
FRONT_PAD = ATTN_BLOCK - N_META
RWKV_CHUNK = 64
RWKV_SUBCHUNKS = 3
RWKV_HEAD_GROUP = 2
OUT_TILE = 128
MOE_BLOCK = 256
NEG_BIG = -1e30
VMEM_LIMIT = 48 * 1024 * 1024

COL_GATE = 0
COL_DIFF = 2 * D_MODEL
COL_RWKV = COL_DIFF + 2 * DIFF_QK_WIDTH + DIFF_V_WIDTH
COL_LORA = COL_RWKV + 3 * RWKV_WIDTH
IN_COLS = COL_LORA + LORA_COLS


def _dot(a, b):
    return jnp.dot(a, b, preferred_element_type=F32)


def _dot_nt(a, b):
    return lax.dot_general(a, b, (((1,), (1,)), ((), ())), preferred_element_type=F32)


def _split2(x):
    hi = x.astype(BF16)
    lo = (x - hi.astype(F32)).astype(BF16)
    return hi, lo


def _mm3(a, b_ref):
    ah, al = _split2(a)
    return _dot(ah, b_ref[0]) + _dot(ah, b_ref[1]) + _dot(al, b_ref[0])


def _hi_lo(w):
    hi = w.astype(BF16)
    return jnp.stack([hi, (w - hi.astype(F32)).astype(BF16)])


def _sigmoid(x):
    return 1.0 / (1.0 + jnp.exp(-x))


def _head_ones(width=LANES, head=RWKV_HEAD_DIM):
    r = lax.broadcasted_iota(jnp.int32, (width, width), 0) // head
    c = lax.broadcasted_iota(jnp.int32, (width, width), 1) // head
    return jnp.where(r == c, 1.0, 0.0).astype(BF16)


def _seg_sum(x, ones_bd):
    hi, lo = _split2(x)
    return _dot(hi, ones_bd) + _dot(lo, ones_bd)


def _seg_sum_wide(x, ones_bd):
    rows, width = x.shape
    n = width // LANES
    xs = jnp.concatenate([x[:, i * LANES:(i + 1) * LANES] for i in range(n)], axis=0)
    ys = _seg_sum(xs, ones_bd)
    return jnp.concatenate([ys[i * rows:(i + 1) * rows] for i in range(n)], axis=1)


def _proj_kernel(h_ref, nw_ref, w_ref, o_ref, u_ref):
    @pl.when(pl.program_id(1) == 0)
    def _():
        x = h_ref[...]
        ms = jnp.mean(x * x, axis=-1, keepdims=True)
        u_ref[...] = (x * lax.rsqrt(ms + NORM_EPS) * nw_ref[...]).astype(BF16)

    o_ref[...] = _dot(u_ref[...], w_ref[...]).astype(o_ref.dtype)


def _proj(h, norm_w, w_bf16, tm, tn):
    t, d = h.shape
    n = w_bf16.shape[1]
    return pl.pallas_call(
        _proj_kernel,
        out_shape=jax.ShapeDtypeStruct((t, n), BF16),
        grid=(t // tm, n // tn),
        in_specs=[
            pl.BlockSpec((tm, d), lambda i, j: (i, 0)),
            pl.BlockSpec((1, d), lambda i, j: (0, 0)),
            pl.BlockSpec((d, tn), lambda i, j: (0, j)),
        ],
        out_specs=pl.BlockSpec((tm, tn), lambda i, j: (i, j)),
        scratch_shapes=[pltpu.VMEM((tm, d), BF16)],
        compiler_params=pltpu.CompilerParams(
            dimension_semantics=("parallel", "arbitrary"), vmem_limit_bytes=VMEM_LIMIT),
        name="norm_proj",
    )(h, norm_w.reshape(1, d), w_bf16)


def _rwkv_kernel(*refs):
    o_ref = refs[16]
    xr, xk, xv, xl, s_ref = refs[17:22]
    c = pl.program_id(1)
    n_skip = FRONT_PAD // o_ref.shape[1]

    @pl.when(c == 0)
    def _init():
        for xs in (xr, xk, xv, xl):
            xs[0:8, :] = jnp.zeros((8, xs.shape[1]), F32)
        s_ref[...] = jnp.zeros(s_ref.shape, F32)

    @pl.when(c < n_skip)
    def _pad():
        o_ref[...] = jnp.zeros(o_ref.shape, o_ref.dtype)

    @pl.when(c >= n_skip)
    def _chunk():
        _rwkv_chunk(*refs)


def _rwkv_chunk(r_ref, k_ref, v_ref, lo_ref, mu_ref, mul_ref, w0_ref, w2_ref, a0_ref, a2_ref, g2_ref,
                kkw_ref, kaw_ref, rkw_ref, lnw_ref, lnb_ref, o_ref,
                xr, xk, xv, xl, s_ref, kt_s, bt_s, kn_s, rt_s, v_s, y_s, gc_s, bon_s, g_s):
    R = r_ref.shape[1]
    C = RWKV_CHUNK
    n_sub = R // C
    W = RWKV_WIDTH

    def shift_mix(in_ref, xs, mu):
        x = in_ref[0].astype(F32)
        xs[8:8 + R, :] = x
        prev = xs[7:7 + R, :]
        xs[7:8, :] = x[R - 1:R, :]
        return x + (prev - x) * mu

    r = shift_mix(r_ref, xr, mu_ref[:, 0:W])
    k = shift_mix(k_ref, xk, mu_ref[:, W:2 * W])
    v = shift_mix(v_ref, xv, mu_ref[:, 2 * W:3 * W])
    lo = shift_mix(lo_ref, xl, mul_ref[...])
    xw = jnp.tanh(lo[:, 0:DECAY_LORA])
    xa = lo[:, DECAY_LORA:DECAY_LORA + AAA_LORA]
    xg = _sigmoid(lo[:, DECAY_LORA + AAA_LORA:LORA_COLS])

    lw = (-math.exp(-0.5)) * _sigmoid(w0_ref[...] + _mm3(xw, w2_ref))
    a = _sigmoid(a0_ref[...] + _mm3(xa, a2_ref))
    g_s[...] = _mm3(xg, g2_ref)

    ones_bd = _head_ones()
    kk = k * kkw_ref[...]
    kkn = kk * lax.rsqrt(jnp.maximum(_seg_sum_wide(kk * kk, ones_bd), 1e-24))
    k2 = k * (1.0 + (a - 1.0) * kaw_ref[...])
    bon_s[...] = _seg_sum_wide(r * k2 * rkw_ref[...], ones_bd) * v

    ti = lax.broadcasted_iota(jnp.int32, (R, R), 0)
    tj = lax.broadcasted_iota(jnp.int32, (R, R), 1)
    ltri = jnp.where((ti >= tj) & (ti // C == tj // C), 1.0, 0.0).astype(BF16)
    l1 = lw.astype(BF16)
    rem = lw - l1.astype(F32)
    l2 = rem.astype(BF16)
    l3 = (rem - l2.astype(F32)).astype(BF16)
    cum = _dot(ltri, l1) + _dot(ltri, l2) + _dot(ltri, l3)
    e_pos = jnp.exp(cum)
    e_neg = jnp.exp(-cum)
    kt_s[...] = kkn * jnp.exp(cum - lw)
    bt_s[...] = kkn * a * e_neg
    kn_s[...] = k2 * e_neg
    rt_s[...] = r * e_pos
    v_s[...] = v
    for s in range(n_sub):
        gc_s[s:s + 1, :] = jnp.exp(cum[(s + 1) * C - 1:(s + 1) * C, :])

    G = RWKV_HEAD_GROUP
    GL = G * RWKV_HEAD_DIM
    GC = G * C
    n_groups = W // GL
    lane_head = lax.broadcasted_iota(jnp.int32, (1, GL), 1) // RWKV_HEAD_DIM
    head_mask = [jnp.where(lane_head == h, 1.0, 0.0).astype(BF16) for h in range(G)]
    trow = lax.broadcasted_iota(jnp.int32, (C, GC), 0)
    tcol = lax.broadcasted_iota(jnp.int32, (C, GC), 1) % C
    strict = trow > tcol
    incl = trow >= tcol
    eye = jnp.where(trow == tcol, 1.0, 0.0)
    br = lax.broadcasted_iota(jnp.int32, (GL, GL), 0) // RWKV_HEAD_DIM
    bc = lax.broadcasted_iota(jnp.int32, (GL, GL), 1) // RWKV_HEAD_DIM
    bdmask = jnp.where(br == bc, 1.0, 0.0)

    def stack(y):
        yb = y.astype(BF16)
        return jnp.concatenate([yb * m for m in head_mask], axis=0)

    groups = range(n_groups)
    units = [(s, p) for s in range(n_sub) for p in groups]
    win = {(s, p): (slice(s * C, (s + 1) * C), slice(p * GL, (p + 1) * GL)) for s, p in units}
    def prepare(units):
        kt = {q: kt_s[win[q]] for q in units}
        bt = {q: bt_s[win[q]] for q in units}
        kn = {q: kn_s[win[q]] for q in units}
        rt = {q: rt_s[win[q]] for q in units}
        vv = {q: v_s[win[q]] for q in units}
        gc = {q: gc_s[q[0]:q[0] + 1, win[q][1]] for q in units}
        kr = {q: jnp.concatenate([kt[q], rt[q]], axis=0).astype(BF16) for q in units}
        m_all = {q: _dot_nt(kr[q], jnp.concatenate([stack(bt[q]), stack(kn[q])], axis=0))
                 for q in units}
        p_mat = {q: jnp.where(incl, m_all[q][C:2 * C, 0:GC], 0.0).astype(BF16) for q in units}
        bq_mat = {q: jnp.concatenate([jnp.where(strict, m_all[q][0:C, GC:2 * GC], 0.0),
                                      jnp.where(incl, m_all[q][C:2 * C, GC:2 * GC], 0.0)], axis=0).astype(BF16)
                  for q in units}
        v_term = {q: _dot(bq_mat[q], stack(vv[q])) for q in units}

        pw = {q: -jnp.where(strict, m_all[q][0:C, 0:GC], 0.0) for q in units}
        t_inv = {q: eye + pw[q] for q in units}
        pw = {q: _dot(pw[q].astype(BF16), stack(pw[q])) for q in units}
        for _ in range(int(math.log2(C)) - 2):
            both = {q: _dot(jnp.concatenate([t_inv[q], pw[q]], axis=0).astype(BF16), stack(pw[q]))
                    for q in units}
            t_inv = {q: t_inv[q] + both[q][0:C] for q in units}
            pw = {q: both[q][C:2 * C] for q in units}
        t_inv = {q: (t_inv[q] + _dot(t_inv[q].astype(BF16), stack(pw[q]))).astype(BF16) for q in units}
        xc = {q: jnp.concatenate([bt[q] * gc[q], kn[q] * gc[q]], axis=0).astype(BF16) for q in units}
        return kr, p_mat, v_term, t_inv, xc, vv, gc

    kr, p_mat, v_term, t_inv, xc, vv, gc = prepare(units)

    state = [s_ref[p] for p in groups]
    for s in range(n_sub):
        qs = [(s, p) for p in groups]
        ks = [_dot_nt(kr[q], state[q[1]].astype(BF16)) for q in qs]
        u = [-_dot(t_inv[q], stack(ks[q[1]][0:C] + v_term[q][0:C])) for q in qs]
        for q in qs:
            y_s[win[q]] = ks[q[1]][C:2 * C] + v_term[q][C:2 * C] + _dot(p_mat[q], stack(u[q[1]]))
        for q in qs:
            uc = jnp.concatenate([u[q[1]], vv[q]], axis=0)
            state[q[1]] = state[q[1]] * gc[q] + bdmask * _dot(jnp.transpose(uc).astype(BF16), xc[q])
    for p in groups:
        s_ref[p] = state[p]

    y = y_s[...]
    inv_n = 1.0 / RWKV_HEAD_DIM
    mean = _seg_sum_wide(y, ones_bd) * inv_n
    dlt = y - mean
    var = _seg_sum_wide(dlt * dlt, ones_bd) * inv_n
    yn = dlt * lax.rsqrt(var + RWKV_GN_EPS) * lnw_ref[...] + lnb_ref[...]
    o_ref[0] = ((yn + bon_s[...]) * g_s[...]).astype(o_ref.dtype)


def _rwkv(proj3, mu_rkv, mu_lo, w0, w2, a0, a2, g2, k_k, k_a, r_k, ln_w, ln_b):
    b, lp, _ = proj3.shape
    C = RWKV_CHUNK * RWKV_SUBCHUNKS
    W = RWKV_WIDTH
    cb = COL_RWKV // W
    lb = COL_LORA // LORA_COLS
    row = lambda x: x.reshape(1, -1)
    full = lambda shape: pl.BlockSpec(shape, lambda i, c: (0,) * len(shape))
    wide = pltpu.VMEM((C, W), F32)
    gl = RWKV_HEAD_GROUP * RWKV_HEAD_DIM
    return pl.pallas_call(
        _rwkv_kernel,
        out_shape=jax.ShapeDtypeStruct((b, lp, W), BF16),
        grid=(b, lp // C),
        in_specs=[
            pl.BlockSpec((1, C, W), lambda i, c: (i, c, cb)),
            pl.BlockSpec((1, C, W), lambda i, c: (i, c, cb + 1)),
            pl.BlockSpec((1, C, W), lambda i, c: (i, c, cb + 2)),
            pl.BlockSpec((1, C, LORA_COLS), lambda i, c: (i, c, lb)),
            full((1, 3 * W)), full((1, LORA_COLS)),
            full((1, W)), full((2, DECAY_LORA, W)), full((1, W)), full((2, AAA_LORA, W)), full((2, GATE_LORA, W)),
            full((1, W)), full((1, W)), full((1, W)), full((1, W)), full((1, W)),
        ],
        out_specs=pl.BlockSpec((1, C, W), lambda i, c: (i, c, 0)),
        scratch_shapes=[
            pltpu.VMEM((C + 8, W), F32), pltpu.VMEM((C + 8, W), F32), pltpu.VMEM((C + 8, W), F32),
            pltpu.VMEM((C + 8, LORA_COLS), F32),
            pltpu.VMEM((W // gl, gl, gl), F32),
            wide, wide, wide, wide, wide, wide, pltpu.VMEM((8, W), F32), wide, wide,
        ],
        compiler_params=pltpu.CompilerParams(
            dimension_semantics=("parallel", "arbitrary"), vmem_limit_bytes=VMEM_LIMIT),
        name="rwkv7_time_mix",
    )(proj3, proj3, proj3, proj3, row(mu_rkv), row(mu_lo), row(w0), _hi_lo(w2), row(a0), _hi_lo(a2), _hi_lo(g2),
      row(k_k), row(k_a), row(r_k), row(ln_w), row(ln_b))


def _attn_kernel(q_ref, k_ref, v_ref, cos_ref, s1_ref, s2_ref, qw_ref, kw_ref, lam_ref, sw_ref, o_ref,
                 kp_s, vp_s, m_s, acc_s, *, tq, tk, lambda_init):
    qi = pl.program_id(2)
    lp = k_ref.shape[1]
    n_real = lp - FRONT_PAD
    lk = kp_s.shape[0]
    nh = k_ref.shape[2] // LANES
    vw = 2 * LANES
    heads = range(nh)
    hs = [slice(h * LANES, (h + 1) * LANES) for h in heads]
    ones_bd = _head_ones(LANES, DIFF_HEAD_DIM)
    shift = ROPE_DIM // 2

    def norm_rope(x, w, rows):
        ms = _seg_sum(x * x, ones_bd) * (1.0 / DIFF_HEAD_DIM)
        xn = x * lax.rsqrt(ms + NORM_EPS) * w
        return (xn * cos_ref[rows, :] + pltpu.roll(xn, shift, 1) * s1_ref[rows, :]
                + pltpu.roll(xn, LANES - shift, 1) * s2_ref[rows, :])

    @pl.when(qi == 0)
    def _prep():
        def put(dst, n):
            src = pl.ds(FRONT_PAD + dst, n)
            rows = pl.ds(dst, n)
            for h in heads:
                kp_s[rows, hs[h]] = norm_rope(k_ref[0, src, hs[h]].astype(F32), kw_ref[...], src).astype(BF16)
                vp_s[rows, h * vw:h * vw + LANES] = v_ref[0, src, hs[h]].astype(BF16)
                vp_s[rows, h * vw + LANES:(h + 1) * vw] = jnp.ones((n, LANES), BF16)

        def body(i, carry):
            put(pl.multiple_of(i * LANES, LANES), LANES)
            return carry
        lax.fori_loop(0, n_real // LANES, body, 0, unroll=2)
        tail = n_real % LANES
        if tail:
            put(n_real - tail, tail)
        if lk > n_real:
            kp_s[n_real:lk, :] = jnp.zeros((lk - n_real, nh * LANES), BF16)
            vp_s[n_real:lk, :] = jnp.zeros((lk - n_real, nh * vw), BF16)

    lane = lax.broadcasted_iota(jnp.int32, (1, LANES), 1)
    m0 = jnp.where(lane < DIFF_HEAD_DIM, 1.0, 0.0)
    m1 = 1.0 - m0
    rows_q = pl.ds(pl.multiple_of(qi * tq, tq), tq)
    qs = []
    for h in heads:
        qn = norm_rope(q_ref[0, :, hs[h]].astype(F32), qw_ref[...], rows_q) * (DIFF_HEAD_DIM ** -0.5)
        qs.append(jnp.concatenate([qn * m0, qn * m1], axis=0).astype(BF16))
    first_row = qi * tq - FRONT_PAD
    row = first_row + lax.broadcasted_iota(jnp.int32, (2 * tq, LANES), 0) % tq
    col0 = lax.broadcasted_iota(jnp.int32, (2 * tq, LANES), 1)

    def step(j, width, causal, init=False):
        start = j * tk
        if not isinstance(j, int):
            start = pl.multiple_of(start, tk)
        ks = pl.ds(start, width)
        n_sub = width // LANES
        s = [_dot_nt(qs[h], kp_s[ks, hs[h]]) for h in heads]
        sub = [[s[h][:, c * LANES:(c + 1) * LANES] for c in range(n_sub)] for h in heads]
        if causal:
            sub = [[jnp.where(start + c * LANES + col0 <= row, sub[h][c], NEG_BIG) for c in range(n_sub)]
                   for h in heads]
        m_new = []
        for h in heads:
            mx = functools.reduce(jnp.maximum, sub[h])
            mx = jnp.broadcast_to(jnp.max(mx, axis=1, keepdims=True), mx.shape)
            m_new.append(mx if init else jnp.maximum(m_s[h], mx))
        p = [jnp.concatenate([jnp.exp(sub[h][c] - m_new[h]) for c in range(n_sub)], axis=1).astype(BF16)
             for h in heads]
        pv = [_dot(p[h], vp_s[ks, h * vw:(h + 1) * vw]) for h in heads]
        for h in heads:
            if init:
                acc_s[h] = pv[h]
            else:
                alpha = jnp.exp(m_s[h] - m_new[h])
                acc_s[h] = jnp.concatenate([alpha, alpha], axis=1) * acc_s[h] + pv[h]
            m_s[h] = m_new[h]

    assert FRONT_PAD + N_META == tq and tk == 2 * tq and N_META <= LANES
    n_full = jnp.maximum(first_row + 1, 0) // tk
    last = jnp.maximum(first_row + tq - 1, 0) // tk

    @pl.when(qi == 0)
    def _meta():
        step(0, LANES, True, init=True)

    @pl.when(lax.rem(qi, 2) == 1)
    def _odd():
        step(n_full, tq + LANES, True, init=True)

    @pl.when((qi > 0) & (lax.rem(qi, 2) == 0))
    def _even():
        step(n_full, tk, True, init=True)

    def mid(j, carry):
        step(j, tk, False)
        return carry
    lax.fori_loop(0, n_full, mid, 0)

    @pl.when(last > n_full)
    def _diag():
        step(last, LANES, True)

    lam = (jnp.exp(jnp.sum(lam_ref[0:1, :] * lam_ref[1:2, :], axis=1, keepdims=True))
           - jnp.exp(jnp.sum(lam_ref[2:3, :] * lam_ref[3:4, :], axis=1, keepdims=True)) + lambda_init)
    for h in heads:
        acc = acc_s[h]
        o = acc[:, 0:LANES] / acc[:, LANES:vw]
        od = o[0:tq] - lam * o[tq:2 * tq]
        ms = jnp.mean(od * od, axis=1, keepdims=True)
        o_ref[0, :, hs[h]] = (od * lax.rsqrt(ms + NORM_EPS) * sw_ref[...]
                              * (1.0 - lambda_init)).astype(o_ref.dtype)


def _rope_tables(lp):
    half = ROPE_DIM // 2
    inv_freq = jnp.exp(-math.log(ROPE_THETA) * jnp.arange(half, dtype=F32) * 2.0 / ROPE_DIM)
    pos = (jnp.arange(lp) - FRONT_PAD).astype(F32)
    ang = pos[:, None] * inv_freq[None, :]
    cos, sin = jnp.cos(ang), jnp.sin(ang)
    one = jnp.ones((lp, DIFF_HEAD_DIM - ROPE_DIM), F32)
    zero = jnp.zeros((lp, DIFF_HEAD_DIM - ROPE_DIM), F32)
    zh = jnp.zeros((lp, half), F32)
    c = jnp.concatenate([cos, cos, one], axis=1)
    s1 = jnp.concatenate([zh, sin, zero], axis=1)
    s2 = jnp.concatenate([-sin, zh, zero], axis=1)
    dup = lambda t: jnp.concatenate([t, t], axis=1)
    return dup(c), dup(s1), dup(s2)


def _attention(proj3, q_norm_w, k_norm_w, lam4, subln_w, lambda_init):
    b, lp, _ = proj3.shape
    tq, tk, nh = ATTN_BLOCK, ATTN_KEY_BLOCK, ATTN_HEADS_PER_STEP
    lk = -(-(lp - FRONT_PAD) // tk) * tk
    hw = nh * LANES
    qb = COL_DIFF // hw
    kb = qb + DIFF_QK_WIDTH // hw
    vb = kb + DIFF_QK_WIDTH // hw
    cos, s1, s2 = _rope_tables(lp)
    dup = lambda w: jnp.concatenate([w, w]).reshape(1, LANES)
    full = lambda shape: pl.BlockSpec(shape, lambda i, h, q: (0,) * len(shape))
    return pl.pallas_call(
        functools.partial(_attn_kernel, tq=tq, tk=tk, lambda_init=lambda_init),
        out_shape=jax.ShapeDtypeStruct((b, lp, DIFF_V_WIDTH), BF16),
        grid=(b, DIFF_HEADS // nh, lp // tq),
        in_specs=[
            pl.BlockSpec((1, tq, hw), lambda i, h, q: (i, q, qb + h)),
            pl.BlockSpec((1, lp, hw), lambda i, h, q: (i, 0, kb + h)),
            pl.BlockSpec((1, lp, hw), lambda i, h, q: (i, 0, vb + h)),
            full((lp, LANES)), full((lp, LANES)), full((lp, LANES)),
            full((1, LANES)), full((1, LANES)), full((4, DIFF_HEAD_DIM)), full((1, LANES)),
        ],
        out_specs=pl.BlockSpec((1, tq, hw), lambda i, h, q: (i, q, h)),
        scratch_shapes=[pltpu.VMEM((lk, hw), BF16), pltpu.VMEM((lk, 2 * hw), BF16),
                        pltpu.VMEM((nh, 2 * tq, LANES), F32), pltpu.VMEM((nh, 2 * tq, 2 * LANES), F32)],
        compiler_params=pltpu.CompilerParams(
            dimension_semantics=("parallel", "parallel", "arbitrary"), vmem_limit_bytes=VMEM_LIMIT),
        name="diff_attention",
    )(proj3, proj3, proj3, cos, s1, s2, dup(q_norm_w), dup(k_norm_w), lam4, subln_w.reshape(1, LANES))


def _merge_kernel(rw_ref, da_ref, g1_ref, g2_ref, h_ref, wbr_ref, wbd_ref, wo_ref, n2_ref, wr_ref, br_ref,
                  h1_ref, u2_ref, il_ref, ic_ref, cnt_ref, base_s, *, lp):
    i = pl.program_id(0)
    tm = rw_ref.shape[0]

    @pl.when(i == 0)
    def _():
        base_s[...] = jnp.zeros(base_s.shape, F32)

    y1 = _dot(rw_ref[...], wbr_ref[...])
    y2 = _dot(da_ref[...], wbd_ref[...])
    merged = _sigmoid(g1_ref[...].astype(F32)) * y1 + _sigmoid(g2_ref[...].astype(F32)) * y2
    h1 = h_ref[...] + _dot(merged.astype(BF16), wo_ref[...])
    h1_ref[...] = h1
    u2 = h1 * lax.rsqrt(jnp.mean(h1 * h1, axis=-1, keepdims=True) + NORM_EPS) * n2_ref[...]
    u2_ref[...] = u2

    uh, ul = _split2(u2)
    wh, wl = _split2(wr_ref[...])
    lt = _dot_nt(wh, uh) + _dot_nt(wh, ul) + _dot_nt(wl, uh) + br_ref[...]

    gi8 = lax.broadcasted_iota(jnp.int32, (8, tm), 0)
    lg = lt[0:8]
    ge = jnp.exp(lg - jnp.max(lg, axis=0, keepdims=True))
    gp = ge / jnp.sum(ge, axis=0, keepdims=True)
    gv = jnp.max(gp, axis=0, keepdims=True)
    gidx = jnp.min(jnp.where(gp == gv, gi8, N_EXPERTS), axis=0, keepdims=True)

    ei = lax.broadcasted_iota(jnp.int32, (N_EXPERTS, tm), 0)
    sel = (ei // EXPERTS_PER_GROUP) == gidx
    le = jnp.where(sel, lt[8:8 + N_EXPERTS], NEG_BIG)
    ee = jnp.where(sel, jnp.exp(le - jnp.max(le, axis=0, keepdims=True)), 0.0)
    ep = jnp.where(sel, ee / jnp.sum(ee, axis=0, keepdims=True), -1.0)
    v1 = jnp.max(ep, axis=0, keepdims=True)
    i1 = jnp.min(jnp.where(ep == v1, ei, N_EXPERTS), axis=0, keepdims=True)
    ep2 = jnp.where(ei == i1, -1.0, ep)
    v2 = jnp.max(ep2, axis=0, keepdims=True)
    i2 = jnp.min(jnp.where(ep2 == v2, ei, N_EXPERTS), axis=0, keepdims=True)
    den = v1 + v2
    gate1 = gv * v1 / den
    gate2 = gv * v2 / den

    tok = (i * tm + lax.broadcasted_iota(jnp.int32, (1, tm), 1)).astype(F32)
    pos = tok - jnp.floor((tok + 0.5) / lp) * lp
    valid = pos > (FRONT_PAD - 0.5)

    oh1 = jnp.where((ei == i1) & valid, 1.0, 0.0)
    oh2 = jnp.where((ei == i2) & valid, 1.0, 0.0)
    oh = oh1 + oh2
    ur = lax.broadcasted_iota(jnp.int32, (tm, tm), 0)
    uc = lax.broadcasted_iota(jnp.int32, (tm, tm), 1)
    before = jnp.where(ur < uc, 1.0, 0.0).astype(BF16)
    tot = base_s[:, 0:1] + _dot(oh.astype(BF16), before)
    rank1 = jnp.sum(oh1 * tot, axis=0, keepdims=True)
    rank2 = jnp.sum(oh2 * tot, axis=0, keepdims=True)
    base_s[...] = base_s[...] + jnp.sum(oh, axis=1, keepdims=True)
    cnt_ref[...] = base_s[...]

    il = jnp.where(gi8 == 0, i1, jnp.where(gi8 == 1, i2, jnp.where(
        gi8 == 2, rank1.astype(jnp.int32), jnp.where(gi8 == 3, rank2.astype(jnp.int32), jnp.where(
            gi8 == 4, valid.astype(jnp.int32), 0)))))
    il_ref[...] = il
    ri = lax.broadcasted_iota(jnp.int32, (LANES, tm), 0)
    ic = jnp.where(ri == 0, gate1, jnp.where(ri == 1, gate2, 0.0))
    ic_ref[...] = jnp.transpose(ic)


def _merge(rw, da, proj, h0, wbr, wbd, wo, norm2_w, wr, br, lp, tm):
    t, d = h0.shape
    gb = COL_GATE // d
    full = lambda shape: pl.BlockSpec(shape, lambda i: (0,) * len(shape))
    tile = lambda c: pl.BlockSpec((tm, d), lambda i: (i, c))
    return pl.pallas_call(
        functools.partial(_merge_kernel, lp=lp),
        out_shape=(
            jax.ShapeDtypeStruct((t, d), F32),
            jax.ShapeDtypeStruct((t, d), F32),
            jax.ShapeDtypeStruct((8, t), jnp.int32),
            jax.ShapeDtypeStruct((t, LANES), F32),
            jax.ShapeDtypeStruct((N_EXPERTS, LANES), F32),
        ),
        grid=(t // tm,),
        in_specs=[tile(0), tile(0), tile(gb), tile(gb + 1), tile(0),
                  full((d, d)), full((d, d)), full((d, d)), full((1, d)), full((LANES, d)), full((LANES, 1))],
        out_specs=(
            tile(0), tile(0),
            pl.BlockSpec((8, tm), lambda i: (0, i)),
            pl.BlockSpec((tm, LANES), lambda i: (i, 0)),
            full((N_EXPERTS, LANES)),
        ),
        scratch_shapes=[pltpu.VMEM((N_EXPERTS, LANES), F32)],
        compiler_params=pltpu.CompilerParams(
            dimension_semantics=("arbitrary",), vmem_limit_bytes=VMEM_LIMIT),
        name="merge_router",
    )(rw, da, proj, proj, h0, wbr, wbd, wo, norm2_w.reshape(1, d), wr, br)


def _dispatch_kernel(d1_ref, d2_ref, u_ref, xin_hbm, xb_hbm, sem):
    del xin_hbm
    tm = d1_ref.shape[2]

    def start(r, carry):
        for prio, d_ref in enumerate((d1_ref, d2_ref)):
            pltpu.make_async_copy(u_ref.at[pl.ds(r, 1)], xb_hbm.at[pl.ds(d_ref[0, 0, r], 1)],
                                  sem).start(priority=prio)
        return carry

    lax.fori_loop(0, tm, start, 0, unroll=8)
    for _ in range(2):
        pltpu.make_async_copy(u_ref, xb_hbm.at[pl.ds(0, tm)], sem).wait()


def _dispatch(dest1, dest2, u2, cap, tm):
    t, d = u2.shape
    nt = t // tm
    smem = lambda: pl.BlockSpec((1, 1, tm), lambda i: (i, 0, 0), memory_space=pltpu.SMEM)
    return pl.pallas_call(
        _dispatch_kernel,
        out_shape=jax.ShapeDtypeStruct((cap, d), F32),
        grid=(nt,),
        in_specs=[smem(), smem(), pl.BlockSpec((tm, d), lambda i: (i, 0)), pl.BlockSpec(memory_space=pl.ANY)],
        out_specs=pl.BlockSpec(memory_space=pl.ANY),
        scratch_shapes=[pltpu.SemaphoreType.DMA(())],
        input_output_aliases={3: 0},
        compiler_params=pltpu.CompilerParams(
            dimension_semantics=("arbitrary",), vmem_limit_bytes=VMEM_LIMIT),
        name="moe_dispatch",
    )(dest1.reshape(nt, 1, tm), dest2.reshape(nt, 1, tm), u2, jnp.zeros((cap, d), F32))


def _moe_kernel(be_ref, nb_ref, grp_ref, nxt_ref, x_ref, wg_hbm, wu_hbm, wd_hbm, o_ref,
                wg_f, wu_f, wd_f, wg_s, wu_s, wd_s, sem):
    i = pl.program_id(0)
    used = i < nb_ref[0]
    slot = lax.rem(grp_ref[i], 2)

    def fetch(expert, sl):
        return [pltpu.make_async_copy(w_hbm.at[expert], w_f.at[sl], sem.at[sl])
                for w_hbm, w_f in ((wg_hbm, wg_f), (wu_hbm, wu_f), (wd_hbm, wd_f))]

    @pl.when(used & ((i == 0) | (grp_ref[i] != grp_ref[jnp.maximum(i - 1, 0)])))
    def _():
        @pl.when(i == 0)
        def _():
            for cp in fetch(be_ref[0], 0):
                cp.start()

        @pl.when(nxt_ref[i] >= 0)
        def _():
            for cp in fetch(nxt_ref[i], 1 - slot):
                cp.start()

        for cp in fetch(be_ref[i], slot):
            cp.wait()
        wg_s[...] = wg_f[slot].astype(BF16)
        wu_s[...] = wu_f[slot].astype(BF16)
        wd_s[...] = wd_f[slot].astype(BF16)

    @pl.when(used)
    def _():
        x = x_ref[...].astype(BF16)
        hg = _dot(x, wg_s[...])
        hu = _dot(x, wu_s[...])
        hid = hg * _sigmoid(hg) * hu
        o_ref[...] = _dot(hid.astype(BF16), wd_s[...])

    @pl.when(i >= nb_ref[0])
    def _():
        o_ref[...] = jnp.zeros(o_ref.shape, F32)


def _moe(block_e, n_used, block_grp, block_nxt, xb, wg, wu, wd):
    d = xb.shape[1]
    ff = wg.shape[2]
    bm = MOE_BLOCK
    cap = block_e.shape[0] * bm
    hbm = pl.BlockSpec(memory_space=pl.ANY)
    return pl.pallas_call(
        _moe_kernel,
        out_shape=jax.ShapeDtypeStruct((cap, d), F32),
        grid_spec=pltpu.PrefetchScalarGridSpec(
            num_scalar_prefetch=4,
            grid=(cap // bm,),
            in_specs=[pl.BlockSpec((bm, d), lambda i, *_: (i, 0)), hbm, hbm, hbm],
            out_specs=pl.BlockSpec((bm, d), lambda i, *_: (i, 0)),
            scratch_shapes=[pltpu.VMEM((2, d, ff), F32), pltpu.VMEM((2, d, ff), F32), pltpu.VMEM((2, ff, d), F32),
                            pltpu.VMEM((d, ff), BF16), pltpu.VMEM((d, ff), BF16), pltpu.VMEM((ff, d), BF16),
                            pltpu.SemaphoreType.DMA((2,))],
        ),
        compiler_params=pltpu.CompilerParams(
            dimension_semantics=("arbitrary",), vmem_limit_bytes=VMEM_LIMIT),
        name="moe_experts",
    )(block_e, n_used, block_grp, block_nxt, xb, wg, wu, wd)


def _combine_kernel(d1_ref, d2_ref, n1_ref, n2_ref, h_ref, ic_ref, yb_hbm, o_ref, ga, gb, sem):
    tm = h_ref.shape[0]
    n = pl.program_id(0) * pl.num_programs(1) + pl.program_id(1)
    total = pl.num_programs(0) * pl.num_programs(1)
    slot = lax.rem(n, 2)

    def issue(da_ref, db_ref, sl):
        def start(r, carry):
            for prio, (d_ref, buf) in enumerate(((da_ref, ga), (db_ref, gb))):
                pltpu.make_async_copy(yb_hbm.at[pl.ds(d_ref[0, 0, r], 1)], buf.at[sl, pl.ds(r, 1)],
                                      sem.at[sl]).start(priority=prio)
            return carry
        lax.fori_loop(0, tm, start, 0, unroll=8)

    @pl.when(n == 0)
    def _first():
        issue(d1_ref, d2_ref, 0)

    @pl.when(n + 1 < total)
    def _next():
        issue(n1_ref, n2_ref, 1 - slot)

    for buf in (ga, gb):
        pltpu.make_async_copy(yb_hbm.at[pl.ds(0, tm)], buf.at[slot], sem.at[slot]).wait()
    ic = ic_ref[...]
    o_ref[0] = h_ref[...] + ic[:, 0:1] * ga[slot] + ic[:, 1:2] * gb[slot]


def _combine(dest1, dest2, h1, ic, yb, b, lp, tm):
    t, d = h1.shape
    per = lp // tm
    first = (FRONT_PAD + N_META) // tm
    steps = per - first
    nt = t // tm
    tile = lambda i, j: i * per + j + first

    def next_tile(i, j):
        nxt = jnp.minimum(i * steps + j + 1, b * steps - 1)
        return tile(nxt // steps, nxt % steps)

    cur = lambda: pl.BlockSpec((1, 1, tm), lambda i, j: (tile(i, j), 0, 0), memory_space=pltpu.SMEM)
    nxt = lambda: pl.BlockSpec((1, 1, tm), lambda i, j: (next_tile(i, j), 0, 0), memory_space=pltpu.SMEM)
    d1 = dest1.reshape(nt, 1, tm)
    d2 = dest2.reshape(nt, 1, tm)
    return pl.pallas_call(
        _combine_kernel,
        out_shape=jax.ShapeDtypeStruct((b, lp - FRONT_PAD - N_META, d), F32),
        grid=(b, steps),
        in_specs=[
            cur(), cur(), nxt(), nxt(),
            pl.BlockSpec((tm, d), lambda i, j: (tile(i, j), 0)),
            pl.BlockSpec((tm, LANES), lambda i, j: (tile(i, j), 0)),
            pl.BlockSpec(memory_space=pl.ANY),
        ],
        out_specs=pl.BlockSpec((1, tm, d), lambda i, j: (i, j, 0)),
        scratch_shapes=[pltpu.VMEM((2, tm, d), F32), pltpu.VMEM((2, tm, d), F32),
                        pltpu.SemaphoreType.DMA((2,))],
        compiler_params=pltpu.CompilerParams(
            dimension_semantics=("arbitrary", "arbitrary"), vmem_limit_bytes=VMEM_LIMIT),
        name="moe_combine",
    )(d1, d2, d1, d2, h1, ic, yb)


def _routing_tables(il, cnt, n_blocks, tm):
    bm = MOE_BLOCK
    counts = cnt[:, 0].astype(jnp.int32)
    padded = (counts + bm - 1) // bm * bm
    pad_end = jnp.cumsum(padded)
    pad_start = pad_end - padded
    valid = il[4] > 0
    spare = n_blocks * bm + jnp.arange(il.shape[1], dtype=jnp.int32) % tm
    dest1 = jnp.where(valid, pad_start[il[0]] + il[2], spare).astype(jnp.int32)
    dest2 = jnp.where(valid, pad_start[il[1]] + il[3], spare + tm).astype(jnp.int32)
    starts = jnp.arange(n_blocks, dtype=jnp.int32) * bm
    block_e = jnp.minimum(jnp.sum((pad_end[None, :] <= starts[:, None]).astype(jnp.int32), axis=1),
                          N_EXPERTS - 1)
    n_used = (pad_end[-1:] // bm).astype(jnp.int32)
    owns = counts > 0
    eid = jnp.arange(N_EXPERTS, dtype=jnp.int32)
    ordinal = jnp.cumsum(owns.astype(jnp.int32)) - 1
    later = (eid[None, :] > eid[:, None]) & owns[None, :]
    nxt = jnp.min(jnp.where(later, eid[None, :], N_EXPERTS), axis=1)
    nxt = jnp.where(nxt < N_EXPERTS, nxt, -1)
    return dest1, dest2, block_e, n_used, ordinal[block_e].astype(jnp.int32), nxt[block_e].astype(jnp.int32)


def _layer(h0, lp, l, norm1_w, w_in, rwkv_mu, rwkv_w0, rwkv_w2, rwkv_a0, rwkv_a2, rwkv_g2,
           rwkv_k_k, rwkv_k_a, rwkv_r_k, rwkv_ln_w, rwkv_ln_b, q_norm_w, k_norm_w,
           lambda_q1, lambda_k1, lambda_q2, lambda_k2, diff_subln_w, w_branch_rwkv, w_branch_diff,
           w_out, norm2_w, router_group_w, router_group_b, router_expert_w, router_expert_b,
           expert_w_gate, expert_w_up, expert_w_down, proj_tm, tok_tm):
    t, d = h0.shape
    b = t // lp
    lambda_init = 0.8 - 0.6 * math.exp(-0.3 * l)
    rw_cols = 3 * RWKV_WIDTH
    diff_cols = 2 * DIFF_QK_WIDTH + DIFF_V_WIDTH
    w_perm = jnp.concatenate([
        w_in[:, rw_cols + LORA_COLS + diff_cols:],
        w_in[:, rw_cols + LORA_COLS:rw_cols + LORA_COLS + diff_cols],
        w_in[:, :rw_cols + LORA_COLS],
    ], axis=1).astype(BF16)
    proj = _proj(h0, norm1_w, w_perm, proj_tm, 768)
    proj3 = proj.reshape(b, lp, IN_COLS)

    rw = _rwkv(proj3, rwkv_mu[:rw_cols], rwkv_mu[rw_cols:], rwkv_w0, rwkv_w2, rwkv_a0, rwkv_a2, rwkv_g2,
               rwkv_k_k, rwkv_k_a, rwkv_r_k.reshape(-1), rwkv_ln_w, rwkv_ln_b)
    lam4 = jnp.stack([lambda_q1, lambda_k1, lambda_q2, lambda_k2])
    da = _attention(proj3, q_norm_w, k_norm_w, lam4, diff_subln_w, lambda_init)

    wr = jnp.zeros((LANES, d), F32).at[0:N_GROUPS].set(router_group_w.T).at[8:8 + N_EXPERTS].set(router_expert_w.T)
    br = jnp.zeros((LANES,), F32).at[0:N_GROUPS].set(router_group_b).at[N_GROUPS:8].set(NEG_BIG)
    br = br.at[8:8 + N_EXPERTS].set(router_expert_b).reshape(LANES, 1)
    h1, u2, il, ic, cnt = _merge(
        rw.reshape(t, RWKV_WIDTH), da.reshape(t, DIFF_V_WIDTH), proj, h0,
        w_branch_rwkv.astype(BF16), w_branch_diff.astype(BF16), w_out.astype(BF16), norm2_w, wr, br, lp, tok_tm)

    n_real = b * (lp - FRONT_PAD)
    n_blocks = -(-(2 * n_real) // MOE_BLOCK) + N_EXPERTS
    dest1, dest2, block_e, n_used, block_grp, block_nxt = _routing_tables(il, cnt, n_blocks, tok_tm)
    xb = _dispatch(dest1, dest2, u2, n_blocks * MOE_BLOCK + 2 * tok_tm, tok_tm)
    yb = _moe(block_e, n_used, block_grp, block_nxt, xb, expert_w_gate, expert_w_up, expert_w_down)
    return h1, ic, dest1, dest2, yb


def kernel(x, meta_tokens, norm1_w, w_in, rwkv_mu, rwkv_w0, rwkv_w2, rwkv_a0, rwkv_a2, rwkv_g2, rwkv_k_k, rwkv_k_a, rwkv_r_k, rwkv_ln_w, rwkv_ln_b, q_norm_w, k_norm_w, lambda_q1, lambda_k1, lambda_q2, lambda_k2, diff_subln_w, w_branch_rwkv, w_branch_diff, w_out, norm2_w, router_group_w, router_group_b, router_expert_w, router_expert_b, expert_w_gate, expert_w_up, expert_w_down):
    b, seq, d = x.shape
    depth = norm1_w.shape[0]
    assert depth == 1, "the combine step emits the final output; deeper stacks need an intermediate form"
    lp = FRONT_PAD + N_META + seq
    meta = jnp.broadcast_to(meta_tokens[None].astype(x.dtype), (b, N_META, d))
    h0 = jnp.concatenate([jnp.zeros((b, FRONT_PAD, d), x.dtype), meta, x], axis=1).reshape(b * lp, d)
    proj_tm = 2048 if (b * lp) % 2048 == 0 else 128
    tok_tm = 512 if (b * lp) % 512 == 0 else 128
    l = 0
    h1, ic, dest1, dest2, yb = _layer(
        h0, lp, l, norm1_w[l], w_in[l], rwkv_mu[l], rwkv_w0[l], rwkv_w2[l], rwkv_a0[l], rwkv_a2[l],
        rwkv_g2[l], rwkv_k_k[l], rwkv_k_a[l], rwkv_r_k[l], rwkv_ln_w[l], rwkv_ln_b[l], q_norm_w[l],
        k_norm_w[l], lambda_q1[l], lambda_k1[l], lambda_q2[l], lambda_k2[l], diff_subln_w[l],
        w_branch_rwkv[l], w_branch_diff[l], w_out[l], norm2_w[l], router_group_w[l], router_group_b[l],
        router_expert_w[l], router_expert_b[l], expert_w_gate[l], expert_w_up[l], expert_w_down[l],
        proj_tm, tok_tm)
    return _combine(dest1, dest2, h1, ic, yb, b, lp, OUT_TILE)
```

```python
import functools
import math

import jax
import jax.numpy as jnp
from jax import lax
from jax.experimental import pallas as pl
from jax.experimental.pallas import tpu as pltpu

F32 = jnp.float32
BF16 = jnp.bfloat16

D_MODEL = 1024
N_META = 16
NORM_EPS = 1e-6
RWKV_HEADS = 16
RWKV_HEAD_DIM = 64
RWKV_WIDTH = RWKV_HEADS * RWKV_HEAD_DIM
DECAY_LORA = 64
AAA_LORA = 64
GATE_LORA = 128
LORA_COLS = DECAY_LORA + AAA_LORA + GATE_LORA
RWKV_GN_EPS = 64e-5
DIFF_HEADS = 8
DIFF_HEAD_DIM = 64
DIFF_V_DIM = 2 * DIFF_HEAD_DIM
DIFF_QK_WIDTH = DIFF_HEADS * 2 * DIFF_HEAD_DIM
DIFF_V_WIDTH = DIFF_HEADS * DIFF_V_DIM
ROPE_THETA = 500000.0
ROPE_DIM = DIFF_HEAD_DIM // 4
N_GROUPS = 4
EXPERTS_PER_GROUP = 8
N_EXPERTS = N_GROUPS * EXPERTS_PER_GROUP
EXPERT_FF = 512

LANES = 128
ATTN_BLOCK = 256
ATTN_KEY_BLOCK = 512
ATTN_HEADS_PER_STEP = 4
FRONT_PAD = ATTN_BLOCK - N_META
RWKV_CHUNK = 64
RWKV_SUBCHUNKS = 3
RWKV_HEAD_GROUP = 2
OUT_TILE = 128
MOE_BLOCK = 256
NEG_BIG = -1e30
VMEM_LIMIT = 48 * 1024 * 1024

COL_GATE = 0
COL_DIFF = 2 * D_MODEL
COL_RWKV = COL_DIFF + 2 * DIFF_QK_WIDTH + DIFF_V_WIDTH
COL_LORA = COL_RWKV + 3 * RWKV_WIDTH
IN_COLS = COL_LORA + LORA_COLS


def _dot(a, b):
    return jnp.dot(a, b, preferred_element_type=F32)


def _dot_nt(a, b):
    return lax.dot_general(a, b, (((1,), (1,)), ((), ())), preferred_element_type=F32)


def _split2(x):
    hi = x.astype(BF16)
    lo = (x - hi.astype(F32)).astype(BF16)
    return hi, lo


def _mm3(a, b_ref):
    ah, al = _split2(a)
    return _dot(ah, b_ref[0]) + _dot(ah, b_ref[1]) + _dot(al, b_ref[0])


def _hi_lo(w):
    hi = w.astype(BF16)
    return jnp.stack([hi, (w - hi.astype(F32)).astype(BF16)])


def _sigmoid(x):
    return 1.0 / (1.0 + jnp.exp(-x))


def _head_ones(width=LANES, head=RWKV_HEAD_DIM):
    r = lax.broadcasted_iota(jnp.int32, (width, width), 0) // head
    c = lax.broadcasted_iota(jnp.int32, (width, width), 1) // head
    return jnp.where(r == c, 1.0, 0.0).astype(BF16)


def _seg_sum(x, ones_bd):
    hi, lo = _split2(x)
    return _dot(hi, ones_bd) + _dot(lo, ones_bd)


def _seg_sum_wide(x, ones_bd):
    rows, width = x.shape
    n = width // LANES
    xs = jnp.concatenate([x[:, i * LANES:(i + 1) * LANES] for i in range(n)], axis=0)
    ys = _seg_sum(xs, ones_bd)
    return jnp.concatenate([ys[i * rows:(i + 1) * rows] for i in range(n)], axis=1)


def _proj_kernel(h_ref, nw_ref, w_ref, o_ref, u_ref):
    @pl.when(pl.program_id(1) == 0)
    def _():
        x = h_ref[...]
        ms = jnp.mean(x * x, axis=-1, keepdims=True)
        u_ref[...] = (x * lax.rsqrt(ms + NORM_EPS) * nw_ref[...]).astype(BF16)

    o_ref[...] = _dot(u_ref[...], w_ref[...]).astype(o_ref.dtype)


def _proj(h, norm_w, w_bf16, tm, tn):
    t, d = h.shape
    n = w_bf16.shape[1]
    return pl.pallas_call(
        _proj_kernel,
        out_shape=jax.ShapeDtypeStruct((t, n), BF16),
        grid=(t // tm, n // tn),
        in_specs=[
            pl.BlockSpec((tm, d), lambda i, j: (i, 0)),
            pl.BlockSpec((1, d), lambda i, j: (0, 0)),
            pl.BlockSpec((d, tn), lambda i, j: (0, j)),
        ],
        out_specs=pl.BlockSpec((tm, tn), lambda i, j: (i, j)),
        scratch_shapes=[pltpu.VMEM((tm, d), BF16)],
        compiler_params=pltpu.CompilerParams(
            dimension_semantics=("parallel", "arbitrary"), vmem_limit_bytes=VMEM_LIMIT),
        name="norm_proj",
    )(h, norm_w.reshape(1, d), w_bf16)


def _rwkv_kernel(*refs):
    o_ref = refs[16]
    xr, xk, xv, xl, s_ref = refs[17:22]
    c = pl.program_id(1)
    n_skip = FRONT_PAD // o_ref.shape[1]

    @pl.when(c == 0)
    def _init():
        for xs in (xr, xk, xv, xl):
            xs[0:8, :] = jnp.zeros((8, xs.shape[1]), F32)
        s_ref[...] = jnp.zeros(s_ref.shape, F32)

    @pl.when(c < n_skip)
    def _pad():
        o_ref[...] = jnp.zeros(o_ref.shape, o_ref.dtype)

    @pl.when(c >= n_skip)
    def _chunk():
        _rwkv_chunk(*refs)


def _rwkv_chunk(r_ref, k_ref, v_ref, lo_ref, mu_ref, mul_ref, w0_ref, w2_ref, a0_ref, a2_ref, g2_ref,
                kkw_ref, kaw_ref, rkw_ref, lnw_ref, lnb_ref, o_ref,
                xr, xk, xv, xl, s_ref, kt_s, bt_s, kn_s, rt_s, v_s, y_s, gc_s, bon_s, g_s):
    R = r_ref.shape[1]
    C = RWKV_CHUNK
    n_sub = R // C
    W = RWKV_WIDTH

    def shift_mix(in_ref, xs, mu):
        x = in_ref[0].astype(F32)
        xs[8:8 + R, :] = x
        prev = xs[7:7 + R, :]
        xs[7:8, :] = x[R - 1:R, :]
        return x + (prev - x) * mu

    r = shift_mix(r_ref, xr, mu_ref[:, 0:W])
    k = shift_mix(k_ref, xk, mu_ref[:, W:2 * W])
    v = shift_mix(v_ref, xv, mu_ref[:, 2 * W:3 * W])
    lo = shift_mix(lo_ref, xl, mul_ref[...])
    xw = jnp.tanh(lo[:, 0:DECAY_LORA])
    xa = lo[:, DECAY_LORA:DECAY_LORA + AAA_LORA]
    xg = _sigmoid(lo[:, DECAY_LORA + AAA_LORA:LORA_COLS])

    lw = (-math.exp(-0.5)) * _sigmoid(w0_ref[...] + _mm3(xw, w2_ref))
    a = _sigmoid(a0_ref[...] + _mm3(xa, a2_ref))
    g_s[...] = _mm3(xg, g2_ref)

    ones_bd = _head_ones()
    kk = k * kkw_ref[...]
    kkn = kk * lax.rsqrt(jnp.maximum(_seg_sum_wide(kk * kk, ones_bd), 1e-24))
    k2 = k * (1.0 + (a - 1.0) * kaw_ref[...])
    bon_s[...] = _seg_sum_wide(r * k2 * rkw_ref[...], ones_bd) * v

    ti = lax.broadcasted_iota(jnp.int32, (R, R), 0)
    tj = lax.broadcasted_iota(jnp.int32, (R, R), 1)
    ltri = jnp.where((ti >= tj) & (ti // C == tj // C), 1.0, 0.0).astype(BF16)
    l1 = lw.astype(BF16)
    rem = lw - l1.astype(F32)
    l2 = rem.astype(BF16)
    l3 = (rem - l2.astype(F32)).astype(BF16)
    cum = _dot(ltri, l1) + _dot(ltri, l2) + _dot(ltri, l3)
    e_pos = jnp.exp(cum)
    e_neg = jnp.exp(-cum)
    kt_s[...] = kkn * jnp.exp(cum - lw)
    bt_s[...] = kkn * a * e_neg
    kn_s[...] = k2 * e_neg
    rt_s[...] = r * e_pos
    v_s[...] = v
    for s in range(n_sub):
        gc_s[s:s + 1, :] = jnp.exp(cum[(s + 1) * C - 1:(s + 1) * C, :])

    G = RWKV_HEAD_GROUP
    GL = G * RWKV_HEAD_DIM
    GC = G * C
    n_groups = W // GL
    lane_head = lax.broadcasted_iota(jnp.int32, (1, GL), 1) // RWKV_HEAD_DIM
    head_mask = [jnp.where(lane_head == h, 1.0, 0.0).astype(BF16) for h in range(G)]
    trow = lax.broadcasted_iota(jnp.int32, (C, GC), 0)
    tcol = lax.broadcasted_iota(jnp.int32, (C, GC), 1) % C
    strict = trow > tcol
    incl = trow >= tcol
    eye = jnp.where(trow == tcol, 1.0, 0.0)
    br = lax.broadcasted_iota(jnp.int32, (GL, GL), 0) // RWKV_HEAD_DIM
    bc = lax.broadcasted_iota(jnp.int32, (GL, GL), 1) // RWKV_HEAD_DIM
    bdmask = jnp.where(br == bc, 1.0, 0.0)

    def stack(y):
        yb = y.astype(BF16)
        return jnp.concatenate([yb * m for m in head_mask], axis=0)

    groups = range(n_groups)
    units = [(s, p) for s in range(n_sub) for p in groups]
    win = {(s, p): (slice(s * C, (s + 1) * C), slice(p * GL, (p + 1) * GL)) for s, p in units}
    def prepare(units):
        kt = {q: kt_s[win[q]] for q in units}
        bt = {q: bt_s[win[q]] for q in units}
        kn = {q: kn_s[win[q]] for q in units}
        rt = {q: rt_s[win[q]] for q in units}
        vv = {q: v_s[win[q]] for q in units}
        gc = {q: gc_s[q[0]:q[0] + 1, win[q][1]] for q in units}
        kr = {q: jnp.concatenate([kt[q], rt[q]], axis=0).astype(BF16) for q in units}
        m_all = {q: _dot_nt(kr[q], jnp.concatenate([stack(bt[q]), stack(kn[q])], axis=0))
                 for q in units}
        p_mat = {q: jnp.where(incl, m_all[q][C:2 * C, 0:GC], 0.0).astype(BF16) for q in units}
        bq_mat = {q: jnp.concatenate([jnp.where(strict, m_all[q][0:C, GC:2 * GC], 0.0),
                                      jnp.where(incl, m_all[q][C:2 * C, GC:2 * GC], 0.0)], axis=0).astype(BF16)
                  for q in units}
        v_term = {q: _dot(bq_mat[q], stack(vv[q])) for q in units}

        pw = {q: -jnp.where(strict, m_all[q][0:C, 0:GC], 0.0) for q in units}
        t_inv = {q: eye + pw[q] for q in units}
        pw = {q: _dot(pw[q].astype(BF16), stack(pw[q])) for q in units}
        for _ in range(int(math.log2(C)) - 2):
            both = {q: _dot(jnp.concatenate([t_inv[q], pw[q]], axis=0).astype(BF16), stack(pw[q]))
                    for q in units}
            t_inv = {q: t_inv[q] + both[q][0:C] for q in units}
            pw = {q: both[q][C:2 * C] for q in units}
        t_inv = {q: (t_inv[q] + _dot(t_inv[q].astype(BF16), stack(pw[q]))).astype(BF16) for q in units}
        xc = {q: jnp.concatenate([bt[q] * gc[q], kn[q] * gc[q]], axis=0).astype(BF16) for q in units}
        return kr, p_mat, v_term, t_inv, xc, vv, gc

    kr, p_mat, v_term, t_inv, xc, vv, gc = prepare(units)

    state = [s_ref[p] for p in groups]
    for s in range(n_sub):
        qs = [(s, p) for p in groups]
        ks = [_dot_nt(kr[q], state[q[1]].astype(BF16)) for q in qs]
        u = [-_dot(t_inv[q], stack(ks[q[1]][0:C] + v_term[q][0:C])) for q in qs]
        for q in qs:
            y_s[win[q]] = ks[q[1]][C:2 * C] + v_term[q][C:2 * C] + _dot(p_mat[q], stack(u[q[1]]))
        for q in qs:
            uc = jnp.concatenate([u[q[1]], vv[q]], axis=0)
            state[q[1]] = state[q[1]] * gc[q] + bdmask * _dot(jnp.transpose(uc).astype(BF16), xc[q])
    for p in groups:
        s_ref[p] = state[p]

    y = y_s[...]
    inv_n = 1.0 / RWKV_HEAD_DIM
    mean = _seg_sum_wide(y, ones_bd) * inv_n
    dlt = y - mean
    var = _seg_sum_wide(dlt * dlt, ones_bd) * inv_n
    yn = dlt * lax.rsqrt(var + RWKV_GN_EPS) * lnw_ref[...] + lnb_ref[...]
    o_ref[0] = ((yn + bon_s[...]) * g_s[...]).astype(o_ref.dtype)


def _rwkv(proj3, mu_rkv, mu_lo, w0, w2, a0, a2, g2, k_k, k_a, r_k, ln_w, ln_b):
    b, lp, _ = proj3.shape
    C = RWKV_CHUNK * RWKV_SUBCHUNKS
    W = RWKV_WIDTH
    cb = COL_RWKV // W
    lb = COL_LORA // LORA_COLS
    row = lambda x: x.reshape(1, -1)
    full = lambda shape: pl.BlockSpec(shape, lambda i, c: (0,) * len(shape))
    wide = pltpu.VMEM((C, W), F32)
    gl = RWKV_HEAD_GROUP * RWKV_HEAD_DIM
    return pl.pallas_call(
        _rwkv_kernel,
        out_shape=jax.ShapeDtypeStruct((b, lp, W), BF16),
        grid=(b, lp // C),
        in_specs=[
            pl.BlockSpec((1, C, W), lambda i, c: (i, c, cb)),
            pl.BlockSpec((1, C, W), lambda i, c: (i, c, cb + 1)),
            pl.BlockSpec((1, C, W), lambda i, c: (i, c, cb + 2)),
            pl.BlockSpec((1, C, LORA_COLS), lambda i, c: (i, c, lb)),
            full((1, 3 * W)), full((1, LORA_COLS)),
            full((1, W)), full((2, DECAY_LORA, W)), full((1, W)), full((2, AAA_LORA, W)), full((2, GATE_LORA, W)),
            full((1, W)), full((1, W)), full((1, W)), full((1, W)), full((1, W)),
        ],
        out_specs=pl.BlockSpec((1, C, W), lambda i, c: (i, c, 0)),
        scratch_shapes=[
            pltpu.VMEM((C + 8, W), F32), pltpu.VMEM((C + 8, W), F32), pltpu.VMEM((C + 8, W), F32),
            pltpu.VMEM((C + 8, LORA_COLS), F32),
            pltpu.VMEM((W // gl, gl, gl), F32),
            wide, wide, wide, wide, wide, wide, pltpu.VMEM((8, W), F32), wide, wide,
        ],
        compiler_params=pltpu.CompilerParams(
            dimension_semantics=("parallel", "arbitrary"), vmem_limit_bytes=VMEM_LIMIT),
        name="rwkv7_time_mix",
    )(proj3, proj3, proj3, proj3, row(mu_rkv), row(mu_lo), row(w0), _hi_lo(w2), row(a0), _hi_lo(a2), _hi_lo(g2),
      row(k_k), row(k_a), row(r_k), row(ln_w), row(ln_b))


def _attn_kernel(q_ref, k_ref, v_ref, cos_ref, s1_ref, s2_ref, qw_ref, kw_ref, lam_ref, sw_ref, o_ref,
                 kp_s, vp_s, m_s, acc_s, *, tq, tk, lambda_init):
    qi = pl.program_id(2)
    lp = k_ref.shape[1]
    n_real = lp - FRONT_PAD
    lk = kp_s.shape[0]
    nh = k_ref.shape[2] // LANES
    vw = 2 * LANES
    heads = range(nh)
    hs = [slice(h * LANES, (h + 1) * LANES) for h in heads]
    ones_bd = _head_ones(LANES, DIFF_HEAD_DIM)
    shift = ROPE_DIM // 2

    def norm_rope(x, w, rows):
        ms = _seg_sum(x * x, ones_bd) * (1.0 / DIFF_HEAD_DIM)
        xn = x * lax.rsqrt(ms + NORM_EPS) * w
        return (xn * cos_ref[rows, :] + pltpu.roll(xn, shift, 1) * s1_ref[rows, :]
                + pltpu.roll(xn, LANES - shift, 1) * s2_ref[rows, :])

    @pl.when(qi == 0)
    def _prep():
        def put(dst, n):
            src = pl.ds(FRONT_PAD + dst, n)
            rows = pl.ds(dst, n)
            for h in heads:
                kp_s[rows, hs[h]] = norm_rope(k_ref[0, src, hs[h]].astype(F32), kw_ref[...], src).astype(BF16)
                vp_s[rows, h * vw:h * vw + LANES] = v_ref[0, src, hs[h]].astype(BF16)
                vp_s[rows, h * vw + LANES:(h + 1) * vw] = jnp.ones((n, LANES), BF16)

        def body(i, carry):
            put(pl.multiple_of(i * LANES, LANES), LANES)
            return carry
        lax.fori_loop(0, n_real // LANES, body, 0, unroll=2)
        tail = n_real % LANES
        if tail:
            put(n_real - tail, tail)
        if lk > n_real:
            kp_s[n_real:lk, :] = jnp.zeros((lk - n_real, nh * LANES), BF16)
            vp_s[n_real:lk, :] = jnp.zeros((lk - n_real, nh * vw), BF16)

    lane = lax.broadcasted_iota(jnp.int32, (1, LANES), 1)
    m0 = jnp.where(lane < DIFF_HEAD_DIM, 1.0, 0.0)
    m1 = 1.0 - m0
    rows_q = pl.ds(pl.multiple_of(qi * tq, tq), tq)
    qs = []
    for h in heads:
        qn = norm_rope(q_ref[0, :, hs[h]].astype(F32), qw_ref[...], rows_q) * (DIFF_HEAD_DIM ** -0.5)
        qs.append(jnp.concatenate([qn * m0, qn * m1], axis=0).astype(BF16))
    first_row = qi * tq - FRONT_PAD
    row = first_row + lax.broadcasted_iota(jnp.int32, (2 * tq, LANES), 0) % tq
    col0 = lax.broadcasted_iota(jnp.int32, (2 * tq, LANES), 1)

    def step(j, width, causal, init=False):
        start = j * tk
        if not isinstance(j, int):
            start = pl.multiple_of(start, tk)
        ks = pl.ds(start, width)
        n_sub = width // LANES
        s = [_dot_nt(qs[h], kp_s[ks, hs[h]]) for h in heads]
        sub = [[s[h][:, c * LANES:(c + 1) * LANES] for c in range(n_sub)] for h in heads]
        if causal:
            sub = [[jnp.where(start + c * LANES + col0 <= row, sub[h][c], NEG_BIG) for c in range(n_sub)]
                   for h in heads]
        m_new = []
        for h in heads:
            mx = functools.reduce(jnp.maximum, sub[h])
            mx = jnp.broadcast_to(jnp.max(mx, axis=1, keepdims=True), mx.shape)
            m_new.append(mx if init else jnp.maximum(m_s[h], mx))
        p = [jnp.concatenate([jnp.exp(sub[h][c] - m_new[h]) for c in range(n_sub)], axis=1).astype(BF16)
             for h in heads]
        pv = [_dot(p[h], vp_s[ks, h * vw:(h + 1) * vw]) for h in heads]
        for h in heads:
            if init:
                acc_s[h] = pv[h]
            else:
                alpha = jnp.exp(m_s[h] - m_new[h])
                acc_s[h] = jnp.concatenate([alpha, alpha], axis=1) * acc_s[h] + pv[h]
            m_s[h] = m_new[h]

    assert FRONT_PAD + N_META == tq and tk % tq == 0 and N_META <= LANES
    n_full = jnp.maximum(first_row + 1, 0) // tk
    last = jnp.maximum(first_row + tq - 1, 0) // tk
    step(n_full, tk, True, init=True)

    def mid(j, carry):
        step(j, tk, False)
        return carry
    lax.fori_loop(0, n_full, mid, 0)

    @pl.when(last > n_full)
    def _diag():
        step(last, LANES, True)

    lam = (jnp.exp(jnp.sum(lam_ref[0:1, :] * lam_ref[1:2, :], axis=1, keepdims=True))
           - jnp.exp(jnp.sum(lam_ref[2:3, :] * lam_ref[3:4, :], axis=1, keepdims=True)) + lambda_init)
    for h in heads:
        acc = acc_s[h]
        o = acc[:, 0:LANES] / acc[:, LANES:vw]
        od = o[0:tq] - lam * o[tq:2 * tq]
        ms = jnp.mean(od * od, axis=1, keepdims=True)
        o_ref[0, :, hs[h]] = (od * lax.rsqrt(ms + NORM_EPS) * sw_ref[...]
                              * (1.0 - lambda_init)).astype(o_ref.dtype)


def _rope_tables(lp):
    half = ROPE_DIM // 2
    inv_freq = jnp.exp(-math.log(ROPE_THETA) * jnp.arange(half, dtype=F32) * 2.0 / ROPE_DIM)
    pos = (jnp.arange(lp) - FRONT_PAD).astype(F32)
    ang = pos[:, None] * inv_freq[None, :]
    cos, sin = jnp.cos(ang), jnp.sin(ang)
    one = jnp.ones((lp, DIFF_HEAD_DIM - ROPE_DIM), F32)
    zero = jnp.zeros((lp, DIFF_HEAD_DIM - ROPE_DIM), F32)
    zh = jnp.zeros((lp, half), F32)
    c = jnp.concatenate([cos, cos, one], axis=1)
    s1 = jnp.concatenate([zh, sin, zero], axis=1)
    s2 = jnp.concatenate([-sin, zh, zero], axis=1)
    dup = lambda t: jnp.concatenate([t, t], axis=1)
    return dup(c), dup(s1), dup(s2)


def _attention(proj3, q_norm_w, k_norm_w, lam4, subln_w, lambda_init):
    b, lp, _ = proj3.shape
    tq, tk, nh = ATTN_BLOCK, ATTN_KEY_BLOCK, ATTN_HEADS_PER_STEP
    lk = -(-(lp - FRONT_PAD) // tk) * tk
    hw = nh * LANES
    qb = COL_DIFF // hw
    kb = qb + DIFF_QK_WIDTH // hw
    vb = kb + DIFF_QK_WIDTH // hw
    cos, s1, s2 = _rope_tables(lp)
    dup = lambda w: jnp.concatenate([w, w]).reshape(1, LANES)
    full = lambda shape: pl.BlockSpec(shape, lambda i, h, q: (0,) * len(shape))
    return pl.pallas_call(
        functools.partial(_attn_kernel, tq=tq, tk=tk, lambda_init=lambda_init),
        out_shape=jax.ShapeDtypeStruct((b, lp, DIFF_V_WIDTH), BF16),
        grid=(b, DIFF_HEADS // nh, lp // tq),
        in_specs=[
            pl.BlockSpec((1, tq, hw), lambda i, h, q: (i, q, qb + h)),
            pl.BlockSpec((1, lp, hw), lambda i, h, q: (i, 0, kb + h)),
            pl.BlockSpec((1, lp, hw), lambda i, h, q: (i, 0, vb + h)),
            full((lp, LANES)), full((lp, LANES)), full((lp, LANES)),
            full((1, LANES)), full((1, LANES)), full((4, DIFF_HEAD_DIM)), full((1, LANES)),
        ],
        out_specs=pl.BlockSpec((1, tq, hw), lambda i, h, q: (i, q, h)),
        scratch_shapes=[pltpu.VMEM((lk, hw), BF16), pltpu.VMEM((lk, 2 * hw), BF16),
                        pltpu.VMEM((nh, 2 * tq, LANES), F32), pltpu.VMEM((nh, 2 * tq, 2 * LANES), F32)],
        compiler_params=pltpu.CompilerParams(
            dimension_semantics=("parallel", "parallel", "arbitrary"), vmem_limit_bytes=VMEM_LIMIT),
        name="diff_attention",
    )(proj3, proj3, proj3, cos, s1, s2, dup(q_norm_w), dup(k_norm_w), lam4, subln_w.reshape(1, LANES))


def _merge_kernel(rw_ref, da_ref, g1_ref, g2_ref, h_ref, wbr_ref, wbd_ref, wo_ref, n2_ref, wr_ref, br_ref,
                  h1_ref, u2_ref, il_ref, ic_ref, cnt_ref, base_s, *, lp):
    i = pl.program_id(0)
    tm = rw_ref.shape[0]

    @pl.when(i == 0)
    def _():
        base_s[...] = jnp.zeros(base_s.shape, F32)

    y1 = _dot(rw_ref[...], wbr_ref[...])
    y2 = _dot(da_ref[...], wbd_ref[...])
    merged = _sigmoid(g1_ref[...].astype(F32)) * y1 + _sigmoid(g2_ref[...].astype(F32)) * y2
    h1 = h_ref[...] + _dot(merged.astype(BF16), wo_ref[...])
    h1_ref[...] = h1
    u2 = h1 * lax.rsqrt(jnp.mean(h1 * h1, axis=-1, keepdims=True) + NORM_EPS) * n2_ref[...]
    u2_ref[...] = u2

    uh, ul = _split2(u2)
    wh, wl = _split2(wr_ref[...])
    lt = _dot_nt(wh, uh) + _dot_nt(wh, ul) + _dot_nt(wl, uh) + br_ref[...]

    gi8 = lax.broadcasted_iota(jnp.int32, (8, tm), 0)
    lg = lt[0:8]
    ge = jnp.exp(lg - jnp.max(lg, axis=0, keepdims=True))
    gp = ge / jnp.sum(ge, axis=0, keepdims=True)
    gv = jnp.max(gp, axis=0, keepdims=True)
    gidx = jnp.min(jnp.where(gp == gv, gi8, N_EXPERTS), axis=0, keepdims=True)

    ei = lax.broadcasted_iota(jnp.int32, (N_EXPERTS, tm), 0)
    sel = (ei // EXPERTS_PER_GROUP) == gidx
    le = jnp.where(sel, lt[8:8 + N_EXPERTS], NEG_BIG)
    ee = jnp.where(sel, jnp.exp(le - jnp.max(le, axis=0, keepdims=True)), 0.0)
    ep = jnp.where(sel, ee / jnp.sum(ee, axis=0, keepdims=True), -1.0)
    v1 = jnp.max(ep, axis=0, keepdims=True)
    i1 = jnp.min(jnp.where(ep == v1, ei, N_EXPERTS), axis=0, keepdims=True)
    ep2 = jnp.where(ei == i1, -1.0, ep)
    v2 = jnp.max(ep2, axis=0, keepdims=True)
    i2 = jnp.min(jnp.where(ep2 == v2, ei, N_EXPERTS), axis=0, keepdims=True)
    den = v1 + v2
    gate1 = gv * v1 / den
    gate2 = gv * v2 / den

    tok = (i * tm + lax.broadcasted_iota(jnp.int32, (1, tm), 1)).astype(F32)
    pos = tok - jnp.floor((tok + 0.5) / lp) * lp
    valid = pos > (FRONT_PAD - 0.5)

    oh1 = jnp.where((ei == i1) & valid, 1.0, 0.0)
    oh2 = jnp.where((ei == i2) & valid, 1.0, 0.0)
    oh = oh1 + oh2
    ur = lax.broadcasted_iota(jnp.int32, (tm, tm), 0)
    uc = lax.broadcasted_iota(jnp.int32, (tm, tm), 1)
    before = jnp.where(ur < uc, 1.0, 0.0).astype(BF16)
    tot = base_s[:, 0:1] + _dot(oh.astype(BF16), before)
    rank1 = jnp.sum(oh1 * tot, axis=0, keepdims=True)
    rank2 = jnp.sum(oh2 * tot, axis=0, keepdims=True)
    base_s[...] = base_s[...] + jnp.sum(oh, axis=1, keepdims=True)
    cnt_ref[...] = base_s[...]

    il = jnp.where(gi8 == 0, i1, jnp.where(gi8 == 1, i2, jnp.where(
        gi8 == 2, rank1.astype(jnp.int32), jnp.where(gi8 == 3, rank2.astype(jnp.int32), jnp.where(
            gi8 == 4, valid.astype(jnp.int32), 0)))))
    il_ref[...] = il
    ri = lax.broadcasted_iota(jnp.int32, (LANES, tm), 0)
    ic = jnp.where(ri == 0, gate1, jnp.where(ri == 1, gate2, 0.0))
    ic_ref[...] = jnp.transpose(ic)


def _merge(rw, da, proj, h0, wbr, wbd, wo, norm2_w, wr, br, lp, tm):
    t, d = h0.shape
    gb = COL_GATE // d
    full = lambda shape: pl.BlockSpec(shape, lambda i: (0,) * len(shape))
    tile = lambda c: pl.BlockSpec((tm, d), lambda i: (i, c))
    return pl.pallas_call(
        functools.partial(_merge_kernel, lp=lp),
        out_shape=(
            jax.ShapeDtypeStruct((t, d), F32),
            jax.ShapeDtypeStruct((t, d), F32),
            jax.ShapeDtypeStruct((8, t), jnp.int32),
            jax.ShapeDtypeStruct((t, LANES), F32),
            jax.ShapeDtypeStruct((N_EXPERTS, LANES), F32),
        ),
        grid=(t // tm,),
        in_specs=[tile(0), tile(0), tile(gb), tile(gb + 1), tile(0),
                  full((d, d)), full((d, d)), full((d, d)), full((1, d)), full((LANES, d)), full((LANES, 1))],
        out_specs=(
            tile(0), tile(0),
            pl.BlockSpec((8, tm), lambda i: (0, i)),
            pl.BlockSpec((tm, LANES), lambda i: (i, 0)),
            full((N_EXPERTS, LANES)),
        ),
        scratch_shapes=[pltpu.VMEM((N_EXPERTS, LANES), F32)],
        compiler_params=pltpu.CompilerParams(
            dimension_semantics=("arbitrary",), vmem_limit_bytes=VMEM_LIMIT),
        name="merge_router",
    )(rw, da, proj, proj, h0, wbr, wbd, wo, norm2_w.reshape(1, d), wr, br)


def _dispatch_kernel(d1_ref, d2_ref, u_ref, xin_hbm, xb_hbm, sem):
    del xin_hbm
    tm = d1_ref.shape[2]

    def start(r, carry):
        for prio, d_ref in enumerate((d1_ref, d2_ref)):
            pltpu.make_async_copy(u_ref.at[pl.ds(r, 1)], xb_hbm.at[pl.ds(d_ref[0, 0, r], 1)],
                                  sem).start(priority=prio)
        return carry

    lax.fori_loop(0, tm, start, 0, unroll=8)
    for _ in range(2):
        pltpu.make_async_copy(u_ref, xb_hbm.at[pl.ds(0, tm)], sem).wait()


def _dispatch(dest1, dest2, u2, cap, tm):
    t, d = u2.shape
    nt = t // tm
    smem = lambda: pl.BlockSpec((1, 1, tm), lambda i: (i, 0, 0), memory_space=pltpu.SMEM)
    return pl.pallas_call(
        _dispatch_kernel,
        out_shape=jax.ShapeDtypeStruct((cap, d), F32),
        grid=(nt,),
        in_specs=[smem(), smem(), pl.BlockSpec((tm, d), lambda i: (i, 0)), pl.BlockSpec(memory_space=pl.ANY)],
        out_specs=pl.BlockSpec(memory_space=pl.ANY),
        scratch_shapes=[pltpu.SemaphoreType.DMA(())],
        input_output_aliases={3: 0},
        compiler_params=pltpu.CompilerParams(
            dimension_semantics=("arbitrary",), vmem_limit_bytes=VMEM_LIMIT),
        name="moe_dispatch",
    )(dest1.reshape(nt, 1, tm), dest2.reshape(nt, 1, tm), u2, jnp.zeros((cap, d), F32))


def _moe_kernel(be_ref, nb_ref, grp_ref, nxt_ref, x_ref, wg_hbm, wu_hbm, wd_hbm, o_ref,
                wg_f, wu_f, wd_f, wg_s, wu_s, wd_s, sem):
    i = pl.program_id(0)
    used = i < nb_ref[0]
    slot = lax.rem(grp_ref[i], 2)

    def fetch(expert, sl):
        return [pltpu.make_async_copy(w_hbm.at[expert], w_f.at[sl], sem.at[sl])
                for w_hbm, w_f in ((wg_hbm, wg_f), (wu_hbm, wu_f), (wd_hbm, wd_f))]

    @pl.when(used & ((i == 0) | (grp_ref[i] != grp_ref[jnp.maximum(i - 1, 0)])))
    def _():
        @pl.when(i == 0)
        def _():
            for cp in fetch(be_ref[0], 0):
                cp.start()

        @pl.when(nxt_ref[i] >= 0)
        def _():
            for cp in fetch(nxt_ref[i], 1 - slot):
                cp.start()

        for cp in fetch(be_ref[i], slot):
            cp.wait()
        wg_s[...] = wg_f[slot].astype(BF16)
        wu_s[...] = wu_f[slot].astype(BF16)
        wd_s[...] = wd_f[slot].astype(BF16)

    @pl.when(used)
    def _():
        x = x_ref[...].astype(BF16)
        hg = _dot(x, wg_s[...])
        hu = _dot(x, wu_s[...])
        hid = hg * _sigmoid(hg) * hu
        o_ref[...] = _dot(hid.astype(BF16), wd_s[...])

    @pl.when(i >= nb_ref[0])
    def _():
        o_ref[...] = jnp.zeros(o_ref.shape, F32)


def _moe(block_e, n_used, block_grp, block_nxt, xb, wg, wu, wd):
    d = xb.shape[1]
    ff = wg.shape[2]
    bm = MOE_BLOCK
    cap = block_e.shape[0] * bm
    hbm = pl.BlockSpec(memory_space=pl.ANY)
    return pl.pallas_call(
        _moe_kernel,
        out_shape=jax.ShapeDtypeStruct((cap, d), F32),
        grid_spec=pltpu.PrefetchScalarGridSpec(
            num_scalar_prefetch=4,
            grid=(cap // bm,),
            in_specs=[pl.BlockSpec((bm, d), lambda i, *_: (i, 0)), hbm, hbm, hbm],
            out_specs=pl.BlockSpec((bm, d), lambda i, *_: (i, 0)),
            scratch_shapes=[pltpu.VMEM((2, d, ff), F32), pltpu.VMEM((2, d, ff), F32), pltpu.VMEM((2, ff, d), F32),
                            pltpu.VMEM((d, ff), BF16), pltpu.VMEM((d, ff), BF16), pltpu.VMEM((ff, d), BF16),
                            pltpu.SemaphoreType.DMA((2,))],
        ),
        compiler_params=pltpu.CompilerParams(
            dimension_semantics=("arbitrary",), vmem_limit_bytes=VMEM_LIMIT),
        name="moe_experts",
    )(block_e, n_used, block_grp, block_nxt, xb, wg, wu, wd)


def _combine_kernel(d1_ref, d2_ref, n1_ref, n2_ref, h_ref, ic_ref, yb_hbm, o_ref, ga, gb, sem):
    tm = h_ref.shape[0]
    n = pl.program_id(0) * pl.num_programs(1) + pl.program_id(1)
    total = pl.num_programs(0) * pl.num_programs(1)
    slot = lax.rem(n, 2)

    def issue(da_ref, db_ref, sl):
        def start(r, carry):
            for prio, (d_ref, buf) in enumerate(((da_ref, ga), (db_ref, gb))):
                pltpu.make_async_copy(yb_hbm.at[pl.ds(d_ref[0, 0, r], 1)], buf.at[sl, pl.ds(r, 1)],
                                      sem.at[sl]).start(priority=prio)
            return carry
        lax.fori_loop(0, tm, start, 0, unroll=8)

    @pl.when(n == 0)
    def _first():
        issue(d1_ref, d2_ref, 0)

    @pl.when(n + 1 < total)
    def _next():
        issue(n1_ref, n2_ref, 1 - slot)

    for buf in (ga, gb):
        pltpu.make_async_copy(yb_hbm.at[pl.ds(0, tm)], buf.at[slot], sem.at[slot]).wait()
    ic = ic_ref[...]
    o_ref[0] = h_ref[...] + ic[:, 0:1] * ga[slot] + ic[:, 1:2] * gb[slot]


def _combine(dest1, dest2, h1, ic, yb, b, lp, tm):
    t, d = h1.shape
    per = lp // tm
    first = (FRONT_PAD + N_META) // tm
    steps = per - first
    nt = t // tm
    tile = lambda i, j: i * per + j + first

    def next_tile(i, j):
        nxt = jnp.minimum(i * steps + j + 1, b * steps - 1)
        return tile(nxt // steps, nxt % steps)

    cur = lambda: pl.BlockSpec((1, 1, tm), lambda i, j: (tile(i, j), 0, 0), memory_space=pltpu.SMEM)
    nxt = lambda: pl.BlockSpec((1, 1, tm), lambda i, j: (next_tile(i, j), 0, 0), memory_space=pltpu.SMEM)
    d1 = dest1.reshape(nt, 1, tm)
    d2 = dest2.reshape(nt, 1, tm)
    return pl.pallas_call(
        _combine_kernel,
        out_shape=jax.ShapeDtypeStruct((b, lp - FRONT_PAD - N_META, d), F32),
        grid=(b, steps),
        in_specs=[
            cur(), cur(), nxt(), nxt(),
            pl.BlockSpec((tm, d), lambda i, j: (tile(i, j), 0)),
            pl.BlockSpec((tm, LANES), lambda i, j: (tile(i, j), 0)),
            pl.BlockSpec(memory_space=pl.ANY),
        ],
        out_specs=pl.BlockSpec((1, tm, d), lambda i, j: (i, j, 0)),
        scratch_shapes=[pltpu.VMEM((2, tm, d), F32), pltpu.VMEM((2, tm, d), F32),
                        pltpu.SemaphoreType.DMA((2,))],
        compiler_params=pltpu.CompilerParams(
            dimension_semantics=("arbitrary", "arbitrary"), vmem_limit_bytes=VMEM_LIMIT),
        name="moe_combine",
    )(d1, d2, d1, d2, h1, ic, yb)


def _routing_tables(il, cnt, n_blocks, tm):
    bm = MOE_BLOCK
    counts = cnt[:, 0].astype(jnp.int32)
    padded = (counts + bm - 1) // bm * bm
    pad_end = jnp.cumsum(padded)
    pad_start = pad_end - padded
    valid = il[4] > 0
    spare = n_blocks * bm + jnp.arange(il.shape[1], dtype=jnp.int32) % tm
    eid = jnp.arange(N_EXPERTS, dtype=jnp.int32)
    first_slot = jnp.sum(jnp.where(il[0:2, :, None] == eid, pad_start, 0), axis=-1)
    dest1 = jnp.where(valid, first_slot[0] + il[2], spare).astype(jnp.int32)
    dest2 = jnp.where(valid, first_slot[1] + il[3], spare + tm).astype(jnp.int32)
    starts = jnp.arange(n_blocks, dtype=jnp.int32) * bm
    block_e = jnp.minimum(jnp.sum((pad_end[None, :] <= starts[:, None]).astype(jnp.int32), axis=1),
                          N_EXPERTS - 1)
    n_used = (pad_end[-1:] // bm).astype(jnp.int32)
    owns = counts > 0
    eid = jnp.arange(N_EXPERTS, dtype=jnp.int32)
    ordinal = jnp.cumsum(owns.astype(jnp.int32)) - 1
    later = (eid[None, :] > eid[:, None]) & owns[None, :]
    nxt = jnp.min(jnp.where(later, eid[None, :], N_EXPERTS), axis=1)
    nxt = jnp.where(nxt < N_EXPERTS, nxt, -1)
    return dest1, dest2, block_e, n_used, ordinal[block_e].astype(jnp.int32), nxt[block_e].astype(jnp.int32)


def _layer(h0, lp, l, norm1_w, w_in, rwkv_mu, rwkv_w0, rwkv_w2, rwkv_a0, rwkv_a2, rwkv_g2,
           rwkv_k_k, rwkv_k_a, rwkv_r_k, rwkv_ln_w, rwkv_ln_b, q_norm_w, k_norm_w,
           lambda_q1, lambda_k1, lambda_q2, lambda_k2, diff_subln_w, w_branch_rwkv, w_branch_diff,
           w_out, norm2_w, router_group_w, router_group_b, router_expert_w, router_expert_b,
           expert_w_gate, expert_w_up, expert_w_down, proj_tm, tok_tm):
    t, d = h0.shape
    b = t // lp
    lambda_init = 0.8 - 0.6 * math.exp(-0.3 * l)
    rw_cols = 3 * RWKV_WIDTH
    diff_cols = 2 * DIFF_QK_WIDTH + DIFF_V_WIDTH
    w_perm = jnp.concatenate([
        w_in[:, rw_cols + LORA_COLS + diff_cols:],
        w_in[:, rw_cols + LORA_COLS:rw_cols + LORA_COLS + diff_cols],
        w_in[:, :rw_cols + LORA_COLS],
    ], axis=1).astype(BF16)
    proj = _proj(h0, norm1_w, w_perm, proj_tm, 768)
    proj3 = proj.reshape(b, lp, IN_COLS)

    rw = _rwkv(proj3, rwkv_mu[:rw_cols], rwkv_mu[rw_cols:], rwkv_w0, rwkv_w2, rwkv_a0, rwkv_a2, rwkv_g2,
               rwkv_k_k, rwkv_k_a, rwkv_r_k.reshape(-1), rwkv_ln_w, rwkv_ln_b)
    lam4 = jnp.stack([lambda_q1, lambda_k1, lambda_q2, lambda_k2])
    da = _attention(proj3, q_norm_w, k_norm_w, lam4, diff_subln_w, lambda_init)

    wr = jnp.zeros((LANES, d), F32).at[0:N_GROUPS].set(router_group_w.T).at[8:8 + N_EXPERTS].set(router_expert_w.T)
    br = jnp.zeros((LANES,), F32).at[0:N_GROUPS].set(router_group_b).at[N_GROUPS:8].set(NEG_BIG)
    br = br.at[8:8 + N_EXPERTS].set(router_expert_b).reshape(LANES, 1)
    h1, u2, il, ic, cnt = _merge(
        rw.reshape(t, RWKV_WIDTH), da.reshape(t, DIFF_V_WIDTH), proj, h0,
        w_branch_rwkv.astype(BF16), w_branch_diff.astype(BF16), w_out.astype(BF16), norm2_w, wr, br, lp, tok_tm)

    n_real = b * (lp - FRONT_PAD)
    n_blocks = -(-(2 * n_real) // MOE_BLOCK) + N_EXPERTS
    dest1, dest2, block_e, n_used, block_grp, block_nxt = _routing_tables(il, cnt, n_blocks, tok_tm)
    xb = _dispatch(dest1, dest2, u2, n_blocks * MOE_BLOCK + 2 * tok_tm, tok_tm)
    yb = _moe(block_e, n_used, block_grp, block_nxt, xb, expert_w_gate, expert_w_up, expert_w_down)
    return h1, ic, dest1, dest2, yb


def kernel(x, meta_tokens, norm1_w, w_in, rwkv_mu, rwkv_w0, rwkv_w2, rwkv_a0, rwkv_a2, rwkv_g2, rwkv_k_k, rwkv_k_a, rwkv_r_k, rwkv_ln_w, rwkv_ln_b, q_norm_w, k_norm_w, lambda_q1, lambda_k1, lambda_q2, lambda_k2, diff_subln_w, w_branch_rwkv, w_branch_diff, w_out, norm2_w, router_group_w, router_group_b, router_expert_w, router_expert_b, expert_w_gate, expert_w_up, expert_w_down):
    b, seq, d = x.shape
    depth = norm1_w.shape[0]
    assert depth == 1, "the combine step emits the final output; deeper stacks need an intermediate form"
    lp = FRONT_PAD + N_META + seq
    meta = jnp.broadcast_to(meta_tokens[None].astype(x.dtype), (b, N_META, d))
    h0 = jnp.concatenate([jnp.zeros((b, FRONT_PAD, d), x.dtype), meta, x], axis=1).reshape(b * lp, d)
    proj_tm = 2048 if (b * lp) % 2048 == 0 else 128
    tok_tm = 512 if (b * lp) % 512 == 0 else 128
    l = 0
    h1, ic, dest1, dest2, yb = _layer(
        h0, lp, l, norm1_w[l], w_in[l], rwkv_mu[l], rwkv_w0[l], rwkv_w2[l], rwkv_a0[l], rwkv_a2[l],
        rwkv_g2[l], rwkv_k_k[l], rwkv_k_a[l], rwkv_r_k[l], rwkv_ln_w[l], rwkv_ln_b[l], q_norm_w[l],
        k_norm_w[l], lambda_q1[l], lambda_k1[l], lambda_q2[l], lambda_k2[l], diff_subln_w[l],
        w_branch_rwkv[l], w_branch_diff[l], w_out[l], norm2_w[l], router_group_w[l], router_group_b[l],
        router_expert_w[l], router_expert_b[l], expert_w_gate[l], expert_w_up[l], expert_w_down[l],
        proj_tm, tok_tm)
    return _combine(dest1, dest2, h1, ic, yb, b, lp, OUT_TILE)
```

```python
import functools
import math

import jax
import jax.numpy as jnp
from jax import lax
from jax.experimental import pallas as pl
from jax.experimental.pallas import tpu as pltpu

F32 = jnp.float32
BF16 = jnp.bfloat16

D_MODEL = 1024
N_META = 16
NORM_EPS = 1e-6
RWKV_HEADS = 16
RWKV_HEAD_DIM = 64
RWKV_WIDTH = RWKV_HEADS * RWKV_HEAD_DIM
DECAY_LORA = 64
AAA_LORA = 64
GATE_LORA = 128
LORA_COLS = DECAY_LORA + AAA_LORA + GATE_LORA
RWKV_GN_EPS = 64e-5
DIFF_HEADS = 8
DIFF_HEAD_DIM = 64
DIFF_V_DIM = 2 * DIFF_HEAD_DIM
DIFF_QK_WIDTH = DIFF_HEADS * 2 * DIFF_HEAD_DIM
DIFF_V_WIDTH = DIFF_HEADS * DIFF_V_DIM
ROPE_THETA = 500000.0
ROPE_DIM = DIFF_HEAD_DIM // 4
N_GROUPS = 4
EXPERTS_PER_GROUP = 8
N_EXPERTS = N_GROUPS * EXPERTS_PER_GROUP
EXPERT_FF = 512

LANES = 128
ATTN_BLOCK = 256
ATTN_KEY_BLOCK = 512
ATTN_HEADS_PER_STEP = 4
FRONT_PAD = ATTN_BLOCK - N_META
RWKV_CHUNK = 64
RWKV_SUBCHUNKS = 3
RWKV_HEAD_GROUP = 2
OUT_TILE = 128
MOE_BLOCK = 256
NEG_BIG = -1e30
VMEM_LIMIT = 48 * 1024 * 1024

COL_GATE = 0
COL_DIFF = 2 * D_MODEL
COL_RWKV = COL_DIFF + 2 * DIFF_QK_WIDTH + DIFF_V_WIDTH
COL_LORA = COL_RWKV + 3 * RWKV_WIDTH
IN_COLS = COL_LORA + LORA_COLS


def _dot(a, b):
    return jnp.dot(a, b, preferred_element_type=F32)


def _dot_nt(a, b):
    return lax.dot_general(a, b, (((1,), (1,)), ((), ())), preferred_element_type=F32)


def _split2(x):
    hi = x.astype(BF16)
    lo = (x - hi.astype(F32)).astype(BF16)
    return hi, lo


def _mm3(a, b_ref):
    ah, al = _split2(a)
    return _dot(ah, b_ref[0]) + _dot(ah, b_ref[1]) + _dot(al, b_ref[0])


def _hi_lo(w):
    hi = w.astype(BF16)
    return jnp.stack([hi, (w - hi.astype(F32)).astype(BF16)])


def _sigmoid(x):
    return 1.0 / (1.0 + jnp.exp(-x))


def _head_ones(width=LANES, head=RWKV_HEAD_DIM):
    r = lax.broadcasted_iota(jnp.int32, (width, width), 0) // head
    c = lax.broadcasted_iota(jnp.int32, (width, width), 1) // head
    return jnp.where(r == c, 1.0, 0.0).astype(BF16)


def _seg_sum(x, ones_bd):
    hi, lo = _split2(x)
    return _dot(hi, ones_bd) + _dot(lo, ones_bd)


def _seg_sum_wide(x, ones_bd):
    rows, width = x.shape
    n = width // LANES
    xs = jnp.concatenate([x[:, i * LANES:(i + 1) * LANES] for i in range(n)], axis=0)
    ys = _seg_sum(xs, ones_bd)
    return jnp.concatenate([ys[i * rows:(i + 1) * rows] for i in range(n)], axis=1)


def _to_token_tiles(ref, x):
    n, d = x.shape
    k = d // LANES
    for s in range(k):
        ref[pl.ds(s, n, stride=k), :] = x[:, s * LANES:(s + 1) * LANES]


def _from_token_tiles(ref, n):
    k = ref.shape[0] // n
    return jnp.concatenate([ref[pl.ds(s, n, stride=k), :] for s in range(k)], axis=1)


def _proj_kernel(h_ref, nw_ref, w_ref, o_ref, u_ref):
    @pl.when(pl.program_id(1) == 0)
    def _():
        x = h_ref[...]
        ms = jnp.mean(x * x, axis=-1, keepdims=True)
        u_ref[...] = (x * lax.rsqrt(ms + NORM_EPS) * nw_ref[...]).astype(BF16)

    o_ref[...] = _dot(u_ref[...], w_ref[...]).astype(o_ref.dtype)


def _proj(h, norm_w, w_bf16, tm, tn):
    t, d = h.shape
    n = w_bf16.shape[1]
    return pl.pallas_call(
        _proj_kernel,
        out_shape=jax.ShapeDtypeStruct((t, n), BF16),
        grid=(t // tm, n // tn),
        in_specs=[
            pl.BlockSpec((tm, d), lambda i, j: (i, 0)),
            pl.BlockSpec((1, d), lambda i, j: (0, 0)),
            pl.BlockSpec((d, tn), lambda i, j: (0, j)),
        ],
        out_specs=pl.BlockSpec((tm, tn), lambda i, j: (i, j)),
        scratch_shapes=[pltpu.VMEM((tm, d), BF16)],
        compiler_params=pltpu.CompilerParams(
            dimension_semantics=("parallel", "arbitrary"), vmem_limit_bytes=VMEM_LIMIT),
        name="norm_proj",
    )(h, norm_w.reshape(1, d), w_bf16)


def _rwkv_kernel(*refs):
    o_ref = refs[16]
    xr, xk, xv, xl, s_ref = refs[17:22]
    c = pl.program_id(1)
    n_skip = FRONT_PAD // o_ref.shape[1]

    @pl.when(c == 0)
    def _init():
        for xs in (xr, xk, xv, xl):
            xs[0:8, :] = jnp.zeros((8, xs.shape[1]), F32)
        s_ref[...] = jnp.zeros(s_ref.shape, F32)

    @pl.when(c < n_skip)
    def _pad():
        o_ref[...] = jnp.zeros(o_ref.shape, o_ref.dtype)

    @pl.when(c >= n_skip)
    def _chunk():
        _rwkv_chunk(*refs)


def _rwkv_chunk(r_ref, k_ref, v_ref, lo_ref, mu_ref, mul_ref, w0_ref, w2_ref, a0_ref, a2_ref, g2_ref,
                kkw_ref, kaw_ref, rkw_ref, lnw_ref, lnb_ref, o_ref,
                xr, xk, xv, xl, s_ref, kt_s, bt_s, kn_s, rt_s, v_s, y_s, gc_s, bon_s, g_s):
    R = r_ref.shape[1]
    C = RWKV_CHUNK
    n_sub = R // C
    W = RWKV_WIDTH

    def shift_mix(in_ref, xs, mu):
        x = in_ref[0].astype(F32)
        xs[8:8 + R, :] = x
        prev = xs[7:7 + R, :]
        xs[7:8, :] = x[R - 1:R, :]
        return x + (prev - x) * mu

    r = shift_mix(r_ref, xr, mu_ref[:, 0:W])
    k = shift_mix(k_ref, xk, mu_ref[:, W:2 * W])
    v = shift_mix(v_ref, xv, mu_ref[:, 2 * W:3 * W])
    lo = shift_mix(lo_ref, xl, mul_ref[...])
    xw = jnp.tanh(lo[:, 0:DECAY_LORA])
    xa = lo[:, DECAY_LORA:DECAY_LORA + AAA_LORA]
    xg = _sigmoid(lo[:, DECAY_LORA + AAA_LORA:LORA_COLS])

    lw = (-math.exp(-0.5)) * _sigmoid(w0_ref[...] + _mm3(xw, w2_ref))
    a = _sigmoid(a0_ref[...] + _mm3(xa, a2_ref))
    g_s[...] = _mm3(xg, g2_ref)

    ones_bd = _head_ones()
    kk = k * kkw_ref[...]
    kkn = kk * lax.rsqrt(jnp.maximum(_seg_sum_wide(kk * kk, ones_bd), 1e-24))
    k2 = k * (1.0 + (a - 1.0) * kaw_ref[...])
    bon_s[...] = _seg_sum_wide(r * k2 * rkw_ref[...], ones_bd) * v

    ti = lax.broadcasted_iota(jnp.int32, (R, R), 0)
    tj = lax.broadcasted_iota(jnp.int32, (R, R), 1)
    ltri = jnp.where((ti >= tj) & (ti // C == tj // C), 1.0, 0.0).astype(BF16)
    l1 = lw.astype(BF16)
    rem = lw - l1.astype(F32)
    l2 = rem.astype(BF16)
    l3 = (rem - l2.astype(F32)).astype(BF16)
    cum = _dot(ltri, l1) + _dot(ltri, l2) + _dot(ltri, l3)
    e_pos = jnp.exp(cum)
    e_neg = jnp.exp(-cum)
    kt_s[...] = kkn * jnp.exp(cum - lw)
    bt_s[...] = kkn * a * e_neg
    kn_s[...] = k2 * e_neg
    rt_s[...] = r * e_pos
    v_s[...] = v
    for s in range(n_sub):
        gc_s[s:s + 1, :] = jnp.exp(cum[(s + 1) * C - 1:(s + 1) * C, :])

    G = RWKV_HEAD_GROUP
    GL = G * RWKV_HEAD_DIM
    GC = G * C
    n_groups = W // GL
    lane_head = lax.broadcasted_iota(jnp.int32, (1, GL), 1) // RWKV_HEAD_DIM
    head_mask = [jnp.where(lane_head == h, 1.0, 0.0).astype(BF16) for h in range(G)]
    trow = lax.broadcasted_iota(jnp.int32, (C, GC), 0)
    tcol = lax.broadcasted_iota(jnp.int32, (C, GC), 1) % C
    strict = trow > tcol
    incl = trow >= tcol
    eye = jnp.where(trow == tcol, 1.0, 0.0)
    br = lax.broadcasted_iota(jnp.int32, (GL, GL), 0) // RWKV_HEAD_DIM
    bc = lax.broadcasted_iota(jnp.int32, (GL, GL), 1) // RWKV_HEAD_DIM
    bdmask = jnp.where(br == bc, 1.0, 0.0)

    def stack(y):
        yb = y.astype(BF16)
        return jnp.concatenate([yb * m for m in head_mask], axis=0)

    groups = range(n_groups)
    units = [(s, p) for s in range(n_sub) for p in groups]
    win = {(s, p): (slice(s * C, (s + 1) * C), slice(p * GL, (p + 1) * GL)) for s, p in units}
    def prepare(units):
        kt = {q: kt_s[win[q]] for q in units}
        bt = {q: bt_s[win[q]] for q in units}
        kn = {q: kn_s[win[q]] for q in units}
        rt = {q: rt_s[win[q]] for q in units}
        vv = {q: v_s[win[q]] for q in units}
        gc = {q: gc_s[q[0]:q[0] + 1, win[q][1]] for q in units}
        kr = {q: jnp.concatenate([kt[q], rt[q]], axis=0).astype(BF16) for q in units}
        m_all = {q: _dot_nt(kr[q], jnp.concatenate([stack(bt[q]), stack(kn[q])], axis=0))
                 for q in units}
        p_mat = {q: jnp.where(incl, m_all[q][C:2 * C, 0:GC], 0.0).astype(BF16) for q in units}
        bq_mat = {q: jnp.concatenate([jnp.where(strict, m_all[q][0:C, GC:2 * GC], 0.0),
                                      jnp.where(incl, m_all[q][C:2 * C, GC:2 * GC], 0.0)], axis=0).astype(BF16)
                  for q in units}
        v_term = {q: _dot(bq_mat[q], stack(vv[q])) for q in units}

        pw = {q: -jnp.where(strict, m_all[q][0:C, 0:GC], 0.0) for q in units}
        t_inv = {q: eye + pw[q] for q in units}
        pw = {q: _dot(pw[q].astype(BF16), stack(pw[q])) for q in units}
        for _ in range(int(math.log2(C)) - 2):
            both = {q: _dot(jnp.concatenate([t_inv[q], pw[q]], axis=0).astype(BF16), stack(pw[q]))
                    for q in units}
            t_inv = {q: t_inv[q] + both[q][0:C] for q in units}
            pw = {q: both[q][C:2 * C] for q in units}
        t_inv = {q: (t_inv[q] + _dot(t_inv[q].astype(BF16), stack(pw[q]))).astype(BF16) for q in units}
        xc = {q: jnp.concatenate([bt[q] * gc[q], kn[q] * gc[q]], axis=0).astype(BF16) for q in units}
        return kr, p_mat, v_term, t_inv, xc, vv, gc

    kr, p_mat, v_term, t_inv, xc, vv, gc = prepare(units)

    state = [s_ref[p] for p in groups]
    for s in range(n_sub):
        qs = [(s, p) for p in groups]
        ks = [_dot_nt(kr[q], state[q[1]].astype(BF16)) for q in qs]
        u = [-_dot(t_inv[q], stack(ks[q[1]][0:C] + v_term[q][0:C])) for q in qs]
        for q in qs:
            y_s[win[q]] = ks[q[1]][C:2 * C] + v_term[q][C:2 * C] + _dot(p_mat[q], stack(u[q[1]]))
        for q in qs:
            uc = jnp.concatenate([u[q[1]], vv[q]], axis=0)
            state[q[1]] = state[q[1]] * gc[q] + bdmask * _dot(jnp.transpose(uc).astype(BF16), xc[q])
    for p in groups:
        s_ref[p] = state[p]

    y = y_s[...]
    inv_n = 1.0 / RWKV_HEAD_DIM
    mean = _seg_sum_wide(y, ones_bd) * inv_n
    dlt = y - mean
    var = _seg_sum_wide(dlt * dlt, ones_bd) * inv_n
    yn = dlt * lax.rsqrt(var + RWKV_GN_EPS) * lnw_ref[...] + lnb_ref[...]
    o_ref[0] = ((yn + bon_s[...]) * g_s[...]).astype(o_ref.dtype)


def _rwkv(proj3, mu_rkv, mu_lo, w0, w2, a0, a2, g2, k_k, k_a, r_k, ln_w, ln_b):
    b, lp, _ = proj3.shape
    C = RWKV_CHUNK * RWKV_SUBCHUNKS
    W = RWKV_WIDTH
    cb = COL_RWKV // W
    lb = COL_LORA // LORA_COLS
    row = lambda x: x.reshape(1, -1)
    full = lambda shape: pl.BlockSpec(shape, lambda i, c: (0,) * len(shape))
    wide = pltpu.VMEM((C, W), F32)
    gl = RWKV_HEAD_GROUP * RWKV_HEAD_DIM
    return pl.pallas_call(
        _rwkv_kernel,
        out_shape=jax.ShapeDtypeStruct((b, lp, W), BF16),
        grid=(b, lp // C),
        in_specs=[
            pl.BlockSpec((1, C, W), lambda i, c: (i, c, cb)),
            pl.BlockSpec((1, C, W), lambda i, c: (i, c, cb + 1)),
            pl.BlockSpec((1, C, W), lambda i, c: (i, c, cb + 2)),
            pl.BlockSpec((1, C, LORA_COLS), lambda i, c: (i, c, lb)),
            full((1, 3 * W)), full((1, LORA_COLS)),
            full((1, W)), full((2, DECAY_LORA, W)), full((1, W)), full((2, AAA_LORA, W)), full((2, GATE_LORA, W)),
            full((1, W)), full((1, W)), full((1, W)), full((1, W)), full((1, W)),
        ],
        out_specs=pl.BlockSpec((1, C, W), lambda i, c: (i, c, 0)),
        scratch_shapes=[
            pltpu.VMEM((C + 8, W), F32), pltpu.VMEM((C + 8, W), F32), pltpu.VMEM((C + 8, W), F32),
            pltpu.VMEM((C + 8, LORA_COLS), F32),
            pltpu.VMEM((W // gl, gl, gl), F32),
            wide, wide, wide, wide, wide, wide, pltpu.VMEM((8, W), F32), wide, wide,
        ],
        compiler_params=pltpu.CompilerParams(
            dimension_semantics=("parallel", "arbitrary"), vmem_limit_bytes=VMEM_LIMIT),
        name="rwkv7_time_mix",
    )(proj3, proj3, proj3, proj3, row(mu_rkv), row(mu_lo), row(w0), _hi_lo(w2), row(a0), _hi_lo(a2), _hi_lo(g2),
      row(k_k), row(k_a), row(r_k), row(ln_w), row(ln_b))


def _attn_kernel(q_ref, k_ref, v_ref, cos_ref, s1_ref, s2_ref, qw_ref, kw_ref, lam_ref, sw_ref, o_ref,
                 kp_s, vp_s, m_s, acc_s, *, tq, tk, lambda_init):
    qi = pl.program_id(2)
    lp = k_ref.shape[1]
    n_real = lp - FRONT_PAD
    lk = kp_s.shape[0]
    nh = k_ref.shape[2] // LANES
    vw = 2 * LANES
    heads = range(nh)
    hs = [slice(h * LANES, (h + 1) * LANES) for h in heads]
    ones_bd = _head_ones(LANES, DIFF_HEAD_DIM)
    shift = ROPE_DIM // 2

    def norm_rope(x, w, rows):
        ms = _seg_sum(x * x, ones_bd) * (1.0 / DIFF_HEAD_DIM)
        xn = x * lax.rsqrt(ms + NORM_EPS) * w
        return (xn * cos_ref[rows, :] + pltpu.roll(xn, shift, 1) * s1_ref[rows, :]
                + pltpu.roll(xn, LANES - shift, 1) * s2_ref[rows, :])

    @pl.when(qi == 0)
    def _prep():
        def put(dst, n):
            src = pl.ds(FRONT_PAD + dst, n)
            rows = pl.ds(dst, n)
            for h in heads:
                kp_s[rows, hs[h]] = norm_rope(k_ref[0, src, hs[h]].astype(F32), kw_ref[...], src).astype(BF16)
                vp_s[rows, h * vw:h * vw + LANES] = v_ref[0, src, hs[h]].astype(BF16)
                vp_s[rows, h * vw + LANES:(h + 1) * vw] = jnp.ones((n, LANES), BF16)

        def body(i, carry):
            put(pl.multiple_of(i * LANES, LANES), LANES)
            return carry
        lax.fori_loop(0, n_real // LANES, body, 0, unroll=2)
        tail = n_real % LANES
        if tail:
            put(n_real - tail, tail)
        if lk > n_real:
            kp_s[n_real:lk, :] = jnp.zeros((lk - n_real, nh * LANES), BF16)
            vp_s[n_real:lk, :] = jnp.zeros((lk - n_real, nh * vw), BF16)

    lane = lax.broadcasted_iota(jnp.int32, (1, LANES), 1)
    m0 = jnp.where(lane < DIFF_HEAD_DIM, 1.0, 0.0)
    m1 = 1.0 - m0
    rows_q = pl.ds(pl.multiple_of(qi * tq, tq), tq)
    qs = []
    for h in heads:
        qn = norm_rope(q_ref[0, :, hs[h]].astype(F32), qw_ref[...], rows_q) * (DIFF_HEAD_DIM ** -0.5)
        qs.append(jnp.concatenate([qn * m0, qn * m1], axis=0).astype(BF16))
    first_row = qi * tq - FRONT_PAD
    row = first_row + lax.broadcasted_iota(jnp.int32, (2 * tq, LANES), 0) % tq
    col0 = lax.broadcasted_iota(jnp.int32, (2 * tq, LANES), 1)

    def step(j, width, causal, init=False):
        start = j * tk
        if not isinstance(j, int):
            start = pl.multiple_of(start, tk)
        ks = pl.ds(start, width)
        n_sub = width // LANES
        s = [_dot_nt(qs[h], kp_s[ks, hs[h]]) for h in heads]
        sub = [[s[h][:, c * LANES:(c + 1) * LANES] for c in range(n_sub)] for h in heads]
        if causal:
            sub = [[jnp.where(start + c * LANES + col0 <= row, sub[h][c], NEG_BIG) for c in range(n_sub)]
                   for h in heads]
        m_new = []
        for h in heads:
            mx = functools.reduce(jnp.maximum, sub[h])
            mx = jnp.broadcast_to(jnp.max(mx, axis=1, keepdims=True), mx.shape)
            m_new.append(mx if init else jnp.maximum(m_s[h], mx))
        p = [jnp.concatenate([jnp.exp(sub[h][c] - m_new[h]) for c in range(n_sub)], axis=1).astype(BF16)
             for h in heads]
        pv = [_dot(p[h], vp_s[ks, h * vw:(h + 1) * vw]) for h in heads]
        for h in heads:
            if init:
                acc_s[h] = pv[h]
            else:
                alpha = jnp.exp(m_s[h] - m_new[h])
                acc_s[h] = jnp.concatenate([alpha, alpha], axis=1) * acc_s[h] + pv[h]
            m_s[h] = m_new[h]

    assert FRONT_PAD + N_META == tq and tk % tq == 0 and N_META <= LANES
    n_full = jnp.maximum(first_row + 1, 0) // tk
    last = jnp.maximum(first_row + tq - 1, 0) // tk
    step(n_full, tk, True, init=True)

    def mid(j, carry):
        step(j, tk, False)
        return carry
    lax.fori_loop(0, n_full, mid, 0)

    @pl.when(last > n_full)
    def _diag():
        step(last, LANES, True)

    lam = (jnp.exp(jnp.sum(lam_ref[0:1, :] * lam_ref[1:2, :], axis=1, keepdims=True))
           - jnp.exp(jnp.sum(lam_ref[2:3, :] * lam_ref[3:4, :], axis=1, keepdims=True)) + lambda_init)
    for h in heads:
        acc = acc_s[h]
        o = acc[:, 0:LANES] / acc[:, LANES:vw]
        od = o[0:tq] - lam * o[tq:2 * tq]
        ms = jnp.mean(od * od, axis=1, keepdims=True)
        o_ref[0, :, hs[h]] = (od * lax.rsqrt(ms + NORM_EPS) * sw_ref[...]
                              * (1.0 - lambda_init)).astype(o_ref.dtype)


def _rope_tables(lp):
    half = ROPE_DIM // 2
    inv_freq = jnp.exp(-math.log(ROPE_THETA) * jnp.arange(half, dtype=F32) * 2.0 / ROPE_DIM)
    pos = (jnp.arange(lp) - FRONT_PAD).astype(F32)
    ang = pos[:, None] * inv_freq[None, :]
    cos, sin = jnp.cos(ang), jnp.sin(ang)
    one = jnp.ones((lp, DIFF_HEAD_DIM - ROPE_DIM), F32)
    zero = jnp.zeros((lp, DIFF_HEAD_DIM - ROPE_DIM), F32)
    zh = jnp.zeros((lp, half), F32)
    c = jnp.concatenate([cos, cos, one], axis=1)
    s1 = jnp.concatenate([zh, sin, zero], axis=1)
    s2 = jnp.concatenate([-sin, zh, zero], axis=1)
    dup = lambda t: jnp.concatenate([t, t], axis=1)
    return dup(c), dup(s1), dup(s2)


def _attention(proj3, q_norm_w, k_norm_w, lam4, subln_w, lambda_init):
    b, lp, _ = proj3.shape
    tq, tk, nh = ATTN_BLOCK, ATTN_KEY_BLOCK, ATTN_HEADS_PER_STEP
    lk = -(-(lp - FRONT_PAD) // tk) * tk
    hw = nh * LANES
    qb = COL_DIFF // hw
    kb = qb + DIFF_QK_WIDTH // hw
    vb = kb + DIFF_QK_WIDTH // hw
    cos, s1, s2 = _rope_tables(lp)
    dup = lambda w: jnp.concatenate([w, w]).reshape(1, LANES)
    full = lambda shape: pl.BlockSpec(shape, lambda i, h, q: (0,) * len(shape))
    return pl.pallas_call(
        functools.partial(_attn_kernel, tq=tq, tk=tk, lambda_init=lambda_init),
        out_shape=jax.ShapeDtypeStruct((b, lp, DIFF_V_WIDTH), BF16),
        grid=(b, DIFF_HEADS // nh, lp // tq),
        in_specs=[
            pl.BlockSpec((1, tq, hw), lambda i, h, q: (i, q, qb + h)),
            pl.BlockSpec((1, lp, hw), lambda i, h, q: (i, 0, kb + h)),
            pl.BlockSpec((1, lp, hw), lambda i, h, q: (i, 0, vb + h)),
            full((lp, LANES)), full((lp, LANES)), full((lp, LANES)),
            full((1, LANES)), full((1, LANES)), full((4, DIFF_HEAD_DIM)), full((1, LANES)),
        ],
        out_specs=pl.BlockSpec((1, tq, hw), lambda i, h, q: (i, q, h)),
        scratch_shapes=[pltpu.VMEM((lk, hw), BF16), pltpu.VMEM((lk, 2 * hw), BF16),
                        pltpu.VMEM((nh, 2 * tq, LANES), F32), pltpu.VMEM((nh, 2 * tq, 2 * LANES), F32)],
        compiler_params=pltpu.CompilerParams(
            dimension_semantics=("parallel", "parallel", "arbitrary"), vmem_limit_bytes=VMEM_LIMIT),
        name="diff_attention",
    )(proj3, proj3, proj3, cos, s1, s2, dup(q_norm_w), dup(k_norm_w), lam4, subln_w.reshape(1, LANES))


def _merge_kernel(rw_ref, da_ref, g1_ref, g2_ref, h_ref, wbr_ref, wbd_ref, wo_ref, n2_ref, wr_ref, br_ref,
                  h1_ref, u2_ref, il_ref, ic_ref, cnt_ref, base_s, *, lp):
    i = pl.program_id(0)
    tm = rw_ref.shape[0]

    @pl.when(i == 0)
    def _():
        base_s[...] = jnp.zeros(base_s.shape, F32)

    y1 = _dot(rw_ref[...], wbr_ref[...])
    y2 = _dot(da_ref[...], wbd_ref[...])
    merged = _sigmoid(g1_ref[...].astype(F32)) * y1 + _sigmoid(g2_ref[...].astype(F32)) * y2
    h1 = h_ref[...] + _dot(merged.astype(BF16), wo_ref[...])
    h1_ref[...] = h1
    u2 = h1 * lax.rsqrt(jnp.mean(h1 * h1, axis=-1, keepdims=True) + NORM_EPS) * n2_ref[...]
    _to_token_tiles(u2_ref, u2)

    uh, ul = _split2(u2)
    wh, wl = _split2(wr_ref[...])
    lt = _dot_nt(wh, uh) + _dot_nt(wh, ul) + _dot_nt(wl, uh) + br_ref[...]

    gi8 = lax.broadcasted_iota(jnp.int32, (8, tm), 0)
    lg = lt[0:8]
    ge = jnp.exp(lg - jnp.max(lg, axis=0, keepdims=True))
    gp = ge / jnp.sum(ge, axis=0, keepdims=True)
    gv = jnp.max(gp, axis=0, keepdims=True)
    gidx = jnp.min(jnp.where(gp == gv, gi8, N_EXPERTS), axis=0, keepdims=True)

    ei = lax.broadcasted_iota(jnp.int32, (N_EXPERTS, tm), 0)
    sel = (ei // EXPERTS_PER_GROUP) == gidx
    le = jnp.where(sel, lt[8:8 + N_EXPERTS], NEG_BIG)
    ee = jnp.where(sel, jnp.exp(le - jnp.max(le, axis=0, keepdims=True)), 0.0)
    ep = jnp.where(sel, ee / jnp.sum(ee, axis=0, keepdims=True), -1.0)
    v1 = jnp.max(ep, axis=0, keepdims=True)
    i1 = jnp.min(jnp.where(ep == v1, ei, N_EXPERTS), axis=0, keepdims=True)
    ep2 = jnp.where(ei == i1, -1.0, ep)
    v2 = jnp.max(ep2, axis=0, keepdims=True)
    i2 = jnp.min(jnp.where(ep2 == v2, ei, N_EXPERTS), axis=0, keepdims=True)
    den = v1 + v2
    gate1 = gv * v1 / den
    gate2 = gv * v2 / den

    tok = (i * tm + lax.broadcasted_iota(jnp.int32, (1, tm), 1)).astype(F32)
    pos = tok - jnp.floor((tok + 0.5) / lp) * lp
    valid = pos > (FRONT_PAD - 0.5)

    oh1 = jnp.where((ei == i1) & valid, 1.0, 0.0)
    oh2 = jnp.where((ei == i2) & valid, 1.0, 0.0)
    oh = oh1 + oh2
    ur = lax.broadcasted_iota(jnp.int32, (tm, tm), 0)
    uc = lax.broadcasted_iota(jnp.int32, (tm, tm), 1)
    before = jnp.where(ur < uc, 1.0, 0.0).astype(BF16)
    tot = base_s[:, 0:1] + _dot(oh.astype(BF16), before)
    rank1 = jnp.sum(oh1 * tot, axis=0, keepdims=True)
    rank2 = jnp.sum(oh2 * tot, axis=0, keepdims=True)
    base_s[...] = base_s[...] + jnp.sum(oh, axis=1, keepdims=True)
    cnt_ref[...] = base_s[...]

    il = jnp.where(gi8 == 0, i1, jnp.where(gi8 == 1, i2, jnp.where(
        gi8 == 2, rank1.astype(jnp.int32), jnp.where(gi8 == 3, rank2.astype(jnp.int32), jnp.where(
            gi8 == 4, valid.astype(jnp.int32), 0)))))
    il_ref[...] = il
    ri = lax.broadcasted_iota(jnp.int32, (LANES, tm), 0)
    ic = jnp.where(ri == 0, gate1, jnp.where(ri == 1, gate2, 0.0))
    ic_ref[...] = jnp.transpose(ic)


def _merge(rw, da, proj, h0, wbr, wbd, wo, norm2_w, wr, br, lp, tm):
    t, d = h0.shape
    gb = COL_GATE // d
    full = lambda shape: pl.BlockSpec(shape, lambda i: (0,) * len(shape))
    tile = lambda c: pl.BlockSpec((tm, d), lambda i: (i, c))
    return pl.pallas_call(
        functools.partial(_merge_kernel, lp=lp),
        out_shape=(
            jax.ShapeDtypeStruct((t, d), F32),
            jax.ShapeDtypeStruct((t * (d // LANES), LANES), F32),
            jax.ShapeDtypeStruct((8, t), jnp.int32),
            jax.ShapeDtypeStruct((t, LANES), F32),
            jax.ShapeDtypeStruct((N_EXPERTS, LANES), F32),
        ),
        grid=(t // tm,),
        in_specs=[tile(0), tile(0), tile(gb), tile(gb + 1), tile(0),
                  full((d, d)), full((d, d)), full((d, d)), full((1, d)), full((LANES, d)), full((LANES, 1))],
        out_specs=(
            tile(0), pl.BlockSpec((tm * (d // LANES), LANES), lambda i: (i, 0)),
            pl.BlockSpec((8, tm), lambda i: (0, i)),
            pl.BlockSpec((tm, LANES), lambda i: (i, 0)),
            full((N_EXPERTS, LANES)),
        ),
        scratch_shapes=[pltpu.VMEM((N_EXPERTS, LANES), F32)],
        compiler_params=pltpu.CompilerParams(
            dimension_semantics=("arbitrary",), vmem_limit_bytes=VMEM_LIMIT),
        name="merge_router",
    )(rw, da, proj, proj, h0, wbr, wbd, wo, norm2_w.reshape(1, d), wr, br)


def _dispatch_kernel(d1_ref, d2_ref, u_ref, xin_hbm, xb_hbm, sem):
    del xin_hbm
    tm = d1_ref.shape[2]
    k = u_ref.shape[0] // tm

    def start(r, carry):
        for prio, d_ref in enumerate((d1_ref, d2_ref)):
            pltpu.make_async_copy(u_ref.at[pl.ds(pl.multiple_of(r * k, k), k)],
                                  xb_hbm.at[pl.ds(pl.multiple_of(d_ref[0, 0, r] * k, k), k)],
                                  sem).start(priority=prio)
        return carry

    lax.fori_loop(0, tm, start, 0, unroll=8)
    for _ in range(2):
        pltpu.make_async_copy(u_ref, xb_hbm.at[pl.ds(0, tm * k)], sem).wait()


def _dispatch(dest1, dest2, u2, cap, tm):
    k = D_MODEL // LANES
    t = u2.shape[0] // k
    nt = t // tm
    smem = lambda: pl.BlockSpec((1, 1, tm), lambda i: (i, 0, 0), memory_space=pltpu.SMEM)
    return pl.pallas_call(
        _dispatch_kernel,
        out_shape=jax.ShapeDtypeStruct((cap * k, LANES), F32),
        grid=(nt,),
        in_specs=[smem(), smem(), pl.BlockSpec((tm * k, LANES), lambda i: (i, 0)),
                  pl.BlockSpec(memory_space=pl.ANY)],
        out_specs=pl.BlockSpec(memory_space=pl.ANY),
        scratch_shapes=[pltpu.SemaphoreType.DMA(())],
        input_output_aliases={3: 0},
        compiler_params=pltpu.CompilerParams(
            dimension_semantics=("arbitrary",), vmem_limit_bytes=VMEM_LIMIT),
        name="moe_dispatch",
    )(dest1.reshape(nt, 1, tm), dest2.reshape(nt, 1, tm), u2, jnp.zeros((cap * k, LANES), F32))


def _moe_kernel(be_ref, nb_ref, grp_ref, nxt_ref, x_ref, wg_hbm, wu_hbm, wd_hbm, o_ref,
                wg_f, wu_f, wd_f, wg_s, wu_s, wd_s, sem):
    i = pl.program_id(0)
    bm = MOE_BLOCK
    used = i < nb_ref[0]
    slot = lax.rem(grp_ref[i], 2)

    def fetch(expert, sl):
        return [pltpu.make_async_copy(w_hbm.at[expert], w_f.at[sl], sem.at[sl])
                for w_hbm, w_f in ((wg_hbm, wg_f), (wu_hbm, wu_f), (wd_hbm, wd_f))]

    @pl.when(used & ((i == 0) | (grp_ref[i] != grp_ref[jnp.maximum(i - 1, 0)])))
    def _():
        @pl.when(i == 0)
        def _():
            for cp in fetch(be_ref[0], 0):
                cp.start()

        @pl.when(nxt_ref[i] >= 0)
        def _():
            for cp in fetch(nxt_ref[i], 1 - slot):
                cp.start()

        for cp in fetch(be_ref[i], slot):
            cp.wait()
        wg_s[...] = wg_f[slot].astype(BF16)
        wu_s[...] = wu_f[slot].astype(BF16)
        wd_s[...] = wd_f[slot].astype(BF16)

    @pl.when(used)
    def _():
        x = _from_token_tiles(x_ref, bm).astype(BF16)
        hg = _dot(x, wg_s[...])
        hu = _dot(x, wu_s[...])
        hid = hg * _sigmoid(hg) * hu
        _to_token_tiles(o_ref, _dot(hid.astype(BF16), wd_s[...]))

    @pl.when(i >= nb_ref[0])
    def _():
        o_ref[...] = jnp.zeros(o_ref.shape, F32)


def _moe(block_e, n_used, block_grp, block_nxt, xb, wg, wu, wd):
    d = wg.shape[1]
    ff = wg.shape[2]
    bm = MOE_BLOCK
    k = d // LANES
    cap = block_e.shape[0] * bm
    hbm = pl.BlockSpec(memory_space=pl.ANY)
    return pl.pallas_call(
        _moe_kernel,
        out_shape=jax.ShapeDtypeStruct((cap * k, LANES), F32),
        grid_spec=pltpu.PrefetchScalarGridSpec(
            num_scalar_prefetch=4,
            grid=(cap // bm,),
            in_specs=[pl.BlockSpec((bm * k, LANES), lambda i, *_: (i, 0)), hbm, hbm, hbm],
            out_specs=pl.BlockSpec((bm * k, LANES), lambda i, *_: (i, 0)),
            scratch_shapes=[pltpu.VMEM((2, d, ff), F32), pltpu.VMEM((2, d, ff), F32), pltpu.VMEM((2, ff, d), F32),
                            pltpu.VMEM((d, ff), BF16), pltpu.VMEM((d, ff), BF16), pltpu.VMEM((ff, d), BF16),
                            pltpu.SemaphoreType.DMA((2,))],
        ),
        compiler_params=pltpu.CompilerParams(
            dimension_semantics=("arbitrary",), vmem_limit_bytes=VMEM_LIMIT),
        name="moe_experts",
    )(block_e, n_used, block_grp, block_nxt, xb, wg, wu, wd)


def _combine_kernel(d1_ref, d2_ref, n1_ref, n2_ref, h_ref, ic_ref, yb_hbm, o_ref, ga, gb, sem):
    tm = h_ref.shape[0]
    k = ga.shape[1] // tm
    n = pl.program_id(0) * pl.num_programs(1) + pl.program_id(1)
    total = pl.num_programs(0) * pl.num_programs(1)
    slot = lax.rem(n, 2)

    def issue(da_ref, db_ref, sl):
        def start(r, carry):
            for prio, (d_ref, buf) in enumerate(((da_ref, ga), (db_ref, gb))):
                pltpu.make_async_copy(yb_hbm.at[pl.ds(pl.multiple_of(d_ref[0, 0, r] * k, k), k)],
                                      buf.at[sl, pl.ds(pl.multiple_of(r * k, k), k)],
                                      sem.at[sl]).start(priority=prio)
            return carry
        lax.fori_loop(0, tm, start, 0, unroll=8)

    @pl.when(n == 0)
    def _first():
        issue(d1_ref, d2_ref, 0)

    @pl.when(n + 1 < total)
    def _next():
        issue(n1_ref, n2_ref, 1 - slot)

    for buf in (ga, gb):
        pltpu.make_async_copy(yb_hbm.at[pl.ds(0, tm * k)], buf.at[slot], sem.at[slot]).wait()
    ic = ic_ref[...]
    o_ref[0] = (h_ref[...] + ic[:, 0:1] * _from_token_tiles(ga.at[slot], tm)
                + ic[:, 1:2] * _from_token_tiles(gb.at[slot], tm))


def _combine(dest1, dest2, h1, ic, yb, b, lp, tm):
    t, d = h1.shape
    per = lp // tm
    first = (FRONT_PAD + N_META) // tm
    steps = per - first
    nt = t // tm
    tile = lambda i, j: i * per + j + first

    def next_tile(i, j):
        nxt = jnp.minimum(i * steps + j + 1, b * steps - 1)
        return tile(nxt // steps, nxt % steps)

    cur = lambda: pl.BlockSpec((1, 1, tm), lambda i, j: (tile(i, j), 0, 0), memory_space=pltpu.SMEM)
    nxt = lambda: pl.BlockSpec((1, 1, tm), lambda i, j: (next_tile(i, j), 0, 0), memory_space=pltpu.SMEM)
    d1 = dest1.reshape(nt, 1, tm)
    d2 = dest2.reshape(nt, 1, tm)
    return pl.pallas_call(
        _combine_kernel,
        out_shape=jax.ShapeDtypeStruct((b, lp - FRONT_PAD - N_META, d), F32),
        grid=(b, steps),
        in_specs=[
            cur(), cur(), nxt(), nxt(),
            pl.BlockSpec((tm, d), lambda i, j: (tile(i, j), 0)),
            pl.BlockSpec((tm, LANES), lambda i, j: (tile(i, j), 0)),
            pl.BlockSpec(memory_space=pl.ANY),
        ],
        out_specs=pl.BlockSpec((1, tm, d), lambda i, j: (i, j, 0)),
        scratch_shapes=[pltpu.VMEM((2, tm * (d // LANES), LANES), F32),
                        pltpu.VMEM((2, tm * (d // LANES), LANES), F32),
                        pltpu.SemaphoreType.DMA((2,))],
        compiler_params=pltpu.CompilerParams(
            dimension_semantics=("arbitrary", "arbitrary"), vmem_limit_bytes=VMEM_LIMIT),
        name="moe_combine",
    )(d1, d2, d1, d2, h1, ic, yb)


def _routing_tables(il, cnt, n_blocks, tm):
    bm = MOE_BLOCK
    counts = cnt[:, 0].astype(jnp.int32)
    padded = (counts + bm - 1) // bm * bm
    pad_end = jnp.cumsum(padded)
    pad_start = pad_end - padded
    valid = il[4] > 0
    spare = n_blocks * bm + jnp.arange(il.shape[1], dtype=jnp.int32) % tm
    eid = jnp.arange(N_EXPERTS, dtype=jnp.int32)
    first_slot = jnp.sum(jnp.where(il[0:2, :, None] == eid, pad_start, 0), axis=-1)
    dest1 = jnp.where(valid, first_slot[0] + il[2], spare).astype(jnp.int32)
    dest2 = jnp.where(valid, first_slot[1] + il[3], spare + tm).astype(jnp.int32)
    starts = jnp.arange(n_blocks, dtype=jnp.int32) * bm
    block_e = jnp.minimum(jnp.sum((pad_end[None, :] <= starts[:, None]).astype(jnp.int32), axis=1),
                          N_EXPERTS - 1)
    n_used = (pad_end[-1:] // bm).astype(jnp.int32)
    owns = counts > 0
    eid = jnp.arange(N_EXPERTS, dtype=jnp.int32)
    ordinal = jnp.cumsum(owns.astype(jnp.int32)) - 1
    later = (eid[None, :] > eid[:, None]) & owns[None, :]
    nxt = jnp.min(jnp.where(later, eid[None, :], N_EXPERTS), axis=1)
    nxt = jnp.where(nxt < N_EXPERTS, nxt, -1)
    return dest1, dest2, block_e, n_used, ordinal[block_e].astype(jnp.int32), nxt[block_e].astype(jnp.int32)


def _layer(h0, lp, l, norm1_w, w_in, rwkv_mu, rwkv_w0, rwkv_w2, rwkv_a0, rwkv_a2, rwkv_g2,
           rwkv_k_k, rwkv_k_a, rwkv_r_k, rwkv_ln_w, rwkv_ln_b, q_norm_w, k_norm_w,
           lambda_q1, lambda_k1, lambda_q2, lambda_k2, diff_subln_w, w_branch_rwkv, w_branch_diff,
           w_out, norm2_w, router_group_w, router_group_b, router_expert_w, router_expert_b,
           expert_w_gate, expert_w_up, expert_w_down, proj_tm, tok_tm):
    t, d = h0.shape
    b = t // lp
    lambda_init = 0.8 - 0.6 * math.exp(-0.3 * l)
    rw_cols = 3 * RWKV_WIDTH
    diff_cols = 2 * DIFF_QK_WIDTH + DIFF_V_WIDTH
    w_perm = jnp.concatenate([
        w_in[:, rw_cols + LORA_COLS + diff_cols:],
        w_in[:, rw_cols + LORA_COLS:rw_cols + LORA_COLS + diff_cols],
        w_in[:, :rw_cols + LORA_COLS],
    ], axis=1).astype(BF16)
    proj = _proj(h0, norm1_w, w_perm, proj_tm, 768)
    proj3 = proj.reshape(b, lp, IN_COLS)

    rw = _rwkv(proj3, rwkv_mu[:rw_cols], rwkv_mu[rw_cols:], rwkv_w0, rwkv_w2, rwkv_a0, rwkv_a2, rwkv_g2,
               rwkv_k_k, rwkv_k_a, rwkv_r_k.reshape(-1), rwkv_ln_w, rwkv_ln_b)
    lam4 = jnp.stack([lambda_q1, lambda_k1, lambda_q2, lambda_k2])
    da = _attention(proj3, q_norm_w, k_norm_w, lam4, diff_subln_w, lambda_init)

    wr = jnp.zeros((LANES, d), F32).at[0:N_GROUPS].set(router_group_w.T).at[8:8 + N_EXPERTS].set(router_expert_w.T)
    br = jnp.zeros((LANES,), F32).at[0:N_GROUPS].set(router_group_b).at[N_GROUPS:8].set(NEG_BIG)
    br = br.at[8:8 + N_EXPERTS].set(router_expert_b).reshape(LANES, 1)
    h1, u2, il, ic, cnt = _merge(
        rw.reshape(t, RWKV_WIDTH), da.reshape(t, DIFF_V_WIDTH), proj, h0,
        w_branch_rwkv.astype(BF16), w_branch_diff.astype(BF16), w_out.astype(BF16), norm2_w, wr, br, lp, tok_tm)

    n_real = b * (lp - FRONT_PAD)
    n_blocks = -(-(2 * n_real) // MOE_BLOCK) + N_EXPERTS
    dest1, dest2, block_e, n_used, block_grp, block_nxt = _routing_tables(il, cnt, n_blocks, tok_tm)
    xb = _dispatch(dest1, dest2, u2, n_blocks * MOE_BLOCK + 2 * tok_tm, tok_tm)
    yb = _moe(block_e, n_used, block_grp, block_nxt, xb, expert_w_gate, expert_w_up, expert_w_down)
    return h1, ic, dest1, dest2, yb


def kernel(x, meta_tokens, norm1_w, w_in, rwkv_mu, rwkv_w0, rwkv_w2, rwkv_a0, rwkv_a2, rwkv_g2, rwkv_k_k, rwkv_k_a, rwkv_r_k, rwkv_ln_w, rwkv_ln_b, q_norm_w, k_norm_w, lambda_q1, lambda_k1, lambda_q2, lambda_k2, diff_subln_w, w_branch_rwkv, w_branch_diff, w_out, norm2_w, router_group_w, router_group_b, router_expert_w, router_expert_b, expert_w_gate, expert_w_up, expert_w_down):
    b, seq, d = x.shape
    depth = norm1_w.shape[0]
    assert depth == 1, "the combine step emits the final output; deeper stacks need an intermediate form"
    lp = FRONT_PAD + N_META + seq
    meta = jnp.broadcast_to(meta_tokens[None].astype(x.dtype), (b, N_META, d))
    h0 = jnp.concatenate([jnp.zeros((b, FRONT_PAD, d), x.dtype), meta, x], axis=1).reshape(b * lp, d)
    proj_tm = 2048 if (b * lp) % 2048 == 0 else 128
    tok_tm = 512 if (b * lp) % 512 == 0 else 128
    l = 0
    h1, ic, dest1, dest2, yb = _layer(
        h0, lp, l, norm1_w[l], w_in[l], rwkv_mu[l], rwkv_w0[l], rwkv_w2[l], rwkv_a0[l], rwkv_a2[l],
        rwkv_g2[l], rwkv_k_k[l], rwkv_k_a[l], rwkv_r_k[l], rwkv_ln_w[l], rwkv_ln_b[l], q_norm_w[l],
        k_norm_w[l], lambda_q1[l], lambda_k1[l], lambda_q2[l], lambda_k2[l], diff_subln_w[l],
        w_branch_rwkv[l], w_branch_diff[l], w_out[l], norm2_w[l], router_group_w[l], router_group_b[l],
        router_expert_w[l], router_expert_b[l], expert_w_gate[l], expert_w_up[l], expert_w_down[l],
        proj_tm, tok_tm)
    return _combine(dest1, dest2, h1, ic, yb, b, lp, OUT_TILE)
```

```python
import functools
import math

import jax
import jax.numpy as jnp
from jax import lax
from jax.experimental import pallas as pl
from jax.experimental.pallas import tpu as pltpu

F32 = jnp.float32
BF16 = jnp.bfloat16

D_MODEL = 1024
N_META = 16
NORM_EPS = 1e-6
RWKV_HEADS = 16
RWKV_HEAD_DIM = 64
RWKV_WIDTH = RWKV_HEADS * RWKV_HEAD_DIM
DECAY_LORA = 64
AAA_LORA = 64
GATE_LORA = 128
LORA_COLS = DECAY_LORA + AAA_LORA + GATE_LORA
RWKV_GN_EPS = 64e-5
DIFF_HEADS = 8
DIFF_HEAD_DIM = 64
DIFF_V_DIM = 2 * DIFF_HEAD_DIM
DIFF_QK_WIDTH = DIFF_HEADS * 2 * DIFF_HEAD_DIM
DIFF_V_WIDTH = DIFF_HEADS * DIFF_V_DIM
ROPE_THETA = 500000.0
ROPE_DIM = DIFF_HEAD_DIM // 4
N_GROUPS = 4
EXPERTS_PER_GROUP = 8
N_EXPERTS = N_GROUPS * EXPERTS_PER_GROUP
EXPERT_FF = 512

LANES = 128
ATTN_BLOCK = 256
ATTN_KEY_BLOCK = 512
ATTN_HEADS_PER_STEP = 4
FRONT_PAD = ATTN_BLOCK - N_META
RWKV_CHUNK = 64
RWKV_SUBCHUNKS = 3
RWKV_PREP_COLS = 256
RWKV_HEAD_GROUP = 2
OUT_TILE = 128
MOE_BLOCK = 256
NEG_BIG = -1e30
VMEM_LIMIT = 48 * 1024 * 1024

COL_GATE = 0
COL_DIFF = 2 * D_MODEL
COL_RWKV = COL_DIFF + 2 * DIFF_QK_WIDTH + DIFF_V_WIDTH
COL_LORA = COL_RWKV + 3 * RWKV_WIDTH
IN_COLS = COL_LORA + LORA_COLS


def _dot(a, b):
    return jnp.dot(a, b, preferred_element_type=F32)


def _dot_nt(a, b):
    return lax.dot_general(a, b, (((1,), (1,)), ((), ())), preferred_element_type=F32)


def _split2(x):
    hi = x.astype(BF16)
    lo = (x - hi.astype(F32)).astype(BF16)
    return hi, lo


def _mm3(a, b_ref):
    ah, al = _split2(a)
    return _dot(ah, b_ref[0]) + _dot(ah, b_ref[1]) + _dot(al, b_ref[0])


def _hi_lo(w):
    hi = w.astype(BF16)
    return jnp.stack([hi, (w - hi.astype(F32)).astype(BF16)])


def _sigmoid(x):
    return 1.0 / (1.0 + jnp.exp(-x))


def _head_ones(width=LANES, head=RWKV_HEAD_DIM):
    r = lax.broadcasted_iota(jnp.int32, (width, width), 0) // head
    c = lax.broadcasted_iota(jnp.int32, (width, width), 1) // head
    return jnp.where(r == c, 1.0, 0.0).astype(BF16)


def _seg_sum(x, ones_bd):
    hi, lo = _split2(x)
    return _dot(hi, ones_bd) + _dot(lo, ones_bd)


def _seg_sum_wide(x, ones_bd):
    rows, width = x.shape
    n = width // LANES
    xs = jnp.concatenate([x[:, i * LANES:(i + 1) * LANES] for i in range(n)], axis=0)
    ys = _seg_sum(xs, ones_bd)
    return jnp.concatenate([ys[i * rows:(i + 1) * rows] for i in range(n)], axis=1)


def _to_token_tiles(ref, x):
    n, d = x.shape
    k = d // LANES
    for s in range(k):
        ref[pl.ds(s, n, stride=k), :] = x[:, s * LANES:(s + 1) * LANES]


def _from_token_tiles(ref, n):
    k = ref.shape[0] // n
    return jnp.concatenate([ref[pl.ds(s, n, stride=k), :] for s in range(k)], axis=1)


def _proj_kernel(h_ref, nw_ref, w_ref, o_ref, u_ref):
    @pl.when(pl.program_id(1) == 0)
    def _():
        x = h_ref[...]
        ms = jnp.mean(x * x, axis=-1, keepdims=True)
        u_ref[...] = (x * lax.rsqrt(ms + NORM_EPS) * nw_ref[...]).astype(BF16)

    o_ref[...] = _dot(u_ref[...], w_ref[...]).astype(o_ref.dtype)


def _proj(h, norm_w, w_bf16, tm, tn):
    t, d = h.shape
    n = w_bf16.shape[1]
    return pl.pallas_call(
        _proj_kernel,
        out_shape=jax.ShapeDtypeStruct((t, n), BF16),
        grid=(t // tm, n // tn),
        in_specs=[
            pl.BlockSpec((tm, d), lambda i, j: (i, 0)),
            pl.BlockSpec((1, d), lambda i, j: (0, 0)),
            pl.BlockSpec((d, tn), lambda i, j: (0, j)),
        ],
        out_specs=pl.BlockSpec((tm, tn), lambda i, j: (i, j)),
        scratch_shapes=[pltpu.VMEM((tm, d), BF16)],
        compiler_params=pltpu.CompilerParams(
            dimension_semantics=("parallel", "arbitrary"), vmem_limit_bytes=VMEM_LIMIT),
        name="norm_proj",
    )(h, norm_w.reshape(1, d), w_bf16)


def _rwkv_kernel(*refs):
    o_ref = refs[16]
    xr, xk, xv, xl, s_ref = refs[17:22]
    c = pl.program_id(1)
    n_skip = FRONT_PAD // o_ref.shape[1]

    @pl.when(c == 0)
    def _init():
        for xs in (xr, xk, xv, xl):
            xs[0:8, :] = jnp.zeros((8, xs.shape[1]), F32)
        s_ref[...] = jnp.zeros(s_ref.shape, F32)

    @pl.when(c < n_skip)
    def _pad():
        o_ref[...] = jnp.zeros(o_ref.shape, o_ref.dtype)

    @pl.when(c >= n_skip)
    def _chunk():
        _rwkv_chunk(*refs)


def _rwkv_chunk(r_ref, k_ref, v_ref, lo_ref, mu_ref, mul_ref, w0_ref, w2_ref, a0_ref, a2_ref, g2_ref,
                kkw_ref, kaw_ref, rkw_ref, lnw_ref, lnb_ref, o_ref,
                xr, xk, xv, xl, s_ref, kt_s, bt_s, kn_s, rt_s, v_s, y_s, gc_s, bon_s, g_s):
    R = r_ref.shape[1]
    C = RWKV_CHUNK
    n_sub = R // C
    W = RWKV_WIDTH

    def shift_mix(in_ref, xs, mu, cs):
        x = in_ref[0, :, cs].astype(F32)
        xs[8:8 + R, cs] = x
        prev = xs[7:7 + R, cs]
        xs[7:8, cs] = x[R - 1:R, :]
        return x + (prev - x) * mu

    lo = shift_mix(lo_ref, xl, mul_ref[...], slice(0, LORA_COLS))
    xw = _split2(jnp.tanh(lo[:, 0:DECAY_LORA]))
    xa = _split2(lo[:, DECAY_LORA:DECAY_LORA + AAA_LORA])
    xg = _split2(_sigmoid(lo[:, DECAY_LORA + AAA_LORA:LORA_COLS]))

    def mm3(a_hi_lo, b_ref, cs):
        ah, al = a_hi_lo
        return _dot(ah, b_ref[0, :, cs]) + _dot(ah, b_ref[1, :, cs]) + _dot(al, b_ref[0, :, cs])

    ones_bd = _head_ones()
    ti = lax.broadcasted_iota(jnp.int32, (R, R), 0)
    tj = lax.broadcasted_iota(jnp.int32, (R, R), 1)
    ltri = jnp.where((ti >= tj) & (ti // C == tj // C), 1.0, 0.0).astype(BF16)

    for c0 in range(0, W, RWKV_PREP_COLS):
        cs = slice(c0, c0 + RWKV_PREP_COLS)
        col = lambda ref, off=0: ref[:, off + c0:off + c0 + RWKV_PREP_COLS]
        r = shift_mix(r_ref, xr, col(mu_ref), cs)
        k = shift_mix(k_ref, xk, col(mu_ref, W), cs)
        v = shift_mix(v_ref, xv, col(mu_ref, 2 * W), cs)

        lw = (-math.exp(-0.5)) * _sigmoid(col(w0_ref) + mm3(xw, w2_ref, cs))
        a = _sigmoid(col(a0_ref) + mm3(xa, a2_ref, cs))
        g_s[:, cs] = mm3(xg, g2_ref, cs)

        kk = k * col(kkw_ref)
        kkn = kk * lax.rsqrt(jnp.maximum(_seg_sum_wide(kk * kk, ones_bd), 1e-24))
        k2 = k * (1.0 + (a - 1.0) * col(kaw_ref))
        bon_s[:, cs] = _seg_sum_wide(r * k2 * col(rkw_ref), ones_bd) * v

        l1 = lw.astype(BF16)
        rem = lw - l1.astype(F32)
        l2 = rem.astype(BF16)
        l3 = (rem - l2.astype(F32)).astype(BF16)
        cum = _dot(ltri, l1) + _dot(ltri, l2) + _dot(ltri, l3)
        e_pos = jnp.exp(cum)
        e_neg = jnp.exp(-cum)
        kt_s[:, cs] = kkn * jnp.exp(cum - lw)
        bt_s[:, cs] = kkn * a * e_neg
        kn_s[:, cs] = k2 * e_neg
        rt_s[:, cs] = r * e_pos
        v_s[:, cs] = v
        for s in range(n_sub):
            gc_s[s:s + 1, cs] = jnp.exp(cum[(s + 1) * C - 1:(s + 1) * C, :])

    G = RWKV_HEAD_GROUP
    GL = G * RWKV_HEAD_DIM
    GC = G * C
    n_groups = W // GL
    lane_head = lax.broadcasted_iota(jnp.int32, (1, GL), 1) // RWKV_HEAD_DIM
    head_mask = [jnp.where(lane_head == h, 1.0, 0.0).astype(BF16) for h in range(G)]
    trow = lax.broadcasted_iota(jnp.int32, (C, GC), 0)
    tcol = lax.broadcasted_iota(jnp.int32, (C, GC), 1) % C
    strict = trow > tcol
    incl = trow >= tcol
    eye = jnp.where(trow == tcol, 1.0, 0.0)
    br = lax.broadcasted_iota(jnp.int32, (GL, GL), 0) // RWKV_HEAD_DIM
    bc = lax.broadcasted_iota(jnp.int32, (GL, GL), 1) // RWKV_HEAD_DIM
    bdmask = jnp.where(br == bc, 1.0, 0.0)

    def stack(y):
        yb = y.astype(BF16)
        return jnp.concatenate([yb * m for m in head_mask], axis=0)

    groups = range(n_groups)
    units = [(s, p) for s in range(n_sub) for p in groups]
    win = {(s, p): (slice(s * C, (s + 1) * C), slice(p * GL, (p + 1) * GL)) for s, p in units}
    def prepare(units):
        kt = {q: kt_s[win[q]] for q in units}
        bt = {q: bt_s[win[q]] for q in units}
        kn = {q: kn_s[win[q]] for q in units}
        rt = {q: rt_s[win[q]] for q in units}
        vv = {q: v_s[win[q]] for q in units}
        gc = {q: gc_s[q[0]:q[0] + 1, win[q][1]] for q in units}
        kr = {q: jnp.concatenate([kt[q], rt[q]], axis=0).astype(BF16) for q in units}
        m_all = {q: _dot_nt(kr[q], jnp.concatenate([stack(bt[q]), stack(kn[q])], axis=0))
                 for q in units}
        p_mat = {q: jnp.where(incl, m_all[q][C:2 * C, 0:GC], 0.0).astype(BF16) for q in units}
        bq_mat = {q: jnp.concatenate([jnp.where(strict, m_all[q][0:C, GC:2 * GC], 0.0),
                                      jnp.where(incl, m_all[q][C:2 * C, GC:2 * GC], 0.0)], axis=0).astype(BF16)
                  for q in units}
        v_term = {q: _dot(bq_mat[q], stack(vv[q])) for q in units}

        pw = {q: -jnp.where(strict, m_all[q][0:C, 0:GC], 0.0) for q in units}
        t_inv = {q: eye + pw[q] for q in units}
        pw = {q: _dot(pw[q].astype(BF16), stack(pw[q])) for q in units}
        for _ in range(int(math.log2(C)) - 2):
            both = {q: _dot(jnp.concatenate([t_inv[q], pw[q]], axis=0).astype(BF16), stack(pw[q]))
                    for q in units}
            t_inv = {q: t_inv[q] + both[q][0:C] for q in units}
            pw = {q: both[q][C:2 * C] for q in units}
        t_inv = {q: (t_inv[q] + _dot(t_inv[q].astype(BF16), stack(pw[q]))).astype(BF16) for q in units}
        xc = {q: jnp.concatenate([bt[q] * gc[q], kn[q] * gc[q]], axis=0).astype(BF16) for q in units}
        return kr, p_mat, v_term, t_inv, xc, vv, gc

    kr, p_mat, v_term, t_inv, xc, vv, gc = prepare(units)

    state = [s_ref[p] for p in groups]
    for s in range(n_sub):
        qs = [(s, p) for p in groups]
        ks = [_dot_nt(kr[q], state[q[1]].astype(BF16)) for q in qs]
        u = [-_dot(t_inv[q], stack(ks[q[1]][0:C] + v_term[q][0:C])) for q in qs]
        for q in qs:
            y_s[win[q]] = ks[q[1]][C:2 * C] + v_term[q][C:2 * C] + _dot(p_mat[q], stack(u[q[1]]))
        for q in qs:
            uc = jnp.concatenate([u[q[1]], vv[q]], axis=0)
            state[q[1]] = state[q[1]] * gc[q] + bdmask * _dot(jnp.transpose(uc).astype(BF16), xc[q])
    for p in groups:
        s_ref[p] = state[p]

    inv_n = 1.0 / RWKV_HEAD_DIM
    for c0 in range(0, W, RWKV_PREP_COLS):
        cs = slice(c0, c0 + RWKV_PREP_COLS)
        y = y_s[:, cs]
        mean = _seg_sum_wide(y, ones_bd) * inv_n
        dlt = y - mean
        var = _seg_sum_wide(dlt * dlt, ones_bd) * inv_n
        yn = dlt * lax.rsqrt(var + RWKV_GN_EPS) * lnw_ref[:, cs] + lnb_ref[:, cs]
        o_ref[0, :, cs] = ((yn + bon_s[:, cs]) * g_s[:, cs]).astype(o_ref.dtype)


def _rwkv(proj3, mu_rkv, mu_lo, w0, w2, a0, a2, g2, k_k, k_a, r_k, ln_w, ln_b):
    b, lp, _ = proj3.shape
    C = RWKV_CHUNK * RWKV_SUBCHUNKS
    W = RWKV_WIDTH
    cb = COL_RWKV // W
    lb = COL_LORA // LORA_COLS
    row = lambda x: x.reshape(1, -1)
    full = lambda shape: pl.BlockSpec(shape, lambda i, c: (0,) * len(shape))
    wide = pltpu.VMEM((C, W), F32)
    gl = RWKV_HEAD_GROUP * RWKV_HEAD_DIM
    return pl.pallas_call(
        _rwkv_kernel,
        out_shape=jax.ShapeDtypeStruct((b, lp, W), BF16),
        grid=(b, lp // C),
        in_specs=[
            pl.BlockSpec((1, C, W), lambda i, c: (i, c, cb)),
            pl.BlockSpec((1, C, W), lambda i, c: (i, c, cb + 1)),
            pl.BlockSpec((1, C, W), lambda i, c: (i, c, cb + 2)),
            pl.BlockSpec((1, C, LORA_COLS), lambda i, c: (i, c, lb)),
            full((1, 3 * W)), full((1, LORA_COLS)),
            full((1, W)), full((2, DECAY_LORA, W)), full((1, W)), full((2, AAA_LORA, W)), full((2, GATE_LORA, W)),
            full((1, W)), full((1, W)), full((1, W)), full((1, W)), full((1, W)),
        ],
        out_specs=pl.BlockSpec((1, C, W), lambda i, c: (i, c, 0)),
        scratch_shapes=[
            pltpu.VMEM((C + 8, W), F32), pltpu.VMEM((C + 8, W), F32), pltpu.VMEM((C + 8, W), F32),
            pltpu.VMEM((C + 8, LORA_COLS), F32),
            pltpu.VMEM((W // gl, gl, gl), F32),
            wide, wide, wide, wide, wide, wide, pltpu.VMEM((8, W), F32), wide, wide,
        ],
        compiler_params=pltpu.CompilerParams(
            dimension_semantics=("parallel", "arbitrary"), vmem_limit_bytes=VMEM_LIMIT),
        name="rwkv7_time_mix",
    )(proj3, proj3, proj3, proj3, row(mu_rkv), row(mu_lo), row(w0), _hi_lo(w2), row(a0), _hi_lo(a2), _hi_lo(g2),
      row(k_k), row(k_a), row(r_k), row(ln_w), row(ln_b))


def _attn_kernel(q_ref, k_ref, v_ref, cos_ref, s1_ref, s2_ref, qw_ref, kw_ref, lam_ref, sw_ref, o_ref,
                 kp_s, vp_s, m_s, acc_s, *, tq, tk, lambda_init):
    qi = pl.program_id(2)
    lp = k_ref.shape[1]
    n_real = lp - FRONT_PAD
    lk = kp_s.shape[0]
    nh = k_ref.shape[2] // LANES
    vw = 2 * LANES
    heads = range(nh)
    hs = [slice(h * LANES, (h + 1) * LANES) for h in heads]
    ones_bd = _head_ones(LANES, DIFF_HEAD_DIM)
    shift = ROPE_DIM // 2

    def norm_rope(x, w, rows):
        ms = _seg_sum(x * x, ones_bd) * (1.0 / DIFF_HEAD_DIM)
        xn = x * lax.rsqrt(ms + NORM_EPS) * w
        return (xn * cos_ref[rows, :] + pltpu.roll(xn, shift, 1) * s1_ref[rows, :]
                + pltpu.roll(xn, LANES - shift, 1) * s2_ref[rows, :])

    @pl.when(qi == 0)
    def _prep():
        def put(dst, n):
            src = pl.ds(FRONT_PAD + dst, n)
            rows = pl.ds(dst, n)
            for h in heads:
                kp_s[rows, hs[h]] = norm_rope(k_ref[0, src, hs[h]].astype(F32), kw_ref[...], src).astype(BF16)
                vp_s[rows, h * vw:h * vw + LANES] = v_ref[0, src, hs[h]].astype(BF16)
                vp_s[rows, h * vw + LANES:(h + 1) * vw] = jnp.ones((n, LANES), BF16)

        def body(i, carry):
            put(pl.multiple_of(i * LANES, LANES), LANES)
            return carry
        lax.fori_loop(0, n_real // LANES, body, 0, unroll=2)
        tail = n_real % LANES
        if tail:
            put(n_real - tail, tail)
        if lk > n_real:
            kp_s[n_real:lk, :] = jnp.zeros((lk - n_real, nh * LANES), BF16)
            vp_s[n_real:lk, :] = jnp.zeros((lk - n_real, nh * vw), BF16)

    lane = lax.broadcasted_iota(jnp.int32, (1, LANES), 1)
    m0 = jnp.where(lane < DIFF_HEAD_DIM, 1.0, 0.0)
    m1 = 1.0 - m0
    rows_q = pl.ds(pl.multiple_of(qi * tq, tq), tq)
    qs = []
    for h in heads:
        qn = norm_rope(q_ref[0, :, hs[h]].astype(F32), qw_ref[...], rows_q) * (DIFF_HEAD_DIM ** -0.5)
        qs.append(jnp.concatenate([qn * m0, qn * m1], axis=0).astype(BF16))
    first_row = qi * tq - FRONT_PAD
    row = first_row + lax.broadcasted_iota(jnp.int32, (2 * tq, LANES), 0) % tq
    col0 = lax.broadcasted_iota(jnp.int32, (2 * tq, LANES), 1)

    def step(j, width, causal, init=False):
        start = j * tk
        if not isinstance(j, int):
            start = pl.multiple_of(start, tk)
        ks = pl.ds(start, width)
        n_sub = width // LANES
        s = [_dot_nt(qs[h], kp_s[ks, hs[h]]) for h in heads]
        sub = [[s[h][:, c * LANES:(c + 1) * LANES] for c in range(n_sub)] for h in heads]
        if causal:
            sub = [[jnp.where(start + c * LANES + col0 <= row, sub[h][c], NEG_BIG) for c in range(n_sub)]
                   for h in heads]
        m_new = []
        for h in heads:
            mx = functools.reduce(jnp.maximum, sub[h])
            mx = jnp.broadcast_to(jnp.max(mx, axis=1, keepdims=True), mx.shape)
            m_new.append(mx if init else jnp.maximum(m_s[h], mx))
        p = [jnp.concatenate([jnp.exp(sub[h][c] - m_new[h]) for c in range(n_sub)], axis=1).astype(BF16)
             for h in heads]
        pv = [_dot(p[h], vp_s[ks, h * vw:(h + 1) * vw]) for h in heads]
        for h in heads:
            if init:
                acc_s[h] = pv[h]
            else:
                alpha = jnp.exp(m_s[h] - m_new[h])
                acc_s[h] = jnp.concatenate([alpha, alpha], axis=1) * acc_s[h] + pv[h]
            m_s[h] = m_new[h]

    assert FRONT_PAD + N_META == tq and tk % tq == 0 and N_META <= LANES
    n_full = jnp.maximum(first_row + 1, 0) // tk
    last = jnp.maximum(first_row + tq - 1, 0) // tk
    step(n_full, tk, True, init=True)

    def mid(j, carry):
        step(j, tk, False)
        return carry
    lax.fori_loop(0, n_full, mid, 0)

    @pl.when(last > n_full)
    def _diag():
        step(last, LANES, True)

    lam = (jnp.exp(jnp.sum(lam_ref[0:1, :] * lam_ref[1:2, :], axis=1, keepdims=True))
           - jnp.exp(jnp.sum(lam_ref[2:3, :] * lam_ref[3:4, :], axis=1, keepdims=True)) + lambda_init)
    for h in heads:
        acc = acc_s[h]
        o = acc[:, 0:LANES] / acc[:, LANES:vw]
        od = o[0:tq] - lam * o[tq:2 * tq]
        ms = jnp.mean(od * od, axis=1, keepdims=True)
        o_ref[0, :, hs[h]] = (od * lax.rsqrt(ms + NORM_EPS) * sw_ref[...]
                              * (1.0 - lambda_init)).astype(o_ref.dtype)


def _rope_tables(lp):
    half = ROPE_DIM // 2
    inv_freq = jnp.exp(-math.log(ROPE_THETA) * jnp.arange(half, dtype=F32) * 2.0 / ROPE_DIM)
    pos = (jnp.arange(lp) - FRONT_PAD).astype(F32)
    ang = pos[:, None] * inv_freq[None, :]
    cos, sin = jnp.cos(ang), jnp.sin(ang)
    one = jnp.ones((lp, DIFF_HEAD_DIM - ROPE_DIM), F32)
    zero = jnp.zeros((lp, DIFF_HEAD_DIM - ROPE_DIM), F32)
    zh = jnp.zeros((lp, half), F32)
    c = jnp.concatenate([cos, cos, one], axis=1)
    s1 = jnp.concatenate([zh, sin, zero], axis=1)
    s2 = jnp.concatenate([-sin, zh, zero], axis=1)
    dup = lambda t: jnp.concatenate([t, t], axis=1)
    return dup(c), dup(s1), dup(s2)


def _attention(proj3, q_norm_w, k_norm_w, lam4, subln_w, lambda_init):
    b, lp, _ = proj3.shape
    tq, tk, nh = ATTN_BLOCK, ATTN_KEY_BLOCK, ATTN_HEADS_PER_STEP
    lk = -(-(lp - FRONT_PAD) // tk) * tk
    hw = nh * LANES
    qb = COL_DIFF // hw
    kb = qb + DIFF_QK_WIDTH // hw
    vb = kb + DIFF_QK_WIDTH // hw
    cos, s1, s2 = _rope_tables(lp)
    dup = lambda w: jnp.concatenate([w, w]).reshape(1, LANES)
    full = lambda shape: pl.BlockSpec(shape, lambda i, h, q: (0,) * len(shape))
    return pl.pallas_call(
        functools.partial(_attn_kernel, tq=tq, tk=tk, lambda_init=lambda_init),
        out_shape=jax.ShapeDtypeStruct((b, lp, DIFF_V_WIDTH), BF16),
        grid=(b, DIFF_HEADS // nh, lp // tq),
        in_specs=[
            pl.BlockSpec((1, tq, hw), lambda i, h, q: (i, q, qb + h)),
            pl.BlockSpec((1, lp, hw), lambda i, h, q: (i, 0, kb + h)),
            pl.BlockSpec((1, lp, hw), lambda i, h, q: (i, 0, vb + h)),
            full((lp, LANES)), full((lp, LANES)), full((lp, LANES)),
            full((1, LANES)), full((1, LANES)), full((4, DIFF_HEAD_DIM)), full((1, LANES)),
        ],
        out_specs=pl.BlockSpec((1, tq, hw), lambda i, h, q: (i, q, h)),
        scratch_shapes=[pltpu.VMEM((lk, hw), BF16), pltpu.VMEM((lk, 2 * hw), BF16),
                        pltpu.VMEM((nh, 2 * tq, LANES), F32), pltpu.VMEM((nh, 2 * tq, 2 * LANES), F32)],
        compiler_params=pltpu.CompilerParams(
            dimension_semantics=("parallel", "parallel", "arbitrary"), vmem_limit_bytes=VMEM_LIMIT),
        name="diff_attention",
    )(proj3, proj3, proj3, cos, s1, s2, dup(q_norm_w), dup(k_norm_w), lam4, subln_w.reshape(1, LANES))


def _merge_kernel(rw_ref, da_ref, g1_ref, g2_ref, h_ref, wbr_ref, wbd_ref, wo_ref, n2_ref, wr_ref, br_ref,
                  h1_ref, u2_ref, il_ref, ic_ref, cnt_ref, base_s, *, lp):
    i = pl.program_id(0)
    tm = rw_ref.shape[0]

    @pl.when(i == 0)
    def _():
        base_s[...] = jnp.zeros(base_s.shape, F32)

    y1 = _dot(rw_ref[...], wbr_ref[...])
    y2 = _dot(da_ref[...], wbd_ref[...])
    merged = _sigmoid(g1_ref[...].astype(F32)) * y1 + _sigmoid(g2_ref[...].astype(F32)) * y2
    h1 = h_ref[...] + _dot(merged.astype(BF16), wo_ref[...])
    h1_ref[...] = h1
    u2 = h1 * lax.rsqrt(jnp.mean(h1 * h1, axis=-1, keepdims=True) + NORM_EPS) * n2_ref[...]
    _to_token_tiles(u2_ref, u2)

    uh, ul = _split2(u2)
    wh, wl = _split2(wr_ref[...])
    lt = _dot_nt(wh, uh) + _dot_nt(wh, ul) + _dot_nt(wl, uh) + br_ref[...]

    gi8 = lax.broadcasted_iota(jnp.int32, (8, tm), 0)
    lg = lt[0:8]
    ge = jnp.exp(lg - jnp.max(lg, axis=0, keepdims=True))
    gp = ge / jnp.sum(ge, axis=0, keepdims=True)
    gv = jnp.max(gp, axis=0, keepdims=True)
    gidx = jnp.min(jnp.where(gp == gv, gi8, N_EXPERTS), axis=0, keepdims=True)

    ei = lax.broadcasted_iota(jnp.int32, (N_EXPERTS, tm), 0)
    sel = (ei // EXPERTS_PER_GROUP) == gidx
    le = jnp.where(sel, lt[8:8 + N_EXPERTS], NEG_BIG)
    ee = jnp.where(sel, jnp.exp(le - jnp.max(le, axis=0, keepdims=True)), 0.0)
    ep = jnp.where(sel, ee / jnp.sum(ee, axis=0, keepdims=True), -1.0)
    v1 = jnp.max(ep, axis=0, keepdims=True)
    i1 = jnp.min(jnp.where(ep == v1, ei, N_EXPERTS), axis=0, keepdims=True)
    ep2 = jnp.where(ei == i1, -1.0, ep)
    v2 = jnp.max(ep2, axis=0, keepdims=True)
    i2 = jnp.min(jnp.where(ep2 == v2, ei, N_EXPERTS), axis=0, keepdims=True)
    den = v1 + v2
    gate1 = gv * v1 / den
    gate2 = gv * v2 / den

    tok = (i * tm + lax.broadcasted_iota(jnp.int32, (1, tm), 1)).astype(F32)
    pos = tok - jnp.floor((tok + 0.5) / lp) * lp
    valid = pos > (FRONT_PAD - 0.5)

    oh1 = jnp.where((ei == i1) & valid, 1.0, 0.0)
    oh2 = jnp.where((ei == i2) & valid, 1.0, 0.0)
    oh = oh1 + oh2
    ur = lax.broadcasted_iota(jnp.int32, (tm, tm), 0)
    uc = lax.broadcasted_iota(jnp.int32, (tm, tm), 1)
    before = jnp.where(ur < uc, 1.0, 0.0).astype(BF16)
    tot = base_s[:, 0:1] + _dot(oh.astype(BF16), before)
    rank1 = jnp.sum(oh1 * tot, axis=0, keepdims=True)
    rank2 = jnp.sum(oh2 * tot, axis=0, keepdims=True)
    base_s[...] = base_s[...] + jnp.sum(oh, axis=1, keepdims=True)
    cnt_ref[...] = base_s[...]

    il = jnp.where(gi8 == 0, i1, jnp.where(gi8 == 1, i2, jnp.where(
        gi8 == 2, rank1.astype(jnp.int32), jnp.where(gi8 == 3, rank2.astype(jnp.int32), jnp.where(
            gi8 == 4, valid.astype(jnp.int32), 0)))))
    il_ref[...] = il
    ri = lax.broadcasted_iota(jnp.int32, (LANES, tm), 0)
    ic = jnp.where(ri == 0, gate1, jnp.where(ri == 1, gate2, 0.0))
    ic_ref[...] = jnp.transpose(ic)


def _merge(rw, da, proj, h0, wbr, wbd, wo, norm2_w, wr, br, lp, tm):
    t, d = h0.shape
    gb = COL_GATE // d
    full = lambda shape: pl.BlockSpec(shape, lambda i: (0,) * len(shape))
    tile = lambda c: pl.BlockSpec((tm, d), lambda i: (i, c))
    return pl.pallas_call(
        functools.partial(_merge_kernel, lp=lp),
        out_shape=(
            jax.ShapeDtypeStruct((t, d), F32),
            jax.ShapeDtypeStruct((t * (d // LANES), LANES), F32),
            jax.ShapeDtypeStruct((8, t), jnp.int32),
            jax.ShapeDtypeStruct((t, LANES), F32),
            jax.ShapeDtypeStruct((N_EXPERTS, LANES), F32),
        ),
        grid=(t // tm,),
        in_specs=[tile(0), tile(0), tile(gb), tile(gb + 1), tile(0),
                  full((d, d)), full((d, d)), full((d, d)), full((1, d)), full((LANES, d)), full((LANES, 1))],
        out_specs=(
            tile(0), pl.BlockSpec((tm * (d // LANES), LANES), lambda i: (i, 0)),
            pl.BlockSpec((8, tm), lambda i: (0, i)),
            pl.BlockSpec((tm, LANES), lambda i: (i, 0)),
            full((N_EXPERTS, LANES)),
        ),
        scratch_shapes=[pltpu.VMEM((N_EXPERTS, LANES), F32)],
        compiler_params=pltpu.CompilerParams(
            dimension_semantics=("arbitrary",), vmem_limit_bytes=VMEM_LIMIT),
        name="merge_router",
    )(rw, da, proj, proj, h0, wbr, wbd, wo, norm2_w.reshape(1, d), wr, br)


def _dispatch_kernel(d1_ref, d2_ref, u_ref, xin_hbm, xb_hbm, sem):
    del xin_hbm
    tm = d1_ref.shape[2]
    k = u_ref.shape[0] // tm

    def start(r, carry):
        for prio, d_ref in enumerate((d1_ref, d2_ref)):
            pltpu.make_async_copy(u_ref.at[pl.ds(pl.multiple_of(r * k, k), k)],
                                  xb_hbm.at[pl.ds(pl.multiple_of(d_ref[0, 0, r] * k, k), k)],
                                  sem).start(priority=prio)
        return carry

    lax.fori_loop(0, tm, start, 0, unroll=8)
    for _ in range(2):
        pltpu.make_async_copy(u_ref, xb_hbm.at[pl.ds(0, tm * k)], sem).wait()


def _dispatch(dest1, dest2, u2, cap, tm):
    k = D_MODEL // LANES
    t = u2.shape[0] // k
    nt = t // tm
    smem = lambda: pl.BlockSpec((1, 1, tm), lambda i: (i, 0, 0), memory_space=pltpu.SMEM)
    return pl.pallas_call(
        _dispatch_kernel,
        out_shape=jax.ShapeDtypeStruct((cap * k, LANES), F32),
        grid=(nt,),
        in_specs=[smem(), smem(), pl.BlockSpec((tm * k, LANES), lambda i: (i, 0)),
                  pl.BlockSpec(memory_space=pl.ANY)],
        out_specs=pl.BlockSpec(memory_space=pl.ANY),
        scratch_shapes=[pltpu.SemaphoreType.DMA(())],
        input_output_aliases={3: 0},
        compiler_params=pltpu.CompilerParams(
            dimension_semantics=("arbitrary",), vmem_limit_bytes=VMEM_LIMIT),
        name="moe_dispatch",
    )(dest1.reshape(nt, 1, tm), dest2.reshape(nt, 1, tm), u2, jnp.zeros((cap * k, LANES), F32))


def _moe_kernel(be_ref, nb_ref, grp_ref, nxt_ref, x_ref, wg_hbm, wu_hbm, wd_hbm, o_ref,
                wg_f, wu_f, wd_f, wg_s, wu_s, wd_s, sem):
    i = pl.program_id(0)
    bm = MOE_BLOCK
    used = i < nb_ref[0]
    slot = lax.rem(grp_ref[i], 2)

    def fetch(expert, sl):
        return [pltpu.make_async_copy(w_hbm.at[expert], w_f.at[sl], sem.at[sl])
                for w_hbm, w_f in ((wg_hbm, wg_f), (wu_hbm, wu_f), (wd_hbm, wd_f))]

    @pl.when(used & ((i == 0) | (grp_ref[i] != grp_ref[jnp.maximum(i - 1, 0)])))
    def _():
        @pl.when(i == 0)
        def _():
            for cp in fetch(be_ref[0], 0):
                cp.start()

        @pl.when(nxt_ref[i] >= 0)
        def _():
            for cp in fetch(nxt_ref[i], 1 - slot):
                cp.start()

        for cp in fetch(be_ref[i], slot):
            cp.wait()
        wg_s[...] = wg_f[slot].astype(BF16)
        wu_s[...] = wu_f[slot].astype(BF16)
        wd_s[...] = wd_f[slot].astype(BF16)

    @pl.when(used)
    def _():
        x = _from_token_tiles(x_ref, bm).astype(BF16)
        hg = _dot(x, wg_s[...])
        hu = _dot(x, wu_s[...])
        hid = hg * _sigmoid(hg) * hu
        _to_token_tiles(o_ref, _dot(hid.astype(BF16), wd_s[...]))

    @pl.when(i >= nb_ref[0])
    def _():
        o_ref[...] = jnp.zeros(o_ref.shape, F32)


def _moe(block_e, n_used, block_grp, block_nxt, xb, wg, wu, wd):
    d = wg.shape[1]
    ff = wg.shape[2]
    bm = MOE_BLOCK
    k = d // LANES
    cap = block_e.shape[0] * bm
    hbm = pl.BlockSpec(memory_space=pl.ANY)
    return pl.pallas_call(
        _moe_kernel,
        out_shape=jax.ShapeDtypeStruct((cap * k, LANES), F32),
        grid_spec=pltpu.PrefetchScalarGridSpec(
            num_scalar_prefetch=4,
            grid=(cap // bm,),
            in_specs=[pl.BlockSpec((bm * k, LANES), lambda i, *_: (i, 0)), hbm, hbm, hbm],
            out_specs=pl.BlockSpec((bm * k, LANES), lambda i, *_: (i, 0)),
            scratch_shapes=[pltpu.VMEM((2, d, ff), F32), pltpu.VMEM((2, d, ff), F32), pltpu.VMEM((2, ff, d), F32),
                            pltpu.VMEM((d, ff), BF16), pltpu.VMEM((d, ff), BF16), pltpu.VMEM((ff, d), BF16),
                            pltpu.SemaphoreType.DMA((2,))],
        ),
        compiler_params=pltpu.CompilerParams(
            dimension_semantics=("arbitrary",), vmem_limit_bytes=VMEM_LIMIT),
        name="moe_experts",
    )(block_e, n_used, block_grp, block_nxt, xb, wg, wu, wd)


def _combine_kernel(d1_ref, d2_ref, n1_ref, n2_ref, h_ref, ic_ref, yb_hbm, o_ref, ga, gb, sem):
    tm = h_ref.shape[0]
    k = ga.shape[1] // tm
    n = pl.program_id(0) * pl.num_programs(1) + pl.program_id(1)
    total = pl.num_programs(0) * pl.num_programs(1)
    slot = lax.rem(n, 2)

    def issue(da_ref, db_ref, sl):
        def start(r, carry):
            for prio, (d_ref, buf) in enumerate(((da_ref, ga), (db_ref, gb))):
                pltpu.make_async_copy(yb_hbm.at[pl.ds(pl.multiple_of(d_ref[0, 0, r] * k, k), k)],
                                      buf.at[sl, pl.ds(pl.multiple_of(r * k, k), k)],
                                      sem.at[sl]).start(priority=prio)
            return carry
        lax.fori_loop(0, tm, start, 0, unroll=8)

    @pl.when(n == 0)
    def _first():
        issue(d1_ref, d2_ref, 0)

    @pl.when(n + 1 < total)
    def _next():
        issue(n1_ref, n2_ref, 1 - slot)

    for buf in (ga, gb):
        pltpu.make_async_copy(yb_hbm.at[pl.ds(0, tm * k)], buf.at[slot], sem.at[slot]).wait()
    ic = ic_ref[...]
    o_ref[0] = (h_ref[...] + ic[:, 0:1] * _from_token_tiles(ga.at[slot], tm)
                + ic[:, 1:2] * _from_token_tiles(gb.at[slot], tm))


def _combine(dest1, dest2, h1, ic, yb, b, lp, tm):
    t, d = h1.shape
    per = lp // tm
    first = (FRONT_PAD + N_META) // tm
    steps = per - first
    nt = t // tm
    tile = lambda i, j: i * per + j + first

    def next_tile(i, j):
        nxt = jnp.minimum(i * steps + j + 1, b * steps - 1)
        return tile(nxt // steps, nxt % steps)

    cur = lambda: pl.BlockSpec((1, 1, tm), lambda i, j: (tile(i, j), 0, 0), memory_space=pltpu.SMEM)
    nxt = lambda: pl.BlockSpec((1, 1, tm), lambda i, j: (next_tile(i, j), 0, 0), memory_space=pltpu.SMEM)
    d1 = dest1.reshape(nt, 1, tm)
    d2 = dest2.reshape(nt, 1, tm)
    return pl.pallas_call(
        _combine_kernel,
        out_shape=jax.ShapeDtypeStruct((b, lp - FRONT_PAD - N_META, d), F32),
        grid=(b, steps),
        in_specs=[
            cur(), cur(), nxt(), nxt(),
            pl.BlockSpec((tm, d), lambda i, j: (tile(i, j), 0)),
            pl.BlockSpec((tm, LANES), lambda i, j: (tile(i, j), 0)),
            pl.BlockSpec(memory_space=pl.ANY),
        ],
        out_specs=pl.BlockSpec((1, tm, d), lambda i, j: (i, j, 0)),
        scratch_shapes=[pltpu.VMEM((2, tm * (d // LANES), LANES), F32),
                        pltpu.VMEM((2, tm * (d // LANES), LANES), F32),
                        pltpu.SemaphoreType.DMA((2,))],
        compiler_params=pltpu.CompilerParams(
            dimension_semantics=("arbitrary", "arbitrary"), vmem_limit_bytes=VMEM_LIMIT),
        name="moe_combine",
    )(d1, d2, d1, d2, h1, ic, yb)


def _routing_tables(il, cnt, n_blocks, tm):
    bm = MOE_BLOCK
    counts = cnt[:, 0].astype(jnp.int32)
    padded = (counts + bm - 1) // bm * bm
    pad_end = jnp.cumsum(padded)
    pad_start = pad_end - padded
    valid = il[4] > 0
    spare = n_blocks * bm + jnp.arange(il.shape[1], dtype=jnp.int32) % tm
    eid = jnp.arange(N_EXPERTS, dtype=jnp.int32)
    first_slot = jnp.sum(jnp.where(il[0:2, :, None] == eid, pad_start, 0), axis=-1)
    dest1 = jnp.where(valid, first_slot[0] + il[2], spare).astype(jnp.int32)
    dest2 = jnp.where(valid, first_slot[1] + il[3], spare + tm).astype(jnp.int32)
    starts = jnp.arange(n_blocks, dtype=jnp.int32) * bm
    block_e = jnp.minimum(jnp.sum((pad_end[None, :] <= starts[:, None]).astype(jnp.int32), axis=1),
                          N_EXPERTS - 1)
    n_used = (pad_end[-1:] // bm).astype(jnp.int32)
    owns = counts > 0
    eid = jnp.arange(N_EXPERTS, dtype=jnp.int32)
    ordinal = jnp.cumsum(owns.astype(jnp.int32)) - 1
    later = (eid[None, :] > eid[:, None]) & owns[None, :]
    nxt = jnp.min(jnp.where(later, eid[None, :], N_EXPERTS), axis=1)
    nxt = jnp.where(nxt < N_EXPERTS, nxt, -1)
    return dest1, dest2, block_e, n_used, ordinal[block_e].astype(jnp.int32), nxt[block_e].astype(jnp.int32)


def _layer(h0, lp, l, norm1_w, w_in, rwkv_mu, rwkv_w0, rwkv_w2, rwkv_a0, rwkv_a2, rwkv_g2,
           rwkv_k_k, rwkv_k_a, rwkv_r_k, rwkv_ln_w, rwkv_ln_b, q_norm_w, k_norm_w,
           lambda_q1, lambda_k1, lambda_q2, lambda_k2, diff_subln_w, w_branch_rwkv, w_branch_diff,
           w_out, norm2_w, router_group_w, router_group_b, router_expert_w, router_expert_b,
           expert_w_gate, expert_w_up, expert_w_down, proj_tm, tok_tm):
    t, d = h0.shape
    b = t // lp
    lambda_init = 0.8 - 0.6 * math.exp(-0.3 * l)
    rw_cols = 3 * RWKV_WIDTH
    diff_cols = 2 * DIFF_QK_WIDTH + DIFF_V_WIDTH
    w_perm = jnp.concatenate([
        w_in[:, rw_cols + LORA_COLS + diff_cols:],
        w_in[:, rw_cols + LORA_COLS:rw_cols + LORA_COLS + diff_cols],
        w_in[:, :rw_cols + LORA_COLS],
    ], axis=1).astype(BF16)
    proj = _proj(h0, norm1_w, w_perm, proj_tm, 768)
    proj3 = proj.reshape(b, lp, IN_COLS)

    rw = _rwkv(proj3, rwkv_mu[:rw_cols], rwkv_mu[rw_cols:], rwkv_w0, rwkv_w2, rwkv_a0, rwkv_a2, rwkv_g2,
               rwkv_k_k, rwkv_k_a, rwkv_r_k.reshape(-1), rwkv_ln_w, rwkv_ln_b)
    lam4 = jnp.stack([lambda_q1, lambda_k1, lambda_q2, lambda_k2])
    da = _attention(proj3, q_norm_w, k_norm_w, lam4, diff_subln_w, lambda_init)

    wr = jnp.zeros((LANES, d), F32).at[0:N_GROUPS].set(router_group_w.T).at[8:8 + N_EXPERTS].set(router_expert_w.T)
    br = jnp.zeros((LANES,), F32).at[0:N_GROUPS].set(router_group_b).at[N_GROUPS:8].set(NEG_BIG)
    br = br.at[8:8 + N_EXPERTS].set(router_expert_b).reshape(LANES, 1)
    h1, u2, il, ic, cnt = _merge(
        rw.reshape(t, RWKV_WIDTH), da.reshape(t, DIFF_V_WIDTH), proj, h0,
        w_branch_rwkv.astype(BF16), w_branch_diff.astype(BF16), w_out.astype(BF16), norm2_w, wr, br, lp, tok_tm)

    n_real = b * (lp - FRONT_PAD)
    n_blocks = -(-(2 * n_real) // MOE_BLOCK) + N_EXPERTS
    dest1, dest2, block_e, n_used, block_grp, block_nxt = _routing_tables(il, cnt, n_blocks, tok_tm)
    xb = _dispatch(dest1, dest2, u2, n_blocks * MOE_BLOCK + 2 * tok_tm, tok_tm)
    yb = _moe(block_e, n_used, block_grp, block_nxt, xb, expert_w_gate, expert_w_up, expert_w_down)
    return h1, ic, dest1, dest2, yb


def kernel(x, meta_tokens, norm1_w, w_in, rwkv_mu, rwkv_w0, rwkv_w2, rwkv_a0, rwkv_a2, rwkv_g2, rwkv_k_k, rwkv_k_a, rwkv_r_k, rwkv_ln_w, rwkv_ln_b, q_norm_w, k_norm_w, lambda_q1, lambda_k1, lambda_q2, lambda_k2, diff_subln_w, w_branch_rwkv, w_branch_diff, w_out, norm2_w, router_group_w, router_group_b, router_expert_w, router_expert_b, expert_w_gate, expert_w_up, expert_w_down):
    b, seq, d = x.shape
    depth = norm1_w.shape[0]
    assert depth == 1, "the combine step emits the final output; deeper stacks need an intermediate form"
    lp = FRONT_PAD + N_META + seq
    meta = jnp.broadcast_to(meta_tokens[None].astype(x.dtype), (b, N_META, d))
    h0 = jnp.concatenate([jnp.zeros((b, FRONT_PAD, d), x.dtype), meta, x], axis=1).reshape(b * lp, d)
    proj_tm = 2048 if (b * lp) % 2048 == 0 else 128
    tok_tm = 512 if (b * lp) % 512 == 0 else 128
    l = 0
    h1, ic, dest1, dest2, yb = _layer(
        h0, lp, l, norm1_w[l], w_in[l], rwkv_mu[l], rwkv_w0[l], rwkv_w2[l], rwkv_a0[l], rwkv_a2[l],
        rwkv_g2[l], rwkv_k_k[l], rwkv_k_a[l], rwkv_r_k[l], rwkv_ln_w[l], rwkv_ln_b[l], q_norm_w[l],
        k_norm_w[l], lambda_q1[l], lambda_k1[l], lambda_q2[l], lambda_k2[l], diff_subln_w[l],
        w_branch_rwkv[l], w_branch_diff[l], w_out[l], norm2_w[l], router_group_w[l], router_group_b[l],
        router_expert_w[l], router_expert_b[l], expert_w_gate[l], expert_w_up[l], expert_w_down[l],
        proj_tm, tok_tm)
    return _combine(dest1, dest2, h1, ic, yb, b, lp, OUT_TILE)
```

```python
import functools
import math

import jax
import jax.numpy as jnp
from jax import lax
from jax.experimental import pallas as pl
from jax.experimental.pallas import tpu as pltpu

F32 = jnp.float32
BF16 = jnp.bfloat16

D_MODEL = 1024
N_META = 16
NORM_EPS = 1e-6
RWKV_HEADS = 16
RWKV_HEAD_DIM = 64
RWKV_WIDTH = RWKV_HEADS * RWKV_HEAD_DIM
DECAY_LORA = 64
AAA_LORA = 64
GATE_LORA = 128
LORA_COLS = DECAY_LORA + AAA_LORA + GATE_LORA
RWKV_GN_EPS = 64e-5
DIFF_HEADS = 8
DIFF_HEAD_DIM = 64
DIFF_V_DIM = 2 * DIFF_HEAD_DIM
DIFF_QK_WIDTH = DIFF_HEADS * 2 * DIFF_HEAD_DIM
DIFF_V_WIDTH = DIFF_HEADS * DIFF_V_DIM
ROPE_THETA = 500000.0
ROPE_DIM = DIFF_HEAD_DIM // 4
N_GROUPS = 4
EXPERTS_PER_GROUP = 8
N_EXPERTS = N_GROUPS * EXPERTS_PER_GROUP
EXPERT_FF = 512

LANES = 128
ATTN_BLOCK = 256
ATTN_KEY_BLOCK = 512
ATTN_HEADS_PER_STEP = 4
FRONT_PAD = ATTN_BLOCK - N_META
RWKV_CHUNK = 64
RWKV_SUBCHUNKS = 3
RWKV_PREP_COLS = 256
RWKV_HEAD_GROUP = 2
OUT_TILE = 256
MOE_BLOCK = 256
NEG_BIG = -1e30
VMEM_LIMIT = 48 * 1024 * 1024

COL_GATE = 0
COL_DIFF = 2 * D_MODEL
COL_RWKV = COL_DIFF + 2 * DIFF_QK_WIDTH + DIFF_V_WIDTH
COL_LORA = COL_RWKV + 3 * RWKV_WIDTH
IN_COLS = COL_LORA + LORA_COLS


def _dot(a, b):
    return jnp.dot(a, b, preferred_element_type=F32)


def _dot_nt(a, b):
    return lax.dot_general(a, b, (((1,), (1,)), ((), ())), preferred_element_type=F32)


def _split2(x):
    hi = x.astype(BF16)
    lo = (x - hi.astype(F32)).astype(BF16)
    return hi, lo


def _mm3(a, b_ref):
    ah, al = _split2(a)
    return _dot(ah, b_ref[0]) + _dot(ah, b_ref[1]) + _dot(al, b_ref[0])


def _hi_lo(w):
    hi = w.astype(BF16)
    return jnp.stack([hi, (w - hi.astype(F32)).astype(BF16)])


def _sigmoid(x):
    return 1.0 / (1.0 + jnp.exp(-x))


def _head_ones(width=LANES, head=RWKV_HEAD_DIM):
    r = lax.broadcasted_iota(jnp.int32, (width, width), 0) // head
    c = lax.broadcasted_iota(jnp.int32, (width, width), 1) // head
    return jnp.where(r == c, 1.0, 0.0).astype(BF16)


def _seg_sum(x, ones_bd):
    hi, lo = _split2(x)
    return _dot(hi, ones_bd) + _dot(lo, ones_bd)


def _seg_sum_wide(x, ones_bd):
    rows, width = x.shape
    n = width // LANES
    xs = jnp.concatenate([x[:, i * LANES:(i + 1) * LANES] for i in range(n)], axis=0)
    ys = _seg_sum(xs, ones_bd)
    return jnp.concatenate([ys[i * rows:(i + 1) * rows] for i in range(n)], axis=1)


def _to_token_tiles(ref, x):
    n, d = x.shape
    k = d // LANES
    for s in range(k):
        ref[pl.ds(s, n, stride=k), :] = x[:, s * LANES:(s + 1) * LANES]


def _from_token_tiles(ref, n):
    k = ref.shape[0] // n
    return jnp.concatenate([ref[pl.ds(s, n, stride=k), :] for s in range(k)], axis=1)


def _proj_kernel(h_ref, nw_ref, w_ref, o_ref, u_ref):
    @pl.when(pl.program_id(1) == 0)
    def _():
        x = h_ref[...]
        ms = jnp.mean(x * x, axis=-1, keepdims=True)
        u_ref[...] = (x * lax.rsqrt(ms + NORM_EPS) * nw_ref[...]).astype(BF16)

    o_ref[...] = _dot(u_ref[...], w_ref[...]).astype(o_ref.dtype)


def _proj(h, norm_w, w_bf16, tm, tn):
    t, d = h.shape
    n = w_bf16.shape[1]
    return pl.pallas_call(
        _proj_kernel,
        out_shape=jax.ShapeDtypeStruct((t, n), BF16),
        grid=(t // tm, n // tn),
        in_specs=[
            pl.BlockSpec((tm, d), lambda i, j: (i, 0)),
            pl.BlockSpec((1, d), lambda i, j: (0, 0)),
            pl.BlockSpec((d, tn), lambda i, j: (0, j)),
        ],
        out_specs=pl.BlockSpec((tm, tn), lambda i, j: (i, j)),
        scratch_shapes=[pltpu.VMEM((tm, d), BF16)],
        compiler_params=pltpu.CompilerParams(
            dimension_semantics=("parallel", "arbitrary"), vmem_limit_bytes=VMEM_LIMIT),
        name="norm_proj",
    )(h, norm_w.reshape(1, d), w_bf16)


def _rwkv_kernel(*refs):
    o_ref = refs[16]
    xr, xk, xv, xl, s_ref = refs[17:22]
    c = pl.program_id(1)
    n_skip = FRONT_PAD // o_ref.shape[1]

    @pl.when(c == 0)
    def _init():
        for xs in (xr, xk, xv, xl):
            xs[0:8, :] = jnp.zeros((8, xs.shape[1]), F32)
        s_ref[...] = jnp.zeros(s_ref.shape, F32)

    @pl.when(c < n_skip)
    def _pad():
        o_ref[...] = jnp.zeros(o_ref.shape, o_ref.dtype)

    @pl.when(c >= n_skip)
    def _chunk():
        _rwkv_chunk(*refs)


def _rwkv_chunk(r_ref, k_ref, v_ref, lo_ref, mu_ref, mul_ref, w0_ref, w2_ref, a0_ref, a2_ref, g2_ref,
                kkw_ref, kaw_ref, rkw_ref, lnw_ref, lnb_ref, o_ref,
                xr, xk, xv, xl, s_ref, kt_s, bt_s, kn_s, rt_s, v_s, y_s, gc_s, bon_s, g_s):
    R = r_ref.shape[1]
    C = RWKV_CHUNK
    n_sub = R // C
    W = RWKV_WIDTH

    def shift_mix(in_ref, xs, mu, cs):
        x = in_ref[0, :, cs].astype(F32)
        xs[8:8 + R, cs] = x
        prev = xs[7:7 + R, cs]
        xs[7:8, cs] = x[R - 1:R, :]
        return x + (prev - x) * mu

    lo = shift_mix(lo_ref, xl, mul_ref[...], slice(0, LORA_COLS))
    xw = _split2(jnp.tanh(lo[:, 0:DECAY_LORA]))
    xa = _split2(lo[:, DECAY_LORA:DECAY_LORA + AAA_LORA])
    xg = _split2(_sigmoid(lo[:, DECAY_LORA + AAA_LORA:LORA_COLS]))

    def mm3(a_hi_lo, b_ref, cs):
        ah, al = a_hi_lo
        return _dot(ah, b_ref[0, :, cs]) + _dot(ah, b_ref[1, :, cs]) + _dot(al, b_ref[0, :, cs])

    ones_bd = _head_ones()
    ti = lax.broadcasted_iota(jnp.int32, (R, R), 0)
    tj = lax.broadcasted_iota(jnp.int32, (R, R), 1)
    ltri = jnp.where((ti >= tj) & (ti // C == tj // C), 1.0, 0.0).astype(BF16)

    for c0 in range(0, W, RWKV_PREP_COLS):
        cs = slice(c0, c0 + RWKV_PREP_COLS)
        col = lambda ref, off=0: ref[:, off + c0:off + c0 + RWKV_PREP_COLS]
        r = shift_mix(r_ref, xr, col(mu_ref), cs)
        k = shift_mix(k_ref, xk, col(mu_ref, W), cs)
        v = shift_mix(v_ref, xv, col(mu_ref, 2 * W), cs)

        lw = (-math.exp(-0.5)) * _sigmoid(col(w0_ref) + mm3(xw, w2_ref, cs))
        a = _sigmoid(col(a0_ref) + mm3(xa, a2_ref, cs))
        g_s[:, cs] = mm3(xg, g2_ref, cs)

        kk = k * col(kkw_ref)
        kkn = kk * lax.rsqrt(jnp.maximum(_seg_sum_wide(kk * kk, ones_bd), 1e-24))
        k2 = k * (1.0 + (a - 1.0) * col(kaw_ref))
        bon_s[:, cs] = _seg_sum_wide(r * k2 * col(rkw_ref), ones_bd) * v

        l1 = lw.astype(BF16)
        rem = lw - l1.astype(F32)
        l2 = rem.astype(BF16)
        l3 = (rem - l2.astype(F32)).astype(BF16)
        cum = _dot(ltri, l1) + _dot(ltri, l2) + _dot(ltri, l3)
        e_pos = jnp.exp(cum)
        e_neg = jnp.exp(-cum)
        kt_s[:, cs] = kkn * jnp.exp(cum - lw)
        bt_s[:, cs] = kkn * a * e_neg
        kn_s[:, cs] = k2 * e_neg
        rt_s[:, cs] = r * e_pos
        v_s[:, cs] = v
        for s in range(n_sub):
            gc_s[s:s + 1, cs] = jnp.exp(cum[(s + 1) * C - 1:(s + 1) * C, :])

    G = RWKV_HEAD_GROUP
    GL = G * RWKV_HEAD_DIM
    GC = G * C
    n_groups = W // GL
    lane_head = lax.broadcasted_iota(jnp.int32, (1, GL), 1) // RWKV_HEAD_DIM
    head_mask = [jnp.where(lane_head == h, 1.0, 0.0).astype(BF16) for h in range(G)]
    trow = lax.broadcasted_iota(jnp.int32, (C, GC), 0)
    tcol = lax.broadcasted_iota(jnp.int32, (C, GC), 1) % C
    strict = trow > tcol
    incl = trow >= tcol
    eye = jnp.where(trow == tcol, 1.0, 0.0)
    br = lax.broadcasted_iota(jnp.int32, (GL, GL), 0) // RWKV_HEAD_DIM
    bc = lax.broadcasted_iota(jnp.int32, (GL, GL), 1) // RWKV_HEAD_DIM
    bdmask = jnp.where(br == bc, 1.0, 0.0)

    def stack(y):
        yb = y.astype(BF16)
        return jnp.concatenate([yb * m for m in head_mask], axis=0)

    groups = range(n_groups)
    units = [(s, p) for s in range(n_sub) for p in groups]
    win = {(s, p): (slice(s * C, (s + 1) * C), slice(p * GL, (p + 1) * GL)) for s, p in units}
    def prepare(units):
        kt = {q: kt_s[win[q]] for q in units}
        bt = {q: bt_s[win[q]] for q in units}
        kn = {q: kn_s[win[q]] for q in units}
        rt = {q: rt_s[win[q]] for q in units}
        vv = {q: v_s[win[q]] for q in units}
        gc = {q: gc_s[q[0]:q[0] + 1, win[q][1]] for q in units}
        kr = {q: jnp.concatenate([kt[q], rt[q]], axis=0).astype(BF16) for q in units}
        m_all = {q: _dot_nt(kr[q], jnp.concatenate([stack(bt[q]), stack(kn[q])], axis=0))
                 for q in units}
        p_mat = {q: jnp.where(incl, m_all[q][C:2 * C, 0:GC], 0.0).astype(BF16) for q in units}
        bq_mat = {q: jnp.concatenate([jnp.where(strict, m_all[q][0:C, GC:2 * GC], 0.0),
                                      jnp.where(incl, m_all[q][C:2 * C, GC:2 * GC], 0.0)], axis=0).astype(BF16)
                  for q in units}
        v_term = {q: _dot(bq_mat[q], stack(vv[q])) for q in units}

        pw = {q: -jnp.where(strict, m_all[q][0:C, 0:GC], 0.0) for q in units}
        t_inv = {q: eye + pw[q] for q in units}
        pw = {q: _dot(pw[q].astype(BF16), stack(pw[q])) for q in units}
        for _ in range(int(math.log2(C)) - 2):
            both = {q: _dot(jnp.concatenate([t_inv[q], pw[q]], axis=0).astype(BF16), stack(pw[q]))
                    for q in units}
            t_inv = {q: t_inv[q] + both[q][0:C] for q in units}
            pw = {q: both[q][C:2 * C] for q in units}
        t_inv = {q: (t_inv[q] + _dot(t_inv[q].astype(BF16), stack(pw[q]))).astype(BF16) for q in units}
        xc = {q: jnp.concatenate([bt[q] * gc[q], kn[q] * gc[q]], axis=0).astype(BF16) for q in units}
        return kr, p_mat, v_term, t_inv, xc, vv, gc

    kr, p_mat, v_term, t_inv, xc, vv, gc = prepare(units)

    state = [s_ref[p] for p in groups]
    for s in range(n_sub):
        qs = [(s, p) for p in groups]
        ks = [_dot_nt(kr[q], state[q[1]].astype(BF16)) for q in qs]
        u = [-_dot(t_inv[q], stack(ks[q[1]][0:C] + v_term[q][0:C])) for q in qs]
        for q in qs:
            y_s[win[q]] = ks[q[1]][C:2 * C] + v_term[q][C:2 * C] + _dot(p_mat[q], stack(u[q[1]]))
        for q in qs:
            uc = jnp.concatenate([u[q[1]], vv[q]], axis=0)
            state[q[1]] = state[q[1]] * gc[q] + bdmask * _dot(jnp.transpose(uc).astype(BF16), xc[q])
    for p in groups:
        s_ref[p] = state[p]

    inv_n = 1.0 / RWKV_HEAD_DIM
    for c0 in range(0, W, RWKV_PREP_COLS):
        cs = slice(c0, c0 + RWKV_PREP_COLS)
        y = y_s[:, cs]
        mean = _seg_sum_wide(y, ones_bd) * inv_n
        dlt = y - mean
        var = _seg_sum_wide(dlt * dlt, ones_bd) * inv_n
        yn = dlt * lax.rsqrt(var + RWKV_GN_EPS) * lnw_ref[:, cs] + lnb_ref[:, cs]
        o_ref[0, :, cs] = ((yn + bon_s[:, cs]) * g_s[:, cs]).astype(o_ref.dtype)


def _rwkv(proj3, mu_rkv, mu_lo, w0, w2, a0, a2, g2, k_k, k_a, r_k, ln_w, ln_b):
    b, lp, _ = proj3.shape
    C = RWKV_CHUNK * RWKV_SUBCHUNKS
    W = RWKV_WIDTH
    cb = COL_RWKV // W
    lb = COL_LORA // LORA_COLS
    row = lambda x: x.reshape(1, -1)
    full = lambda shape: pl.BlockSpec(shape, lambda i, c: (0,) * len(shape))
    wide = pltpu.VMEM((C, W), F32)
    gl = RWKV_HEAD_GROUP * RWKV_HEAD_DIM
    return pl.pallas_call(
        _rwkv_kernel,
        out_shape=jax.ShapeDtypeStruct((b, lp, W), BF16),
        grid=(b, lp // C),
        in_specs=[
            pl.BlockSpec((1, C, W), lambda i, c: (i, c, cb)),
            pl.BlockSpec((1, C, W), lambda i, c: (i, c, cb + 1)),
            pl.BlockSpec((1, C, W), lambda i, c: (i, c, cb + 2)),
            pl.BlockSpec((1, C, LORA_COLS), lambda i, c: (i, c, lb)),
            full((1, 3 * W)), full((1, LORA_COLS)),
            full((1, W)), full((2, DECAY_LORA, W)), full((1, W)), full((2, AAA_LORA, W)), full((2, GATE_LORA, W)),
            full((1, W)), full((1, W)), full((1, W)), full((1, W)), full((1, W)),
        ],
        out_specs=pl.BlockSpec((1, C, W), lambda i, c: (i, c, 0)),
        scratch_shapes=[
            pltpu.VMEM((C + 8, W), F32), pltpu.VMEM((C + 8, W), F32), pltpu.VMEM((C + 8, W), F32),
            pltpu.VMEM((C + 8, LORA_COLS), F32),
            pltpu.VMEM((W // gl, gl, gl), F32),
            wide, wide, wide, wide, wide, wide, pltpu.VMEM((8, W), F32), wide, wide,
        ],
        compiler_params=pltpu.CompilerParams(
            dimension_semantics=("parallel", "arbitrary"), vmem_limit_bytes=VMEM_LIMIT),
        name="rwkv7_time_mix",
    )(proj3, proj3, proj3, proj3, row(mu_rkv), row(mu_lo), row(w0), _hi_lo(w2), row(a0), _hi_lo(a2), _hi_lo(g2),
      row(k_k), row(k_a), row(r_k), row(ln_w), row(ln_b))


def _attn_kernel(q_ref, k_ref, v_ref, cos_ref, s1_ref, s2_ref, qw_ref, kw_ref, lam_ref, sw_ref, o_ref,
                 kp_s, vp_s, m_s, acc_s, *, tq, tk, lambda_init):
    qi = pl.program_id(2)
    lp = k_ref.shape[1]
    n_real = lp - FRONT_PAD
    lk = kp_s.shape[0]
    nh = k_ref.shape[2] // LANES
    vw = 2 * LANES
    heads = range(nh)
    hs = [slice(h * LANES, (h + 1) * LANES) for h in heads]
    ones_bd = _head_ones(LANES, DIFF_HEAD_DIM)
    shift = ROPE_DIM // 2

    def norm_rope(x, w, rows):
        ms = _seg_sum(x * x, ones_bd) * (1.0 / DIFF_HEAD_DIM)
        xn = x * lax.rsqrt(ms + NORM_EPS) * w
        return (xn * cos_ref[rows, :] + pltpu.roll(xn, shift, 1) * s1_ref[rows, :]
                + pltpu.roll(xn, LANES - shift, 1) * s2_ref[rows, :])

    @pl.when(qi == 0)
    def _prep():
        def put(dst, n):
            src = pl.ds(FRONT_PAD + dst, n)
            rows = pl.ds(dst, n)
            for h in heads:
                kp_s[rows, hs[h]] = norm_rope(k_ref[0, src, hs[h]].astype(F32), kw_ref[...], src).astype(BF16)
                vp_s[rows, h * vw:h * vw + LANES] = v_ref[0, src, hs[h]].astype(BF16)
                vp_s[rows, h * vw + LANES:(h + 1) * vw] = jnp.ones((n, LANES), BF16)

        def body(i, carry):
            put(pl.multiple_of(i * LANES, LANES), LANES)
            return carry
        lax.fori_loop(0, n_real // LANES, body, 0, unroll=2)
        tail = n_real % LANES
        if tail:
            put(n_real - tail, tail)
        if lk > n_real:
            kp_s[n_real:lk, :] = jnp.zeros((lk - n_real, nh * LANES), BF16)
            vp_s[n_real:lk, :] = jnp.zeros((lk - n_real, nh * vw), BF16)

    lane = lax.broadcasted_iota(jnp.int32, (1, LANES), 1)
    m0 = jnp.where(lane < DIFF_HEAD_DIM, 1.0, 0.0)
    m1 = 1.0 - m0
    rows_q = pl.ds(pl.multiple_of(qi * tq, tq), tq)
    qs = []
    for h in heads:
        qn = norm_rope(q_ref[0, :, hs[h]].astype(F32), qw_ref[...], rows_q) * (DIFF_HEAD_DIM ** -0.5)
        qs.append(jnp.concatenate([qn * m0, qn * m1], axis=0).astype(BF16))
    first_row = qi * tq - FRONT_PAD
    row = first_row + lax.broadcasted_iota(jnp.int32, (2 * tq, LANES), 0) % tq
    col0 = lax.broadcasted_iota(jnp.int32, (2 * tq, LANES), 1)

    def step(j, width, causal, init=False):
        start = j * tk
        if not isinstance(j, int):
            start = pl.multiple_of(start, tk)
        ks = pl.ds(start, width)
        n_sub = width // LANES
        s = [_dot_nt(qs[h], kp_s[ks, hs[h]]) for h in heads]
        sub = [[s[h][:, c * LANES:(c + 1) * LANES] for c in range(n_sub)] for h in heads]
        if causal:
            sub = [[jnp.where(start + c * LANES + col0 <= row, sub[h][c], NEG_BIG) for c in range(n_sub)]
                   for h in heads]
        m_new = []
        for h in heads:
            mx = functools.reduce(jnp.maximum, sub[h])
            mx = jnp.broadcast_to(jnp.max(mx, axis=1, keepdims=True), mx.shape)
            m_new.append(mx if init else jnp.maximum(m_s[h], mx))
        p = [jnp.concatenate([jnp.exp(sub[h][c] - m_new[h]) for c in range(n_sub)], axis=1).astype(BF16)
             for h in heads]
        pv = [_dot(p[h], vp_s[ks, h * vw:(h + 1) * vw]) for h in heads]
        for h in heads:
            if init:
                acc_s[h] = pv[h]
            else:
                alpha = jnp.exp(m_s[h] - m_new[h])
                acc_s[h] = jnp.concatenate([alpha, alpha], axis=1) * acc_s[h] + pv[h]
            m_s[h] = m_new[h]

    assert FRONT_PAD + N_META == tq and tk % tq == 0 and N_META <= LANES
    n_full = jnp.maximum(first_row + 1, 0) // tk
    last = jnp.maximum(first_row + tq - 1, 0) // tk
    step(n_full, tk, True, init=True)

    def mid(j, carry):
        step(j, tk, False)
        return carry
    lax.fori_loop(0, n_full, mid, 0)

    @pl.when(last > n_full)
    def _diag():
        step(last, LANES, True)

    lam = (jnp.exp(jnp.sum(lam_ref[0:1, :] * lam_ref[1:2, :], axis=1, keepdims=True))
           - jnp.exp(jnp.sum(lam_ref[2:3, :] * lam_ref[3:4, :], axis=1, keepdims=True)) + lambda_init)
    for h in heads:
        acc = acc_s[h]
        o = acc[:, 0:LANES] / acc[:, LANES:vw]
        od = o[0:tq] - lam * o[tq:2 * tq]
        ms = jnp.mean(od * od, axis=1, keepdims=True)
        o_ref[0, :, hs[h]] = (od * lax.rsqrt(ms + NORM_EPS) * sw_ref[...]
                              * (1.0 - lambda_init)).astype(o_ref.dtype)


def _rope_tables(lp):
    half = ROPE_DIM // 2
    inv_freq = jnp.exp(-math.log(ROPE_THETA) * jnp.arange(half, dtype=F32) * 2.0 / ROPE_DIM)
    pos = (jnp.arange(lp) - FRONT_PAD).astype(F32)
    ang = pos[:, None] * inv_freq[None, :]
    cos, sin = jnp.cos(ang), jnp.sin(ang)
    one = jnp.ones((lp, DIFF_HEAD_DIM - ROPE_DIM), F32)
    zero = jnp.zeros((lp, DIFF_HEAD_DIM - ROPE_DIM), F32)
    zh = jnp.zeros((lp, half), F32)
    c = jnp.concatenate([cos, cos, one], axis=1)
    s1 = jnp.concatenate([zh, sin, zero], axis=1)
    s2 = jnp.concatenate([-sin, zh, zero], axis=1)
    dup = lambda t: jnp.concatenate([t, t], axis=1)
    return dup(c), dup(s1), dup(s2)


def _attention(proj3, q_norm_w, k_norm_w, lam4, subln_w, lambda_init):
    b, lp, _ = proj3.shape
    tq, tk, nh = ATTN_BLOCK, ATTN_KEY_BLOCK, ATTN_HEADS_PER_STEP
    lk = -(-(lp - FRONT_PAD) // tk) * tk
    hw = nh * LANES
    qb = COL_DIFF // hw
    kb = qb + DIFF_QK_WIDTH // hw
    vb = kb + DIFF_QK_WIDTH // hw
    cos, s1, s2 = _rope_tables(lp)
    dup = lambda w: jnp.concatenate([w, w]).reshape(1, LANES)
    full = lambda shape: pl.BlockSpec(shape, lambda i, h, q: (0,) * len(shape))
    return pl.pallas_call(
        functools.partial(_attn_kernel, tq=tq, tk=tk, lambda_init=lambda_init),
        out_shape=jax.ShapeDtypeStruct((b, lp, DIFF_V_WIDTH), BF16),
        grid=(b, DIFF_HEADS // nh, lp // tq),
        in_specs=[
            pl.BlockSpec((1, tq, hw), lambda i, h, q: (i, q, qb + h)),
            pl.BlockSpec((1, lp, hw), lambda i, h, q: (i, 0, kb + h)),
            pl.BlockSpec((1, lp, hw), lambda i, h, q: (i, 0, vb + h)),
            full((lp, LANES)), full((lp, LANES)), full((lp, LANES)),
            full((1, LANES)), full((1, LANES)), full((4, DIFF_HEAD_DIM)), full((1, LANES)),
        ],
        out_specs=pl.BlockSpec((1, tq, hw), lambda i, h, q: (i, q, h)),
        scratch_shapes=[pltpu.VMEM((lk, hw), BF16), pltpu.VMEM((lk, 2 * hw), BF16),
                        pltpu.VMEM((nh, 2 * tq, LANES), F32), pltpu.VMEM((nh, 2 * tq, 2 * LANES), F32)],
        compiler_params=pltpu.CompilerParams(
            dimension_semantics=("parallel", "parallel", "arbitrary"), vmem_limit_bytes=VMEM_LIMIT),
        name="diff_attention",
    )(proj3, proj3, proj3, cos, s1, s2, dup(q_norm_w), dup(k_norm_w), lam4, subln_w.reshape(1, LANES))


def _merge_kernel(rw_ref, da_ref, g1_ref, g2_ref, h_ref, wbr_ref, wbd_ref, wo_ref, n2_ref, wr_ref, br_ref,
                  h1_ref, u2_ref, il_ref, ic_ref, cnt_ref, base_s, *, lp):
    i = pl.program_id(0)
    tm = rw_ref.shape[0]

    @pl.when(i == 0)
    def _():
        base_s[...] = jnp.zeros(base_s.shape, F32)

    y1 = _dot(rw_ref[...], wbr_ref[...])
    y2 = _dot(da_ref[...], wbd_ref[...])
    merged = _sigmoid(g1_ref[...].astype(F32)) * y1 + _sigmoid(g2_ref[...].astype(F32)) * y2
    h1 = h_ref[...] + _dot(merged.astype(BF16), wo_ref[...])
    h1_ref[...] = h1
    u2 = h1 * lax.rsqrt(jnp.mean(h1 * h1, axis=-1, keepdims=True) + NORM_EPS) * n2_ref[...]
    _to_token_tiles(u2_ref, u2)

    uh, ul = _split2(u2)
    wh, wl = _split2(wr_ref[...])
    lt = _dot_nt(wh, uh) + _dot_nt(wh, ul) + _dot_nt(wl, uh) + br_ref[...]

    gi8 = lax.broadcasted_iota(jnp.int32, (8, tm), 0)
    lg = lt[0:8]
    ge = jnp.exp(lg - jnp.max(lg, axis=0, keepdims=True))
    gp = ge / jnp.sum(ge, axis=0, keepdims=True)
    gv = jnp.max(gp, axis=0, keepdims=True)
    gidx = jnp.min(jnp.where(gp == gv, gi8, N_EXPERTS), axis=0, keepdims=True)

    ei = lax.broadcasted_iota(jnp.int32, (N_EXPERTS, tm), 0)
    sel = (ei // EXPERTS_PER_GROUP) == gidx
    le = jnp.where(sel, lt[8:8 + N_EXPERTS], NEG_BIG)
    ee = jnp.where(sel, jnp.exp(le - jnp.max(le, axis=0, keepdims=True)), 0.0)
    ep = jnp.where(sel, ee / jnp.sum(ee, axis=0, keepdims=True), -1.0)
    v1 = jnp.max(ep, axis=0, keepdims=True)
    i1 = jnp.min(jnp.where(ep == v1, ei, N_EXPERTS), axis=0, keepdims=True)
    ep2 = jnp.where(ei == i1, -1.0, ep)
    v2 = jnp.max(ep2, axis=0, keepdims=True)
    i2 = jnp.min(jnp.where(ep2 == v2, ei, N_EXPERTS), axis=0, keepdims=True)
    den = v1 + v2
    gate1 = gv * v1 / den
    gate2 = gv * v2 / den

    tok = (i * tm + lax.broadcasted_iota(jnp.int32, (1, tm), 1)).astype(F32)
    pos = tok - jnp.floor((tok + 0.5) / lp) * lp
    valid = pos > (FRONT_PAD - 0.5)

    oh1 = jnp.where((ei == i1) & valid, 1.0, 0.0)
    oh2 = jnp.where((ei == i2) & valid, 1.0, 0.0)
    oh = oh1 + oh2
    ur = lax.broadcasted_iota(jnp.int32, (tm, tm), 0)
    uc = lax.broadcasted_iota(jnp.int32, (tm, tm), 1)
    before = jnp.where(ur < uc, 1.0, 0.0).astype(BF16)
    tot = base_s[:, 0:1] + _dot(oh.astype(BF16), before)
    rank1 = jnp.sum(oh1 * tot, axis=0, keepdims=True)
    rank2 = jnp.sum(oh2 * tot, axis=0, keepdims=True)
    base_s[...] = base_s[...] + jnp.sum(oh, axis=1, keepdims=True)
    cnt_ref[...] = base_s[...]

    il = jnp.where(gi8 == 0, i1, jnp.where(gi8 == 1, i2, jnp.where(
        gi8 == 2, rank1.astype(jnp.int32), jnp.where(gi8 == 3, rank2.astype(jnp.int32), jnp.where(
            gi8 == 4, valid.astype(jnp.int32), 0)))))
    il_ref[...] = il
    ri = lax.broadcasted_iota(jnp.int32, (LANES, tm), 0)
    ic = jnp.where(ri == 0, gate1, jnp.where(ri == 1, gate2, 0.0))
    ic_ref[...] = jnp.transpose(ic)


def _merge(rw, da, proj, h0, wbr, wbd, wo, norm2_w, wr, br, lp, tm):
    t, d = h0.shape
    gb = COL_GATE // d
    full = lambda shape: pl.BlockSpec(shape, lambda i: (0,) * len(shape))
    tile = lambda c: pl.BlockSpec((tm, d), lambda i: (i, c))
    return pl.pallas_call(
        functools.partial(_merge_kernel, lp=lp),
        out_shape=(
            jax.ShapeDtypeStruct((t, d), F32),
            jax.ShapeDtypeStruct((t * (d // LANES), LANES), F32),
            jax.ShapeDtypeStruct((8, t), jnp.int32),
            jax.ShapeDtypeStruct((t, LANES), F32),
            jax.ShapeDtypeStruct((N_EXPERTS, LANES), F32),
        ),
        grid=(t // tm,),
        in_specs=[tile(0), tile(0), tile(gb), tile(gb + 1), tile(0),
                  full((d, d)), full((d, d)), full((d, d)), full((1, d)), full((LANES, d)), full((LANES, 1))],
        out_specs=(
            tile(0), pl.BlockSpec((tm * (d // LANES), LANES), lambda i: (i, 0)),
            pl.BlockSpec((8, tm), lambda i: (0, i)),
            pl.BlockSpec((tm, LANES), lambda i: (i, 0)),
            full((N_EXPERTS, LANES)),
        ),
        scratch_shapes=[pltpu.VMEM((N_EXPERTS, LANES), F32)],
        compiler_params=pltpu.CompilerParams(
            dimension_semantics=("arbitrary",), vmem_limit_bytes=VMEM_LIMIT),
        name="merge_router",
    )(rw, da, proj, proj, h0, wbr, wbd, wo, norm2_w.reshape(1, d), wr, br)


def _dispatch_kernel(d1_ref, d2_ref, u_ref, xin_hbm, xb_hbm, sem):
    del xin_hbm
    tm = d1_ref.shape[2]
    k = u_ref.shape[0] // tm

    def start(r, carry):
        for prio, d_ref in enumerate((d1_ref, d2_ref)):
            pltpu.make_async_copy(u_ref.at[pl.ds(pl.multiple_of(r * k, k), k)],
                                  xb_hbm.at[pl.ds(pl.multiple_of(d_ref[0, 0, r] * k, k), k)],
                                  sem).start(priority=prio)
        return carry

    lax.fori_loop(0, tm, start, 0, unroll=8)
    for _ in range(2):
        pltpu.make_async_copy(u_ref, xb_hbm.at[pl.ds(0, tm * k)], sem).wait()


def _dispatch(dest1, dest2, u2, cap, tm):
    k = D_MODEL // LANES
    t = u2.shape[0] // k
    nt = t // tm
    smem = lambda: pl.BlockSpec((1, 1, tm), lambda i: (i, 0, 0), memory_space=pltpu.SMEM)
    return pl.pallas_call(
        _dispatch_kernel,
        out_shape=jax.ShapeDtypeStruct((cap * k, LANES), F32),
        grid=(nt,),
        in_specs=[smem(), smem(), pl.BlockSpec((tm * k, LANES), lambda i: (i, 0)),
                  pl.BlockSpec(memory_space=pl.ANY)],
        out_specs=pl.BlockSpec(memory_space=pl.ANY),
        scratch_shapes=[pltpu.SemaphoreType.DMA(())],
        input_output_aliases={3: 0},
        compiler_params=pltpu.CompilerParams(
            dimension_semantics=("arbitrary",), vmem_limit_bytes=VMEM_LIMIT),
        name="moe_dispatch",
    )(dest1.reshape(nt, 1, tm), dest2.reshape(nt, 1, tm), u2, jnp.zeros((cap * k, LANES), F32))


def _moe_kernel(be_ref, nb_ref, grp_ref, nxt_ref, x_ref, wg_hbm, wu_hbm, wd_hbm, o_ref,
                wg_f, wu_f, wd_f, wg_s, wu_s, wd_s, sem):
    i = pl.program_id(0)
    bm = MOE_BLOCK
    used = i < nb_ref[0]
    slot = lax.rem(grp_ref[i], 2)

    def fetch(expert, sl):
        return [pltpu.make_async_copy(w_hbm.at[expert], w_f.at[sl], sem.at[sl])
                for w_hbm, w_f in ((wg_hbm, wg_f), (wu_hbm, wu_f), (wd_hbm, wd_f))]

    @pl.when(used & ((i == 0) | (grp_ref[i] != grp_ref[jnp.maximum(i - 1, 0)])))
    def _():
        @pl.when(i == 0)
        def _():
            for cp in fetch(be_ref[0], 0):
                cp.start()

        @pl.when(nxt_ref[i] >= 0)
        def _():
            for cp in fetch(nxt_ref[i], 1 - slot):
                cp.start()

        for cp in fetch(be_ref[i], slot):
            cp.wait()
        wg_s[...] = wg_f[slot].astype(BF16)
        wu_s[...] = wu_f[slot].astype(BF16)
        wd_s[...] = wd_f[slot].astype(BF16)

    @pl.when(used)
    def _():
        x = _from_token_tiles(x_ref, bm).astype(BF16)
        hg = _dot(x, wg_s[...])
        hu = _dot(x, wu_s[...])
        hid = hg * _sigmoid(hg) * hu
        _to_token_tiles(o_ref, _dot(hid.astype(BF16), wd_s[...]))

    @pl.when(i >= nb_ref[0])
    def _():
        o_ref[...] = jnp.zeros(o_ref.shape, F32)


def _moe(block_e, n_used, block_grp, block_nxt, xb, wg, wu, wd):
    d = wg.shape[1]
    ff = wg.shape[2]
    bm = MOE_BLOCK
    k = d // LANES
    cap = block_e.shape[0] * bm
    hbm = pl.BlockSpec(memory_space=pl.ANY)
    return pl.pallas_call(
        _moe_kernel,
        out_shape=jax.ShapeDtypeStruct((cap * k, LANES), F32),
        grid_spec=pltpu.PrefetchScalarGridSpec(
            num_scalar_prefetch=4,
            grid=(cap // bm,),
            in_specs=[pl.BlockSpec((bm * k, LANES), lambda i, *_: (i, 0)), hbm, hbm, hbm],
            out_specs=pl.BlockSpec((bm * k, LANES), lambda i, *_: (i, 0)),
            scratch_shapes=[pltpu.VMEM((2, d, ff), F32), pltpu.VMEM((2, d, ff), F32), pltpu.VMEM((2, ff, d), F32),
                            pltpu.VMEM((d, ff), BF16), pltpu.VMEM((d, ff), BF16), pltpu.VMEM((ff, d), BF16),
                            pltpu.SemaphoreType.DMA((2,))],
        ),
        compiler_params=pltpu.CompilerParams(
            dimension_semantics=("arbitrary",), vmem_limit_bytes=VMEM_LIMIT),
        name="moe_experts",
    )(block_e, n_used, block_grp, block_nxt, xb, wg, wu, wd)


def _combine_kernel(d1_ref, d2_ref, n1_ref, n2_ref, h_ref, ic_ref, yb_hbm, o_ref, ga, gb, sem):
    tm = h_ref.shape[0]
    k = ga.shape[1] // tm
    n = pl.program_id(0) * pl.num_programs(1) + pl.program_id(1)
    total = pl.num_programs(0) * pl.num_programs(1)
    slot = lax.rem(n, 2)

    def issue(da_ref, db_ref, sl):
        def start(r, carry):
            for prio, (d_ref, buf) in enumerate(((da_ref, ga), (db_ref, gb))):
                pltpu.make_async_copy(yb_hbm.at[pl.ds(pl.multiple_of(d_ref[0, 0, r] * k, k), k)],
                                      buf.at[sl, pl.ds(pl.multiple_of(r * k, k), k)],
                                      sem.at[sl]).start(priority=prio)
            return carry
        lax.fori_loop(0, tm, start, 0, unroll=8)

    @pl.when(n == 0)
    def _first():
        issue(d1_ref, d2_ref, 0)

    @pl.when(n + 1 < total)
    def _next():
        issue(n1_ref, n2_ref, 1 - slot)

    for buf in (ga, gb):
        pltpu.make_async_copy(yb_hbm.at[pl.ds(0, tm * k)], buf.at[slot], sem.at[slot]).wait()
    ic = ic_ref[...]
    o_ref[0] = (h_ref[...] + ic[:, 0:1] * _from_token_tiles(ga.at[slot], tm)
                + ic[:, 1:2] * _from_token_tiles(gb.at[slot], tm))


def _combine(dest1, dest2, h1, ic, yb, b, lp, tm):
    t, d = h1.shape
    per = lp // tm
    first = (FRONT_PAD + N_META) // tm
    steps = per - first
    nt = t // tm
    tile = lambda i, j: i * per + j + first

    def next_tile(i, j):
        nxt = jnp.minimum(i * steps + j + 1, b * steps - 1)
        return tile(nxt // steps, nxt % steps)

    cur = lambda: pl.BlockSpec((1, 1, tm), lambda i, j: (tile(i, j), 0, 0), memory_space=pltpu.SMEM)
    nxt = lambda: pl.BlockSpec((1, 1, tm), lambda i, j: (next_tile(i, j), 0, 0), memory_space=pltpu.SMEM)
    d1 = dest1.reshape(nt, 1, tm)
    d2 = dest2.reshape(nt, 1, tm)
    return pl.pallas_call(
        _combine_kernel,
        out_shape=jax.ShapeDtypeStruct((b, lp - FRONT_PAD - N_META, d), F32),
        grid=(b, steps),
        in_specs=[
            cur(), cur(), nxt(), nxt(),
            pl.BlockSpec((tm, d), lambda i, j: (tile(i, j), 0)),
            pl.BlockSpec((tm, LANES), lambda i, j: (tile(i, j), 0)),
            pl.BlockSpec(memory_space=pl.ANY),
        ],
        out_specs=pl.BlockSpec((1, tm, d), lambda i, j: (i, j, 0)),
        scratch_shapes=[pltpu.VMEM((2, tm * (d // LANES), LANES), F32),
                        pltpu.VMEM((2, tm * (d // LANES), LANES), F32),
                        pltpu.SemaphoreType.DMA((2,))],
        compiler_params=pltpu.CompilerParams(
            dimension_semantics=("arbitrary", "arbitrary"), vmem_limit_bytes=VMEM_LIMIT),
        name="moe_combine",
    )(d1, d2, d1, d2, h1, ic, yb)


def _routing_tables(il, cnt, n_blocks, tm):
    bm = MOE_BLOCK
    counts = cnt[:, 0].astype(jnp.int32)
    padded = (counts + bm - 1) // bm * bm
    pad_end = jnp.cumsum(padded)
    pad_start = pad_end - padded
    valid = il[4] > 0
    spare = n_blocks * bm + jnp.arange(il.shape[1], dtype=jnp.int32) % tm
    eid = jnp.arange(N_EXPERTS, dtype=jnp.int32)
    first_slot = jnp.sum(jnp.where(il[0:2, :, None] == eid, pad_start, 0), axis=-1)
    dest1 = jnp.where(valid, first_slot[0] + il[2], spare).astype(jnp.int32)
    dest2 = jnp.where(valid, first_slot[1] + il[3], spare + tm).astype(jnp.int32)
    starts = jnp.arange(n_blocks, dtype=jnp.int32) * bm
    block_e = jnp.minimum(jnp.sum((pad_end[None, :] <= starts[:, None]).astype(jnp.int32), axis=1),
                          N_EXPERTS - 1)
    n_used = (pad_end[-1:] // bm).astype(jnp.int32)
    owns = counts > 0
    eid = jnp.arange(N_EXPERTS, dtype=jnp.int32)
    ordinal = jnp.cumsum(owns.astype(jnp.int32)) - 1
    later = (eid[None, :] > eid[:, None]) & owns[None, :]
    nxt = jnp.min(jnp.where(later, eid[None, :], N_EXPERTS), axis=1)
    nxt = jnp.where(nxt < N_EXPERTS, nxt, -1)
    return dest1, dest2, block_e, n_used, ordinal[block_e].astype(jnp.int32), nxt[block_e].astype(jnp.int32)


def _layer(h0, lp, l, norm1_w, w_in, rwkv_mu, rwkv_w0, rwkv_w2, rwkv_a0, rwkv_a2, rwkv_g2,
           rwkv_k_k, rwkv_k_a, rwkv_r_k, rwkv_ln_w, rwkv_ln_b, q_norm_w, k_norm_w,
           lambda_q1, lambda_k1, lambda_q2, lambda_k2, diff_subln_w, w_branch_rwkv, w_branch_diff,
           w_out, norm2_w, router_group_w, router_group_b, router_expert_w, router_expert_b,
           expert_w_gate, expert_w_up, expert_w_down, proj_tm, tok_tm):
    t, d = h0.shape
    b = t // lp
    lambda_init = 0.8 - 0.6 * math.exp(-0.3 * l)
    rw_cols = 3 * RWKV_WIDTH
    diff_cols = 2 * DIFF_QK_WIDTH + DIFF_V_WIDTH
    w_perm = jnp.concatenate([
        w_in[:, rw_cols + LORA_COLS + diff_cols:],
        w_in[:, rw_cols + LORA_COLS:rw_cols + LORA_COLS + diff_cols],
        w_in[:, :rw_cols + LORA_COLS],
    ], axis=1).astype(BF16)
    proj = _proj(h0, norm1_w, w_perm, proj_tm, 768)
    proj3 = proj.reshape(b, lp, IN_COLS)

    rw = _rwkv(proj3, rwkv_mu[:rw_cols], rwkv_mu[rw_cols:], rwkv_w0, rwkv_w2, rwkv_a0, rwkv_a2, rwkv_g2,
               rwkv_k_k, rwkv_k_a, rwkv_r_k.reshape(-1), rwkv_ln_w, rwkv_ln_b)
    lam4 = jnp.stack([lambda_q1, lambda_k1, lambda_q2, lambda_k2])
    da = _attention(proj3, q_norm_w, k_norm_w, lam4, diff_subln_w, lambda_init)

    wr = jnp.zeros((LANES, d), F32).at[0:N_GROUPS].set(router_group_w.T).at[8:8 + N_EXPERTS].set(router_expert_w.T)
    br = jnp.zeros((LANES,), F32).at[0:N_GROUPS].set(router_group_b).at[N_GROUPS:8].set(NEG_BIG)
    br = br.at[8:8 + N_EXPERTS].set(router_expert_b).reshape(LANES, 1)
    h1, u2, il, ic, cnt = _merge(
        rw.reshape(t, RWKV_WIDTH), da.reshape(t, DIFF_V_WIDTH), proj, h0,
        w_branch_rwkv.astype(BF16), w_branch_diff.astype(BF16), w_out.astype(BF16), norm2_w, wr, br, lp, tok_tm)

    n_real = b * (lp - FRONT_PAD)
    n_blocks = -(-(2 * n_real) // MOE_BLOCK) + N_EXPERTS
    dest1, dest2, block_e, n_used, block_grp, block_nxt = _routing_tables(il, cnt, n_blocks, tok_tm)
    xb = _dispatch(dest1, dest2, u2, n_blocks * MOE_BLOCK + 2 * tok_tm, tok_tm)
    yb = _moe(block_e, n_used, block_grp, block_nxt, xb, expert_w_gate, expert_w_up, expert_w_down)
    return h1, ic, dest1, dest2, yb


def kernel(x, meta_tokens, norm1_w, w_in, rwkv_mu, rwkv_w0, rwkv_w2, rwkv_a0, rwkv_a2, rwkv_g2, rwkv_k_k, rwkv_k_a, rwkv_r_k, rwkv_ln_w, rwkv_ln_b, q_norm_w, k_norm_w, lambda_q1, lambda_k1, lambda_q2, lambda_k2, diff_subln_w, w_branch_rwkv, w_branch_diff, w_out, norm2_w, router_group_w, router_group_b, router_expert_w, router_expert_b, expert_w_gate, expert_w_up, expert_w_down):
    b, seq, d = x.shape
    depth = norm1_w.shape[0]
    assert depth == 1, "the combine step emits the final output; deeper stacks need an intermediate form"
    lp = FRONT_PAD + N_META + seq
    meta = jnp.broadcast_to(meta_tokens[None].astype(x.dtype), (b, N_META, d))
    h0 = jnp.concatenate([jnp.zeros((b, FRONT_PAD, d), x.dtype), meta, x], axis=1).reshape(b * lp, d)
    proj_tm = 2048 if (b * lp) % 2048 == 0 else 128
    tok_tm = 512 if (b * lp) % 512 == 0 else 128
    l = 0
    h1, ic, dest1, dest2, yb = _layer(
        h0, lp, l, norm1_w[l], w_in[l], rwkv_mu[l], rwkv_w0[l], rwkv_w2[l], rwkv_a0[l], rwkv_a2[l],
        rwkv_g2[l], rwkv_k_k[l], rwkv_k_a[l], rwkv_r_k[l], rwkv_ln_w[l], rwkv_ln_b[l], q_norm_w[l],
        k_norm_w[l], lambda_q1[l], lambda_k1[l], lambda_q2[l], lambda_k2[l], diff_subln_w[l],
        w_branch_rwkv[l], w_branch_diff[l], w_out[l], norm2_w[l], router_group_w[l], router_group_b[l],
        router_expert_w[l], router_expert_b[l], expert_w_gate[l], expert_w_up[l], expert_w_down[l],
        proj_tm, tok_tm)
    return _combine(dest1, dest2, h1, ic, yb, b, lp, OUT_TILE)
```

```python
import functools
import math

import jax
import jax.numpy as jnp
from jax import lax
from jax.experimental import pallas as pl
from jax.experimental.pallas import tpu as pltpu

F32 = jnp.float32
BF16 = jnp.bfloat16

D_MODEL = 1024
N_META = 16
NORM_EPS = 1e-6
RWKV_HEADS = 16
RWKV_HEAD_DIM = 64
RWKV_WIDTH = RWKV_HEADS * RWKV_HEAD_DIM
DECAY_LORA = 64
AAA_LORA = 64
GATE_LORA = 128
LORA_COLS = DECAY_LORA + AAA_LORA + GATE_LORA
RWKV_GN_EPS = 64e-5
DIFF_HEADS = 8
DIFF_HEAD_DIM = 64
DIFF_V_DIM = 2 * DIFF_HEAD_DIM
DIFF_QK_WIDTH = DIFF_HEADS * 2 * DIFF_HEAD_DIM
DIFF_V_WIDTH = DIFF_HEADS * DIFF_V_DIM
ROPE_THETA = 500000.0
ROPE_DIM = DIFF_HEAD_DIM // 4
N_GROUPS = 4
EXPERTS_PER_GROUP = 8
N_EXPERTS = N_GROUPS * EXPERTS_PER_GROUP
EXPERT_FF = 512

LANES = 128
ATTN_BLOCK = 256
ATTN_KEY_BLOCK = 512
ATTN_HEADS_PER_STEP = 4
FRONT_PAD = ATTN_BLOCK - N_META
RWKV_CHUNK = 64
RWKV_SUBCHUNKS = 3
RWKV_PREP_COLS = 256
RWKV_HEAD_GROUP = 2
OUT_TILE = 256
MOE_BLOCK = 256
MERGE_COLS = 256
NEG_BIG = -1e30
VMEM_LIMIT = 48 * 1024 * 1024

COL_GATE = 0
COL_DIFF = 2 * D_MODEL
COL_RWKV = COL_DIFF + 2 * DIFF_QK_WIDTH + DIFF_V_WIDTH
COL_LORA = COL_RWKV + 3 * RWKV_WIDTH
IN_COLS = COL_LORA + LORA_COLS


def _dot(a, b):
    return jnp.dot(a, b, preferred_element_type=F32)


def _dot_nt(a, b):
    return lax.dot_general(a, b, (((1,), (1,)), ((), ())), preferred_element_type=F32)


def _split2(x):
    hi = x.astype(BF16)
    lo = (x - hi.astype(F32)).astype(BF16)
    return hi, lo


def _mm3(a, b_ref):
    ah, al = _split2(a)
    return _dot(ah, b_ref[0]) + _dot(ah, b_ref[1]) + _dot(al, b_ref[0])


def _hi_lo(w):
    hi = w.astype(BF16)
    return jnp.stack([hi, (w - hi.astype(F32)).astype(BF16)])


def _sigmoid(x):
    return 1.0 / (1.0 + jnp.exp(-x))


def _head_ones(width=LANES, head=RWKV_HEAD_DIM):
    r = lax.broadcasted_iota(jnp.int32, (width, width), 0) // head
    c = lax.broadcasted_iota(jnp.int32, (width, width), 1) // head
    return jnp.where(r == c, 1.0, 0.0).astype(BF16)


def _seg_sum(x, ones_bd):
    hi, lo = _split2(x)
    return _dot(hi, ones_bd) + _dot(lo, ones_bd)


def _seg_sum_wide(x, ones_bd):
    rows, width = x.shape
    n = width // LANES
    xs = jnp.concatenate([x[:, i * LANES:(i + 1) * LANES] for i in range(n)], axis=0)
    ys = _seg_sum(xs, ones_bd)
    return jnp.concatenate([ys[i * rows:(i + 1) * rows] for i in range(n)], axis=1)


def _to_token_tiles(ref, x):
    n, d = x.shape
    k = d // LANES
    for s in range(k):
        ref[pl.ds(s, n, stride=k), :] = x[:, s * LANES:(s + 1) * LANES]


def _from_token_tiles(ref, n):
    k = ref.shape[0] // n
    return jnp.concatenate([ref[pl.ds(s, n, stride=k), :] for s in range(k)], axis=1)


def _proj_kernel(h_ref, nw_ref, w_ref, o_ref, u_ref):
    @pl.when(pl.program_id(1) == 0)
    def _():
        x = h_ref[...]
        ms = jnp.mean(x * x, axis=-1, keepdims=True)
        u_ref[...] = (x * lax.rsqrt(ms + NORM_EPS) * nw_ref[...]).astype(BF16)

    o_ref[...] = _dot(u_ref[...], w_ref[...]).astype(o_ref.dtype)


def _proj(h, norm_w, w_bf16, tm, tn):
    t, d = h.shape
    n = w_bf16.shape[1]
    return pl.pallas_call(
        _proj_kernel,
        out_shape=jax.ShapeDtypeStruct((t, n), BF16),
        grid=(t // tm, n // tn),
        in_specs=[
            pl.BlockSpec((tm, d), lambda i, j: (i, 0)),
            pl.BlockSpec((1, d), lambda i, j: (0, 0)),
            pl.BlockSpec((d, tn), lambda i, j: (0, j)),
        ],
        out_specs=pl.BlockSpec((tm, tn), lambda i, j: (i, j)),
        scratch_shapes=[pltpu.VMEM((tm, d), BF16)],
        compiler_params=pltpu.CompilerParams(
            dimension_semantics=("parallel", "arbitrary"), vmem_limit_bytes=VMEM_LIMIT),
        name="norm_proj",
    )(h, norm_w.reshape(1, d), w_bf16)


def _rwkv_kernel(*refs):
    o_ref = refs[16]
    xr, xk, xv, xl, s_ref = refs[17:22]
    c = pl.program_id(1)
    n_skip = FRONT_PAD // o_ref.shape[1]

    @pl.when(c == 0)
    def _init():
        for xs in (xr, xk, xv, xl):
            xs[0:8, :] = jnp.zeros((8, xs.shape[1]), F32)
        s_ref[...] = jnp.zeros(s_ref.shape, F32)

    @pl.when(c < n_skip)
    def _pad():
        o_ref[...] = jnp.zeros(o_ref.shape, o_ref.dtype)

    @pl.when(c >= n_skip)
    def _chunk():
        _rwkv_chunk(*refs)


def _rwkv_chunk(r_ref, k_ref, v_ref, lo_ref, mu_ref, mul_ref, w0_ref, w2_ref, a0_ref, a2_ref, g2_ref,
                kkw_ref, kaw_ref, rkw_ref, lnw_ref, lnb_ref, o_ref,
                xr, xk, xv, xl, s_ref, kt_s, bt_s, kn_s, rt_s, v_s, y_s, gc_s, bon_s, g_s):
    R = r_ref.shape[1]
    C = RWKV_CHUNK
    n_sub = R // C
    W = RWKV_WIDTH

    def shift_mix(in_ref, xs, mu, cs):
        x = in_ref[0, :, cs].astype(F32)
        xs[8:8 + R, cs] = x
        prev = xs[7:7 + R, cs]
        xs[7:8, cs] = x[R - 1:R, :]
        return x + (prev - x) * mu

    lo = shift_mix(lo_ref, xl, mul_ref[...], slice(0, LORA_COLS))
    xw = _split2(jnp.tanh(lo[:, 0:DECAY_LORA]))
    xa = _split2(lo[:, DECAY_LORA:DECAY_LORA + AAA_LORA])
    xg = _split2(_sigmoid(lo[:, DECAY_LORA + AAA_LORA:LORA_COLS]))

    def mm3(a_hi_lo, b_ref, cs):
        ah, al = a_hi_lo
        return _dot(ah, b_ref[0, :, cs]) + _dot(ah, b_ref[1, :, cs]) + _dot(al, b_ref[0, :, cs])

    ones_bd = _head_ones()
    ti = lax.broadcasted_iota(jnp.int32, (R, R), 0)
    tj = lax.broadcasted_iota(jnp.int32, (R, R), 1)
    ltri = jnp.where((ti >= tj) & (ti // C == tj // C), 1.0, 0.0).astype(BF16)

    for c0 in range(0, W, RWKV_PREP_COLS):
        cs = slice(c0, c0 + RWKV_PREP_COLS)
        col = lambda ref, off=0: ref[:, off + c0:off + c0 + RWKV_PREP_COLS]
        r = shift_mix(r_ref, xr, col(mu_ref), cs)
        k = shift_mix(k_ref, xk, col(mu_ref, W), cs)
        v = shift_mix(v_ref, xv, col(mu_ref, 2 * W), cs)

        lw = (-math.exp(-0.5)) * _sigmoid(col(w0_ref) + mm3(xw, w2_ref, cs))
        a = _sigmoid(col(a0_ref) + mm3(xa, a2_ref, cs))
        g_s[:, cs] = mm3(xg, g2_ref, cs)

        kk = k * col(kkw_ref)
        kkn = kk * lax.rsqrt(jnp.maximum(_seg_sum_wide(kk * kk, ones_bd), 1e-24))
        k2 = k * (1.0 + (a - 1.0) * col(kaw_ref))
        bon_s[:, cs] = _seg_sum_wide(r * k2 * col(rkw_ref), ones_bd) * v

        l1 = lw.astype(BF16)
        rem = lw - l1.astype(F32)
        l2 = rem.astype(BF16)
        l3 = (rem - l2.astype(F32)).astype(BF16)
        cum = _dot(ltri, l1) + _dot(ltri, l2) + _dot(ltri, l3)
        e_pos = jnp.exp(cum)
        e_neg = jnp.exp(-cum)
        kt_s[:, cs] = kkn * jnp.exp(cum - lw)
        bt_s[:, cs] = kkn * a * e_neg
        kn_s[:, cs] = k2 * e_neg
        rt_s[:, cs] = r * e_pos
        v_s[:, cs] = v
        for s in range(n_sub):
            gc_s[s:s + 1, cs] = jnp.exp(cum[(s + 1) * C - 1:(s + 1) * C, :])

    G = RWKV_HEAD_GROUP
    GL = G * RWKV_HEAD_DIM
    GC = G * C
    n_groups = W // GL
    lane_head = lax.broadcasted_iota(jnp.int32, (1, GL), 1) // RWKV_HEAD_DIM
    head_mask = [jnp.where(lane_head == h, 1.0, 0.0).astype(BF16) for h in range(G)]
    trow = lax.broadcasted_iota(jnp.int32, (C, GC), 0)
    tcol = lax.broadcasted_iota(jnp.int32, (C, GC), 1) % C
    strict = trow > tcol
    incl = trow >= tcol
    eye = jnp.where(trow == tcol, 1.0, 0.0)
    br = lax.broadcasted_iota(jnp.int32, (GL, GL), 0) // RWKV_HEAD_DIM
    bc = lax.broadcasted_iota(jnp.int32, (GL, GL), 1) // RWKV_HEAD_DIM
    bdmask = jnp.where(br == bc, 1.0, 0.0)

    def stack(y):
        yb = y.astype(BF16)
        return jnp.concatenate([yb * m for m in head_mask], axis=0)

    groups = range(n_groups)
    units = [(s, p) for s in range(n_sub) for p in groups]
    win = {(s, p): (slice(s * C, (s + 1) * C), slice(p * GL, (p + 1) * GL)) for s, p in units}
    def prepare(units):
        kt = {q: kt_s[win[q]] for q in units}
        bt = {q: bt_s[win[q]] for q in units}
        kn = {q: kn_s[win[q]] for q in units}
        rt = {q: rt_s[win[q]] for q in units}
        vv = {q: v_s[win[q]] for q in units}
        gc = {q: gc_s[q[0]:q[0] + 1, win[q][1]] for q in units}
        kr = {q: jnp.concatenate([kt[q], rt[q]], axis=0).astype(BF16) for q in units}
        m_all = {q: _dot_nt(kr[q], jnp.concatenate([stack(bt[q]), stack(kn[q])], axis=0))
                 for q in units}
        p_mat = {q: jnp.where(incl, m_all[q][C:2 * C, 0:GC], 0.0).astype(BF16) for q in units}
        bq_mat = {q: jnp.concatenate([jnp.where(strict, m_all[q][0:C, GC:2 * GC], 0.0),
                                      jnp.where(incl, m_all[q][C:2 * C, GC:2 * GC], 0.0)], axis=0).astype(BF16)
                  for q in units}
        v_term = {q: _dot(bq_mat[q], stack(vv[q])) for q in units}

        pw = {q: -jnp.where(strict, m_all[q][0:C, 0:GC], 0.0) for q in units}
        t_inv = {q: eye + pw[q] for q in units}
        pw = {q: _dot(pw[q].astype(BF16), stack(pw[q])) for q in units}
        for _ in range(int(math.log2(C)) - 2):
            both = {q: _dot(jnp.concatenate([t_inv[q], pw[q]], axis=0).astype(BF16), stack(pw[q]))
                    for q in units}
            t_inv = {q: t_inv[q] + both[q][0:C] for q in units}
            pw = {q: both[q][C:2 * C] for q in units}
        t_inv = {q: (t_inv[q] + _dot(t_inv[q].astype(BF16), stack(pw[q]))).astype(BF16) for q in units}
        xc = {q: jnp.concatenate([bt[q] * gc[q], kn[q] * gc[q]], axis=0).astype(BF16) for q in units}
        return kr, p_mat, v_term, t_inv, xc, vv, gc

    kr, p_mat, v_term, t_inv, xc, vv, gc = prepare(units)

    state = [s_ref[p] for p in groups]
    for s in range(n_sub):
        qs = [(s, p) for p in groups]
        ks = [_dot_nt(kr[q], state[q[1]].astype(BF16)) for q in qs]
        u = [-_dot(t_inv[q], stack(ks[q[1]][0:C] + v_term[q][0:C])) for q in qs]
        for q in qs:
            y_s[win[q]] = ks[q[1]][C:2 * C] + v_term[q][C:2 * C] + _dot(p_mat[q], stack(u[q[1]]))
        for q in qs:
            uc = jnp.concatenate([u[q[1]], vv[q]], axis=0)
            state[q[1]] = state[q[1]] * gc[q] + bdmask * _dot(jnp.transpose(uc).astype(BF16), xc[q])
    for p in groups:
        s_ref[p] = state[p]

    inv_n = 1.0 / RWKV_HEAD_DIM
    for c0 in range(0, W, RWKV_PREP_COLS):
        cs = slice(c0, c0 + RWKV_PREP_COLS)
        y = y_s[:, cs]
        mean = _seg_sum_wide(y, ones_bd) * inv_n
        dlt = y - mean
        var = _seg_sum_wide(dlt * dlt, ones_bd) * inv_n
        yn = dlt * lax.rsqrt(var + RWKV_GN_EPS) * lnw_ref[:, cs] + lnb_ref[:, cs]
        o_ref[0, :, cs] = ((yn + bon_s[:, cs]) * g_s[:, cs]).astype(o_ref.dtype)


def _rwkv(proj3, mu_rkv, mu_lo, w0, w2, a0, a2, g2, k_k, k_a, r_k, ln_w, ln_b):
    b, lp, _ = proj3.shape
    C = RWKV_CHUNK * RWKV_SUBCHUNKS
    W = RWKV_WIDTH
    cb = COL_RWKV // W
    lb = COL_LORA // LORA_COLS
    row = lambda x: x.reshape(1, -1)
    full = lambda shape: pl.BlockSpec(shape, lambda i, c: (0,) * len(shape))
    wide = pltpu.VMEM((C, W), F32)
    gl = RWKV_HEAD_GROUP * RWKV_HEAD_DIM
    return pl.pallas_call(
        _rwkv_kernel,
        out_shape=jax.ShapeDtypeStruct((b, lp, W), BF16),
        grid=(b, lp // C),
        in_specs=[
            pl.BlockSpec((1, C, W), lambda i, c: (i, c, cb)),
            pl.BlockSpec((1, C, W), lambda i, c: (i, c, cb + 1)),
            pl.BlockSpec((1, C, W), lambda i, c: (i, c, cb + 2)),
            pl.BlockSpec((1, C, LORA_COLS), lambda i, c: (i, c, lb)),
            full((1, 3 * W)), full((1, LORA_COLS)),
            full((1, W)), full((2, DECAY_LORA, W)), full((1, W)), full((2, AAA_LORA, W)), full((2, GATE_LORA, W)),
            full((1, W)), full((1, W)), full((1, W)), full((1, W)), full((1, W)),
        ],
        out_specs=pl.BlockSpec((1, C, W), lambda i, c: (i, c, 0)),
        scratch_shapes=[
            pltpu.VMEM((C + 8, W), F32), pltpu.VMEM((C + 8, W), F32), pltpu.VMEM((C + 8, W), F32),
            pltpu.VMEM((C + 8, LORA_COLS), F32),
            pltpu.VMEM((W // gl, gl, gl), F32),
            wide, wide, wide, wide, wide, wide, pltpu.VMEM((8, W), F32), wide, wide,
        ],
        compiler_params=pltpu.CompilerParams(
            dimension_semantics=("parallel", "arbitrary"), vmem_limit_bytes=VMEM_LIMIT),
        name="rwkv7_time_mix",
    )(proj3, proj3, proj3, proj3, row(mu_rkv), row(mu_lo), row(w0), _hi_lo(w2), row(a0), _hi_lo(a2), _hi_lo(g2),
      row(k_k), row(k_a), row(r_k), row(ln_w), row(ln_b))


def _attn_kernel(q_ref, k_ref, v_ref, cos_ref, s1_ref, s2_ref, qw_ref, kw_ref, lam_ref, sw_ref, o_ref,
                 kp_s, vp_s, m_s, acc_s, *, tq, tk, lambda_init):
    qi = pl.program_id(2)
    lp = k_ref.shape[1]
    n_real = lp - FRONT_PAD
    lk = kp_s.shape[0]
    nh = k_ref.shape[2] // LANES
    vw = 2 * LANES
    heads = range(nh)
    hs = [slice(h * LANES, (h + 1) * LANES) for h in heads]
    ones_bd = _head_ones(LANES, DIFF_HEAD_DIM)
    shift = ROPE_DIM // 2

    def norm_rope(x, w, rows):
        ms = _seg_sum(x * x, ones_bd) * (1.0 / DIFF_HEAD_DIM)
        xn = x * lax.rsqrt(ms + NORM_EPS) * w
        return (xn * cos_ref[rows, :] + pltpu.roll(xn, shift, 1) * s1_ref[rows, :]
                + pltpu.roll(xn, LANES - shift, 1) * s2_ref[rows, :])

    @pl.when(qi == 0)
    def _prep():
        def put(dst, n):
            src = pl.ds(FRONT_PAD + dst, n)
            rows = pl.ds(dst, n)
            for h in heads:
                kp_s[rows, hs[h]] = norm_rope(k_ref[0, src, hs[h]].astype(F32), kw_ref[...], src).astype(BF16)
                vp_s[rows, h * vw:h * vw + LANES] = v_ref[0, src, hs[h]].astype(BF16)
                vp_s[rows, h * vw + LANES:(h + 1) * vw] = jnp.ones((n, LANES), BF16)

        def body(i, carry):
            put(pl.multiple_of(i * LANES, LANES), LANES)
            return carry
        lax.fori_loop(0, n_real // LANES, body, 0, unroll=2)
        tail = n_real % LANES
        if tail:
            put(n_real - tail, tail)
        if lk > n_real:
            kp_s[n_real:lk, :] = jnp.zeros((lk - n_real, nh * LANES), BF16)
            vp_s[n_real:lk, :] = jnp.zeros((lk - n_real, nh * vw), BF16)

    lane = lax.broadcasted_iota(jnp.int32, (1, LANES), 1)
    m0 = jnp.where(lane < DIFF_HEAD_DIM, 1.0, 0.0)
    m1 = 1.0 - m0
    rows_q = pl.ds(pl.multiple_of(qi * tq, tq), tq)
    qs = []
    for h in heads:
        qn = norm_rope(q_ref[0, :, hs[h]].astype(F32), qw_ref[...], rows_q) * (DIFF_HEAD_DIM ** -0.5)
        qs.append(jnp.concatenate([qn * m0, qn * m1], axis=0).astype(BF16))
    first_row = qi * tq - FRONT_PAD
    row = first_row + lax.broadcasted_iota(jnp.int32, (2 * tq, LANES), 0) % tq
    col0 = lax.broadcasted_iota(jnp.int32, (2 * tq, LANES), 1)

    def step(j, width, causal, init=False):
        start = j * tk
        if not isinstance(j, int):
            start = pl.multiple_of(start, tk)
        ks = pl.ds(start, width)
        n_sub = width // LANES
        s = [_dot_nt(qs[h], kp_s[ks, hs[h]]) for h in heads]
        sub = [[s[h][:, c * LANES:(c + 1) * LANES] for c in range(n_sub)] for h in heads]
        if causal:
            sub = [[jnp.where(start + c * LANES + col0 <= row, sub[h][c], NEG_BIG) for c in range(n_sub)]
                   for h in heads]
        m_new = []
        for h in heads:
            mx = functools.reduce(jnp.maximum, sub[h])
            mx = jnp.broadcast_to(jnp.max(mx, axis=1, keepdims=True), mx.shape)
            m_new.append(mx if init else jnp.maximum(m_s[h], mx))
        p = [jnp.concatenate([jnp.exp(sub[h][c] - m_new[h]) for c in range(n_sub)], axis=1).astype(BF16)
             for h in heads]
        pv = [_dot(p[h], vp_s[ks, h * vw:(h + 1) * vw]) for h in heads]
        for h in heads:
            if init:
                acc_s[h] = pv[h]
            else:
                alpha = jnp.exp(m_s[h] - m_new[h])
                acc_s[h] = jnp.concatenate([alpha, alpha], axis=1) * acc_s[h] + pv[h]
            m_s[h] = m_new[h]

    assert FRONT_PAD + N_META == tq and tk % tq == 0 and N_META <= LANES
    n_full = jnp.maximum(first_row + 1, 0) // tk
    last = jnp.maximum(first_row + tq - 1, 0) // tk
    step(n_full, tk, True, init=True)

    def mid(j, carry):
        step(j, tk, False)
        return carry
    lax.fori_loop(0, n_full, mid, 0)

    @pl.when(last > n_full)
    def _diag():
        step(last, LANES, True)

    lam = (jnp.exp(jnp.sum(lam_ref[0:1, :] * lam_ref[1:2, :], axis=1, keepdims=True))
           - jnp.exp(jnp.sum(lam_ref[2:3, :] * lam_ref[3:4, :], axis=1, keepdims=True)) + lambda_init)
    for h in heads:
        acc = acc_s[h]
        o = acc[:, 0:LANES] / acc[:, LANES:vw]
        od = o[0:tq] - lam * o[tq:2 * tq]
        ms = jnp.mean(od * od, axis=1, keepdims=True)
        o_ref[0, :, hs[h]] = (od * lax.rsqrt(ms + NORM_EPS) * sw_ref[...]
                              * (1.0 - lambda_init)).astype(o_ref.dtype)


def _rope_tables(lp):
    half = ROPE_DIM // 2
    inv_freq = jnp.exp(-math.log(ROPE_THETA) * jnp.arange(half, dtype=F32) * 2.0 / ROPE_DIM)
    pos = (jnp.arange(lp) - FRONT_PAD).astype(F32)
    ang = pos[:, None] * inv_freq[None, :]
    cos, sin = jnp.cos(ang), jnp.sin(ang)
    one = jnp.ones((lp, DIFF_HEAD_DIM - ROPE_DIM), F32)
    zero = jnp.zeros((lp, DIFF_HEAD_DIM - ROPE_DIM), F32)
    zh = jnp.zeros((lp, half), F32)
    c = jnp.concatenate([cos, cos, one], axis=1)
    s1 = jnp.concatenate([zh, sin, zero], axis=1)
    s2 = jnp.concatenate([-sin, zh, zero], axis=1)
    dup = lambda t: jnp.concatenate([t, t], axis=1)
    return dup(c), dup(s1), dup(s2)


def _attention(proj3, q_norm_w, k_norm_w, lam4, subln_w, lambda_init):
    b, lp, _ = proj3.shape
    tq, tk, nh = ATTN_BLOCK, ATTN_KEY_BLOCK, ATTN_HEADS_PER_STEP
    lk = -(-(lp - FRONT_PAD) // tk) * tk
    hw = nh * LANES
    qb = COL_DIFF // hw
    kb = qb + DIFF_QK_WIDTH // hw
    vb = kb + DIFF_QK_WIDTH // hw
    cos, s1, s2 = _rope_tables(lp)
    dup = lambda w: jnp.concatenate([w, w]).reshape(1, LANES)
    full = lambda shape: pl.BlockSpec(shape, lambda i, h, q: (0,) * len(shape))
    return pl.pallas_call(
        functools.partial(_attn_kernel, tq=tq, tk=tk, lambda_init=lambda_init),
        out_shape=jax.ShapeDtypeStruct((b, lp, DIFF_V_WIDTH), BF16),
        grid=(b, DIFF_HEADS // nh, lp // tq),
        in_specs=[
            pl.BlockSpec((1, tq, hw), lambda i, h, q: (i, q, qb + h)),
            pl.BlockSpec((1, lp, hw), lambda i, h, q: (i, 0, kb + h)),
            pl.BlockSpec((1, lp, hw), lambda i, h, q: (i, 0, vb + h)),
            full((lp, LANES)), full((lp, LANES)), full((lp, LANES)),
            full((1, LANES)), full((1, LANES)), full((4, DIFF_HEAD_DIM)), full((1, LANES)),
        ],
        out_specs=pl.BlockSpec((1, tq, hw), lambda i, h, q: (i, q, h)),
        scratch_shapes=[pltpu.VMEM((lk, hw), BF16), pltpu.VMEM((lk, 2 * hw), BF16),
                        pltpu.VMEM((nh, 2 * tq, LANES), F32), pltpu.VMEM((nh, 2 * tq, 2 * LANES), F32)],
        compiler_params=pltpu.CompilerParams(
            dimension_semantics=("parallel", "parallel", "arbitrary"), vmem_limit_bytes=VMEM_LIMIT),
        name="diff_attention",
    )(proj3, proj3, proj3, cos, s1, s2, dup(q_norm_w), dup(k_norm_w), lam4, subln_w.reshape(1, LANES))


def _merge_kernel(rw_ref, da_ref, g1_ref, g2_ref, h_ref, wbr_ref, wbd_ref, wo_ref, n2_ref, wr_ref, br_ref,
                  h1_ref, u2_ref, il_ref, ic_ref, cnt_ref, base_s, mg_s, *, lp):
    i = pl.program_id(0)
    tm = rw_ref.shape[0]

    @pl.when(i == 0)
    def _():
        base_s[...] = jnp.zeros(base_s.shape, F32)

    d = h_ref.shape[1]
    for c0 in range(0, d, MERGE_COLS):
        cs = slice(c0, c0 + MERGE_COLS)
        y1 = _dot(rw_ref[...], wbr_ref[:, cs])
        y2 = _dot(da_ref[...], wbd_ref[:, cs])
        mg_s[:, cs] = (_sigmoid(g1_ref[:, cs].astype(F32)) * y1
                       + _sigmoid(g2_ref[:, cs].astype(F32)) * y2).astype(BF16)
    ss = jnp.zeros((tm, 1), F32)
    for c0 in range(0, d, MERGE_COLS):
        cs = slice(c0, c0 + MERGE_COLS)
        h1c = h_ref[:, cs] + _dot(mg_s[...], wo_ref[:, cs])
        h1_ref[:, cs] = h1c
        ss = ss + jnp.sum(h1c * h1c, axis=-1, keepdims=True)
    u2 = h1_ref[...] * lax.rsqrt(ss * (1.0 / d) + NORM_EPS) * n2_ref[...]
    _to_token_tiles(u2_ref, u2)

    uh, ul = _split2(u2)
    wh, wl = _split2(wr_ref[...])
    lt = _dot_nt(wh, uh) + _dot_nt(wh, ul) + _dot_nt(wl, uh) + br_ref[...]

    gi8 = lax.broadcasted_iota(jnp.int32, (8, tm), 0)
    lg = lt[0:8]
    ge = jnp.exp(lg - jnp.max(lg, axis=0, keepdims=True))
    gp = ge / jnp.sum(ge, axis=0, keepdims=True)
    gv = jnp.max(gp, axis=0, keepdims=True)
    gidx = jnp.min(jnp.where(gp == gv, gi8, N_EXPERTS), axis=0, keepdims=True)

    ei = lax.broadcasted_iota(jnp.int32, (N_EXPERTS, tm), 0)
    sel = (ei // EXPERTS_PER_GROUP) == gidx
    le = jnp.where(sel, lt[8:8 + N_EXPERTS], NEG_BIG)
    ee = jnp.where(sel, jnp.exp(le - jnp.max(le, axis=0, keepdims=True)), 0.0)
    ep = jnp.where(sel, ee / jnp.sum(ee, axis=0, keepdims=True), -1.0)
    v1 = jnp.max(ep, axis=0, keepdims=True)
    i1 = jnp.min(jnp.where(ep == v1, ei, N_EXPERTS), axis=0, keepdims=True)
    ep2 = jnp.where(ei == i1, -1.0, ep)
    v2 = jnp.max(ep2, axis=0, keepdims=True)
    i2 = jnp.min(jnp.where(ep2 == v2, ei, N_EXPERTS), axis=0, keepdims=True)
    den = v1 + v2
    gate1 = gv * v1 / den
    gate2 = gv * v2 / den

    tok = (i * tm + lax.broadcasted_iota(jnp.int32, (1, tm), 1)).astype(F32)
    pos = tok - jnp.floor((tok + 0.5) / lp) * lp
    valid = pos > (FRONT_PAD - 0.5)

    oh1 = jnp.where((ei == i1) & valid, 1.0, 0.0)
    oh2 = jnp.where((ei == i2) & valid, 1.0, 0.0)
    oh = oh1 + oh2
    ur = lax.broadcasted_iota(jnp.int32, (tm, tm), 0)
    uc = lax.broadcasted_iota(jnp.int32, (tm, tm), 1)
    before = jnp.where(ur < uc, 1.0, 0.0).astype(BF16)
    tot = base_s[:, 0:1] + _dot(oh.astype(BF16), before)
    rank1 = jnp.sum(oh1 * tot, axis=0, keepdims=True)
    rank2 = jnp.sum(oh2 * tot, axis=0, keepdims=True)
    base_s[...] = base_s[...] + jnp.sum(oh, axis=1, keepdims=True)
    cnt_ref[...] = base_s[...]

    il = jnp.where(gi8 == 0, i1, jnp.where(gi8 == 1, i2, jnp.where(
        gi8 == 2, rank1.astype(jnp.int32), jnp.where(gi8 == 3, rank2.astype(jnp.int32), jnp.where(
            gi8 == 4, valid.astype(jnp.int32), 0)))))
    il_ref[...] = il
    ri = lax.broadcasted_iota(jnp.int32, (LANES, tm), 0)
    ic = jnp.where(ri == 0, gate1, jnp.where(ri == 1, gate2, 0.0))
    ic_ref[...] = jnp.transpose(ic)


def _merge(rw, da, proj, h0, wbr, wbd, wo, norm2_w, wr, br, lp, tm):
    t, d = h0.shape
    gb = COL_GATE // d
    full = lambda shape: pl.BlockSpec(shape, lambda i: (0,) * len(shape))
    tile = lambda c: pl.BlockSpec((tm, d), lambda i: (i, c))
    return pl.pallas_call(
        functools.partial(_merge_kernel, lp=lp),
        out_shape=(
            jax.ShapeDtypeStruct((t, d), F32),
            jax.ShapeDtypeStruct((t * (d // LANES), LANES), F32),
            jax.ShapeDtypeStruct((8, t), jnp.int32),
            jax.ShapeDtypeStruct((t, LANES), F32),
            jax.ShapeDtypeStruct((N_EXPERTS, LANES), F32),
        ),
        grid=(t // tm,),
        in_specs=[tile(0), tile(0), tile(gb), tile(gb + 1), tile(0),
                  full((d, d)), full((d, d)), full((d, d)), full((1, d)), full((LANES, d)), full((LANES, 1))],
        out_specs=(
            tile(0), pl.BlockSpec((tm * (d // LANES), LANES), lambda i: (i, 0)),
            pl.BlockSpec((8, tm), lambda i: (0, i)),
            pl.BlockSpec((tm, LANES), lambda i: (i, 0)),
            full((N_EXPERTS, LANES)),
        ),
        scratch_shapes=[pltpu.VMEM((N_EXPERTS, LANES), F32), pltpu.VMEM((tm, d), BF16)],
        compiler_params=pltpu.CompilerParams(
            dimension_semantics=("arbitrary",), vmem_limit_bytes=VMEM_LIMIT),
        name="merge_router",
    )(rw, da, proj, proj, h0, wbr, wbd, wo, norm2_w.reshape(1, d), wr, br)


def _dispatch_kernel(d1_ref, d2_ref, u_ref, xin_hbm, xb_hbm, sem):
    del xin_hbm
    tm = d1_ref.shape[2]
    k = u_ref.shape[0] // tm

    def start(r, carry):
        for prio, d_ref in enumerate((d1_ref, d2_ref)):
            pltpu.make_async_copy(u_ref.at[pl.ds(pl.multiple_of(r * k, k), k)],
                                  xb_hbm.at[pl.ds(pl.multiple_of(d_ref[0, 0, r] * k, k), k)],
                                  sem).start(priority=prio)
        return carry

    lax.fori_loop(0, tm, start, 0, unroll=8)
    for _ in range(2):
        pltpu.make_async_copy(u_ref, xb_hbm.at[pl.ds(0, tm * k)], sem).wait()


def _dispatch(dest1, dest2, u2, cap, tm):
    k = D_MODEL // LANES
    t = u2.shape[0] // k
    nt = t // tm
    smem = lambda: pl.BlockSpec((1, 1, tm), lambda i: (i, 0, 0), memory_space=pltpu.SMEM)
    return pl.pallas_call(
        _dispatch_kernel,
        out_shape=jax.ShapeDtypeStruct((cap * k, LANES), F32),
        grid=(nt,),
        in_specs=[smem(), smem(), pl.BlockSpec((tm * k, LANES), lambda i: (i, 0)),
                  pl.BlockSpec(memory_space=pl.ANY)],
        out_specs=pl.BlockSpec(memory_space=pl.ANY),
        scratch_shapes=[pltpu.SemaphoreType.DMA(())],
        input_output_aliases={3: 0},
        compiler_params=pltpu.CompilerParams(
            dimension_semantics=("arbitrary",), vmem_limit_bytes=VMEM_LIMIT),
        name="moe_dispatch",
    )(dest1.reshape(nt, 1, tm), dest2.reshape(nt, 1, tm), u2, jnp.zeros((cap * k, LANES), F32))


def _moe_kernel(be_ref, nb_ref, grp_ref, nxt_ref, x_ref, wg_hbm, wu_hbm, wd_hbm, o_ref,
                wg_f, wu_f, wd_f, wg_s, wu_s, wd_s, sem):
    i = pl.program_id(0)
    bm = MOE_BLOCK
    used = i < nb_ref[0]
    slot = lax.rem(grp_ref[i], 2)

    def fetch(expert, sl):
        return [pltpu.make_async_copy(w_hbm.at[expert], w_f.at[sl], sem.at[sl])
                for w_hbm, w_f in ((wg_hbm, wg_f), (wu_hbm, wu_f), (wd_hbm, wd_f))]

    @pl.when(used & ((i == 0) | (grp_ref[i] != grp_ref[jnp.maximum(i - 1, 0)])))
    def _():
        @pl.when(i == 0)
        def _():
            for cp in fetch(be_ref[0], 0):
                cp.start()

        @pl.when(nxt_ref[i] >= 0)
        def _():
            for cp in fetch(nxt_ref[i], 1 - slot):
                cp.start()

        for cp in fetch(be_ref[i], slot):
            cp.wait()
        wg_s[...] = wg_f[slot].astype(BF16)
        wu_s[...] = wu_f[slot].astype(BF16)
        wd_s[...] = wd_f[slot].astype(BF16)

    @pl.when(used)
    def _():
        x = _from_token_tiles(x_ref, bm).astype(BF16)
        hg = _dot(x, wg_s[...])
        hu = _dot(x, wu_s[...])
        hid = hg * _sigmoid(hg) * hu
        _to_token_tiles(o_ref, _dot(hid.astype(BF16), wd_s[...]))

    @pl.when(i >= nb_ref[0])
    def _():
        o_ref[...] = jnp.zeros(o_ref.shape, F32)


def _moe(block_e, n_used, block_grp, block_nxt, xb, wg, wu, wd):
    d = wg.shape[1]
    ff = wg.shape[2]
    bm = MOE_BLOCK
    k = d // LANES
    cap = block_e.shape[0] * bm
    hbm = pl.BlockSpec(memory_space=pl.ANY)
    return pl.pallas_call(
        _moe_kernel,
        out_shape=jax.ShapeDtypeStruct((cap * k, LANES), F32),
        grid_spec=pltpu.PrefetchScalarGridSpec(
            num_scalar_prefetch=4,
            grid=(cap // bm,),
            in_specs=[pl.BlockSpec((bm * k, LANES), lambda i, *_: (i, 0)), hbm, hbm, hbm],
            out_specs=pl.BlockSpec((bm * k, LANES), lambda i, *_: (i, 0)),
            scratch_shapes=[pltpu.VMEM((2, d, ff), F32), pltpu.VMEM((2, d, ff), F32), pltpu.VMEM((2, ff, d), F32),
                            pltpu.VMEM((d, ff), BF16), pltpu.VMEM((d, ff), BF16), pltpu.VMEM((ff, d), BF16),
                            pltpu.SemaphoreType.DMA((2,))],
        ),
        compiler_params=pltpu.CompilerParams(
            dimension_semantics=("arbitrary",), vmem_limit_bytes=VMEM_LIMIT),
        name="moe_experts",
    )(block_e, n_used, block_grp, block_nxt, xb, wg, wu, wd)


def _combine_kernel(d1_ref, d2_ref, n1_ref, n2_ref, h_ref, ic_ref, yb_hbm, o_ref, ga, gb, sem):
    tm = h_ref.shape[0]
    k = ga.shape[1] // tm
    n = pl.program_id(0) * pl.num_programs(1) + pl.program_id(1)
    total = pl.num_programs(0) * pl.num_programs(1)
    slot = lax.rem(n, 2)

    def issue(da_ref, db_ref, sl):
        def start(r, carry):
            for prio, (d_ref, buf) in enumerate(((da_ref, ga), (db_ref, gb))):
                pltpu.make_async_copy(yb_hbm.at[pl.ds(pl.multiple_of(d_ref[0, 0, r] * k, k), k)],
                                      buf.at[sl, pl.ds(pl.multiple_of(r * k, k), k)],
                                      sem.at[sl]).start(priority=prio)
            return carry
        lax.fori_loop(0, tm, start, 0, unroll=8)

    @pl.when(n == 0)
    def _first():
        issue(d1_ref, d2_ref, 0)

    @pl.when(n + 1 < total)
    def _next():
        issue(n1_ref, n2_ref, 1 - slot)

    for buf in (ga, gb):
        pltpu.make_async_copy(yb_hbm.at[pl.ds(0, tm * k)], buf.at[slot], sem.at[slot]).wait()
    ic = ic_ref[...]
    o_ref[0] = (h_ref[...] + ic[:, 0:1] * _from_token_tiles(ga.at[slot], tm)
                + ic[:, 1:2] * _from_token_tiles(gb.at[slot], tm))


def _combine(dest1, dest2, h1, ic, yb, b, lp, tm):
    t, d = h1.shape
    per = lp // tm
    first = (FRONT_PAD + N_META) // tm
    steps = per - first
    nt = t // tm
    tile = lambda i, j: i * per + j + first

    def next_tile(i, j):
        nxt = jnp.minimum(i * steps + j + 1, b * steps - 1)
        return tile(nxt // steps, nxt % steps)

    cur = lambda: pl.BlockSpec((1, 1, tm), lambda i, j: (tile(i, j), 0, 0), memory_space=pltpu.SMEM)
    nxt = lambda: pl.BlockSpec((1, 1, tm), lambda i, j: (next_tile(i, j), 0, 0), memory_space=pltpu.SMEM)
    d1 = dest1.reshape(nt, 1, tm)
    d2 = dest2.reshape(nt, 1, tm)
    return pl.pallas_call(
        _combine_kernel,
        out_shape=jax.ShapeDtypeStruct((b, lp - FRONT_PAD - N_META, d), F32),
        grid=(b, steps),
        in_specs=[
            cur(), cur(), nxt(), nxt(),
            pl.BlockSpec((tm, d), lambda i, j: (tile(i, j), 0)),
            pl.BlockSpec((tm, LANES), lambda i, j: (tile(i, j), 0)),
            pl.BlockSpec(memory_space=pl.ANY),
        ],
        out_specs=pl.BlockSpec((1, tm, d), lambda i, j: (i, j, 0)),
        scratch_shapes=[pltpu.VMEM((2, tm * (d // LANES), LANES), F32),
                        pltpu.VMEM((2, tm * (d // LANES), LANES), F32),
                        pltpu.SemaphoreType.DMA((2,))],
        compiler_params=pltpu.CompilerParams(
            dimension_semantics=("arbitrary", "arbitrary"), vmem_limit_bytes=VMEM_LIMIT),
        name="moe_combine",
    )(d1, d2, d1, d2, h1, ic, yb)


def _routing_tables(il, cnt, n_blocks, tm):
    bm = MOE_BLOCK
    counts = cnt[:, 0].astype(jnp.int32)
    padded = (counts + bm - 1) // bm * bm
    pad_end = jnp.cumsum(padded)
    pad_start = pad_end - padded
    valid = il[4] > 0
    spare = n_blocks * bm + jnp.arange(il.shape[1], dtype=jnp.int32) % tm
    eid = jnp.arange(N_EXPERTS, dtype=jnp.int32)
    first_slot = jnp.sum(jnp.where(il[0:2, :, None] == eid, pad_start, 0), axis=-1)
    dest1 = jnp.where(valid, first_slot[0] + il[2], spare).astype(jnp.int32)
    dest2 = jnp.where(valid, first_slot[1] + il[3], spare + tm).astype(jnp.int32)
    starts = jnp.arange(n_blocks, dtype=jnp.int32) * bm
    block_e = jnp.minimum(jnp.sum((pad_end[None, :] <= starts[:, None]).astype(jnp.int32), axis=1),
                          N_EXPERTS - 1)
    n_used = (pad_end[-1:] // bm).astype(jnp.int32)
    owns = counts > 0
    eid = jnp.arange(N_EXPERTS, dtype=jnp.int32)
    ordinal = jnp.cumsum(owns.astype(jnp.int32)) - 1
    later = (eid[None, :] > eid[:, None]) & owns[None, :]
    nxt = jnp.min(jnp.where(later, eid[None, :], N_EXPERTS), axis=1)
    nxt = jnp.where(nxt < N_EXPERTS, nxt, -1)
    return dest1, dest2, block_e, n_used, ordinal[block_e].astype(jnp.int32), nxt[block_e].astype(jnp.int32)


def _layer(h0, lp, l, norm1_w, w_in, rwkv_mu, rwkv_w0, rwkv_w2, rwkv_a0, rwkv_a2, rwkv_g2,
           rwkv_k_k, rwkv_k_a, rwkv_r_k, rwkv_ln_w, rwkv_ln_b, q_norm_w, k_norm_w,
           lambda_q1, lambda_k1, lambda_q2, lambda_k2, diff_subln_w, w_branch_rwkv, w_branch_diff,
           w_out, norm2_w, router_group_w, router_group_b, router_expert_w, router_expert_b,
           expert_w_gate, expert_w_up, expert_w_down, proj_tm, tok_tm):
    t, d = h0.shape
    b = t // lp
    lambda_init = 0.8 - 0.6 * math.exp(-0.3 * l)
    rw_cols = 3 * RWKV_WIDTH
    diff_cols = 2 * DIFF_QK_WIDTH + DIFF_V_WIDTH
    w_perm = jnp.concatenate([
        w_in[:, rw_cols + LORA_COLS + diff_cols:],
        w_in[:, rw_cols + LORA_COLS:rw_cols + LORA_COLS + diff_cols],
        w_in[:, :rw_cols + LORA_COLS],
    ], axis=1).astype(BF16)
    proj = _proj(h0, norm1_w, w_perm, proj_tm, 768)
    proj3 = proj.reshape(b, lp, IN_COLS)

    rw = _rwkv(proj3, rwkv_mu[:rw_cols], rwkv_mu[rw_cols:], rwkv_w0, rwkv_w2, rwkv_a0, rwkv_a2, rwkv_g2,
               rwkv_k_k, rwkv_k_a, rwkv_r_k.reshape(-1), rwkv_ln_w, rwkv_ln_b)
    lam4 = jnp.stack([lambda_q1, lambda_k1, lambda_q2, lambda_k2])
    da = _attention(proj3, q_norm_w, k_norm_w, lam4, diff_subln_w, lambda_init)

    wr = jnp.zeros((LANES, d), F32).at[0:N_GROUPS].set(router_group_w.T).at[8:8 + N_EXPERTS].set(router_expert_w.T)
    br = jnp.zeros((LANES,), F32).at[0:N_GROUPS].set(router_group_b).at[N_GROUPS:8].set(NEG_BIG)
    br = br.at[8:8 + N_EXPERTS].set(router_expert_b).reshape(LANES, 1)
    h1, u2, il, ic, cnt = _merge(
        rw.reshape(t, RWKV_WIDTH), da.reshape(t, DIFF_V_WIDTH), proj, h0,
        w_branch_rwkv.astype(BF16), w_branch_diff.astype(BF16), w_out.astype(BF16), norm2_w, wr, br, lp, tok_tm)

    n_real = b * (lp - FRONT_PAD)
    n_blocks = -(-(2 * n_real) // MOE_BLOCK) + N_EXPERTS
    dest1, dest2, block_e, n_used, block_grp, block_nxt = _routing_tables(il, cnt, n_blocks, tok_tm)
    xb = _dispatch(dest1, dest2, u2, n_blocks * MOE_BLOCK + 2 * tok_tm, tok_tm)
    yb = _moe(block_e, n_used, block_grp, block_nxt, xb, expert_w_gate, expert_w_up, expert_w_down)
    return h1, ic, dest1, dest2, yb


def kernel(x, meta_tokens, norm1_w, w_in, rwkv_mu, rwkv_w0, rwkv_w2, rwkv_a0, rwkv_a2, rwkv_g2, rwkv_k_k, rwkv_k_a, rwkv_r_k, rwkv_ln_w, rwkv_ln_b, q_norm_w, k_norm_w, lambda_q1, lambda_k1, lambda_q2, lambda_k2, diff_subln_w, w_branch_rwkv, w_branch_diff, w_out, norm2_w, router_group_w, router_group_b, router_expert_w, router_expert_b, expert_w_gate, expert_w_up, expert_w_down):
    b, seq, d = x.shape
    depth = norm1_w.shape[0]
    assert depth == 1, "the combine step emits the final output; deeper stacks need an intermediate form"
    lp = FRONT_PAD + N_META + seq
    meta = jnp.broadcast_to(meta_tokens[None].astype(x.dtype), (b, N_META, d))
    h0 = jnp.concatenate([jnp.zeros((b, FRONT_PAD, d), x.dtype), meta, x], axis=1).reshape(b * lp, d)
    proj_tm = 2048 if (b * lp) % 2048 == 0 else 128
    tok_tm = 512 if (b * lp) % 512 == 0 else 128
    l = 0
    h1, ic, dest1, dest2, yb = _layer(
        h0, lp, l, norm1_w[l], w_in[l], rwkv_mu[l], rwkv_w0[l], rwkv_w2[l], rwkv_a0[l], rwkv_a2[l],
        rwkv_g2[l], rwkv_k_k[l], rwkv_k_a[l], rwkv_r_k[l], rwkv_ln_w[l], rwkv_ln_b[l], q_norm_w[l],
        k_norm_w[l], lambda_q1[l], lambda_k1[l], lambda_q2[l], lambda_k2[l], diff_subln_w[l],
        w_branch_rwkv[l], w_branch_diff[l], w_out[l], norm2_w[l], router_group_w[l], router_group_b[l],
        router_expert_w[l], router_expert_b[l], expert_w_gate[l], expert_w_up[l], expert_w_down[l],
        proj_tm, tok_tm)
    return _combine(dest1, dest2, h1, ic, yb, b, lp, OUT_TILE)
```
